```python
import jax
import jax.numpy as jnp
from jax import lax
import numpy as np

D_MODEL = 4096
BATCH = 1
SEQ = 8192
DEPTH = 1

CHUNK = 64
NORM_EPS = 1e-6

M_HEADS = 4
M_QK_DIM = 256
M_V_DIM = 512
M_WIDTH = M_HEADS * M_V_DIM
CONV_WIDTH = 4
GATE_SOFTCAP = 15.0

A_HEADS = 16
A_NOPE_DIM = 128
A_ROPE_DIM = 64
A_V_DIM = 128
A_Q_RANK = 768
A_KV_RANK = 512
A_WIDTH = A_HEADS * A_V_DIM
ROPE_THETA = 10000.0
Q_BLOCK = 128

MIX_WIDTH = M_WIDTH + A_WIDTH

IN_SIZES = (2 * M_HEADS * M_QK_DIM, M_WIDTH, M_WIDTH, M_HEADS, M_HEADS, A_Q_RANK, A_KV_RANK, A_ROPE_DIM)
N_IN = sum(IN_SIZES)

N_EXPERTS = 64
EXPERT_FF = 512
SHARED_FF = 512
TOP_K = 8
N_GROUPS = 8
TOPK_GROUPS = 4
ROUTED_SCALE = 2.5
EXPERT_BLOCK = 128

kernel_name = 'hybrid_mlstm_mla_moe_block'


def rms_norm(x, g):
    xf = x.astype(jnp.float32)
    y = xf * lax.rsqrt(jnp.mean(xf * xf, axis=-1, keepdims=True) + NORM_EPS)
    return (y * g.astype(jnp.float32)).astype(x.dtype)


def softcap(z, cap):
    return cap * jnp.tanh(z / cap)


def rope_tables(seq_len):
    pos = jnp.arange(seq_len, dtype=jnp.float32)
    inv_freq = 1.0 / (ROPE_THETA ** (jnp.arange(0, A_ROPE_DIM, 2, dtype=jnp.float32) / A_ROPE_DIM))
    ang = pos[:, None] * inv_freq[None, :]
    return jnp.cos(ang), jnp.sin(ang)


def apply_rope(x, cos, sin):
    half = x.shape[-1] // 2
    x1 = x[..., :half].astype(jnp.float32)
    x2 = x[..., half:].astype(jnp.float32)
    return jnp.concatenate([x1 * cos - x2 * sin, x2 * cos + x1 * sin], axis=-1).astype(x.dtype)


def causal_depthwise_conv(u, w, b):
    out = lax.conv_general_dilated(u, w[:, None, :].astype(u.dtype), window_strides=(1,),
                                   padding=[(CONV_WIDTH - 1, 0)],
                                   dimension_numbers=('NWC', 'WIO', 'NWC'),
                                   feature_group_count=u.shape[-1])
    return out + b


def mlstm_chunkwise(q, k, v, i_pre, f_pre):
    b_sz, s_len, n_h, d_k = q.shape
    d_v = v.shape[-1]
    n_c = s_len // CHUNK

    def chunks(t):
        return t.astype(jnp.float32).reshape(b_sz, n_c, CHUNK, n_h, -1).transpose(1, 0, 3, 2, 4)

    qc = chunks(q) * (d_k ** -0.5)
    kc = chunks(k)
    vc = chunks(v)
    log_i = chunks(i_pre[..., None])[..., 0]
    log_f = chunks(jax.nn.log_sigmoid(f_pre)[..., None])[..., 0]
    causal = jnp.tril(jnp.ones((CHUNK, CHUNK), dtype=bool))

    def step(carry, inp):
        c_state, n_state, m_state = carry
        q_t, k_t, v_t, li, lf = inp
        b = jnp.cumsum(lf, axis=-1)
        d_mat = jnp.where(causal, b[..., :, None] - b[..., None, :] + li[..., None, :], -jnp.inf)
        inter = b + m_state[..., None]
        m_t = jnp.maximum(jnp.max(d_mat, axis=-1), inter)
        decay_in = jnp.exp(inter - m_t)
        s = jnp.einsum('bhtd,bhsd->bhts', q_t, k_t) * jnp.exp(d_mat - m_t[..., None])
        num = jnp.einsum('bhts,bhse->bhte', s, v_t) + decay_in[..., None] * jnp.einsum('bhtd,bhde->bhte', q_t, c_state)
        den = jnp.sum(s, axis=-1) + decay_in * jnp.einsum('bhtd,bhd->bht', q_t, n_state)
        h = num / jnp.maximum(jnp.abs(den), jnp.exp(-m_t))[..., None]
        b_last = b[..., -1]
        w_log = b_last[..., None] - b + li
        m_new = jnp.maximum(b_last + m_state, jnp.max(w_log, axis=-1))
        carry_decay = jnp.exp(b_last + m_state - m_new)
        w = jnp.exp(w_log - m_new[..., None])
        c_new = carry_decay[..., None, None] * c_state + jnp.einsum('bhs,bhsd,bhse->bhde', w, k_t, v_t)
        n_new = carry_decay[..., None] * n_state + jnp.einsum('bhs,bhsd->bhd', w, k_t)
        return (c_new, n_new, m_new), h

    init = (jnp.zeros((b_sz, n_h, d_k, d_v), jnp.float32),
            jnp.zeros((b_sz, n_h, d_k), jnp.float32),
            jnp.zeros((b_sz, n_h), jnp.float32))
    _, hc = lax.scan(step, init, (qc, kc, vc, log_i, log_f))
    return hc.transpose(1, 0, 3, 2, 4).reshape(b_sz, s_len, n_h, d_v)


def mla_attention(c_q, c_kv, k_rope, g_q_norm, w_uq, g_kv_norm, w_ukv, cos, sin, chunk_id):
    b_sz, s_len, _ = c_q.shape
    q = (rms_norm(c_q, g_q_norm) @ w_uq).reshape(b_sz, s_len, A_HEADS, A_NOPE_DIM + A_ROPE_DIM)
    q_nope = q[..., :A_NOPE_DIM]
    q_rope = apply_rope(q[..., A_NOPE_DIM:], cos[:, None, :], sin[:, None, :])
    kv = (rms_norm(c_kv, g_kv_norm) @ w_ukv).reshape(b_sz, s_len, A_HEADS, A_NOPE_DIM + A_V_DIM)
    k_nope = kv[..., :A_NOPE_DIM]
    v = kv[..., A_NOPE_DIM:]
    k_pe = apply_rope(k_rope, cos, sin)
    scale = (A_NOPE_DIM + A_ROPE_DIM) ** -0.5
    n_qb = s_len // Q_BLOCK

    def blocks(t):
        return t.reshape(b_sz, n_qb, Q_BLOCK, *t.shape[2:]).swapaxes(0, 1)

    def attend(args):
        qn, qr, q_cid = args
        s = jnp.einsum('bqhd,bkhd->bhqk', qn, k_nope) + jnp.einsum('bqhd,bkd->bhqk', qr, k_pe)
        s = s.astype(jnp.float32) * scale
        mask = chunk_id[None, :] <= q_cid[:, None]
        p = jax.nn.softmax(jnp.where(mask, s, -jnp.inf), axis=-1).astype(v.dtype)
        return jnp.einsum('bhqk,bkhd->bqhd', p, v)

    out = lax.map(attend, (blocks(q_nope), blocks(q_rope), chunk_id.reshape(n_qb, Q_BLOCK)))
    return out.swapaxes(0, 1).reshape(b_sz, s_len, A_WIDTH)


def token_mixer(h, w_in, conv_w, conv_b, b_igate, b_fgate, g_mlstm_out, g_q_norm, w_uq,
                g_kv_norm, w_ukv, w_out, cos, sin, chunk_id):
    b_sz, s_len, _ = h.shape
    proj = h @ w_in
    split_points = np.cumsum(IN_SIZES)[:-1].tolist()
    qk, v_m, o_m, i_m, f_m, c_q, c_kv, k_rope = jnp.split(proj, split_points, axis=-1)
    qk = jax.nn.silu(causal_depthwise_conv(qk, conv_w, conv_b))
    q_m, k_m = jnp.split(qk, 2, axis=-1)
    shp = (b_sz, s_len, M_HEADS)
    i_pre = softcap(i_m.astype(jnp.float32) + b_igate.astype(jnp.float32), GATE_SOFTCAP)
    f_pre = softcap(f_m.astype(jnp.float32) + b_fgate.astype(jnp.float32), GATE_SOFTCAP)
    h_m = mlstm_chunkwise(q_m.reshape(*shp, M_QK_DIM), k_m.reshape(*shp, M_QK_DIM),
                          v_m.reshape(*shp, M_V_DIM), i_pre, f_pre)
    h_m = rms_norm(h_m, g_mlstm_out.reshape(M_HEADS, M_V_DIM)) * jax.nn.sigmoid(o_m.astype(jnp.float32)).reshape(*shp, M_V_DIM)
    h_m = h_m.reshape(b_sz, s_len, M_WIDTH).astype(h.dtype)
    h_a = mla_attention(c_q, c_kv, k_rope, g_q_norm, w_uq, g_kv_norm, w_ukv, cos, sin, chunk_id)
    return jnp.concatenate([h_m, h_a], axis=-1) @ w_out


def swiglu(h, w_g, w_u, w_d):
    return (jax.nn.silu(h @ w_g) * (h @ w_u)) @ w_d


def route(h, w_router, b_router):
    n_tok = h.shape[0]
    scores = jax.nn.sigmoid(h.astype(jnp.float32) @ w_router.astype(jnp.float32))
    biased = scores + b_router.astype(jnp.float32)
    grp = biased.reshape(n_tok, N_GROUPS, N_EXPERTS // N_GROUPS)
    grp_score = jnp.sum(lax.top_k(grp, 2)[0], axis=-1)
    _, top_grp = lax.top_k(grp_score, TOPK_GROUPS)
    grp_mask = jnp.any(top_grp[..., None] == jnp.arange(N_GROUPS)[None, None, :], axis=1)
    expert_mask = jnp.repeat(grp_mask, N_EXPERTS // N_GROUPS, axis=1)
    _, idx = lax.top_k(jnp.where(expert_mask, biased, -jnp.inf), TOP_K)
    wts = jnp.take_along_axis(scores, idx, axis=-1)
    wts = wts / jnp.sum(wts, axis=-1, keepdims=True) * ROUTED_SCALE
    return idx, wts


def routed_experts(h, idx, wts, w_gate, w_up, w_down):
    n_tok, d = h.shape
    n_assign = n_tok * TOP_K
    n_slots = -(-(n_assign + N_EXPERTS * (EXPERT_BLOCK - 1)) // EXPERT_BLOCK) * EXPERT_BLOCK
    n_blocks = n_slots // EXPERT_BLOCK
    flat_e = idx.reshape(-1).astype(jnp.int32)
    flat_tok = jnp.repeat(jnp.arange(n_tok, dtype=jnp.int32), TOP_K)
    flat_w = wts.reshape(-1)
    order = jnp.argsort(flat_e)
    sorted_e = flat_e[order]
    counts = jnp.bincount(flat_e, length=N_EXPERTS).astype(jnp.int32)
    padded = (counts + EXPERT_BLOCK - 1) // EXPERT_BLOCK * EXPERT_BLOCK
    ends_pad = jnp.cumsum(padded)
    starts_pad = ends_pad - padded
    starts = jnp.cumsum(counts) - counts
    dest = starts_pad[sorted_e] + jnp.arange(n_assign, dtype=jnp.int32) - starts[sorted_e]
    slot_tok = jnp.zeros((n_slots,), jnp.int32).at[dest].set(flat_tok[order])
    slot_w = jnp.zeros((n_slots,), jnp.float32).at[dest].set(flat_w[order])
    block_start = jnp.arange(n_blocks, dtype=jnp.int32) * EXPERT_BLOCK
    block_e = jnp.minimum(jnp.searchsorted(ends_pad, block_start, side='right'), N_EXPERTS - 1)

    def expert_block(args):
        tok, e = args
        xb = h[tok]
        return (jax.nn.silu(xb @ w_gate[e]) * (xb @ w_up[e])) @ w_down[e]

    yb = lax.map(expert_block, (slot_tok.reshape(n_blocks, EXPERT_BLOCK), block_e))
    y = yb.reshape(n_slots, d) * slot_w[:, None].astype(yb.dtype)
    return jnp.zeros((n_tok, d), h.dtype).at[slot_tok].add(y.astype(h.dtype))


def moe_ffn(h, w_router, b_router, w_gate, w_up, w_down, w_shared_gate, w_shared_up, w_shared_down):
    idx, wts = route(h, w_router, b_router)
    return routed_experts(h, idx, wts, w_gate, w_up, w_down) + swiglu(h, w_shared_gate, w_shared_up, w_shared_down)


def setup_inputs(seed: int = 0) -> dict:
    key = jax.random.key(seed)
    ks = jax.random.split(key, 32)
    f32 = jnp.float32
    L = DEPTH

    def nrm(k, shape, scale):
        return jax.random.normal(k, shape, f32) * scale

    def gain(k, n):
        return 1.0 + 0.02 * jax.random.normal(k, (L, n), f32)

    return {
        'x': nrm(ks[0], (BATCH, SEQ, D_MODEL), 1.0),
        'c': nrm(ks[1], (BATCH, D_MODEL), 1.0),
        'w_ada': nrm(ks[2], (L, D_MODEL, 6 * D_MODEL), 0.5 * D_MODEL ** -0.5),
        'b_ada': nrm(ks[3], (L, 6 * D_MODEL), 0.02),
        'g_pre_mix': gain(ks[4], D_MODEL),
        'g_post_mix': gain(ks[5], D_MODEL),
        'w_in': nrm(ks[6], (L, D_MODEL, N_IN), D_MODEL ** -0.5),
        'conv_w': nrm(ks[7], (L, CONV_WIDTH, 2 * M_HEADS * M_QK_DIM), CONV_WIDTH ** -0.5),
        'conv_b': nrm(ks[8], (L, 2 * M_HEADS * M_QK_DIM), 0.02),
        'b_igate': nrm(ks[9], (L, M_HEADS), 0.5),
        'b_fgate': 3.0 + nrm(ks[10], (L, M_HEADS), 0.5),
        'g_mlstm_out': gain(ks[11], M_WIDTH),
        'g_q_norm': gain(ks[12], A_Q_RANK),
        'w_uq': nrm(ks[13], (L, A_Q_RANK, A_HEADS * (A_NOPE_DIM + A_ROPE_DIM)), A_Q_RANK ** -0.5),
        'g_kv_norm': gain(ks[14], A_KV_RANK),
        'w_ukv': nrm(ks[15], (L, A_KV_RANK, A_HEADS * (A_NOPE_DIM + A_V_DIM)), A_KV_RANK ** -0.5),
        'w_out': nrm(ks[16], (L, MIX_WIDTH, D_MODEL), MIX_WIDTH ** -0.5),
        'g_pre_ffn': gain(ks[17], D_MODEL),
        'g_post_ffn': gain(ks[18], D_MODEL),
        'w_router': nrm(ks[19], (L, D_MODEL, N_EXPERTS), D_MODEL ** -0.5),
        'b_router': nrm(ks[20], (L, N_EXPERTS), 0.01),
        'w_gate': nrm(ks[21], (L, N_EXPERTS, D_MODEL, EXPERT_FF), D_MODEL ** -0.5),
        'w_up': nrm(ks[22], (L, N_EXPERTS, D_MODEL, EXPERT_FF), D_MODEL ** -0.5),
        'w_down': nrm(ks[23], (L, N_EXPERTS, EXPERT_FF, D_MODEL), EXPERT_FF ** -0.5),
        'w_shared_gate': nrm(ks[24], (L, D_MODEL, SHARED_FF), D_MODEL ** -0.5),
        'w_shared_up': nrm(ks[25], (L, D_MODEL, SHARED_FF), D_MODEL ** -0.5),
        'w_shared_down': nrm(ks[26], (L, SHARED_FF, D_MODEL), SHARED_FF ** -0.5),
    }


def reference(x, c, w_ada, b_ada, g_pre_mix, g_post_mix, w_in, conv_w, conv_b, b_igate, b_fgate,
              g_mlstm_out, g_q_norm, w_uq, g_kv_norm, w_ukv, w_out, g_pre_ffn, g_post_ffn,
              w_router, b_router, w_gate, w_up, w_down, w_shared_gate, w_shared_up, w_shared_down):
    b_sz, s_len, d = x.shape
    cos, sin = rope_tables(s_len)
    chunk_id = jnp.arange(s_len, dtype=jnp.int32) // CHUNK
    c_act = jax.nn.silu(c)
    for l in range(DEPTH):
        mod = c_act @ w_ada[l] + b_ada[l]
        sh1, sc1, gt1, sh2, sc2, gt2 = [m[:, None, :] for m in jnp.split(mod, 6, axis=-1)]
        h = rms_norm(x, g_pre_mix[l]) * (1.0 + sc1) + sh1
        y = token_mixer(h, w_in[l], conv_w[l], conv_b[l], b_igate[l], b_fgate[l], g_mlstm_out[l],
                        g_q_norm[l], w_uq[l], g_kv_norm[l], w_ukv[l], w_out[l], cos, sin, chunk_id)
        x = x + gt1 * rms_norm(y, g_post_mix[l])
        h = rms_norm(x, g_pre_ffn[l]) * (1.0 + sc2) + sh2
        y = moe_ffn(h.reshape(b_sz * s_len, d), w_router[l], b_router[l], w_gate[l], w_up[l], w_down[l],
                    w_shared_gate[l], w_shared_up[l], w_shared_down[l]).reshape(b_sz, s_len, d)
        x = x + gt2 * rms_norm(y, g_post_ffn[l])
    return x
```

```python
import functools

import jax
import jax.numpy as jnp
import numpy as np
from jax import lax
from jax.experimental import pallas as pl
from jax.experimental.pallas import tpu as pltpu

F32 = jnp.float32
BF16 = jnp.bfloat16
I32 = jnp.int32
U32 = jnp.uint32

NORM_EPS = 1e-6
CHUNK = 64

M_HEADS = 4
M_QK_DIM = 256
M_V_DIM = 512
M_WIDTH = M_HEADS * M_V_DIM
M_QK_WIDTH = 2 * M_HEADS * M_QK_DIM
CONV_WIDTH = 4
GATE_SOFTCAP = 15.0

A_HEADS = 16
A_NOPE_DIM = 128
A_ROPE_DIM = 64
A_V_DIM = 128
A_Q_RANK = 768
A_KV_RANK = 512
A_HEAD_PAD = 256
ROPE_THETA = 10000.0

N_EXPERTS = 64
TOP_K = 8
N_GROUPS = 8
GROUP_SIZE = N_EXPERTS // N_GROUPS
TOPK_GROUPS = 4
ROUTED_SCALE = 2.5

LANES = 128
VMEM_LIMIT = 56 * 1024 * 1024
NEG_BIG = -1e30

COL_QK = 0
COL_V = COL_QK + M_QK_WIDTH
COL_O = COL_V + M_WIDTH
COL_CQ = COL_O + M_WIDTH
COL_CKV = COL_CQ + A_Q_RANK
COL_TAIL = COL_CKV + A_KV_RANK
N_IN_PAD = COL_TAIL + 2 * LANES
TAIL_GATE_LANE = A_ROPE_DIM


def _params(sem, vmem=VMEM_LIMIT):
    return pltpu.CompilerParams(dimension_semantics=sem, vmem_limit_bytes=vmem)


def _row_loop(n_rows, rc, body):
    def step(i, carry):
        body(pl.multiple_of(i * rc, rc))
        return carry
    lax.fori_loop(0, n_rows // rc, step, 0)


def _pack_pair(a, b):
    lo = lax.bitcast_convert_type(a.astype(BF16).astype(F32), U32) >> 16
    hi = lax.bitcast_convert_type(b.astype(BF16).astype(F32), U32) & jnp.uint32(0xFFFF0000)
    return lo | hi


def _unpack_pair(u):
    lo = lax.bitcast_convert_type(u << 16, F32)
    hi = lax.bitcast_convert_type(u & jnp.uint32(0xFFFF0000), F32)
    return lo, hi


def _adaln_kernel(c_ref, w_ref, b_ref, o_ref, *, rc):
    d, tn = w_ref.shape
    nl = tn // LANES

    def step(i, accs):
        r = pl.multiple_of(i * rc, rc)
        c = c_ref[pl.ds(r, rc), :]
        ca = c * jax.nn.sigmoid(c)
        out = []
        for j in range(nl):
            prod = w_ref[pl.ds(r, rc), j * LANES:(j + 1) * LANES] * ca
            out.append(accs[j] + jnp.sum(prod.reshape(rc // 8, 8, LANES), axis=0))
        return tuple(out)

    accs = lax.fori_loop(0, d // rc, step, tuple(jnp.zeros((8, LANES), F32) for _ in range(nl)))
    for j in range(nl):
        o_ref[:, j * LANES:(j + 1) * LANES] = (
            jnp.sum(accs[j], axis=0, keepdims=True) + b_ref[:, j * LANES:(j + 1) * LANES])


def _adaln(c, w_ada, b_ada):
    d, n = w_ada.shape
    tn = 512
    c_b = jnp.broadcast_to(c.reshape(d, 1), (d, LANES))
    return pl.pallas_call(
        functools.partial(_adaln_kernel, rc=64),
        grid=(n // tn,),
        in_specs=[pl.BlockSpec((d, LANES), lambda j: (0, 0)),
                  pl.BlockSpec((d, tn), lambda j: (0, j)),
                  pl.BlockSpec((1, tn), lambda j: (0, j))],
        out_specs=pl.BlockSpec((1, tn), lambda j: (0, j)),
        out_shape=jax.ShapeDtypeStruct((1, n), F32),
        compiler_params=_params(("arbitrary",)),
        name="adaln",
    )(c_b, w_ada, b_ada.reshape(1, n))


def _norm_mm_kernel(x_ref, gs_ref, sh_ref, w_ref, o_ref, *rest, rc, tail):
    if tail:
        t_ref, h_scr = rest
    else:
        (h_scr,) = rest
    tm = x_ref.shape[0]
    j = pl.program_id(1)

    @pl.when(j == 0)
    def _():
        def body(r):
            x = x_ref[pl.ds(r, rc), :].astype(F32)
            ms = jnp.mean(x * x, axis=-1, keepdims=True)
            y = x * lax.rsqrt(ms + NORM_EPS)
            h_scr[pl.ds(r, rc), :] = (y * gs_ref[...] + sh_ref[...]).astype(BF16)
        _row_loop(tm, rc, body)

    res = jnp.dot(h_scr[...], w_ref[...], preferred_element_type=F32)
    o_ref[...] = res.astype(o_ref.dtype)
    if tail:
        @pl.when(j == pl.num_programs(1) - 1)
        def _():
            t_ref[...] = res[:, res.shape[1] - tail:]


def _norm_mm(x, gs, sh, w, *, tm, tn, tail=0):
    m, k = x.shape
    n = w.shape[1]
    tm = min(tm, m)
    out_shape = [jax.ShapeDtypeStruct((m, n), BF16)]
    out_specs = [pl.BlockSpec((tm, tn), lambda i, j: (i, j))]
    if tail:
        out_shape.append(jax.ShapeDtypeStruct((m, tail), F32))
        out_specs.append(pl.BlockSpec((tm, tail), lambda i, j: (i, 0)))
    res = pl.pallas_call(
        functools.partial(_norm_mm_kernel, rc=32, tail=tail),
        grid=(m // tm, n // tn),
        in_specs=[pl.BlockSpec((tm, k), lambda i, j: (i, 0)),
                  pl.BlockSpec((1, k), lambda i, j: (0, 0)),
                  pl.BlockSpec((1, k), lambda i, j: (0, 0)),
                  pl.BlockSpec((k, tn), lambda i, j: (0, j))],
        out_specs=out_specs,
        out_shape=out_shape,
        scratch_shapes=[pltpu.VMEM((tm, k), BF16)],
        compiler_params=_params(("parallel", "arbitrary")),
        name="norm_mm",
    )(x, gs.reshape(1, k), sh.reshape(1, k), w)
    return res if tail else res[0]


def _mm2_kernel(a1_ref, a2_ref, w_ref, o_ref):
    k1 = a1_ref.shape[1]
    acc = jnp.dot(a1_ref[...], w_ref[:k1, :], preferred_element_type=F32)
    acc = acc + jnp.dot(a2_ref[...], w_ref[k1:, :], preferred_element_type=F32)
    o_ref[...] = acc.astype(o_ref.dtype)


def _mm2(a1, a2, w, *, tm, tn):
    m, k1 = a1.shape
    k2 = a2.shape[1]
    n = w.shape[1]
    tm = min(tm, m)
    return pl.pallas_call(
        _mm2_kernel,
        grid=(m // tm, n // tn),
        in_specs=[pl.BlockSpec((tm, k1), lambda i, j: (i, 0)),
                  pl.BlockSpec((tm, k2), lambda i, j: (i, 0)),
                  pl.BlockSpec((k1 + k2, tn), lambda i, j: (0, j))],
        out_specs=pl.BlockSpec((tm, tn), lambda i, j: (i, j)),
        out_shape=jax.ShapeDtypeStruct((m, n), BF16),
        compiler_params=_params(("parallel", "arbitrary")),
        name="out_proj",
    )(a1, a2, w)


def _softcap(z):
    return GATE_SOFTCAP * jnp.tanh(z * (1.0 / GATE_SOFTCAP))


def _log_sigmoid(z):
    return jnp.minimum(z, 0.0) - jnp.log1p(jnp.exp(-jnp.abs(z)))


def _mlstm_kernel(qk_ref, prev_ref, v_ref, o_ref, gcol_ref, grow_ref, cw_ref, cb_ref, bcol_ref, brow_ref,
                  gout_ref, out_ref, c_scr, n_scr, m_scr, u_scr):
    c = pl.program_id(0)
    L = qk_ref.shape[0]
    halo = prev_ref.shape[0]
    dk, dv = M_QK_DIM, M_V_DIM

    @pl.when(c == 0)
    def _():
        c_scr[...] = jnp.zeros_like(c_scr)
        n_scr[...] = jnp.zeros_like(n_scr)
        m_scr[...] = jnp.zeros_like(m_scr)

    prev = prev_ref[...].astype(F32)
    u_scr[0:halo, :] = jnp.where(c == 0, jnp.zeros_like(prev), prev)
    u_scr[halo:halo + L, :] = qk_ref[...].astype(F32)

    def conv_silu(col, width):
        acc = cb_ref[:, col:col + width]
        for j in range(CONV_WIDTH):
            r0 = halo - (CONV_WIDTH - 1) + j
            acc = acc + u_scr[r0:r0 + L, col:col + width] * cw_ref[j:j + 1, col:col + width]
        return acc * jax.nn.sigmoid(acc)

    pre_c = _softcap(gcol_ref[...] + bcol_ref[...])
    lf_c = _log_sigmoid(pre_c)
    pre_r = _softcap(grow_ref[0] + brow_ref[...])
    lf_r = _log_sigmoid(pre_r)
    row = lax.broadcasted_iota(I32, (L, L), 0)
    col = lax.broadcasted_iota(I32, (L, L), 1)
    causal = col <= row
    tril = causal.astype(F32)
    triu = (row <= col).astype(F32)
    bcum_c = jnp.dot(tril, lf_c, preferred_element_type=F32, precision=lax.Precision.HIGHEST)
    bcum_r = jnp.dot(lf_r, triu, preferred_element_type=F32, precision=lax.Precision.HIGHEST)

    for h in range(M_HEADS):
        li_lane = TAIL_GATE_LANE + h
        lf_lane = TAIL_GATE_LANE + M_HEADS + h
        q = (conv_silu(h * dk, dk) * (dk ** -0.5)).astype(BF16)
        kf = conv_silu(M_HEADS * dk + h * dk, dk)
        kb = kf.astype(BF16)
        v = v_ref[:, h * dv:(h + 1) * dv]
        b_c = bcum_c[:, lf_lane:lf_lane + 1]
        li_c = pre_c[:, li_lane:li_lane + 1]
        b_r = bcum_r[M_HEADS + h:M_HEADS + h + 1, :]
        li_r = pre_r[h:h + 1, :]
        m_prev = m_scr[h:h + 1, 0:1]

        dm = jnp.where(causal, b_c - b_r + li_r, NEG_BIG)
        inter = b_c + m_prev
        m_t = jnp.maximum(jnp.max(dm, axis=1, keepdims=True), inter)
        decay = jnp.exp(inter - m_t)
        s = lax.dot_general(q, kb, (((1,), (1,)), ((), ())), preferred_element_type=F32) * jnp.exp(dm - m_t)
        c_state = c_scr[h]
        n_state = n_scr[h:h + 1, :]
        num = jnp.dot(s.astype(BF16), v, preferred_element_type=F32)
        num = num + decay * jnp.dot(q, c_state.astype(BF16), preferred_element_type=F32)
        den = jnp.sum(s, axis=1, keepdims=True) + decay * jnp.sum(q.astype(F32) * n_state, axis=1, keepdims=True)
        hh = num / jnp.maximum(jnp.abs(den), jnp.exp(-m_t))

        b_last = b_c[L - 1:L, :]
        m_new = jnp.maximum(b_last + m_prev, jnp.max(b_last - b_r + li_r, axis=1, keepdims=True))
        carry = jnp.exp(b_last + m_prev - m_new)
        w_c = jnp.exp(b_last - b_c + li_c - m_new)
        kw = kf * w_c
        c_scr[h] = carry * c_state + lax.dot_general(kw.astype(BF16), v, (((0,), (0,)), ((), ())),
                                                     preferred_element_type=F32)
        n_scr[h:h + 1, :] = carry * n_state + jnp.sum(kw, axis=0, keepdims=True)
        m_scr[h:h + 1, :] = jnp.broadcast_to(m_new, (1, LANES))

        hn = hh * lax.rsqrt(jnp.mean(hh * hh, axis=1, keepdims=True) + NORM_EPS) * gout_ref[:, h * dv:(h + 1) * dv]
        gate = jax.nn.sigmoid(o_ref[:, h * dv:(h + 1) * dv].astype(F32))
        out_ref[:, h * dv:(h + 1) * dv] = (hn * gate).astype(out_ref.dtype)


def _mlstm(proj, tail, conv_w, conv_b, b_igate, b_fgate, g_out, *, chunk):
    s_len = proj.shape[0]
    L = chunk
    n_c = s_len // L
    halo = 16
    gates = tail[:, TAIL_GATE_LANE:TAIL_GATE_LANE + 2 * M_HEADS]
    g_row = gates.reshape(n_c, L, 2 * M_HEADS).transpose(0, 2, 1)
    bias = jnp.concatenate([b_igate, b_fgate]).astype(F32)
    b_col = jnp.zeros((1, LANES), F32).at[0, TAIL_GATE_LANE:TAIL_GATE_LANE + 2 * M_HEADS].set(bias)
    b_row = bias.reshape(2 * M_HEADS, 1)
    w = M_QK_WIDTH
    return pl.pallas_call(
        _mlstm_kernel,
        grid=(n_c,),
        in_specs=[pl.BlockSpec((L, w), lambda c: (c, COL_QK // w)),
                  pl.BlockSpec((halo, w), lambda c: (jnp.maximum(c * (L // halo) - 1, 0), COL_QK // w)),
                  pl.BlockSpec((L, M_WIDTH), lambda c: (c, COL_V // M_WIDTH)),
                  pl.BlockSpec((L, M_WIDTH), lambda c: (c, COL_O // M_WIDTH)),
                  pl.BlockSpec((L, LANES), lambda c: (c, 0)),
                  pl.BlockSpec((1, 2 * M_HEADS, L), lambda c: (c, 0, 0)),
                  pl.BlockSpec((CONV_WIDTH, w), lambda c: (0, 0)),
                  pl.BlockSpec((1, w), lambda c: (0, 0)),
                  pl.BlockSpec((1, LANES), lambda c: (0, 0)),
                  pl.BlockSpec((2 * M_HEADS, 1), lambda c: (0, 0)),
                  pl.BlockSpec((1, M_WIDTH), lambda c: (0, 0))],
        out_specs=pl.BlockSpec((L, M_WIDTH), lambda c: (c, 0)),
        out_shape=jax.ShapeDtypeStruct((s_len, M_WIDTH), BF16),
        scratch_shapes=[pltpu.VMEM((M_HEADS, M_QK_DIM, M_V_DIM), F32),
                        pltpu.VMEM((8, M_QK_DIM), F32),
                        pltpu.VMEM((8, LANES), F32),
                        pltpu.VMEM((halo + L, w), F32)],
        compiler_params=_params(("arbitrary",)),
        name="mlstm",
    )(proj, proj, proj, proj, tail, g_row, conv_w, conv_b.reshape(1, w), b_col, b_row, g_out.reshape(1, M_WIDTH))


def _rope_kernel(q_ref, kn_ref, tail_ref, cos_ref, sin_ref, qo_ref, ko_ref):
    cos = cos_ref[...]
    sin = sin_ref[...]
    nope_w = A_HEADS * A_NOPE_DIM
    kpe = (tail_ref[:, LANES:2 * LANES] * cos + tail_ref[:, 0:LANES] * sin).astype(BF16)
    for h in range(A_HEADS):
        lo = h * A_HEAD_PAD
        qo_ref[:, lo:lo + LANES] = q_ref[:, h * LANES:(h + 1) * LANES]
        qr = q_ref[:, nope_w + h * LANES:nope_w + (h + 1) * LANES].astype(F32)
        qs = q_ref[:, 2 * nope_w + h * LANES:2 * nope_w + (h + 1) * LANES].astype(F32)
        qo_ref[:, lo + LANES:lo + 2 * LANES] = (qr * cos + qs * sin).astype(BF16)
        ko_ref[:, lo:lo + LANES] = kn_ref[:, h * LANES:(h + 1) * LANES]
        ko_ref[:, lo + LANES:lo + 2 * LANES] = kpe


def _rope_assemble(q_raw, kv_raw, tail, cos_t, sin_t, *, tm):
    s_len = q_raw.shape[0]
    tm = min(tm, s_len)
    nope_w = A_HEADS * A_NOPE_DIM
    wide = A_HEADS * A_HEAD_PAD
    return pl.pallas_call(
        _rope_kernel,
        grid=(s_len // tm,),
        in_specs=[pl.BlockSpec((tm, 3 * nope_w), lambda i: (i, 0)),
                  pl.BlockSpec((tm, nope_w), lambda i: (i, 0)),
                  pl.BlockSpec((tm, 2 * LANES), lambda i: (i, 0)),
                  pl.BlockSpec((tm, LANES), lambda i: (i, 0)),
                  pl.BlockSpec((tm, LANES), lambda i: (i, 0))],
        out_specs=[pl.BlockSpec((tm, wide), lambda i: (i, 0)),
                   pl.BlockSpec((tm, wide), lambda i: (i, 0))],
        out_shape=[jax.ShapeDtypeStruct((s_len, wide), BF16),
                   jax.ShapeDtypeStruct((s_len, wide), BF16)],
        compiler_params=_params(("parallel",)),
        name="rope_assemble",
    )(q_raw, kv_raw, tail, cos_t, sin_t)


def _attn_kernel(qi_ref, kj_ref, q_ref, k_ref, v_ref, o_ref, m_scr, l_scr, acc_scr, *, chunk):
    p = pl.program_id(1)
    qi = qi_ref[p]
    kj = kj_ref[p]
    tq, tk = q_ref.shape[0], k_ref.shape[0]

    @pl.when(kj == 0)
    def _():
        m_scr[...] = jnp.full_like(m_scr, NEG_BIG)
        l_scr[...] = jnp.zeros_like(l_scr)
        acc_scr[...] = jnp.zeros_like(acc_scr)

    def update(masked):
        s = lax.dot_general(q_ref[...], k_ref[...], (((1,), (1,)), ((), ())), preferred_element_type=F32)
        if masked:
            rq = lax.broadcasted_iota(I32, (tq, tk), 0) // chunk
            ck = lax.broadcasted_iota(I32, (tq, tk), 1) // chunk
            s = jnp.where(ck <= rq, s, NEG_BIG)
        m_prev = m_scr[...]
        m_new = jnp.maximum(m_prev, jnp.max(s, axis=1, keepdims=True))
        alpha = jnp.exp(m_prev - m_new)
        pr = jnp.exp(s - m_new)
        l_scr[...] = alpha * l_scr[...] + jnp.sum(pr, axis=1, keepdims=True)
        acc_scr[...] = alpha * acc_scr[...] + jnp.dot(pr.astype(BF16), v_ref[...], preferred_element_type=F32)
        m_scr[...] = m_new

    @pl.when(kj < qi)
    def _():
        update(False)

    @pl.when(kj == qi)
    def _():
        update(True)
        o_ref[...] = (acc_scr[...] / l_scr[...]).astype(o_ref.dtype)


def _attention(q_cat, k_cat, kv_raw, *, tq, chunk):
    s_len = q_cat.shape[0]
    tq = min(tq, s_len)
    nq = s_len // tq
    pairs = [(i, j) for i in range(nq) for j in range(i + 1)]
    qi = jnp.asarray([p[0] for p in pairs], I32)
    kj = jnp.asarray([p[1] for p in pairs], I32)
    v_blk0 = (A_HEADS * A_NOPE_DIM) // A_V_DIM
    grid_spec = pltpu.PrefetchScalarGridSpec(
        num_scalar_prefetch=2,
        grid=(A_HEADS, len(pairs)),
        in_specs=[pl.BlockSpec((tq, A_HEAD_PAD), lambda h, p, qi, kj: (qi[p], h)),
                  pl.BlockSpec((tq, A_HEAD_PAD), lambda h, p, qi, kj: (kj[p], h)),
                  pl.BlockSpec((tq, A_V_DIM), lambda h, p, qi, kj: (kj[p], v_blk0 + h))],
        out_specs=pl.BlockSpec((tq, A_V_DIM), lambda h, p, qi, kj: (qi[p], h)),
        scratch_shapes=[pltpu.VMEM((tq, 1), F32), pltpu.VMEM((tq, 1), F32), pltpu.VMEM((tq, A_V_DIM), F32)],
    )
    return pl.pallas_call(
        functools.partial(_attn_kernel, chunk=chunk),
        grid_spec=grid_spec,
        out_shape=jax.ShapeDtypeStruct((s_len, A_HEADS * A_V_DIM), BF16),
        compiler_params=_params(("parallel", "arbitrary")),
        name="mla_attention",
    )(qi, kj, q_cat, k_cat, kv_raw)


def _route_t(logits, bias_col):
    n = logits.shape[1]
    scores = jax.nn.sigmoid(logits)
    biased = scores + bias_col
    sub = lax.broadcasted_iota(I32, (GROUP_SIZE, n), 0)
    rows = []
    for g in range(N_GROUPS):
        x = biased[g * GROUP_SIZE:(g + 1) * GROUP_SIZE, :]
        m1 = jnp.max(x, axis=0, keepdims=True)
        i1 = jnp.min(jnp.where(x == m1, sub, GROUP_SIZE), axis=0, keepdims=True)
        m2 = jnp.max(jnp.where(sub == i1, -jnp.inf, x), axis=0, keepdims=True)
        rows.append(m1 + m2)
    gscore = jnp.concatenate(rows, axis=0)
    gio = lax.broadcasted_iota(I32, (N_GROUPS, n), 0)
    grank = jnp.zeros((N_GROUPS, n), I32)
    for g in range(N_GROUPS):
        r = gscore[g:g + 1, :]
        grank = grank + jnp.where(gio > g, jnp.where(r >= gscore, 1, 0), jnp.where(r > gscore, 1, 0))
    gsel = grank < TOPK_GROUPS
    masked = jnp.concatenate(
        [jnp.where(gsel[g:g + 1, :], biased[g * GROUP_SIZE:(g + 1) * GROUP_SIZE, :], -jnp.inf)
         for g in range(N_GROUPS)], axis=0)
    eio = lax.broadcasted_iota(I32, (N_EXPERTS, n), 0)
    rank = jnp.zeros((N_EXPERTS, n), I32)
    for e in range(N_EXPERTS):
        r = masked[e:e + 1, :]
        rank = rank + jnp.where(eio > e, jnp.where(r >= masked, 1, 0), jnp.where(r > masked, 1, 0))
    sel = rank < TOP_K
    denom = jnp.sum(jnp.where(sel, scores, 0.0), axis=0, keepdims=True)
    wnorm = scores / denom * ROUTED_SCALE
    eio_f = eio.astype(F32)
    ids, wts = [], []
    for k in range(TOP_K):
        hit = rank == k
        ids.append(jnp.sum(jnp.where(hit, eio_f, 0.0), axis=0, keepdims=True))
        wts.append(jnp.sum(jnp.where(hit, wnorm, 0.0), axis=0, keepdims=True))
    return jnp.concatenate(ids, axis=0).astype(I32), jnp.concatenate(wts, axis=0)


def _mid_kernel(y_ref, x_ref, gt_ref, gpost_ref, gs_ref, sh_ref, wr_ref, br_ref,
                x1_ref, hp_ref, idx_ref, wts_ref, h_scr, *, rc):
    tm, d = x_ref.shape
    half = d // 2

    def body(r):
        y = y_ref[pl.ds(r, rc), :].astype(F32)
        yn = y * lax.rsqrt(jnp.mean(y * y, axis=-1, keepdims=True) + NORM_EPS) * gpost_ref[...]
        x1 = x_ref[pl.ds(r, rc), :] + gt_ref[...] * yn
        x1_ref[pl.ds(r, rc), :] = x1
        hn = x1 * lax.rsqrt(jnp.mean(x1 * x1, axis=-1, keepdims=True) + NORM_EPS)
        h = hn * gs_ref[...] + sh_ref[...]
        h_scr[pl.ds(r, rc), :] = h
        hp_ref[pl.ds(r, rc), :] = _pack_pair(h[:, :half], h[:, half:])
    _row_loop(tm, rc, body)

    logits = lax.dot_general(wr_ref[...], h_scr[...], (((1,), (1,)), ((), ())),
                             preferred_element_type=F32, precision=lax.Precision.HIGHEST)
    ids, wts = _route_t(logits, br_ref[...])
    idx_ref[...] = ids
    wts_ref[...] = wts


def _mid(y, x, gt1, g_post, gs2, sh2, w_router, b_router, *, tm):
    s_len, d = x.shape
    tm = min(tm, s_len)
    vec = lambda a: a.reshape(1, d).astype(F32)
    return pl.pallas_call(
        functools.partial(_mid_kernel, rc=16),
        grid=(s_len // tm,),
        in_specs=[pl.BlockSpec((tm, d), lambda i: (i, 0)),
                  pl.BlockSpec((tm, d), lambda i: (i, 0)),
                  pl.BlockSpec((1, d), lambda i: (0, 0)),
                  pl.BlockSpec((1, d), lambda i: (0, 0)),
                  pl.BlockSpec((1, d), lambda i: (0, 0)),
                  pl.BlockSpec((1, d), lambda i: (0, 0)),
                  pl.BlockSpec((N_EXPERTS, d), lambda i: (0, 0)),
                  pl.BlockSpec((N_EXPERTS, 1), lambda i: (0, 0))],
        out_specs=[pl.BlockSpec((tm, d), lambda i: (i, 0)),
                   pl.BlockSpec((tm, d // 2), lambda i: (i, 0)),
                   pl.BlockSpec((TOP_K, tm), lambda i: (0, i)),
                   pl.BlockSpec((TOP_K, tm), lambda i: (0, i))],
        out_shape=[jax.ShapeDtypeStruct((s_len, d), F32),
                   jax.ShapeDtypeStruct((s_len, d // 2), U32),
                   jax.ShapeDtypeStruct((TOP_K, s_len), I32),
                   jax.ShapeDtypeStruct((TOP_K, s_len), F32)],
        scratch_shapes=[pltpu.VMEM((tm, d), F32)],
        compiler_params=_params(("parallel",)),
        name="mid_norm_route",
    )(y, x, vec(gt1), vec(g_post), vec(gs2), vec(sh2), w_router.T.astype(F32), b_router.reshape(N_EXPERTS, 1))


def _row_copy(src_hbm, dst_vmem, sem, src_row, dst_row):
    return pltpu.make_async_copy(src_hbm.at[pl.ds(src_row, 1), :], dst_vmem.at[pl.ds(dst_row, 1), :], sem)


def _gather_kernel(tok_ref, h_hbm, o_ref, sem):
    g = o_ref.shape[0]
    base = pl.program_id(0) * g

    def issue(r, carry):
        _row_copy(h_hbm, o_ref, sem, tok_ref[base + r], r).start()
        return carry
    lax.fori_loop(0, g, issue, 0)

    def drain(r, carry):
        _row_copy(h_hbm, o_ref, sem, 0, r).wait()
        return carry
    lax.fori_loop(0, g, drain, 0)


def _gather_rows(slot_tok, hp, *, g):
    n_slots = slot_tok.shape[0]
    w = hp.shape[1]
    grid_spec = pltpu.PrefetchScalarGridSpec(
        num_scalar_prefetch=1,
        grid=(n_slots // g,),
        in_specs=[pl.BlockSpec(memory_space=pl.ANY)],
        out_specs=pl.BlockSpec((g, w), lambda i, tok: (i, 0)),
        scratch_shapes=[pltpu.SemaphoreType.DMA(())],
    )
    return pl.pallas_call(
        _gather_kernel,
        grid_spec=grid_spec,
        out_shape=jax.ShapeDtypeStruct((n_slots, w), hp.dtype),
        compiler_params=_params(("arbitrary",)),
        name="moe_gather",
    )(slot_tok, hp)


def _expert_kernel(be_ref, nb_ref, x_ref, wg_ref, wu_ref, wd_ref, o_ref):
    b = pl.program_id(0)

    @pl.when(b < nb_ref[0])
    def _():
        lo, hi = _unpack_pair(x_ref[...])
        x = jnp.concatenate([lo, hi], axis=1).astype(BF16)
        g = jnp.dot(x, wg_ref[0], preferred_element_type=F32)
        u = jnp.dot(x, wu_ref[0], preferred_element_type=F32)
        a = (g * jax.nn.sigmoid(g) * u).astype(BF16)
        y = jnp.dot(a, wd_ref[0], preferred_element_type=F32)
        half = y.shape[1] // 2
        o_ref[...] = _pack_pair(y[:, :half], y[:, half:])

    @pl.when(b >= nb_ref[0])
    def _():
        o_ref[...] = jnp.zeros_like(o_ref)


def _experts(block_e, n_used, xs, wg, wu, wd, *, tmb):
    n_slots, wp = xs.shape
    d, ff = wg.shape[1], wg.shape[2]
    nb = n_slots // tmb
    clamp = lambda b, nbr: jnp.minimum(b, nbr[0] - 1)
    grid_spec = pltpu.PrefetchScalarGridSpec(
        num_scalar_prefetch=2,
        grid=(nb,),
        in_specs=[pl.BlockSpec((tmb, wp), lambda b, be, nbr: (clamp(b, nbr), 0)),
                  pl.BlockSpec((1, d, ff), lambda b, be, nbr: (be[b], 0, 0)),
                  pl.BlockSpec((1, d, ff), lambda b, be, nbr: (be[b], 0, 0)),
                  pl.BlockSpec((1, ff, d), lambda b, be, nbr: (be[b], 0, 0))],
        out_specs=pl.BlockSpec((tmb, wp), lambda b, be, nbr: (b, 0)),
    )
    return pl.pallas_call(
        _expert_kernel,
        grid_spec=grid_spec,
        out_shape=jax.ShapeDtypeStruct((n_slots, wp), U32),
        compiler_params=_params(("arbitrary",)),
        name="moe_experts",
    )(block_e, n_used, xs, wg, wu, wd)


def _combine_kernel(slot_ref, w_ref, x1_ref, gt_ref, g_ref, ys_hbm, o_ref, buf, sem, *, rc, n_k):
    tc, d = x1_ref.shape
    half = d // 2
    base = pl.program_id(0) * tc * n_k

    def issue(t, carry):
        for k in range(n_k):
            _row_copy(ys_hbm, buf.at[k], sem, slot_ref[base + t * n_k + k], t).start()
        return carry
    lax.fori_loop(0, tc, issue, 0)

    def drain(t, carry):
        for k in range(n_k):
            _row_copy(ys_hbm, buf.at[k], sem, 0, t).wait()
        return carry
    lax.fori_loop(0, tc, drain, 0)

    def body(r):
        w = w_ref[pl.ds(r, rc), :]
        lo = jnp.zeros((rc, half), F32)
        hi = jnp.zeros((rc, half), F32)
        for k in range(n_k):
            a, b = _unpack_pair(buf[k, pl.ds(r, rc), :])
            lo = lo + w[:, k:k + 1] * a
            hi = hi + w[:, k:k + 1] * b
        ms = (jnp.sum(lo * lo, axis=-1, keepdims=True) + jnp.sum(hi * hi, axis=-1, keepdims=True)) * (1.0 / d)
        rs = lax.rsqrt(ms + NORM_EPS)
        o_ref[pl.ds(r, rc), 0:half] = (x1_ref[pl.ds(r, rc), 0:half]
                                       + gt_ref[:, 0:half] * (lo * rs * g_ref[:, 0:half]))
        o_ref[pl.ds(r, rc), half:d] = (x1_ref[pl.ds(r, rc), half:d]
                                       + gt_ref[:, half:d] * (hi * rs * g_ref[:, half:d]))
    _row_loop(tc, rc, body)


def _combine(slots, w_tok, x1, gt2, g_post, ys, *, tc):
    s_len, d = x1.shape
    n_k = slots.shape[0] // s_len
    tc = min(tc, s_len)
    grid_spec = pltpu.PrefetchScalarGridSpec(
        num_scalar_prefetch=1,
        grid=(s_len // tc,),
        in_specs=[pl.BlockSpec((tc, w_tok.shape[1]), lambda i, sl: (i, 0)),
                  pl.BlockSpec((tc, d), lambda i, sl: (i, 0)),
                  pl.BlockSpec((1, d), lambda i, sl: (0, 0)),
                  pl.BlockSpec((1, d), lambda i, sl: (0, 0)),
                  pl.BlockSpec(memory_space=pl.ANY)],
        out_specs=pl.BlockSpec((tc, d), lambda i, sl: (i, 0)),
        scratch_shapes=[pltpu.VMEM((n_k, tc, d // 2), U32), pltpu.SemaphoreType.DMA(())],
    )
    return pl.pallas_call(
        functools.partial(_combine_kernel, rc=8, n_k=n_k),
        grid_spec=grid_spec,
        out_shape=jax.ShapeDtypeStruct((s_len, d), F32),
        compiler_params=_params(("arbitrary",)),
        name="moe_combine",
    )(slots, w_tok, x1, gt2.reshape(1, d), g_post.reshape(1, d), ys)


def _moe_plan(idx_t, wts_t, *, tmb):
    n_k, n_tok = idx_t.shape
    n_e = N_EXPERTS + 1
    eid = jnp.concatenate([idx_t, jnp.full((1, n_tok), N_EXPERTS, I32)], axis=0)
    wts = jnp.concatenate([wts_t, jnp.ones((1, n_tok), F32)], axis=0)
    sel = (eid[:, None, :] == jnp.arange(n_e, dtype=I32)[None, :, None]).any(axis=0).astype(I32)
    csum = jnp.cumsum(sel, axis=1)
    counts = csum[:, -1]
    padded = (counts + tmb - 1) // tmb * tmb
    ends = jnp.cumsum(padded)
    starts = ends - padded
    slot_dense = starts[:, None] + csum - 1
    slot = jnp.take_along_axis(slot_dense, eid, axis=0)
    n_slots = -(-(n_tok * (n_k + 1) + n_e * (tmb - 1)) // tmb) * tmb
    tok = jnp.broadcast_to(jnp.arange(n_tok, dtype=I32)[None, :], slot.shape)
    slot_tok = jnp.zeros((n_slots,), I32).at[slot.reshape(-1)].set(tok.reshape(-1))
    n_used = (ends[-1] // tmb).astype(I32).reshape(1)
    block_start = jnp.arange(n_slots // tmb, dtype=I32) * tmb
    block_e = jnp.minimum(jnp.searchsorted(ends, block_start, side='right'), n_e - 1).astype(I32)
    slots_tok_major = slot.T.reshape(-1)
    w_tok = jnp.zeros((n_tok, 16), F32).at[:, :n_k + 1].set(wts.T)
    return slot_tok, block_e, n_used, slots_tok_major, w_tok


def _in_proj_weight(w_in):
    sizes = (M_QK_WIDTH, M_WIDTH, M_WIDTH, M_HEADS, M_HEADS, A_Q_RANK, A_KV_RANK, A_ROPE_DIM)
    offs = np.cumsum((0,) + sizes)
    qk, v, o, ig, fg, cq, ckv, kr = [w_in[:, offs[i]:offs[i + 1]] for i in range(8)]
    d = w_in.shape[0]
    half = A_ROPE_DIM // 2
    kswap = jnp.concatenate([-kr[:, half:], kr[:, :half]], axis=1)
    z = lambda n: jnp.zeros((d, n), w_in.dtype)
    tail0 = jnp.concatenate([kswap, ig, fg, z(LANES - A_ROPE_DIM - 2 * M_HEADS)], axis=1)
    tail1 = jnp.concatenate([kr, z(LANES - A_ROPE_DIM)], axis=1)
    return jnp.concatenate([qk, v, o, cq, ckv, tail0, tail1], axis=1).astype(BF16)


def _q_up_weight(w_uq):
    r = w_uq.shape[0]
    w = w_uq.reshape(r, A_HEADS, A_NOPE_DIM + A_ROPE_DIM)
    nope = w[:, :, :A_NOPE_DIM]
    rope = w[:, :, A_NOPE_DIM:]
    half = A_ROPE_DIM // 2
    swap = jnp.concatenate([-rope[:, :, half:], rope[:, :, :half]], axis=2)
    pad = jnp.zeros((r, A_HEADS, LANES - A_ROPE_DIM), w_uq.dtype)
    rope_p = jnp.concatenate([rope, pad], axis=2)
    swap_p = jnp.concatenate([swap, pad], axis=2)
    flat = lambda a: a.reshape(r, -1)
    return jnp.concatenate([flat(nope), flat(rope_p), flat(swap_p)], axis=1).astype(BF16)


def _kv_up_weight(w_ukv):
    r = w_ukv.shape[0]
    w = w_ukv.reshape(r, A_HEADS, A_NOPE_DIM + A_V_DIM)
    return jnp.concatenate([w[:, :, :A_NOPE_DIM].reshape(r, -1), w[:, :, A_NOPE_DIM:].reshape(r, -1)],
                           axis=1).astype(BF16)


def _rope_tables(s_len):
    pos = jnp.arange(s_len, dtype=F32)
    inv_freq = 1.0 / (ROPE_THETA ** (jnp.arange(0, A_ROPE_DIM, 2, dtype=F32) / A_ROPE_DIM))
    ang = pos[:, None] * inv_freq[None, :]
    pad = jnp.zeros((s_len, LANES - A_ROPE_DIM), F32)
    cos_t = jnp.concatenate([jnp.cos(ang), jnp.cos(ang), pad], axis=1)
    sin_t = jnp.concatenate([jnp.sin(ang), jnp.sin(ang), pad], axis=1)
    return cos_t, sin_t


def _block(x, c, w_ada, b_ada, g_pre_mix, g_post_mix, w_in, conv_w, conv_b, b_igate, b_fgate, g_mlstm_out,
           g_q_norm, w_uq, g_kv_norm, w_ukv, w_out, g_pre_ffn, g_post_ffn, w_router, b_router,
           w_gate, w_up, w_down, w_shared_gate, w_shared_up, w_shared_down):
    s_len, d = x.shape
    mod = _adaln(c, w_ada, b_ada)[0]
    sh1, sc1, gt1, sh2, sc2, gt2 = [mod[i * d:(i + 1) * d] for i in range(6)]

    proj, tail = _norm_mm(x, g_pre_mix * (1.0 + sc1), sh1, _in_proj_weight(w_in), tm=512, tn=768, tail=2 * LANES)
    h_m = _mlstm(proj, tail, conv_w, conv_b, b_igate, b_fgate, g_mlstm_out, chunk=CHUNK)
    scale = (A_NOPE_DIM + A_ROPE_DIM) ** -0.5
    zq = jnp.zeros((A_Q_RANK,), F32)
    zkv = jnp.zeros((A_KV_RANK,), F32)
    q_raw = _norm_mm(proj[:, COL_CQ:COL_CQ + A_Q_RANK], g_q_norm * scale, zq, _q_up_weight(w_uq), tm=1024, tn=1536)
    kv_raw = _norm_mm(proj[:, COL_CKV:COL_CKV + A_KV_RANK], g_kv_norm, zkv, _kv_up_weight(w_ukv), tm=1024, tn=1024)
    cos_t, sin_t = _rope_tables(s_len)
    q_cat, k_cat = _rope_assemble(q_raw, kv_raw, tail, cos_t, sin_t, tm=256)
    h_a = _attention(q_cat, k_cat, kv_raw, tq=512, chunk=CHUNK)
    y = _mm2(h_m, h_a, w_out.astype(BF16), tm=1024, tn=1024)

    x1, hp, idx_t, wts_t = _mid(y, x, gt1, g_post_mix, g_pre_ffn * (1.0 + sc2), sh2, w_router, b_router, tm=256)
    tmb = 256
    slot_tok, block_e, n_used, slots, w_tok = _moe_plan(idx_t, wts_t, tmb=tmb)
    xs = _gather_rows(slot_tok, hp, g=tmb)
    wg = jnp.concatenate([w_gate.astype(BF16), w_shared_gate.astype(BF16)[None]], axis=0)
    wu = jnp.concatenate([w_up.astype(BF16), w_shared_up.astype(BF16)[None]], axis=0)
    wd = jnp.concatenate([w_down.astype(BF16), w_shared_down.astype(BF16)[None]], axis=0)
    ys = _experts(block_e, n_used, xs, wg, wu, wd, tmb=tmb)
    return _combine(slots, w_tok, x1, gt2, g_post_ffn, ys, tc=128)


def kernel(x, c, w_ada, b_ada, g_pre_mix, g_post_mix, w_in, conv_w, conv_b, b_igate, b_fgate, g_mlstm_out,
           g_q_norm, w_uq, g_kv_norm, w_ukv, w_out, g_pre_ffn, g_post_ffn, w_router, b_router,
           w_gate, w_up, w_down, w_shared_gate, w_shared_up, w_shared_down):
    assert x.shape[0] == 1 and w_ada.shape[0] == 1, "single sequence, single layer"
    layer = (w_ada, b_ada, g_pre_mix, g_post_mix, w_in, conv_w, conv_b, b_igate, b_fgate, g_mlstm_out,
             g_q_norm, w_uq, g_kv_norm, w_ukv, w_out, g_pre_ffn, g_post_ffn, w_router, b_router,
             w_gate, w_up, w_down, w_shared_gate, w_shared_up, w_shared_down)
    out = _block(x[0], c[0], *[p[0] for p in layer])
    return out[None]
```

```python
import functools

import jax
import jax.numpy as jnp
from jax import lax
from jax.experimental import pallas as pl
from jax.experimental.pallas import tpu as pltpu

F32 = jnp.float32
BF16 = jnp.bfloat16
I32 = jnp.int32
U32 = jnp.uint32

NORM_EPS = 1e-6
CHUNK = 64

M_HEADS = 4
M_QK_DIM = 256
M_V_DIM = 512
M_WIDTH = M_HEADS * M_V_DIM
M_QK_WIDTH = 2 * M_HEADS * M_QK_DIM
CONV_WIDTH = 4
GATE_SOFTCAP = 15.0

A_HEADS = 16
A_NOPE_DIM = 128
A_ROPE_DIM = 64
A_V_DIM = 128
A_Q_RANK = 768
A_KV_RANK = 512
A_HEAD_PAD = 256
ROPE_THETA = 10000.0

N_EXPERTS = 64
TOP_K = 8
N_GROUPS = 8
GROUP_SIZE = N_EXPERTS // N_GROUPS
TOPK_GROUPS = 4
ROUTED_SCALE = 2.5

LANES = 128
VMEM_LIMIT = 56 * 1024 * 1024
NEG_BIG = -1e30

COL_QK = 0
COL_V = COL_QK + M_QK_WIDTH
COL_O = COL_V + M_WIDTH
COL_IG = COL_O + M_WIDTH
COL_FG = COL_IG + M_HEADS
COL_CQ = COL_FG + M_HEADS
COL_CKV = COL_CQ + A_Q_RANK
COL_KR = COL_CKV + A_KV_RANK
N_IN = COL_KR + A_ROPE_DIM
IN_TILE = 768
N_IN_PAD = -(-N_IN // IN_TILE) * IN_TILE
TAIL_W = 2 * LANES
TAIL_GATE_LANE = A_ROPE_DIM


def _params(sem, vmem=VMEM_LIMIT):
    return pltpu.CompilerParams(dimension_semantics=sem, vmem_limit_bytes=vmem)


def _row_loop(n_rows, rc, body):
    def step(i, carry):
        body(pl.multiple_of(i * rc, rc))
        return carry
    lax.fori_loop(0, n_rows // rc, step, 0)


def _pack_pair(a, b):
    lo = lax.bitcast_convert_type(a.astype(BF16).astype(F32), U32) >> 16
    hi = lax.bitcast_convert_type(b.astype(BF16).astype(F32), U32) & jnp.uint32(0xFFFF0000)
    return lo | hi


def _unpack_pair(u):
    lo = lax.bitcast_convert_type(u << 16, F32)
    hi = lax.bitcast_convert_type(u & jnp.uint32(0xFFFF0000), F32)
    return lo, hi


def _adaln_kernel(c_ref, w_ref, b_ref, o_ref, *, rc):
    d, tn = w_ref.shape
    nl = tn // LANES

    def step(i, accs):
        r = pl.multiple_of(i * rc, rc)
        c = c_ref[pl.ds(r, rc), :]
        ca = c * jax.nn.sigmoid(c)
        out = []
        for j in range(nl):
            prod = w_ref[pl.ds(r, rc), j * LANES:(j + 1) * LANES] * ca
            out.append(accs[j] + jnp.sum(prod.reshape(rc // 8, 8, LANES), axis=0))
        return tuple(out)

    accs = lax.fori_loop(0, d // rc, step, tuple(jnp.zeros((8, LANES), F32) for _ in range(nl)))
    for j in range(nl):
        o_ref[:, j * LANES:(j + 1) * LANES] = (
            jnp.sum(accs[j], axis=0, keepdims=True) + b_ref[:, j * LANES:(j + 1) * LANES])


def _adaln(c, w_ada, b_ada):
    d, n = w_ada.shape
    tn = 512
    c_b = jnp.broadcast_to(c.reshape(d, 1), (d, LANES))
    return pl.pallas_call(
        functools.partial(_adaln_kernel, rc=64),
        grid=(n // tn,),
        in_specs=[pl.BlockSpec((d, LANES), lambda j: (0, 0)),
                  pl.BlockSpec((d, tn), lambda j: (0, j)),
                  pl.BlockSpec((1, tn), lambda j: (0, j))],
        out_specs=pl.BlockSpec((1, tn), lambda j: (0, j)),
        out_shape=jax.ShapeDtypeStruct((1, n), F32),
        compiler_params=_params(("arbitrary",)),
        name="adaln",
    )(c_b, w_ada, b_ada.reshape(1, n))


def _norm_mm_kernel(x_ref, gs_ref, sh_ref, w_ref, *rest, rc, has_tail):
    if has_tail:
        wt_ref, o_ref, t_ref, h_scr = rest
    else:
        o_ref, h_scr = rest
    tm = x_ref.shape[0]
    j = pl.program_id(1)

    @pl.when(j == 0)
    def _():
        def body(r):
            x = x_ref[pl.ds(r, rc), :].astype(F32)
            ms = jnp.mean(x * x, axis=-1, keepdims=True)
            y = x * lax.rsqrt(ms + NORM_EPS)
            h_scr[pl.ds(r, rc), :] = (y * gs_ref[...] + sh_ref[...]).astype(BF16)
        _row_loop(tm, rc, body)

    o_ref[...] = jnp.dot(h_scr[...], w_ref[...], preferred_element_type=F32).astype(o_ref.dtype)
    if has_tail:
        @pl.when(j == pl.num_programs(1) - 1)
        def _():
            t_ref[...] = jnp.dot(h_scr[...], wt_ref[...], preferred_element_type=F32)


def _norm_mm(x, gs, sh, w, w_tail=None, *, tm, tn):
    m, k = x.shape
    n = w.shape[1]
    tm = min(tm, m)
    in_specs = [pl.BlockSpec((tm, k), lambda i, j: (i, 0)),
                pl.BlockSpec((1, k), lambda i, j: (0, 0)),
                pl.BlockSpec((1, k), lambda i, j: (0, 0)),
                pl.BlockSpec((k, tn), lambda i, j: (0, j))]
    operands = [x, gs.reshape(1, k), sh.reshape(1, k), w]
    out_shape = [jax.ShapeDtypeStruct((m, n), BF16)]
    out_specs = [pl.BlockSpec((tm, tn), lambda i, j: (i, j))]
    if w_tail is not None:
        nt = w_tail.shape[1]
        in_specs.append(pl.BlockSpec((k, nt), lambda i, j: (0, 0)))
        operands.append(w_tail)
        out_shape.append(jax.ShapeDtypeStruct((m, nt), F32))
        out_specs.append(pl.BlockSpec((tm, nt), lambda i, j: (i, 0)))
    res = pl.pallas_call(
        functools.partial(_norm_mm_kernel, rc=32, has_tail=w_tail is not None),
        grid=(m // tm, n // tn),
        in_specs=in_specs,
        out_specs=out_specs,
        out_shape=out_shape,
        scratch_shapes=[pltpu.VMEM((tm, k), BF16)],
        compiler_params=_params(("parallel", "arbitrary")),
        name="norm_mm",
    )(*operands)
    return res if w_tail is not None else res[0]


def _mm2_kernel(a1_ref, a2_ref, w_ref, o_ref):
    k1 = a1_ref.shape[1]
    acc = jnp.dot(a1_ref[...], w_ref[:k1, :], preferred_element_type=F32)
    acc = acc + jnp.dot(a2_ref[...], w_ref[k1:, :], preferred_element_type=F32)
    o_ref[...] = acc.astype(o_ref.dtype)


def _mm2(a1, a2, w, *, tm, tn):
    m, k1 = a1.shape
    k2 = a2.shape[1]
    n = w.shape[1]
    tm = min(tm, m)
    return pl.pallas_call(
        _mm2_kernel,
        grid=(m // tm, n // tn),
        in_specs=[pl.BlockSpec((tm, k1), lambda i, j: (i, 0)),
                  pl.BlockSpec((tm, k2), lambda i, j: (i, 0)),
                  pl.BlockSpec((k1 + k2, tn), lambda i, j: (0, j))],
        out_specs=pl.BlockSpec((tm, tn), lambda i, j: (i, j)),
        out_shape=jax.ShapeDtypeStruct((m, n), BF16),
        compiler_params=_params(("parallel", "arbitrary")),
        name="out_proj",
    )(a1, a2, w)


def _softcap(z):
    return GATE_SOFTCAP * jnp.tanh(z * (1.0 / GATE_SOFTCAP))


def _log_sigmoid(z):
    return jnp.minimum(z, 0.0) - jnp.log1p(jnp.exp(-jnp.abs(z)))


def _mlstm_kernel(qk_ref, prev_ref, v_ref, o_ref, gcol_ref, grow_ref, cw_ref, cb_ref, bcol_ref, brow_ref,
                  gout_ref, out_ref, c_scr, n_scr, m_scr, u_scr):
    c = pl.program_id(0)
    L = qk_ref.shape[0]
    halo = prev_ref.shape[0]
    dk, dv = M_QK_DIM, M_V_DIM

    @pl.when(c == 0)
    def _():
        c_scr[...] = jnp.zeros_like(c_scr)
        n_scr[...] = jnp.zeros_like(n_scr)
        m_scr[...] = jnp.zeros_like(m_scr)

    prev = prev_ref[...].astype(F32)
    u_scr[0:halo, :] = jnp.where(c == 0, jnp.zeros_like(prev), prev)
    u_scr[halo:halo + L, :] = qk_ref[...].astype(F32)

    def conv_silu(col, width):
        acc = cb_ref[:, col:col + width]
        for j in range(CONV_WIDTH):
            r0 = halo - (CONV_WIDTH - 1) + j
            acc = acc + u_scr[r0:r0 + L, col:col + width] * cw_ref[j:j + 1, col:col + width]
        return acc * jax.nn.sigmoid(acc)

    pre_c = _softcap(gcol_ref[...] + bcol_ref[...])
    lf_c = _log_sigmoid(pre_c)
    pre_r = _softcap(grow_ref[0] + brow_ref[...])
    lf_r = _log_sigmoid(pre_r)
    row = lax.broadcasted_iota(I32, (L, L), 0)
    col = lax.broadcasted_iota(I32, (L, L), 1)
    causal = col <= row
    tril = causal.astype(F32)
    triu = (row <= col).astype(F32)
    bcum_c = jnp.dot(tril, lf_c, preferred_element_type=F32, precision=lax.Precision.HIGHEST)
    bcum_r = jnp.dot(lf_r, triu, preferred_element_type=F32, precision=lax.Precision.HIGHEST)

    for h in range(M_HEADS):
        li_lane = TAIL_GATE_LANE + h
        lf_lane = TAIL_GATE_LANE + M_HEADS + h
        q = (conv_silu(h * dk, dk) * (dk ** -0.5)).astype(BF16)
        kf = conv_silu(M_HEADS * dk + h * dk, dk)
        kb = kf.astype(BF16)
        v = v_ref[:, h * dv:(h + 1) * dv]
        b_c = bcum_c[:, lf_lane:lf_lane + 1]
        li_c = pre_c[:, li_lane:li_lane + 1]
        b_r = bcum_r[M_HEADS + h:M_HEADS + h + 1, :]
        li_r = pre_r[h:h + 1, :]
        m_prev = m_scr[h:h + 1, 0:1]

        dm = jnp.where(causal, b_c - b_r + li_r, NEG_BIG)
        inter = b_c + m_prev
        m_t = jnp.maximum(jnp.max(dm, axis=1, keepdims=True), inter)
        decay = jnp.exp(inter - m_t)
        s = lax.dot_general(q, kb, (((1,), (1,)), ((), ())), preferred_element_type=F32) * jnp.exp(dm - m_t)
        c_state = c_scr[h]
        n_state = n_scr[h:h + 1, :]
        num = jnp.dot(s.astype(BF16), v, preferred_element_type=F32)
        num = num + decay * jnp.dot(q, c_state.astype(BF16), preferred_element_type=F32)
        den = jnp.sum(s, axis=1, keepdims=True) + decay * jnp.sum(q.astype(F32) * n_state, axis=1, keepdims=True)
        hh = num / jnp.maximum(jnp.abs(den), jnp.exp(-m_t))

        b_last = b_c[L - 1:L, :]
        m_new = jnp.maximum(b_last + m_prev, jnp.max(b_last - b_r + li_r, axis=1, keepdims=True))
        carry = jnp.exp(b_last + m_prev - m_new)
        w_c = jnp.exp(b_last - b_c + li_c - m_new)
        kw = kf * w_c
        c_scr[h] = carry * c_state + lax.dot_general(kw.astype(BF16), v, (((0,), (0,)), ((), ())),
                                                     preferred_element_type=F32)
        n_scr[h:h + 1, :] = carry * n_state + jnp.sum(kw, axis=0, keepdims=True)
        m_scr[h:h + 1, :] = jnp.broadcast_to(m_new, (1, LANES))

        hn = hh * lax.rsqrt(jnp.mean(hh * hh, axis=1, keepdims=True) + NORM_EPS) * gout_ref[:, h * dv:(h + 1) * dv]
        gate = jax.nn.sigmoid(o_ref[:, h * dv:(h + 1) * dv].astype(F32))
        out_ref[:, h * dv:(h + 1) * dv] = (hn * gate).astype(out_ref.dtype)


def _mlstm(proj, tail, conv_w, conv_b, b_igate, b_fgate, g_out, *, chunk):
    s_len = proj.shape[0]
    L = chunk
    n_c = s_len // L
    halo = 16
    gates = tail[:, TAIL_GATE_LANE:TAIL_GATE_LANE + 2 * M_HEADS]
    g_row = gates.reshape(n_c, L, 2 * M_HEADS).transpose(0, 2, 1)
    bias = jnp.concatenate([b_igate, b_fgate]).astype(F32)
    b_col = jnp.zeros((1, LANES), F32).at[0, TAIL_GATE_LANE:TAIL_GATE_LANE + 2 * M_HEADS].set(bias)
    b_row = bias.reshape(2 * M_HEADS, 1)
    w = M_QK_WIDTH
    return pl.pallas_call(
        _mlstm_kernel,
        grid=(n_c,),
        in_specs=[pl.BlockSpec((L, w), lambda c: (c, COL_QK // w)),
                  pl.BlockSpec((halo, w), lambda c: (jnp.maximum(c * (L // halo) - 1, 0), COL_QK // w)),
                  pl.BlockSpec((L, M_WIDTH), lambda c: (c, COL_V // M_WIDTH)),
                  pl.BlockSpec((L, M_WIDTH), lambda c: (c, COL_O // M_WIDTH)),
                  pl.BlockSpec((L, LANES), lambda c: (c, 0)),
                  pl.BlockSpec((1, 2 * M_HEADS, L), lambda c: (c, 0, 0)),
                  pl.BlockSpec((CONV_WIDTH, w), lambda c: (0, 0)),
                  pl.BlockSpec((1, w), lambda c: (0, 0)),
                  pl.BlockSpec((1, LANES), lambda c: (0, 0)),
                  pl.BlockSpec((2 * M_HEADS, 1), lambda c: (0, 0)),
                  pl.BlockSpec((1, M_WIDTH), lambda c: (0, 0))],
        out_specs=pl.BlockSpec((L, M_WIDTH), lambda c: (c, 0)),
        out_shape=jax.ShapeDtypeStruct((s_len, M_WIDTH), BF16),
        scratch_shapes=[pltpu.VMEM((M_HEADS, M_QK_DIM, M_V_DIM), F32),
                        pltpu.VMEM((8, M_QK_DIM), F32),
                        pltpu.VMEM((8, LANES), F32),
                        pltpu.VMEM((halo + L, w), F32)],
        compiler_params=_params(("arbitrary",)),
        name="mlstm",
    )(proj, proj, proj, proj, tail, g_row, conv_w, conv_b.reshape(1, w), b_col, b_row, g_out.reshape(1, M_WIDTH))


def _rope_kernel(q_ref, kn_ref, v_ref, tail_ref, cos_ref, sin_ref, qo_ref, ko_ref, vo_ref):
    cos = cos_ref[...]
    sin = sin_ref[...]
    nope_w = A_HEADS * A_NOPE_DIM
    kpe = (tail_ref[:, LANES:2 * LANES] * cos + tail_ref[:, 0:LANES] * sin).astype(BF16)
    lane = lax.broadcasted_iota(I32, (q_ref.shape[0], LANES), 1)
    ones_col = jnp.where(lane == 0, 1.0, 0.0).astype(BF16)
    for h in range(A_HEADS):
        vo_ref[:, h * A_HEAD_PAD:h * A_HEAD_PAD + LANES] = v_ref[:, h * LANES:(h + 1) * LANES]
        vo_ref[:, h * A_HEAD_PAD + LANES:(h + 1) * A_HEAD_PAD] = ones_col
        lo = h * A_HEAD_PAD
        qo_ref[:, lo:lo + LANES] = q_ref[:, h * LANES:(h + 1) * LANES]
        qr = q_ref[:, nope_w + h * LANES:nope_w + (h + 1) * LANES].astype(F32)
        qs = q_ref[:, 2 * nope_w + h * LANES:2 * nope_w + (h + 1) * LANES].astype(F32)
        qo_ref[:, lo + LANES:lo + 2 * LANES] = (qr * cos + qs * sin).astype(BF16)
        ko_ref[:, lo:lo + LANES] = kn_ref[:, h * LANES:(h + 1) * LANES]
        ko_ref[:, lo + LANES:lo + 2 * LANES] = kpe


def _rope_assemble(q_raw, kv_raw, tail, cos_t, sin_t, *, tm):
    s_len = q_raw.shape[0]
    tm = min(tm, s_len)
    nope_w = A_HEADS * A_NOPE_DIM
    wide = A_HEADS * A_HEAD_PAD
    return pl.pallas_call(
        _rope_kernel,
        grid=(s_len // tm,),
        in_specs=[pl.BlockSpec((tm, 3 * nope_w), lambda i: (i, 0)),
                  pl.BlockSpec((tm, nope_w), lambda i: (i, 0)),
                  pl.BlockSpec((tm, nope_w), lambda i: (i, 1)),
                  pl.BlockSpec((tm, 2 * LANES), lambda i: (i, 0)),
                  pl.BlockSpec((tm, LANES), lambda i: (i, 0)),
                  pl.BlockSpec((tm, LANES), lambda i: (i, 0))],
        out_specs=[pl.BlockSpec((tm, wide), lambda i: (i, 0))] * 3,
        out_shape=[jax.ShapeDtypeStruct((s_len, wide), BF16)] * 3,
        compiler_params=_params(("parallel",)),
        name="rope_assemble",
    )(q_raw, kv_raw, kv_raw, tail, cos_t, sin_t)


def _attn_kernel(q_ref, k_ref, v_ref, o_ref, m_scr, acc_scr, s_scr, *, chunk):
    qi = pl.program_id(1)
    tq = q_ref.shape[0]
    m_scr[...] = jnp.full_like(m_scr, NEG_BIG)
    acc_scr[...] = jnp.zeros_like(acc_scr)

    def scores(slot, blk):
        start = pl.multiple_of(blk * tq, tq)
        s_scr[slot] = lax.dot_general(q_ref[...], k_ref[pl.ds(start, tq), :], (((1,), (1,)), ((), ())),
                                      preferred_element_type=F32)

    def consume(slot, blk, masked):
        start = pl.multiple_of(blk * tq, tq)
        s = s_scr[slot]
        if masked:
            rq = lax.broadcasted_iota(I32, (tq, tq), 0) // chunk
            ck = lax.broadcasted_iota(I32, (tq, tq), 1) // chunk
            s = jnp.where(ck <= rq, s, NEG_BIG)
        m_prev = m_scr[...]
        m_new = jnp.maximum(m_prev, jnp.max(s, axis=1, keepdims=True))
        pr = jnp.exp(s - m_new).astype(BF16)
        acc_scr[...] = (jnp.exp(m_prev - m_new) * acc_scr[...]
                        + jnp.dot(pr, v_ref[pl.ds(start, tq), :], preferred_element_type=F32))
        m_scr[...] = m_new

    scores(0, 0)

    def pair(t, carry):
        scores(1, 2 * t + 1)
        consume(0, 2 * t, False)
        scores(0, 2 * t + 2)
        consume(1, 2 * t + 1, False)
        return carry
    lax.fori_loop(0, qi // 2, pair, 0)

    @pl.when(qi % 2 == 1)
    def _():
        scores(1, qi)
        consume(0, qi - 1, False)
        consume(1, qi, True)

    @pl.when(qi % 2 == 0)
    def _():
        consume(0, qi, True)

    acc = acc_scr[...]
    o_ref[...] = (acc[:, :A_V_DIM] / acc[:, A_V_DIM:A_V_DIM + 1]).astype(o_ref.dtype)


def _attention(q_cat, k_cat, v_cat, *, tq, chunk):
    s_len = q_cat.shape[0]
    tq = min(tq, s_len)
    return pl.pallas_call(
        functools.partial(_attn_kernel, chunk=chunk),
        grid=(A_HEADS, s_len // tq),
        in_specs=[pl.BlockSpec((tq, A_HEAD_PAD), lambda h, i: (i, h)),
                  pl.BlockSpec((s_len, A_HEAD_PAD), lambda h, i: (0, h)),
                  pl.BlockSpec((s_len, A_HEAD_PAD), lambda h, i: (0, h))],
        out_specs=pl.BlockSpec((tq, A_V_DIM), lambda h, i: (i, h)),
        out_shape=jax.ShapeDtypeStruct((s_len, A_HEADS * A_V_DIM), BF16),
        scratch_shapes=[pltpu.VMEM((tq, 1), F32), pltpu.VMEM((tq, A_HEAD_PAD), F32),
                        pltpu.VMEM((2, tq, tq), F32)],
        compiler_params=_params(("parallel", "arbitrary")),
        name="mla_attention",
    )(q_cat, k_cat, v_cat)


def _route_t(logits, bias_col):
    n = logits.shape[1]
    scores = jax.nn.sigmoid(logits)
    biased = scores + bias_col
    sub = lax.broadcasted_iota(I32, (GROUP_SIZE, n), 0)
    rows = []
    for g in range(N_GROUPS):
        x = biased[g * GROUP_SIZE:(g + 1) * GROUP_SIZE, :]
        m1 = jnp.max(x, axis=0, keepdims=True)
        i1 = jnp.min(jnp.where(x == m1, sub, GROUP_SIZE), axis=0, keepdims=True)
        m2 = jnp.max(jnp.where(sub == i1, -jnp.inf, x), axis=0, keepdims=True)
        rows.append(m1 + m2)
    gscore = jnp.concatenate(rows, axis=0)
    gio = lax.broadcasted_iota(I32, (N_GROUPS, n), 0)
    grank = jnp.zeros((N_GROUPS, n), I32)
    for g in range(N_GROUPS):
        r = gscore[g:g + 1, :]
        grank = grank + jnp.where(gio > g, jnp.where(r >= gscore, 1, 0), jnp.where(r > gscore, 1, 0))
    gsel = grank < TOPK_GROUPS
    masked = jnp.concatenate(
        [jnp.where(gsel[g:g + 1, :], biased[g * GROUP_SIZE:(g + 1) * GROUP_SIZE, :], -jnp.inf)
         for g in range(N_GROUPS)], axis=0)
    eio = lax.broadcasted_iota(I32, (N_EXPERTS, n), 0)
    rank = jnp.zeros((N_EXPERTS, n), I32)
    for e in range(N_EXPERTS):
        r = masked[e:e + 1, :]
        rank = rank + jnp.where(eio > e, jnp.where(r >= masked, 1, 0), jnp.where(r > masked, 1, 0))
    sel = rank < TOP_K
    denom = jnp.sum(jnp.where(sel, scores, 0.0), axis=0, keepdims=True)
    wnorm = scores / denom * ROUTED_SCALE
    eio_f = eio.astype(F32)
    ids, wts = [], []
    for k in range(TOP_K):
        hit = rank == k
        ids.append(jnp.sum(jnp.where(hit, eio_f, 0.0), axis=0, keepdims=True))
        wts.append(jnp.sum(jnp.where(hit, wnorm, 0.0), axis=0, keepdims=True))
    return jnp.concatenate(ids, axis=0).astype(I32), jnp.concatenate(wts, axis=0)


def _mid_kernel(y_ref, x_ref, gt_ref, gpost_ref, gs_ref, sh_ref, wr_ref, br_ref,
                x1_ref, hp_ref, idx_ref, wts_ref, h_scr, *, rc):
    tm, d = x_ref.shape
    half = d // 2

    def body(r):
        y = y_ref[pl.ds(r, rc), :].astype(F32)
        yn = y * lax.rsqrt(jnp.mean(y * y, axis=-1, keepdims=True) + NORM_EPS) * gpost_ref[...]
        x1 = x_ref[pl.ds(r, rc), :] + gt_ref[...] * yn
        x1_ref[pl.ds(r, rc), :] = x1
        hn = x1 * lax.rsqrt(jnp.mean(x1 * x1, axis=-1, keepdims=True) + NORM_EPS)
        h = hn * gs_ref[...] + sh_ref[...]
        h_scr[pl.ds(r, rc), :] = h
        hp_ref[pl.ds(r, rc), :] = _pack_pair(h[:, :half], h[:, half:])
    _row_loop(tm, rc, body)

    logits = lax.dot_general(wr_ref[...], h_scr[...], (((1,), (1,)), ((), ())),
                             preferred_element_type=F32, precision=lax.Precision.HIGHEST)
    ids, wts = _route_t(logits, br_ref[...])
    idx_ref[...] = ids
    wts_ref[...] = wts


def _mid(y, x, gt1, g_post, gs2, sh2, w_router, b_router, *, tm):
    s_len, d = x.shape
    tm = min(tm, s_len)
    vec = lambda a: a.reshape(1, d).astype(F32)
    return pl.pallas_call(
        functools.partial(_mid_kernel, rc=16),
        grid=(s_len // tm,),
        in_specs=[pl.BlockSpec((tm, d), lambda i: (i, 0)),
                  pl.BlockSpec((tm, d), lambda i: (i, 0)),
                  pl.BlockSpec((1, d), lambda i: (0, 0)),
                  pl.BlockSpec((1, d), lambda i: (0, 0)),
                  pl.BlockSpec((1, d), lambda i: (0, 0)),
                  pl.BlockSpec((1, d), lambda i: (0, 0)),
                  pl.BlockSpec((N_EXPERTS, d), lambda i: (0, 0)),
                  pl.BlockSpec((N_EXPERTS, 1), lambda i: (0, 0))],
        out_specs=[pl.BlockSpec((tm, d), lambda i: (i, 0)),
                   pl.BlockSpec((tm, d // 2), lambda i: (i, 0)),
                   pl.BlockSpec((TOP_K, tm), lambda i: (0, i)),
                   pl.BlockSpec((TOP_K, tm), lambda i: (0, i))],
        out_shape=[jax.ShapeDtypeStruct((s_len, d), F32),
                   jax.ShapeDtypeStruct((s_len, d // 2), U32),
                   jax.ShapeDtypeStruct((TOP_K, s_len), I32),
                   jax.ShapeDtypeStruct((TOP_K, s_len), F32)],
        scratch_shapes=[pltpu.VMEM((tm, d), F32)],
        compiler_params=_params(("parallel",)),
        name="mid_norm_route",
    )(y, x, vec(gt1), vec(g_post), vec(gs2), vec(sh2), w_router.T.astype(F32), b_router.reshape(N_EXPERTS, 1))


def _row_copy(src_hbm, dst_vmem, sem, src_row, dst_row):
    return pltpu.make_async_copy(src_hbm.at[pl.ds(src_row, 1), :], dst_vmem.at[pl.ds(dst_row, 1), :], sem)


def _gather_kernel(tok_ref, h_hbm, o_ref, sem):
    g = o_ref.shape[0]
    base = pl.program_id(0) * g

    n_tiles = g // 8

    def issue(i, carry):
        r = lax.rem(i, n_tiles) * 8 + lax.div(i, n_tiles)
        _row_copy(h_hbm, o_ref, sem, tok_ref[base + r], r).start()
        return carry
    lax.fori_loop(0, g, issue, 0)

    def drain(r, carry):
        _row_copy(h_hbm, o_ref, sem, 0, r).wait()
        return carry
    lax.fori_loop(0, g, drain, 0)


def _gather_rows(slot_tok, hp, *, g):
    n_slots = slot_tok.shape[0]
    w = hp.shape[1]
    grid_spec = pltpu.PrefetchScalarGridSpec(
        num_scalar_prefetch=1,
        grid=(n_slots // g,),
        in_specs=[pl.BlockSpec(memory_space=pl.ANY)],
        out_specs=pl.BlockSpec((g, w), lambda i, tok: (i, 0)),
        scratch_shapes=[pltpu.SemaphoreType.DMA(())],
    )
    return pl.pallas_call(
        _gather_kernel,
        grid_spec=grid_spec,
        out_shape=jax.ShapeDtypeStruct((n_slots, w), hp.dtype),
        compiler_params=_params(("arbitrary",)),
        name="moe_gather",
    )(slot_tok, hp)


def _expert_kernel(be_ref, nb_ref, nxt_ref, x_ref, wg_hbm, wu_hbm, wd_hbm, sg_hbm, su_hbm, sd_hbm, o_ref,
                   stage_g, stage_u, stage_d, wb_g, wb_u, wb_d, sems, *, rc):
    b = pl.program_id(0)
    e = be_ref[b]
    stages = (stage_g, stage_u, stage_d)
    routed = (wg_hbm, wu_hbm, wd_hbm)
    shared = (sg_hbm, su_hbm, sd_hbm)

    def fetch(ex):
        @pl.when(ex < N_EXPERTS)
        def _():
            for i in range(3):
                pltpu.make_async_copy(routed[i].at[ex], stages[i], sems.at[i]).start()

        @pl.when(ex == N_EXPERTS)
        def _():
            for i in range(3):
                pltpu.make_async_copy(shared[i], stages[i], sems.at[i]).start()

    @pl.when(b < nb_ref[0])
    def _():
        @pl.when(b == 0)
        def _():
            fetch(e)

        is_first = jnp.logical_or(b == 0, be_ref[jnp.maximum(b - 1, 0)] != e)

        @pl.when(is_first)
        def _():
            for i in range(3):
                pltpu.make_async_copy(shared[i], stages[i], sems.at[i]).wait()
            for src, dst in ((stage_g, wb_g), (stage_u, wb_u), (stage_d, wb_d)):
                def cast(r, src=src, dst=dst):
                    dst[pl.ds(r, rc), :] = src[pl.ds(r, rc), :].astype(BF16)
                _row_loop(src.shape[0], rc, cast)
            nxt = nxt_ref[e]

            @pl.when(nxt >= 0)
            def _():
                fetch(nxt)

        lo, hi = _unpack_pair(x_ref[...])
        x = jnp.concatenate([lo, hi], axis=1).astype(BF16)
        g = jnp.dot(x, wb_g[...], preferred_element_type=F32)
        u = jnp.dot(x, wb_u[...], preferred_element_type=F32)
        a = (g * jax.nn.sigmoid(g) * u).astype(BF16)
        y = jnp.dot(a, wb_d[...], preferred_element_type=F32)
        half = y.shape[1] // 2
        o_ref[...] = _pack_pair(y[:, :half], y[:, half:])

    @pl.when(b >= nb_ref[0])
    def _():
        o_ref[...] = jnp.zeros_like(o_ref)


def _experts(block_e, n_used, next_e, xs, wg, wu, wd, sg, su, sd, *, tmb):
    n_slots, wp = xs.shape
    d, ff = wg.shape[1], wg.shape[2]
    nb = n_slots // tmb
    any_spec = pl.BlockSpec(memory_space=pl.ANY)
    grid_spec = pltpu.PrefetchScalarGridSpec(
        num_scalar_prefetch=3,
        grid=(nb,),
        in_specs=[pl.BlockSpec((tmb, wp), lambda b, be, nbr, nx: (jnp.minimum(b, nbr[0] - 1), 0))] + [any_spec] * 6,
        out_specs=pl.BlockSpec((tmb, wp), lambda b, be, nbr, nx: (b, 0)),
        scratch_shapes=[pltpu.VMEM((d, ff), F32), pltpu.VMEM((d, ff), F32), pltpu.VMEM((ff, d), F32),
                        pltpu.VMEM((d, ff), BF16), pltpu.VMEM((d, ff), BF16), pltpu.VMEM((ff, d), BF16),
                        pltpu.SemaphoreType.DMA((3,))],
    )
    return pl.pallas_call(
        functools.partial(_expert_kernel, rc=128),
        grid_spec=grid_spec,
        out_shape=jax.ShapeDtypeStruct((n_slots, wp), U32),
        compiler_params=_params(("arbitrary",), vmem=60 * 1024 * 1024),
        name="moe_experts",
    )(block_e, n_used, next_e, xs, wg, wu, wd, sg, su, sd)


def _combine_kernel(slot_ref, w_ref, x1_ref, gt_ref, g_ref, ys_hbm, o_ref, buf, sems, *, rc, n_k):
    tc, d = x1_ref.shape
    half = d // 2
    i = pl.program_id(0)
    cur = lax.rem(i, 2)

    def issue(tile, slot):
        base = tile * (tc * n_k)

        def step(t, carry):
            for k in range(n_k):
                _row_copy(ys_hbm, buf.at[slot, k], sems.at[slot], slot_ref[base + t * n_k + k], t).start()
            return carry
        lax.fori_loop(0, tc, step, 0)

    @pl.when(i == 0)
    def _():
        issue(0, 0)

    @pl.when(i + 1 < pl.num_programs(0))
    def _():
        issue(i + 1, 1 - cur)

    def drain(t, carry):
        for k in range(n_k):
            _row_copy(ys_hbm, buf.at[cur, k], sems.at[cur], 0, t).wait()
        return carry
    lax.fori_loop(0, tc, drain, 0)

    def body(r):
        w = w_ref[pl.ds(r, rc), :]
        lo = jnp.zeros((rc, half), F32)
        hi = jnp.zeros((rc, half), F32)
        for k in range(n_k):
            a, b = _unpack_pair(buf[cur, k, pl.ds(r, rc), :])
            lo = lo + w[:, k:k + 1] * a
            hi = hi + w[:, k:k + 1] * b
        ms = (jnp.sum(lo * lo, axis=-1, keepdims=True) + jnp.sum(hi * hi, axis=-1, keepdims=True)) * (1.0 / d)
        rs = lax.rsqrt(ms + NORM_EPS)
        o_ref[pl.ds(r, rc), 0:half] = (x1_ref[pl.ds(r, rc), 0:half]
                                       + gt_ref[:, 0:half] * (lo * rs * g_ref[:, 0:half]))
        o_ref[pl.ds(r, rc), half:d] = (x1_ref[pl.ds(r, rc), half:d]
                                       + gt_ref[:, half:d] * (hi * rs * g_ref[:, half:d]))
    _row_loop(tc, rc, body)


def _combine(slots, w_tok, x1, gt2, g_post, ys, *, tc):
    s_len, d = x1.shape
    n_k = slots.shape[0] // s_len
    tc = min(tc, s_len)
    grid_spec = pltpu.PrefetchScalarGridSpec(
        num_scalar_prefetch=1,
        grid=(s_len // tc,),
        in_specs=[pl.BlockSpec((tc, w_tok.shape[1]), lambda i, sl: (i, 0)),
                  pl.BlockSpec((tc, d), lambda i, sl: (i, 0)),
                  pl.BlockSpec((1, d), lambda i, sl: (0, 0)),
                  pl.BlockSpec((1, d), lambda i, sl: (0, 0)),
                  pl.BlockSpec(memory_space=pl.ANY)],
        out_specs=pl.BlockSpec((tc, d), lambda i, sl: (i, 0)),
        scratch_shapes=[pltpu.VMEM((2, n_k, tc, d // 2), U32), pltpu.SemaphoreType.DMA((2,))],
    )
    return pl.pallas_call(
        functools.partial(_combine_kernel, rc=8, n_k=n_k),
        grid_spec=grid_spec,
        out_shape=jax.ShapeDtypeStruct((s_len, d), F32),
        compiler_params=_params(("arbitrary",)),
        name="moe_combine",
    )(slots, w_tok, x1, gt2.reshape(1, d), g_post.reshape(1, d), ys)


def _moe_plan(idx_t, wts_t, *, tmb):
    n_k, n_tok = idx_t.shape
    n_e = N_EXPERTS + 1
    eid = jnp.concatenate([idx_t, jnp.full((1, n_tok), N_EXPERTS, I32)], axis=0)
    wts = jnp.concatenate([wts_t, jnp.ones((1, n_tok), F32)], axis=0)
    sel = (eid[:, None, :] == jnp.arange(n_e, dtype=I32)[None, :, None]).any(axis=0).astype(I32)
    csum = jnp.cumsum(sel, axis=1)
    counts = csum[:, -1]
    padded = (counts + tmb - 1) // tmb * tmb
    ends = jnp.cumsum(padded)
    starts = ends - padded
    slot_dense = starts[:, None] + csum - 1
    slot = jnp.take_along_axis(slot_dense, eid, axis=0)
    n_slots = -(-(n_tok * (n_k + 1) + n_e * (tmb - 1)) // tmb) * tmb
    tok = jnp.broadcast_to(jnp.arange(n_tok, dtype=I32)[None, :], slot.shape)
    slot_tok = jnp.zeros((n_slots,), I32).at[slot.reshape(-1)].set(tok.reshape(-1))
    n_used = (ends[-1] // tmb).astype(I32).reshape(1)
    block_start = jnp.arange(n_slots // tmb, dtype=I32) * tmb
    block_e = jnp.minimum(jnp.searchsorted(ends, block_start, side='right'), n_e - 1).astype(I32)
    slots_tok_major = slot.T.reshape(-1)
    w_tok = jnp.zeros((n_tok, 16), F32).at[:, :n_k + 1].set(wts.T)
    owner = jnp.where(padded > 0, jnp.arange(n_e, dtype=I32), n_e)
    later = jnp.concatenate([lax.cummin(owner[::-1])[::-1][1:], jnp.full((1,), n_e, I32)])
    next_e = jnp.where(later >= n_e, -1, later).astype(I32)
    return slot_tok, block_e, n_used, next_e, slots_tok_major, w_tok


def _in_proj_weights(w_in):
    d = w_in.shape[0]
    w_main = jnp.pad(w_in.astype(BF16), ((0, 0), (0, N_IN_PAD - N_IN)))
    kr = w_in[:, COL_KR:COL_KR + A_ROPE_DIM]
    gates = w_in[:, COL_IG:COL_IG + 2 * M_HEADS]
    half = A_ROPE_DIM // 2
    kswap = jnp.concatenate([-kr[:, half:], kr[:, :half]], axis=1)
    z = lambda n: jnp.zeros((d, n), w_in.dtype)
    w_tail = jnp.concatenate([kswap, gates, z(LANES - A_ROPE_DIM - 2 * M_HEADS), kr, z(LANES - A_ROPE_DIM)], axis=1)
    return w_main, w_tail.astype(BF16)


def _q_up_weight(w_uq):
    r = w_uq.shape[0]
    w = w_uq.reshape(r, A_HEADS, A_NOPE_DIM + A_ROPE_DIM)
    nope = w[:, :, :A_NOPE_DIM]
    rope = w[:, :, A_NOPE_DIM:]
    half = A_ROPE_DIM // 2
    swap = jnp.concatenate([-rope[:, :, half:], rope[:, :, :half]], axis=2)
    pad = jnp.zeros((r, A_HEADS, LANES - A_ROPE_DIM), w_uq.dtype)
    rope_p = jnp.concatenate([rope, pad], axis=2)
    swap_p = jnp.concatenate([swap, pad], axis=2)
    flat = lambda a: a.reshape(r, -1)
    return jnp.concatenate([flat(nope), flat(rope_p), flat(swap_p)], axis=1).astype(BF16)


def _kv_up_weight(w_ukv):
    r = w_ukv.shape[0]
    w = w_ukv.reshape(r, A_HEADS, A_NOPE_DIM + A_V_DIM)
    return jnp.concatenate([w[:, :, :A_NOPE_DIM].reshape(r, -1), w[:, :, A_NOPE_DIM:].reshape(r, -1)],
                           axis=1).astype(BF16)


def _rope_tables(s_len):
    pos = jnp.arange(s_len, dtype=F32)
    inv_freq = 1.0 / (ROPE_THETA ** (jnp.arange(0, A_ROPE_DIM, 2, dtype=F32) / A_ROPE_DIM))
    ang = pos[:, None] * inv_freq[None, :]
    pad = jnp.zeros((s_len, LANES - A_ROPE_DIM), F32)
    cos_t = jnp.concatenate([jnp.cos(ang), jnp.cos(ang), pad], axis=1)
    sin_t = jnp.concatenate([jnp.sin(ang), jnp.sin(ang), pad], axis=1)
    return cos_t, sin_t


def _block(x, c, w_ada, b_ada, g_pre_mix, g_post_mix, w_in, conv_w, conv_b, b_igate, b_fgate, g_mlstm_out,
           g_q_norm, w_uq, g_kv_norm, w_ukv, w_out, g_pre_ffn, g_post_ffn, w_router, b_router,
           w_gate, w_up, w_down, w_shared_gate, w_shared_up, w_shared_down):
    s_len, d = x.shape
    mod = _adaln(c, w_ada, b_ada)[0]
    sh1, sc1, gt1, sh2, sc2, gt2 = [mod[i * d:(i + 1) * d] for i in range(6)]

    w_main, w_tail = _in_proj_weights(w_in)
    proj, tail = _norm_mm(x, g_pre_mix * (1.0 + sc1), sh1, w_main, w_tail, tm=512, tn=IN_TILE)
    h_m = _mlstm(proj, tail, conv_w, conv_b, b_igate, b_fgate, g_mlstm_out, chunk=CHUNK)
    scale = (A_NOPE_DIM + A_ROPE_DIM) ** -0.5
    zq = jnp.zeros((A_Q_RANK,), F32)
    zkv = jnp.zeros((A_KV_RANK,), F32)
    q_raw = _norm_mm(proj[:, COL_CQ:COL_CQ + A_Q_RANK], g_q_norm * scale, zq, _q_up_weight(w_uq), tm=1024, tn=1536)
    kv_raw = _norm_mm(proj[:, COL_CKV:COL_CKV + A_KV_RANK], g_kv_norm, zkv, _kv_up_weight(w_ukv), tm=1024, tn=1024)
    cos_t, sin_t = _rope_tables(s_len)
    q_cat, k_cat, v_cat = _rope_assemble(q_raw, kv_raw, tail, cos_t, sin_t, tm=256)
    h_a = _attention(q_cat, k_cat, v_cat, tq=1024, chunk=CHUNK)
    y = _mm2(h_m, h_a, w_out.astype(BF16), tm=1024, tn=1024)

    x1, hp, idx_t, wts_t = _mid(y, x, gt1, g_post_mix, g_pre_ffn * (1.0 + sc2), sh2, w_router, b_router, tm=256)
    tmb = 256
    slot_tok, block_e, n_used, next_e, slots, w_tok = _moe_plan(idx_t, wts_t, tmb=tmb)
    xs = _gather_rows(slot_tok, hp, g=tmb)
    ys = _experts(block_e, n_used, next_e, xs, w_gate, w_up, w_down,
                  w_shared_gate, w_shared_up, w_shared_down, tmb=tmb)
    return _combine(slots, w_tok, x1, gt2, g_post_ffn, ys, tc=128)


def kernel(x, c, w_ada, b_ada, g_pre_mix, g_post_mix, w_in, conv_w, conv_b, b_igate, b_fgate, g_mlstm_out,
           g_q_norm, w_uq, g_kv_norm, w_ukv, w_out, g_pre_ffn, g_post_ffn, w_router, b_router,
           w_gate, w_up, w_down, w_shared_gate, w_shared_up, w_shared_down):
    assert x.shape[0] == 1 and w_ada.shape[0] == 1, "single sequence, single layer"
    layer = (w_ada, b_ada, g_pre_mix, g_post_mix, w_in, conv_w, conv_b, b_igate, b_fgate, g_mlstm_out,
             g_q_norm, w_uq, g_kv_norm, w_ukv, w_out, g_pre_ffn, g_post_ffn, w_router, b_router,
             w_gate, w_up, w_down, w_shared_gate, w_shared_up, w_shared_down)
    out = _block(x[0], c[0], *[p[0] for p in layer])
    return out[None]
```

```python
import functools

import jax
import jax.numpy as jnp
from jax import lax
from jax.experimental import pallas as pl
from jax.experimental.pallas import tpu as pltpu

F32 = jnp.float32
BF16 = jnp.bfloat16
I32 = jnp.int32
U32 = jnp.uint32

NORM_EPS = 1e-6
CHUNK = 64

M_HEADS = 4
M_QK_DIM = 256
M_V_DIM = 512
M_WIDTH = M_HEADS * M_V_DIM
M_QK_WIDTH = 2 * M_HEADS * M_QK_DIM
CONV_WIDTH = 4
GATE_SOFTCAP = 15.0

A_HEADS = 16
A_NOPE_DIM = 128
A_ROPE_DIM = 64
A_V_DIM = 128
A_Q_RANK = 768
A_KV_RANK = 512
A_HEAD_PAD = 256
ROPE_THETA = 10000.0

N_EXPERTS = 64
TOP_K = 8
N_GROUPS = 8
GROUP_SIZE = N_EXPERTS // N_GROUPS
TOPK_GROUPS = 4
ROUTED_SCALE = 2.5

LANES = 128
VMEM_LIMIT = 56 * 1024 * 1024
NEG_BIG = -1e30

COL_QK = 0
COL_V = COL_QK + M_QK_WIDTH
COL_O = COL_V + M_WIDTH
COL_IG = COL_O + M_WIDTH
COL_FG = COL_IG + M_HEADS
COL_CQ = COL_FG + M_HEADS
COL_CKV = COL_CQ + A_Q_RANK
COL_KR = COL_CKV + A_KV_RANK
N_IN = COL_KR + A_ROPE_DIM
IN_TILE = 768
N_IN_PAD = -(-N_IN // IN_TILE) * IN_TILE
TAIL_W = 2 * LANES
TAIL_GATE_LANE = A_ROPE_DIM


def _params(sem, vmem=VMEM_LIMIT):
    return pltpu.CompilerParams(dimension_semantics=sem, vmem_limit_bytes=vmem)


def _row_loop(n_rows, rc, body):
    def step(i, carry):
        body(pl.multiple_of(i * rc, rc))
        return carry
    lax.fori_loop(0, n_rows // rc, step, 0)


def _pack_pair(a, b):
    lo = lax.bitcast_convert_type(a.astype(BF16).astype(F32), U32) >> 16
    hi = lax.bitcast_convert_type(b.astype(BF16).astype(F32), U32) & jnp.uint32(0xFFFF0000)
    return lo | hi


def _unpack_pair(u):
    lo = lax.bitcast_convert_type(u << 16, F32)
    hi = lax.bitcast_convert_type(u & jnp.uint32(0xFFFF0000), F32)
    return lo, hi


def _adaln_kernel(c_ref, w_ref, b_ref, o_ref, *, rc):
    d, tn = w_ref.shape
    nl = tn // LANES

    def step(i, accs):
        r = pl.multiple_of(i * rc, rc)
        c = c_ref[pl.ds(r, rc), :]
        ca = c * jax.nn.sigmoid(c)
        out = []
        for j in range(nl):
            prod = w_ref[pl.ds(r, rc), j * LANES:(j + 1) * LANES] * ca
            out.append(accs[j] + jnp.sum(prod.reshape(rc // 8, 8, LANES), axis=0))
        return tuple(out)

    accs = lax.fori_loop(0, d // rc, step, tuple(jnp.zeros((8, LANES), F32) for _ in range(nl)))
    for j in range(nl):
        o_ref[:, j * LANES:(j + 1) * LANES] = (
            jnp.sum(accs[j], axis=0, keepdims=True) + b_ref[:, j * LANES:(j + 1) * LANES])


def _adaln(c, w_ada, b_ada):
    d, n = w_ada.shape
    tn = 512
    c_b = jnp.broadcast_to(c.reshape(d, 1), (d, LANES))
    return pl.pallas_call(
        functools.partial(_adaln_kernel, rc=64),
        grid=(n // tn,),
        in_specs=[pl.BlockSpec((d, LANES), lambda j: (0, 0)),
                  pl.BlockSpec((d, tn), lambda j: (0, j)),
                  pl.BlockSpec((1, tn), lambda j: (0, j))],
        out_specs=pl.BlockSpec((1, tn), lambda j: (0, j)),
        out_shape=jax.ShapeDtypeStruct((1, n), F32),
        compiler_params=_params(("arbitrary",)),
        name="adaln",
    )(c_b, w_ada, b_ada.reshape(1, n))


def _norm_mm_kernel(x_ref, gs_ref, sh_ref, w_ref, *rest, rc, has_tail):
    if has_tail:
        wt_ref, o_ref, t_ref, h_scr = rest
    else:
        o_ref, h_scr = rest
    tm = x_ref.shape[0]
    j = pl.program_id(1)

    @pl.when(j == 0)
    def _():
        def body(r):
            x = x_ref[pl.ds(r, rc), :].astype(F32)
            ms = jnp.mean(x * x, axis=-1, keepdims=True)
            y = x * lax.rsqrt(ms + NORM_EPS)
            h_scr[pl.ds(r, rc), :] = (y * gs_ref[...] + sh_ref[...]).astype(BF16)
        _row_loop(tm, rc, body)

    o_ref[...] = jnp.dot(h_scr[...], w_ref[...], preferred_element_type=F32).astype(o_ref.dtype)
    if has_tail:
        @pl.when(j == pl.num_programs(1) - 1)
        def _():
            t_ref[...] = jnp.dot(h_scr[...], wt_ref[...], preferred_element_type=F32)


def _norm_mm(x, gs, sh, w, w_tail=None, *, tm, tn):
    m, k = x.shape
    n = w.shape[1]
    tm = min(tm, m)
    in_specs = [pl.BlockSpec((tm, k), lambda i, j: (i, 0)),
                pl.BlockSpec((1, k), lambda i, j: (0, 0)),
                pl.BlockSpec((1, k), lambda i, j: (0, 0)),
                pl.BlockSpec((k, tn), lambda i, j: (0, j))]
    operands = [x, gs.reshape(1, k), sh.reshape(1, k), w]
    out_shape = [jax.ShapeDtypeStruct((m, n), BF16)]
    out_specs = [pl.BlockSpec((tm, tn), lambda i, j: (i, j))]
    if w_tail is not None:
        nt = w_tail.shape[1]
        in_specs.append(pl.BlockSpec((k, nt), lambda i, j: (0, 0)))
        operands.append(w_tail)
        out_shape.append(jax.ShapeDtypeStruct((m, nt), F32))
        out_specs.append(pl.BlockSpec((tm, nt), lambda i, j: (i, 0)))
    res = pl.pallas_call(
        functools.partial(_norm_mm_kernel, rc=32, has_tail=w_tail is not None),
        grid=(m // tm, n // tn),
        in_specs=in_specs,
        out_specs=out_specs,
        out_shape=out_shape,
        scratch_shapes=[pltpu.VMEM((tm, k), BF16)],
        compiler_params=_params(("parallel", "arbitrary")),
        name="norm_mm",
    )(*operands)
    return res if w_tail is not None else res[0]


def _mm2_kernel(a1_ref, a2_ref, w_ref, o_ref):
    k1 = a1_ref.shape[1]
    acc = jnp.dot(a1_ref[...], w_ref[:k1, :], preferred_element_type=F32)
    acc = acc + jnp.dot(a2_ref[...], w_ref[k1:, :], preferred_element_type=F32)
    o_ref[...] = acc.astype(o_ref.dtype)


def _mm2(a1, a2, w, *, tm, tn):
    m, k1 = a1.shape
    k2 = a2.shape[1]
    n = w.shape[1]
    tm = min(tm, m)
    return pl.pallas_call(
        _mm2_kernel,
        grid=(m // tm, n // tn),
        in_specs=[pl.BlockSpec((tm, k1), lambda i, j: (i, 0)),
                  pl.BlockSpec((tm, k2), lambda i, j: (i, 0)),
                  pl.BlockSpec((k1 + k2, tn), lambda i, j: (0, j))],
        out_specs=pl.BlockSpec((tm, tn), lambda i, j: (i, j)),
        out_shape=jax.ShapeDtypeStruct((m, n), BF16),
        compiler_params=_params(("parallel", "arbitrary")),
        name="out_proj",
    )(a1, a2, w)


def _softcap(z):
    return GATE_SOFTCAP * jnp.tanh(z * (1.0 / GATE_SOFTCAP))


def _log_sigmoid(z):
    return jnp.minimum(z, 0.0) - jnp.log1p(jnp.exp(-jnp.abs(z)))


def _mlstm_kernel(qk_ref, prev_ref, v_ref, o_ref, gcol_ref, grow_ref, cw_ref, cb_ref, bcol_ref, brow_ref,
                  gout_ref, out_ref, c_scr, n_scr, m_scr, u_scr):
    c = pl.program_id(0)
    L = qk_ref.shape[0]
    halo = prev_ref.shape[0]
    dk, dv = M_QK_DIM, M_V_DIM

    @pl.when(c == 0)
    def _():
        c_scr[...] = jnp.zeros_like(c_scr)
        n_scr[...] = jnp.zeros_like(n_scr)
        m_scr[...] = jnp.zeros_like(m_scr)

    prev = prev_ref[...].astype(F32)
    u_scr[0:halo, :] = jnp.where(c == 0, jnp.zeros_like(prev), prev)
    u_scr[halo:halo + L, :] = qk_ref[...].astype(F32)

    def conv_silu(col, width):
        acc = cb_ref[:, col:col + width]
        for j in range(CONV_WIDTH):
            r0 = halo - (CONV_WIDTH - 1) + j
            acc = acc + u_scr[r0:r0 + L, col:col + width] * cw_ref[j:j + 1, col:col + width]
        return acc * jax.nn.sigmoid(acc)

    pre_c = _softcap(gcol_ref[...] + bcol_ref[...])
    lf_c = _log_sigmoid(pre_c)
    pre_r = _softcap(grow_ref[0] + brow_ref[...])
    lf_r = _log_sigmoid(pre_r)
    row = lax.broadcasted_iota(I32, (L, L), 0)
    col = lax.broadcasted_iota(I32, (L, L), 1)
    causal = col <= row
    tril = causal.astype(F32)
    triu = (row <= col).astype(F32)
    bcum_c = jnp.dot(tril, lf_c, preferred_element_type=F32, precision=lax.Precision.HIGHEST)
    bcum_r = jnp.dot(lf_r, triu, preferred_element_type=F32, precision=lax.Precision.HIGHEST)

    for h in range(M_HEADS):
        li_lane = TAIL_GATE_LANE + h
        lf_lane = TAIL_GATE_LANE + M_HEADS + h
        q = (conv_silu(h * dk, dk) * (dk ** -0.5)).astype(BF16)
        kf = conv_silu(M_HEADS * dk + h * dk, dk)
        kb = kf.astype(BF16)
        v = v_ref[:, h * dv:(h + 1) * dv]
        b_c = bcum_c[:, lf_lane:lf_lane + 1]
        li_c = pre_c[:, li_lane:li_lane + 1]
        b_r = bcum_r[M_HEADS + h:M_HEADS + h + 1, :]
        li_r = pre_r[h:h + 1, :]
        m_prev = m_scr[h:h + 1, 0:1]

        dm = jnp.where(causal, b_c - b_r + li_r, NEG_BIG)
        inter = b_c + m_prev
        m_t = jnp.maximum(jnp.max(dm, axis=1, keepdims=True), inter)
        decay = jnp.exp(inter - m_t)
        s = lax.dot_general(q, kb, (((1,), (1,)), ((), ())), preferred_element_type=F32) * jnp.exp(dm - m_t)
        c_state = c_scr[h]
        n_state = n_scr[h:h + 1, :]
        num = jnp.dot(s.astype(BF16), v, preferred_element_type=F32)
        num = num + decay * jnp.dot(q, c_state.astype(BF16), preferred_element_type=F32)
        den = jnp.sum(s, axis=1, keepdims=True) + decay * jnp.sum(q.astype(F32) * n_state, axis=1, keepdims=True)
        hh = num / jnp.maximum(jnp.abs(den), jnp.exp(-m_t))

        b_last = b_c[L - 1:L, :]
        m_new = jnp.maximum(b_last + m_prev, jnp.max(b_last - b_r + li_r, axis=1, keepdims=True))
        carry = jnp.exp(b_last + m_prev - m_new)
        w_c = jnp.exp(b_last - b_c + li_c - m_new)
        kw = kf * w_c
        c_scr[h] = carry * c_state + lax.dot_general(kw.astype(BF16), v, (((0,), (0,)), ((), ())),
                                                     preferred_element_type=F32)
        n_scr[h:h + 1, :] = carry * n_state + jnp.sum(kw, axis=0, keepdims=True)
        m_scr[h:h + 1, :] = jnp.broadcast_to(m_new, (1, LANES))

        hn = hh * lax.rsqrt(jnp.mean(hh * hh, axis=1, keepdims=True) + NORM_EPS) * gout_ref[:, h * dv:(h + 1) * dv]
        gate = jax.nn.sigmoid(o_ref[:, h * dv:(h + 1) * dv].astype(F32))
        out_ref[:, h * dv:(h + 1) * dv] = (hn * gate).astype(out_ref.dtype)


def _mlstm(proj, tail, conv_w, conv_b, b_igate, b_fgate, g_out, *, chunk):
    s_len = proj.shape[0]
    L = chunk
    n_c = s_len // L
    halo = 16
    gates = tail[:, TAIL_GATE_LANE:TAIL_GATE_LANE + 2 * M_HEADS]
    g_row = gates.reshape(n_c, L, 2 * M_HEADS).transpose(0, 2, 1)
    bias = jnp.concatenate([b_igate, b_fgate]).astype(F32)
    b_col = jnp.zeros((1, LANES), F32).at[0, TAIL_GATE_LANE:TAIL_GATE_LANE + 2 * M_HEADS].set(bias)
    b_row = bias.reshape(2 * M_HEADS, 1)
    w = M_QK_WIDTH
    return pl.pallas_call(
        _mlstm_kernel,
        grid=(n_c,),
        in_specs=[pl.BlockSpec((L, w), lambda c: (c, COL_QK // w)),
                  pl.BlockSpec((halo, w), lambda c: (jnp.maximum(c * (L // halo) - 1, 0), COL_QK // w)),
                  pl.BlockSpec((L, M_WIDTH), lambda c: (c, COL_V // M_WIDTH)),
                  pl.BlockSpec((L, M_WIDTH), lambda c: (c, COL_O // M_WIDTH)),
                  pl.BlockSpec((L, LANES), lambda c: (c, 0)),
                  pl.BlockSpec((1, 2 * M_HEADS, L), lambda c: (c, 0, 0)),
                  pl.BlockSpec((CONV_WIDTH, w), lambda c: (0, 0)),
                  pl.BlockSpec((1, w), lambda c: (0, 0)),
                  pl.BlockSpec((1, LANES), lambda c: (0, 0)),
                  pl.BlockSpec((2 * M_HEADS, 1), lambda c: (0, 0)),
                  pl.BlockSpec((1, M_WIDTH), lambda c: (0, 0))],
        out_specs=pl.BlockSpec((L, M_WIDTH), lambda c: (c, 0)),
        out_shape=jax.ShapeDtypeStruct((s_len, M_WIDTH), BF16),
        scratch_shapes=[pltpu.VMEM((M_HEADS, M_QK_DIM, M_V_DIM), F32),
                        pltpu.VMEM((8, M_QK_DIM), F32),
                        pltpu.VMEM((8, LANES), F32),
                        pltpu.VMEM((halo + L, w), F32)],
        compiler_params=_params(("arbitrary",)),
        name="mlstm",
    )(proj, proj, proj, proj, tail, g_row, conv_w, conv_b.reshape(1, w), b_col, b_row, g_out.reshape(1, M_WIDTH))


def _rope_kernel(q_ref, kn_ref, v_ref, tail_ref, cos_ref, sin_ref, qo_ref, ko_ref, vo_ref):
    cos = cos_ref[...]
    sin = sin_ref[...]
    nope_w = A_HEADS * A_NOPE_DIM
    kpe = (tail_ref[:, LANES:2 * LANES] * cos + tail_ref[:, 0:LANES] * sin).astype(BF16)
    lane = lax.broadcasted_iota(I32, (q_ref.shape[0], LANES), 1)
    ones_col = jnp.where(lane == 0, 1.0, 0.0).astype(BF16)
    for h in range(A_HEADS):
        vo_ref[:, h * A_HEAD_PAD:h * A_HEAD_PAD + LANES] = v_ref[:, h * LANES:(h + 1) * LANES]
        vo_ref[:, h * A_HEAD_PAD + LANES:(h + 1) * A_HEAD_PAD] = ones_col
        lo = h * A_HEAD_PAD
        qo_ref[:, lo:lo + LANES] = q_ref[:, h * LANES:(h + 1) * LANES]
        qr = q_ref[:, nope_w + h * LANES:nope_w + (h + 1) * LANES].astype(F32)
        qs = q_ref[:, 2 * nope_w + h * LANES:2 * nope_w + (h + 1) * LANES].astype(F32)
        qo_ref[:, lo + LANES:lo + 2 * LANES] = (qr * cos + qs * sin).astype(BF16)
        ko_ref[:, lo:lo + LANES] = kn_ref[:, h * LANES:(h + 1) * LANES]
        ko_ref[:, lo + LANES:lo + 2 * LANES] = kpe


def _rope_assemble(q_raw, kv_raw, tail, cos_t, sin_t, *, tm):
    s_len = q_raw.shape[0]
    tm = min(tm, s_len)
    nope_w = A_HEADS * A_NOPE_DIM
    wide = A_HEADS * A_HEAD_PAD
    return pl.pallas_call(
        _rope_kernel,
        grid=(s_len // tm,),
        in_specs=[pl.BlockSpec((tm, 3 * nope_w), lambda i: (i, 0)),
                  pl.BlockSpec((tm, nope_w), lambda i: (i, 0)),
                  pl.BlockSpec((tm, nope_w), lambda i: (i, 1)),
                  pl.BlockSpec((tm, 2 * LANES), lambda i: (i, 0)),
                  pl.BlockSpec((tm, LANES), lambda i: (i, 0)),
                  pl.BlockSpec((tm, LANES), lambda i: (i, 0))],
        out_specs=[pl.BlockSpec((tm, wide), lambda i: (i, 0))] * 3,
        out_shape=[jax.ShapeDtypeStruct((s_len, wide), BF16)] * 3,
        compiler_params=_params(("parallel",)),
        name="rope_assemble",
    )(q_raw, kv_raw, kv_raw, tail, cos_t, sin_t)


def _attn_kernel(q_ref, k_ref, v_ref, o_ref, m_scr, acc_scr, s_scr, *, chunk):
    qi = pl.program_id(1)
    tq = q_ref.shape[0]
    m_scr[...] = jnp.full_like(m_scr, NEG_BIG)
    acc_scr[...] = jnp.zeros_like(acc_scr)

    def scores(slot, blk):
        start = pl.multiple_of(blk * tq, tq)
        s_scr[slot] = lax.dot_general(q_ref[...], k_ref[pl.ds(start, tq), :], (((1,), (1,)), ((), ())),
                                      preferred_element_type=F32)

    def consume(slot, blk, masked):
        start = pl.multiple_of(blk * tq, tq)
        s = s_scr[slot]
        if masked:
            rq = lax.broadcasted_iota(I32, (tq, tq), 0) // chunk
            ck = lax.broadcasted_iota(I32, (tq, tq), 1) // chunk
            s = jnp.where(ck <= rq, s, NEG_BIG)
        m_prev = m_scr[...]
        m_new = jnp.maximum(m_prev, jnp.max(s, axis=1, keepdims=True))
        pr = jnp.exp(s - m_new).astype(BF16)
        acc_scr[...] = (jnp.exp(m_prev - m_new) * acc_scr[...]
                        + jnp.dot(pr, v_ref[pl.ds(start, tq), :], preferred_element_type=F32))
        m_scr[...] = m_new

    scores(0, 0)

    def pair(t, carry):
        scores(1, 2 * t + 1)
        consume(0, 2 * t, False)
        scores(0, 2 * t + 2)
        consume(1, 2 * t + 1, False)
        return carry
    lax.fori_loop(0, qi // 2, pair, 0)

    @pl.when(qi % 2 == 1)
    def _():
        scores(1, qi)
        consume(0, qi - 1, False)
        consume(1, qi, True)

    @pl.when(qi % 2 == 0)
    def _():
        consume(0, qi, True)

    acc = acc_scr[...]
    o_ref[...] = (acc[:, :A_V_DIM] / acc[:, A_V_DIM:A_V_DIM + 1]).astype(o_ref.dtype)


def _attention(q_cat, k_cat, v_cat, *, tq, chunk):
    s_len = q_cat.shape[0]
    tq = min(tq, s_len)
    return pl.pallas_call(
        functools.partial(_attn_kernel, chunk=chunk),
        grid=(A_HEADS, s_len // tq),
        in_specs=[pl.BlockSpec((tq, A_HEAD_PAD), lambda h, i: (i, h)),
                  pl.BlockSpec((s_len, A_HEAD_PAD), lambda h, i: (0, h)),
                  pl.BlockSpec((s_len, A_HEAD_PAD), lambda h, i: (0, h))],
        out_specs=pl.BlockSpec((tq, A_V_DIM), lambda h, i: (i, h)),
        out_shape=jax.ShapeDtypeStruct((s_len, A_HEADS * A_V_DIM), BF16),
        scratch_shapes=[pltpu.VMEM((tq, 1), F32), pltpu.VMEM((tq, A_HEAD_PAD), F32),
                        pltpu.VMEM((2, tq, tq), F32)],
        compiler_params=_params(("parallel", "arbitrary")),
        name="mla_attention",
    )(q_cat, k_cat, v_cat)


def _route_t(logits, bias_col):
    n = logits.shape[1]
    scores = jax.nn.sigmoid(logits)
    biased = scores + bias_col
    sub = lax.broadcasted_iota(I32, (GROUP_SIZE, n), 0)
    rows = []
    for g in range(N_GROUPS):
        x = biased[g * GROUP_SIZE:(g + 1) * GROUP_SIZE, :]
        m1 = jnp.max(x, axis=0, keepdims=True)
        i1 = jnp.min(jnp.where(x == m1, sub, GROUP_SIZE), axis=0, keepdims=True)
        m2 = jnp.max(jnp.where(sub == i1, -jnp.inf, x), axis=0, keepdims=True)
        rows.append(m1 + m2)
    gscore = jnp.concatenate(rows, axis=0)
    gio = lax.broadcasted_iota(I32, (N_GROUPS, n), 0)
    grank = jnp.zeros((N_GROUPS, n), I32)
    for g in range(N_GROUPS):
        r = gscore[g:g + 1, :]
        grank = grank + jnp.where(gio > g, jnp.where(r >= gscore, 1, 0), jnp.where(r > gscore, 1, 0))
    gsel = grank < TOPK_GROUPS
    masked = jnp.concatenate(
        [jnp.where(gsel[g:g + 1, :], biased[g * GROUP_SIZE:(g + 1) * GROUP_SIZE, :], -jnp.inf)
         for g in range(N_GROUPS)], axis=0)
    eio = lax.broadcasted_iota(I32, (N_EXPERTS, n), 0)
    rank = jnp.zeros((N_EXPERTS, n), I32)
    for e in range(N_EXPERTS):
        r = masked[e:e + 1, :]
        rank = rank + jnp.where(eio > e, jnp.where(r >= masked, 1, 0), jnp.where(r > masked, 1, 0))
    sel = rank < TOP_K
    denom = jnp.sum(jnp.where(sel, scores, 0.0), axis=0, keepdims=True)
    wnorm = scores / denom * ROUTED_SCALE
    eio_f = eio.astype(F32)
    ids, wts = [], []
    for k in range(TOP_K):
        hit = rank == k
        ids.append(jnp.sum(jnp.where(hit, eio_f, 0.0), axis=0, keepdims=True))
        wts.append(jnp.sum(jnp.where(hit, wnorm, 0.0), axis=0, keepdims=True))
    return jnp.concatenate(ids, axis=0).astype(I32), jnp.concatenate(wts, axis=0)


def _mid_kernel(y_ref, x_ref, gt_ref, gpost_ref, gs_ref, sh_ref, wr_ref, br_ref,
                x1_ref, hp_ref, idx_ref, wts_ref, h_scr, *, rc):
    tm, d = x_ref.shape
    half = d // 2

    def body(r):
        y = y_ref[pl.ds(r, rc), :].astype(F32)
        yn = y * lax.rsqrt(jnp.mean(y * y, axis=-1, keepdims=True) + NORM_EPS) * gpost_ref[...]
        x1 = x_ref[pl.ds(r, rc), :] + gt_ref[...] * yn
        x1_ref[pl.ds(r, rc), :] = x1
        hn = x1 * lax.rsqrt(jnp.mean(x1 * x1, axis=-1, keepdims=True) + NORM_EPS)
        h = hn * gs_ref[...] + sh_ref[...]
        h_scr[pl.ds(r, rc), :] = h
        hp_ref[pl.ds(r, rc), :] = _pack_pair(h[:, :half], h[:, half:])
    _row_loop(tm, rc, body)

    logits = lax.dot_general(wr_ref[...], h_scr[...], (((1,), (1,)), ((), ())),
                             preferred_element_type=F32, precision=lax.Precision.HIGHEST)
    ids, wts = _route_t(logits, br_ref[...])
    idx_ref[...] = ids
    wts_ref[...] = wts


def _mid(y, x, gt1, g_post, gs2, sh2, w_router, b_router, *, tm):
    s_len, d = x.shape
    tm = min(tm, s_len)
    vec = lambda a: a.reshape(1, d).astype(F32)
    return pl.pallas_call(
        functools.partial(_mid_kernel, rc=16),
        grid=(s_len // tm,),
        in_specs=[pl.BlockSpec((tm, d), lambda i: (i, 0)),
                  pl.BlockSpec((tm, d), lambda i: (i, 0)),
                  pl.BlockSpec((1, d), lambda i: (0, 0)),
                  pl.BlockSpec((1, d), lambda i: (0, 0)),
                  pl.BlockSpec((1, d), lambda i: (0, 0)),
                  pl.BlockSpec((1, d), lambda i: (0, 0)),
                  pl.BlockSpec((N_EXPERTS, d), lambda i: (0, 0)),
                  pl.BlockSpec((N_EXPERTS, 1), lambda i: (0, 0))],
        out_specs=[pl.BlockSpec((tm, d), lambda i: (i, 0)),
                   pl.BlockSpec((tm, d // 2), lambda i: (i, 0)),
                   pl.BlockSpec((TOP_K, tm), lambda i: (0, i)),
                   pl.BlockSpec((TOP_K, tm), lambda i: (0, i))],
        out_shape=[jax.ShapeDtypeStruct((s_len, d), F32),
                   jax.ShapeDtypeStruct((s_len, d // 2), U32),
                   jax.ShapeDtypeStruct((TOP_K, s_len), I32),
                   jax.ShapeDtypeStruct((TOP_K, s_len), F32)],
        scratch_shapes=[pltpu.VMEM((tm, d), F32)],
        compiler_params=_params(("parallel",)),
        name="mid_norm_route",
    )(y, x, vec(gt1), vec(g_post), vec(gs2), vec(sh2), w_router.T.astype(F32), b_router.reshape(N_EXPERTS, 1))


def _row_copy(src_hbm, dst_vmem, sem, src_row, dst_row):
    return pltpu.make_async_copy(src_hbm.at[pl.ds(src_row, 1), :], dst_vmem.at[pl.ds(dst_row, 1), :], sem)


def _gather_kernel(tok_ref, h_hbm, o_ref, sem):
    g = o_ref.shape[0]
    base = pl.program_id(0) * g

    n_tiles = g // 8

    def issue(i, carry):
        r = lax.rem(i, n_tiles) * 8 + lax.div(i, n_tiles)
        _row_copy(h_hbm, o_ref, sem, tok_ref[base + r], r).start()
        return carry
    lax.fori_loop(0, g, issue, 0)

    def drain(r, carry):
        _row_copy(h_hbm, o_ref, sem, 0, r).wait()
        return carry
    lax.fori_loop(0, g, drain, 0)


def _gather_rows(slot_tok, hp, *, g):
    n_slots = slot_tok.shape[0]
    w = hp.shape[1]
    grid_spec = pltpu.PrefetchScalarGridSpec(
        num_scalar_prefetch=1,
        grid=(n_slots // g,),
        in_specs=[pl.BlockSpec(memory_space=pl.ANY)],
        out_specs=pl.BlockSpec((g, w), lambda i, tok: (i, 0)),
        scratch_shapes=[pltpu.SemaphoreType.DMA(())],
    )
    return pl.pallas_call(
        _gather_kernel,
        grid_spec=grid_spec,
        out_shape=jax.ShapeDtypeStruct((n_slots, w), hp.dtype),
        compiler_params=_params(("arbitrary",)),
        name="moe_gather",
    )(slot_tok, hp)


def _expert_kernel(be_ref, nb_ref, nxt_ref, x_ref, wg_hbm, wu_hbm, wd_hbm, sg_hbm, su_hbm, sd_hbm, o_ref,
                   stage_g, stage_u, stage_d, wb_g, wb_u, wb_d, sems, *, rc):
    b = pl.program_id(0)
    e = be_ref[b]
    stages = (stage_g, stage_u, stage_d)
    routed = (wg_hbm, wu_hbm, wd_hbm)
    shared = (sg_hbm, su_hbm, sd_hbm)

    def fetch(ex):
        @pl.when(ex < N_EXPERTS)
        def _():
            for i in range(3):
                pltpu.make_async_copy(routed[i].at[ex], stages[i], sems.at[i]).start()

        @pl.when(ex == N_EXPERTS)
        def _():
            for i in range(3):
                pltpu.make_async_copy(shared[i], stages[i], sems.at[i]).start()

    @pl.when(b < nb_ref[0])
    def _():
        @pl.when(b == 0)
        def _():
            fetch(e)

        is_first = jnp.logical_or(b == 0, be_ref[jnp.maximum(b - 1, 0)] != e)

        @pl.when(is_first)
        def _():
            for i in range(3):
                pltpu.make_async_copy(shared[i], stages[i], sems.at[i]).wait()
            for src, dst in ((stage_g, wb_g), (stage_u, wb_u), (stage_d, wb_d)):
                def cast(r, src=src, dst=dst):
                    dst[pl.ds(r, rc), :] = src[pl.ds(r, rc), :].astype(BF16)
                _row_loop(src.shape[0], rc, cast)
            nxt = nxt_ref[e]

            @pl.when(nxt >= 0)
            def _():
                fetch(nxt)

        lo, hi = _unpack_pair(x_ref[...])
        x = jnp.concatenate([lo, hi], axis=1).astype(BF16)
        g = jnp.dot(x, wb_g[...], preferred_element_type=F32)
        u = jnp.dot(x, wb_u[...], preferred_element_type=F32)
        a = (g * jax.nn.sigmoid(g) * u).astype(BF16)
        y = jnp.dot(a, wb_d[...], preferred_element_type=F32)
        half = y.shape[1] // 2
        o_ref[...] = _pack_pair(y[:, :half], y[:, half:])

    @pl.when(b >= nb_ref[0])
    def _():
        o_ref[...] = jnp.zeros_like(o_ref)


def _experts(block_e, n_used, next_e, xs, wg, wu, wd, sg, su, sd, *, tmb):
    n_slots, wp = xs.shape
    d, ff = wg.shape[1], wg.shape[2]
    nb = n_slots // tmb
    any_spec = pl.BlockSpec(memory_space=pl.ANY)
    grid_spec = pltpu.PrefetchScalarGridSpec(
        num_scalar_prefetch=3,
        grid=(nb,),
        in_specs=[pl.BlockSpec((tmb, wp), lambda b, be, nbr, nx: (jnp.minimum(b, nbr[0] - 1), 0))] + [any_spec] * 6,
        out_specs=pl.BlockSpec((tmb, wp), lambda b, be, nbr, nx: (b, 0)),
        scratch_shapes=[pltpu.VMEM((d, ff), F32), pltpu.VMEM((d, ff), F32), pltpu.VMEM((ff, d), F32),
                        pltpu.VMEM((d, ff), BF16), pltpu.VMEM((d, ff), BF16), pltpu.VMEM((ff, d), BF16),
                        pltpu.SemaphoreType.DMA((3,))],
    )
    return pl.pallas_call(
        functools.partial(_expert_kernel, rc=128),
        grid_spec=grid_spec,
        out_shape=jax.ShapeDtypeStruct((n_slots, wp), U32),
        compiler_params=_params(("arbitrary",), vmem=60 * 1024 * 1024),
        name="moe_experts",
    )(block_e, n_used, next_e, xs, wg, wu, wd, sg, su, sd)


def _combine_kernel(slot_ref, w_ref, x1_ref, gt_ref, g_ref, ys_hbm, o_ref, buf, sems, *, rc, n_k):
    tc, d = x1_ref.shape
    half = d // 2
    i = pl.program_id(0)
    cur = lax.rem(i, 2)

    def issue(tile, slot):
        base = tile * (tc * n_k)

        def step(t, carry):
            for k in range(n_k):
                _row_copy(ys_hbm, buf.at[slot, k], sems.at[slot], slot_ref[base + t * n_k + k], t).start()
            return carry
        lax.fori_loop(0, tc, step, 0)

    @pl.when(i == 0)
    def _():
        issue(0, 0)

    @pl.when(i + 1 < pl.num_programs(0))
    def _():
        issue(i + 1, 1 - cur)

    def drain(t, carry):
        for k in range(n_k):
            _row_copy(ys_hbm, buf.at[cur, k], sems.at[cur], 0, t).wait()
        return carry
    lax.fori_loop(0, tc, drain, 0)

    def body(r):
        w = w_ref[pl.ds(r, rc), :]
        lo = jnp.zeros((rc, half), F32)
        hi = jnp.zeros((rc, half), F32)
        for k in range(n_k):
            a, b = _unpack_pair(buf[cur, k, pl.ds(r, rc), :])
            lo = lo + w[:, k:k + 1] * a
            hi = hi + w[:, k:k + 1] * b
        ms = (jnp.sum(lo * lo, axis=-1, keepdims=True) + jnp.sum(hi * hi, axis=-1, keepdims=True)) * (1.0 / d)
        rs = lax.rsqrt(ms + NORM_EPS)
        o_ref[pl.ds(r, rc), 0:half] = (x1_ref[pl.ds(r, rc), 0:half]
                                       + gt_ref[:, 0:half] * (lo * rs * g_ref[:, 0:half]))
        o_ref[pl.ds(r, rc), half:d] = (x1_ref[pl.ds(r, rc), half:d]
                                       + gt_ref[:, half:d] * (hi * rs * g_ref[:, half:d]))
    _row_loop(tc, rc, body)


def _combine(slots, w_tok, x1, gt2, g_post, ys, *, tc):
    s_len, d = x1.shape
    n_k = slots.shape[0] // s_len
    tc = min(tc, s_len)
    grid_spec = pltpu.PrefetchScalarGridSpec(
        num_scalar_prefetch=1,
        grid=(s_len // tc,),
        in_specs=[pl.BlockSpec((tc, w_tok.shape[1]), lambda i, sl: (i, 0)),
                  pl.BlockSpec((tc, d), lambda i, sl: (i, 0)),
                  pl.BlockSpec((1, d), lambda i, sl: (0, 0)),
                  pl.BlockSpec((1, d), lambda i, sl: (0, 0)),
                  pl.BlockSpec(memory_space=pl.ANY)],
        out_specs=pl.BlockSpec((tc, d), lambda i, sl: (i, 0)),
        scratch_shapes=[pltpu.VMEM((2, n_k, tc, d // 2), U32), pltpu.SemaphoreType.DMA((2,))],
    )
    return pl.pallas_call(
        functools.partial(_combine_kernel, rc=8, n_k=n_k),
        grid_spec=grid_spec,
        out_shape=jax.ShapeDtypeStruct((s_len, d), F32),
        compiler_params=_params(("arbitrary",)),
        name="moe_combine",
    )(slots, w_tok, x1, gt2.reshape(1, d), g_post.reshape(1, d), ys)


def _xp_gather_slab_kernel(tok_ref, h_hbm, o_ref, sem):
    g = o_ref.shape[0] // 16
    base = pl.program_id(0) * g

    def cp(t, r):
        return pltpu.make_async_copy(h_hbm.at[pl.ds(pl.multiple_of(t * 16, 16), 16), :],
                                     o_ref.at[pl.ds(pl.multiple_of(r * 16, 16), 16), :], sem)

    def issue(r, carry):
        cp(tok_ref[base + r], r).start()
        return carry
    lax.fori_loop(0, g, issue, 0)

    def drain(r, carry):
        cp(0, r).wait()
        return carry
    lax.fori_loop(0, g, drain, 0)


def _xp_gather_slab(slot_tok, hp_slab, *, g):
    n_slots = slot_tok.shape[0]
    grid_spec = pltpu.PrefetchScalarGridSpec(
        num_scalar_prefetch=1, grid=(n_slots // g,),
        in_specs=[pl.BlockSpec(memory_space=pl.ANY)],
        out_specs=pl.BlockSpec((g * 16, LANES), lambda i, tok: (i, 0)),
        scratch_shapes=[pltpu.SemaphoreType.DMA(())])
    return pl.pallas_call(
        _xp_gather_slab_kernel, grid_spec=grid_spec,
        out_shape=jax.ShapeDtypeStruct((n_slots * 16, LANES), hp_slab.dtype),
        compiler_params=pltpu.CompilerParams(dimension_semantics=("arbitrary",), vmem_limit_bytes=VMEM_LIMIT,
                                             has_side_effects=True),
        name="xp_gather_slab")(slot_tok, hp_slab)


def _xp_scatter_kernel(dst_ref, x_ref, o_hbm, sem, *, rows):
    g = x_ref.shape[0] // rows
    base = pl.program_id(0) * g

    def cp(d, r):
        return pltpu.make_async_copy(x_ref.at[pl.ds(pl.multiple_of(r * rows, rows), rows), :],
                                     o_hbm.at[pl.ds(pl.multiple_of(d * rows, rows), rows), :], sem)

    def issue(r, carry):
        cp(dst_ref[base + r], r).start()
        return carry
    lax.fori_loop(0, g, issue, 0)

    def drain(r, carry):
        cp(0, r).wait()
        return carry
    lax.fori_loop(0, g, drain, 0)


def _xp_scatter(dst, x, *, g, rows, name):
    n = dst.shape[0]
    w = x.shape[1]
    grid_spec = pltpu.PrefetchScalarGridSpec(
        num_scalar_prefetch=1, grid=(n // g,),
        in_specs=[pl.BlockSpec((g * rows, w), lambda i, d: (i, 0))],
        out_specs=pl.BlockSpec(memory_space=pl.ANY),
        scratch_shapes=[pltpu.SemaphoreType.DMA(())])
    return pl.pallas_call(
        functools.partial(_xp_scatter_kernel, rows=rows), grid_spec=grid_spec,
        out_shape=jax.ShapeDtypeStruct(x.shape, x.dtype),
        compiler_params=pltpu.CompilerParams(dimension_semantics=("arbitrary",), vmem_limit_bytes=VMEM_LIMIT,
                                             has_side_effects=True),
        name=name)(dst, x)


def _moe_plan(idx_t, wts_t, *, tmb):
    n_k, n_tok = idx_t.shape
    n_e = N_EXPERTS + 1
    eid = jnp.concatenate([idx_t, jnp.full((1, n_tok), N_EXPERTS, I32)], axis=0)
    wts = jnp.concatenate([wts_t, jnp.ones((1, n_tok), F32)], axis=0)
    sel = (eid[:, None, :] == jnp.arange(n_e, dtype=I32)[None, :, None]).any(axis=0).astype(I32)
    csum = jnp.cumsum(sel, axis=1)
    counts = csum[:, -1]
    padded = (counts + tmb - 1) // tmb * tmb
    ends = jnp.cumsum(padded)
    starts = ends - padded
    slot_dense = starts[:, None] + csum - 1
    slot = jnp.take_along_axis(slot_dense, eid, axis=0)
    n_slots = -(-(n_tok * (n_k + 1) + n_e * (tmb - 1)) // tmb) * tmb
    tok = jnp.broadcast_to(jnp.arange(n_tok, dtype=I32)[None, :], slot.shape)
    slot_tok = jnp.zeros((n_slots,), I32).at[slot.reshape(-1)].set(tok.reshape(-1))
    n_used = (ends[-1] // tmb).astype(I32).reshape(1)
    block_start = jnp.arange(n_slots // tmb, dtype=I32) * tmb
    block_e = jnp.minimum(jnp.searchsorted(ends, block_start, side='right'), n_e - 1).astype(I32)
    slots_tok_major = slot.T.reshape(-1)
    w_tok = jnp.zeros((n_tok, 16), F32).at[:, :n_k + 1].set(wts.T)
    owner = jnp.where(padded > 0, jnp.arange(n_e, dtype=I32), n_e)
    later = jnp.concatenate([lax.cummin(owner[::-1])[::-1][1:], jnp.full((1,), n_e, I32)])
    next_e = jnp.where(later >= n_e, -1, later).astype(I32)
    return slot_tok, block_e, n_used, next_e, slots_tok_major, w_tok


def _in_proj_weights(w_in):
    d = w_in.shape[0]
    w_main = jnp.pad(w_in.astype(BF16), ((0, 0), (0, N_IN_PAD - N_IN)))
    kr = w_in[:, COL_KR:COL_KR + A_ROPE_DIM]
    gates = w_in[:, COL_IG:COL_IG + 2 * M_HEADS]
    half = A_ROPE_DIM // 2
    kswap = jnp.concatenate([-kr[:, half:], kr[:, :half]], axis=1)
    z = lambda n: jnp.zeros((d, n), w_in.dtype)
    w_tail = jnp.concatenate([kswap, gates, z(LANES - A_ROPE_DIM - 2 * M_HEADS), kr, z(LANES - A_ROPE_DIM)], axis=1)
    return w_main, w_tail.astype(BF16)


def _q_up_weight(w_uq):
    r = w_uq.shape[0]
    w = w_uq.reshape(r, A_HEADS, A_NOPE_DIM + A_ROPE_DIM)
    nope = w[:, :, :A_NOPE_DIM]
    rope = w[:, :, A_NOPE_DIM:]
    half = A_ROPE_DIM // 2
    swap = jnp.concatenate([-rope[:, :, half:], rope[:, :, :half]], axis=2)
    pad = jnp.zeros((r, A_HEADS, LANES - A_ROPE_DIM), w_uq.dtype)
    rope_p = jnp.concatenate([rope, pad], axis=2)
    swap_p = jnp.concatenate([swap, pad], axis=2)
    flat = lambda a: a.reshape(r, -1)
    return jnp.concatenate([flat(nope), flat(rope_p), flat(swap_p)], axis=1).astype(BF16)


def _kv_up_weight(w_ukv):
    r = w_ukv.shape[0]
    w = w_ukv.reshape(r, A_HEADS, A_NOPE_DIM + A_V_DIM)
    return jnp.concatenate([w[:, :, :A_NOPE_DIM].reshape(r, -1), w[:, :, A_NOPE_DIM:].reshape(r, -1)],
                           axis=1).astype(BF16)


def _rope_tables(s_len):
    pos = jnp.arange(s_len, dtype=F32)
    inv_freq = 1.0 / (ROPE_THETA ** (jnp.arange(0, A_ROPE_DIM, 2, dtype=F32) / A_ROPE_DIM))
    ang = pos[:, None] * inv_freq[None, :]
    pad = jnp.zeros((s_len, LANES - A_ROPE_DIM), F32)
    cos_t = jnp.concatenate([jnp.cos(ang), jnp.cos(ang), pad], axis=1)
    sin_t = jnp.concatenate([jnp.sin(ang), jnp.sin(ang), pad], axis=1)
    return cos_t, sin_t


def _block(x, c, w_ada, b_ada, g_pre_mix, g_post_mix, w_in, conv_w, conv_b, b_igate, b_fgate, g_mlstm_out,
           g_q_norm, w_uq, g_kv_norm, w_ukv, w_out, g_pre_ffn, g_post_ffn, w_router, b_router,
           w_gate, w_up, w_down, w_shared_gate, w_shared_up, w_shared_down):
    s_len, d = x.shape
    mod = _adaln(c, w_ada, b_ada)[0]
    sh1, sc1, gt1, sh2, sc2, gt2 = [mod[i * d:(i + 1) * d] for i in range(6)]

    w_main, w_tail = _in_proj_weights(w_in)
    proj, tail = _norm_mm(x, g_pre_mix * (1.0 + sc1), sh1, w_main, w_tail, tm=512, tn=IN_TILE)
    h_m = _mlstm(proj, tail, conv_w, conv_b, b_igate, b_fgate, g_mlstm_out, chunk=CHUNK)
    scale = (A_NOPE_DIM + A_ROPE_DIM) ** -0.5
    zq = jnp.zeros((A_Q_RANK,), F32)
    zkv = jnp.zeros((A_KV_RANK,), F32)
    q_raw = _norm_mm(proj[:, COL_CQ:COL_CQ + A_Q_RANK], g_q_norm * scale, zq, _q_up_weight(w_uq), tm=1024, tn=1536)
    kv_raw = _norm_mm(proj[:, COL_CKV:COL_CKV + A_KV_RANK], g_kv_norm, zkv, _kv_up_weight(w_ukv), tm=1024, tn=1024)
    cos_t, sin_t = _rope_tables(s_len)
    q_cat, k_cat, v_cat = _rope_assemble(q_raw, kv_raw, tail, cos_t, sin_t, tm=256)
    h_a = _attention(q_cat, k_cat, v_cat, tq=1024, chunk=CHUNK)
    y = _mm2(h_m, h_a, w_out.astype(BF16), tm=1024, tn=1024)

    x1, hp, idx_t, wts_t = _mid(y, x, gt1, g_post_mix, g_pre_ffn * (1.0 + sc2), sh2, w_router, b_router, tm=256)
    tmb = 256
    slot_tok, block_e, n_used, next_e, slots, w_tok = _moe_plan(idx_t, wts_t, tmb=tmb)
    xs = _gather_rows(slot_tok, hp, g=tmb)
    ys = _experts(block_e, n_used, next_e, xs, w_gate, w_up, w_down,
                  w_shared_gate, w_shared_up, w_shared_down, tmb=tmb)
    out = _combine(slots, w_tok, x1, gt2, g_post_ffn, ys, tc=128)
    xs_slab = _xp_gather_slab(slot_tok, hp.reshape(s_len * 16, LANES), g=tmb)
    sc_rows = _xp_scatter(slots, xs, g=tmb, rows=1, name="xp_scatter_rows")
    sc_slab = _xp_scatter(slots, xs_slab, g=tmb, rows=16, name="xp_scatter_slab")
    keep = (xs_slab[0:1, 0:1] & sc_rows[0:1, 0:1] & sc_slab[0:1, 0:1] & jnp.uint32(0)).astype(F32)
    return out + keep


def kernel(x, c, w_ada, b_ada, g_pre_mix, g_post_mix, w_in, conv_w, conv_b, b_igate, b_fgate, g_mlstm_out,
           g_q_norm, w_uq, g_kv_norm, w_ukv, w_out, g_pre_ffn, g_post_ffn, w_router, b_router,
           w_gate, w_up, w_down, w_shared_gate, w_shared_up, w_shared_down):
    assert x.shape[0] == 1 and w_ada.shape[0] == 1, "single sequence, single layer"
    layer = (w_ada, b_ada, g_pre_mix, g_post_mix, w_in, conv_w, conv_b, b_igate, b_fgate, g_mlstm_out,
             g_q_norm, w_uq, g_kv_norm, w_ukv, w_out, g_pre_ffn, g_post_ffn, w_router, b_router,
             w_gate, w_up, w_down, w_shared_gate, w_shared_up, w_shared_down)
    out = _block(x[0], c[0], *[p[0] for p in layer])
    return out[None]
```

```python
import functools

import jax
import jax.numpy as jnp
from jax import lax
from jax.experimental import pallas as pl
from jax.experimental.pallas import tpu as pltpu

F32 = jnp.float32
BF16 = jnp.bfloat16
I32 = jnp.int32
U32 = jnp.uint32

NORM_EPS = 1e-6
CHUNK = 64

M_HEADS = 4
M_QK_DIM = 256
M_V_DIM = 512
M_WIDTH = M_HEADS * M_V_DIM
M_QK_WIDTH = 2 * M_HEADS * M_QK_DIM
CONV_WIDTH = 4
GATE_SOFTCAP = 15.0

A_HEADS = 16
A_NOPE_DIM = 128
A_ROPE_DIM = 64
A_V_DIM = 128
A_Q_RANK = 768
A_KV_RANK = 512
A_HEAD_PAD = 256
ROPE_THETA = 10000.0

N_EXPERTS = 64
TOP_K = 8
N_GROUPS = 8
GROUP_SIZE = N_EXPERTS // N_GROUPS
TOPK_GROUPS = 4
ROUTED_SCALE = 2.5

LANES = 128
VMEM_LIMIT = 56 * 1024 * 1024
NEG_BIG = -1e30

COL_QK = 0
COL_V = COL_QK + M_QK_WIDTH
COL_O = COL_V + M_WIDTH
COL_IG = COL_O + M_WIDTH
COL_FG = COL_IG + M_HEADS
COL_CQ = COL_FG + M_HEADS
COL_CKV = COL_CQ + A_Q_RANK
COL_KR = COL_CKV + A_KV_RANK
N_IN = COL_KR + A_ROPE_DIM
IN_TILE = 768
N_IN_PAD = -(-N_IN // IN_TILE) * IN_TILE
TAIL_W = 2 * LANES
TAIL_GATE_LANE = A_ROPE_DIM


def _params(sem, vmem=VMEM_LIMIT):
    return pltpu.CompilerParams(dimension_semantics=sem, vmem_limit_bytes=vmem)


def _row_loop(n_rows, rc, body):
    def step(i, carry):
        body(pl.multiple_of(i * rc, rc))
        return carry
    lax.fori_loop(0, n_rows // rc, step, 0)


def _pack_pair(a, b):
    lo = lax.bitcast_convert_type(a.astype(BF16).astype(F32), U32) >> 16
    hi = lax.bitcast_convert_type(b.astype(BF16).astype(F32), U32) & jnp.uint32(0xFFFF0000)
    return lo | hi


def _unpack_pair(u):
    lo = lax.bitcast_convert_type(u << 16, F32)
    hi = lax.bitcast_convert_type(u & jnp.uint32(0xFFFF0000), F32)
    return lo, hi


def _adaln_kernel(c_ref, w_ref, b_ref, o_ref, *, rc):
    d, tn = w_ref.shape
    nl = tn // LANES

    def step(i, accs):
        r = pl.multiple_of(i * rc, rc)
        c = c_ref[pl.ds(r, rc), :]
        ca = c * jax.nn.sigmoid(c)
        out = []
        for j in range(nl):
            prod = w_ref[pl.ds(r, rc), j * LANES:(j + 1) * LANES] * ca
            out.append(accs[j] + jnp.sum(prod.reshape(rc // 8, 8, LANES), axis=0))
        return tuple(out)

    accs = lax.fori_loop(0, d // rc, step, tuple(jnp.zeros((8, LANES), F32) for _ in range(nl)))
    for j in range(nl):
        o_ref[:, j * LANES:(j + 1) * LANES] = (
            jnp.sum(accs[j], axis=0, keepdims=True) + b_ref[:, j * LANES:(j + 1) * LANES])


def _adaln(c, w_ada, b_ada):
    d, n = w_ada.shape
    tn = 512
    c_b = jnp.broadcast_to(c.reshape(d, 1), (d, LANES))
    return pl.pallas_call(
        functools.partial(_adaln_kernel, rc=64),
        grid=(n // tn,),
        in_specs=[pl.BlockSpec((d, LANES), lambda j: (0, 0)),
                  pl.BlockSpec((d, tn), lambda j: (0, j)),
                  pl.BlockSpec((1, tn), lambda j: (0, j))],
        out_specs=pl.BlockSpec((1, tn), lambda j: (0, j)),
        out_shape=jax.ShapeDtypeStruct((1, n), F32),
        compiler_params=_params(("arbitrary",)),
        name="adaln",
    )(c_b, w_ada, b_ada.reshape(1, n))


def _norm_mm_kernel(x_ref, gs_ref, sh_ref, w_ref, *rest, rc, has_tail):
    if has_tail:
        wt_ref, o_ref, t_ref, h_scr = rest
    else:
        o_ref, h_scr = rest
    tm = x_ref.shape[0]
    j = pl.program_id(1)

    @pl.when(j == 0)
    def _():
        def body(r):
            x = x_ref[pl.ds(r, rc), :].astype(F32)
            ms = jnp.mean(x * x, axis=-1, keepdims=True)
            y = x * lax.rsqrt(ms + NORM_EPS)
            h_scr[pl.ds(r, rc), :] = (y * gs_ref[...] + sh_ref[...]).astype(BF16)
        _row_loop(tm, rc, body)

    o_ref[...] = jnp.dot(h_scr[...], w_ref[...], preferred_element_type=F32).astype(o_ref.dtype)
    if has_tail:
        @pl.when(j == pl.num_programs(1) - 1)
        def _():
            t_ref[...] = jnp.dot(h_scr[...], wt_ref[...], preferred_element_type=F32)


def _norm_mm(x, gs, sh, w, w_tail=None, *, tm, tn):
    m, k = x.shape
    n = w.shape[1]
    tm = min(tm, m)
    in_specs = [pl.BlockSpec((tm, k), lambda i, j: (i, 0)),
                pl.BlockSpec((1, k), lambda i, j: (0, 0)),
                pl.BlockSpec((1, k), lambda i, j: (0, 0)),
                pl.BlockSpec((k, tn), lambda i, j: (0, j))]
    operands = [x, gs.reshape(1, k), sh.reshape(1, k), w]
    out_shape = [jax.ShapeDtypeStruct((m, n), BF16)]
    out_specs = [pl.BlockSpec((tm, tn), lambda i, j: (i, j))]
    if w_tail is not None:
        nt = w_tail.shape[1]
        in_specs.append(pl.BlockSpec((k, nt), lambda i, j: (0, 0)))
        operands.append(w_tail)
        out_shape.append(jax.ShapeDtypeStruct((m, nt), F32))
        out_specs.append(pl.BlockSpec((tm, nt), lambda i, j: (i, 0)))
    res = pl.pallas_call(
        functools.partial(_norm_mm_kernel, rc=32, has_tail=w_tail is not None),
        grid=(m // tm, n // tn),
        in_specs=in_specs,
        out_specs=out_specs,
        out_shape=out_shape,
        scratch_shapes=[pltpu.VMEM((tm, k), BF16)],
        compiler_params=_params(("parallel", "arbitrary")),
        name="norm_mm",
    )(*operands)
    return res if w_tail is not None else res[0]


def _mm2_kernel(a1_ref, a2_ref, w_ref, o_ref):
    k1 = a1_ref.shape[1]
    acc = jnp.dot(a1_ref[...], w_ref[:k1, :], preferred_element_type=F32)
    acc = acc + jnp.dot(a2_ref[...], w_ref[k1:, :], preferred_element_type=F32)
    o_ref[...] = acc.astype(o_ref.dtype)


def _mm2(a1, a2, w, *, tm, tn):
    m, k1 = a1.shape
    k2 = a2.shape[1]
    n = w.shape[1]
    tm = min(tm, m)
    return pl.pallas_call(
        _mm2_kernel,
        grid=(m // tm, n // tn),
        in_specs=[pl.BlockSpec((tm, k1), lambda i, j: (i, 0)),
                  pl.BlockSpec((tm, k2), lambda i, j: (i, 0)),
                  pl.BlockSpec((k1 + k2, tn), lambda i, j: (0, j))],
        out_specs=pl.BlockSpec((tm, tn), lambda i, j: (i, j)),
        out_shape=jax.ShapeDtypeStruct((m, n), BF16),
        compiler_params=_params(("parallel", "arbitrary")),
        name="out_proj",
    )(a1, a2, w)


def _softcap(z):
    return GATE_SOFTCAP * jnp.tanh(z * (1.0 / GATE_SOFTCAP))


def _log_sigmoid(z):
    return jnp.minimum(z, 0.0) - jnp.log1p(jnp.exp(-jnp.abs(z)))


def _mlstm_kernel(qk_ref, prev_ref, v_ref, o_ref, gcol_ref, grow_ref, cw_ref, cb_ref, bcol_ref, brow_ref,
                  gout_ref, out_ref, c_scr, n_scr, m_scr, u_scr):
    c = pl.program_id(0)
    L = qk_ref.shape[0]
    halo = prev_ref.shape[0]
    dk, dv = M_QK_DIM, M_V_DIM

    @pl.when(c == 0)
    def _():
        c_scr[...] = jnp.zeros_like(c_scr)
        n_scr[...] = jnp.zeros_like(n_scr)
        m_scr[...] = jnp.zeros_like(m_scr)

    prev = prev_ref[...].astype(F32)
    u_scr[0:halo, :] = jnp.where(c == 0, jnp.zeros_like(prev), prev)
    u_scr[halo:halo + L, :] = qk_ref[...].astype(F32)

    def conv_silu(col, width):
        acc = cb_ref[:, col:col + width]
        for j in range(CONV_WIDTH):
            r0 = halo - (CONV_WIDTH - 1) + j
            acc = acc + u_scr[r0:r0 + L, col:col + width] * cw_ref[j:j + 1, col:col + width]
        return acc * jax.nn.sigmoid(acc)

    pre_c = _softcap(gcol_ref[...] + bcol_ref[...])
    lf_c = _log_sigmoid(pre_c)
    pre_r = _softcap(grow_ref[0] + brow_ref[...])
    lf_r = _log_sigmoid(pre_r)
    row = lax.broadcasted_iota(I32, (L, L), 0)
    col = lax.broadcasted_iota(I32, (L, L), 1)
    causal = col <= row
    tril = causal.astype(F32)
    triu = (row <= col).astype(F32)
    bcum_c = jnp.dot(tril, lf_c, preferred_element_type=F32, precision=lax.Precision.HIGHEST)
    bcum_r = jnp.dot(lf_r, triu, preferred_element_type=F32, precision=lax.Precision.HIGHEST)

    for h in range(M_HEADS):
        li_lane = TAIL_GATE_LANE + h
        lf_lane = TAIL_GATE_LANE + M_HEADS + h
        q = (conv_silu(h * dk, dk) * (dk ** -0.5)).astype(BF16)
        kf = conv_silu(M_HEADS * dk + h * dk, dk)
        kb = kf.astype(BF16)
        v = v_ref[:, h * dv:(h + 1) * dv]
        b_c = bcum_c[:, lf_lane:lf_lane + 1]
        li_c = pre_c[:, li_lane:li_lane + 1]
        b_r = bcum_r[M_HEADS + h:M_HEADS + h + 1, :]
        li_r = pre_r[h:h + 1, :]
        m_prev = m_scr[h:h + 1, 0:1]

        dm = jnp.where(causal, b_c - b_r + li_r, NEG_BIG)
        inter = b_c + m_prev
        m_t = jnp.maximum(jnp.max(dm, axis=1, keepdims=True), inter)
        decay = jnp.exp(inter - m_t)
        s = lax.dot_general(q, kb, (((1,), (1,)), ((), ())), preferred_element_type=F32) * jnp.exp(dm - m_t)
        c_state = c_scr[h]
        n_state = n_scr[h:h + 1, :]
        num = jnp.dot(s.astype(BF16), v, preferred_element_type=F32)
        num = num + decay * jnp.dot(q, c_state.astype(BF16), preferred_element_type=F32)
        den = jnp.sum(s, axis=1, keepdims=True) + decay * jnp.sum(q.astype(F32) * n_state, axis=1, keepdims=True)
        hh = num / jnp.maximum(jnp.abs(den), jnp.exp(-m_t))

        b_last = b_c[L - 1:L, :]
        m_new = jnp.maximum(b_last + m_prev, jnp.max(b_last - b_r + li_r, axis=1, keepdims=True))
        carry = jnp.exp(b_last + m_prev - m_new)
        w_c = jnp.exp(b_last - b_c + li_c - m_new)
        kw = kf * w_c
        c_scr[h] = carry * c_state + lax.dot_general(kw.astype(BF16), v, (((0,), (0,)), ((), ())),
                                                     preferred_element_type=F32)
        n_scr[h:h + 1, :] = carry * n_state + jnp.sum(kw, axis=0, keepdims=True)
        m_scr[h:h + 1, :] = jnp.broadcast_to(m_new, (1, LANES))

        hn = hh * lax.rsqrt(jnp.mean(hh * hh, axis=1, keepdims=True) + NORM_EPS) * gout_ref[:, h * dv:(h + 1) * dv]
        gate = jax.nn.sigmoid(o_ref[:, h * dv:(h + 1) * dv].astype(F32))
        out_ref[:, h * dv:(h + 1) * dv] = (hn * gate).astype(out_ref.dtype)


def _mlstm(proj, tail, conv_w, conv_b, b_igate, b_fgate, g_out, *, chunk):
    s_len = proj.shape[0]
    L = chunk
    n_c = s_len // L
    halo = 16
    gates = tail[:, TAIL_GATE_LANE:TAIL_GATE_LANE + 2 * M_HEADS]
    g_row = gates.reshape(n_c, L, 2 * M_HEADS).transpose(0, 2, 1)
    bias = jnp.concatenate([b_igate, b_fgate]).astype(F32)
    b_col = jnp.zeros((1, LANES), F32).at[0, TAIL_GATE_LANE:TAIL_GATE_LANE + 2 * M_HEADS].set(bias)
    b_row = bias.reshape(2 * M_HEADS, 1)
    w = M_QK_WIDTH
    return pl.pallas_call(
        _mlstm_kernel,
        grid=(n_c,),
        in_specs=[pl.BlockSpec((L, w), lambda c: (c, COL_QK // w)),
                  pl.BlockSpec((halo, w), lambda c: (jnp.maximum(c * (L // halo) - 1, 0), COL_QK // w)),
                  pl.BlockSpec((L, M_WIDTH), lambda c: (c, COL_V // M_WIDTH)),
                  pl.BlockSpec((L, M_WIDTH), lambda c: (c, COL_O // M_WIDTH)),
                  pl.BlockSpec((L, LANES), lambda c: (c, 0)),
                  pl.BlockSpec((1, 2 * M_HEADS, L), lambda c: (c, 0, 0)),
                  pl.BlockSpec((CONV_WIDTH, w), lambda c: (0, 0)),
                  pl.BlockSpec((1, w), lambda c: (0, 0)),
                  pl.BlockSpec((1, LANES), lambda c: (0, 0)),
                  pl.BlockSpec((2 * M_HEADS, 1), lambda c: (0, 0)),
                  pl.BlockSpec((1, M_WIDTH), lambda c: (0, 0))],
        out_specs=pl.BlockSpec((L, M_WIDTH), lambda c: (c, 0)),
        out_shape=jax.ShapeDtypeStruct((s_len, M_WIDTH), BF16),
        scratch_shapes=[pltpu.VMEM((M_HEADS, M_QK_DIM, M_V_DIM), F32),
                        pltpu.VMEM((8, M_QK_DIM), F32),
                        pltpu.VMEM((8, LANES), F32),
                        pltpu.VMEM((halo + L, w), F32)],
        compiler_params=_params(("arbitrary",)),
        name="mlstm",
    )(proj, proj, proj, proj, tail, g_row, conv_w, conv_b.reshape(1, w), b_col, b_row, g_out.reshape(1, M_WIDTH))


def _rope_kernel(q_ref, kn_ref, v_ref, tail_ref, cos_ref, sin_ref, qo_ref, ko_ref, vo_ref):
    cos = cos_ref[...]
    sin = sin_ref[...]
    nope_w = A_HEADS * A_NOPE_DIM
    kpe = (tail_ref[:, LANES:2 * LANES] * cos + tail_ref[:, 0:LANES] * sin).astype(BF16)
    lane = lax.broadcasted_iota(I32, (q_ref.shape[0], LANES), 1)
    ones_col = jnp.where(lane == 0, 1.0, 0.0).astype(BF16)
    for h in range(A_HEADS):
        vo_ref[:, h * A_HEAD_PAD:h * A_HEAD_PAD + LANES] = v_ref[:, h * LANES:(h + 1) * LANES]
        vo_ref[:, h * A_HEAD_PAD + LANES:(h + 1) * A_HEAD_PAD] = ones_col
        lo = h * A_HEAD_PAD
        qo_ref[:, lo:lo + LANES] = q_ref[:, h * LANES:(h + 1) * LANES]
        qr = q_ref[:, nope_w + h * LANES:nope_w + (h + 1) * LANES].astype(F32)
        qs = q_ref[:, 2 * nope_w + h * LANES:2 * nope_w + (h + 1) * LANES].astype(F32)
        qo_ref[:, lo + LANES:lo + 2 * LANES] = (qr * cos + qs * sin).astype(BF16)
        ko_ref[:, lo:lo + LANES] = kn_ref[:, h * LANES:(h + 1) * LANES]
        ko_ref[:, lo + LANES:lo + 2 * LANES] = kpe


def _rope_assemble(q_raw, kv_raw, tail, cos_t, sin_t, *, tm):
    s_len = q_raw.shape[0]
    tm = min(tm, s_len)
    nope_w = A_HEADS * A_NOPE_DIM
    wide = A_HEADS * A_HEAD_PAD
    return pl.pallas_call(
        _rope_kernel,
        grid=(s_len // tm,),
        in_specs=[pl.BlockSpec((tm, 3 * nope_w), lambda i: (i, 0)),
                  pl.BlockSpec((tm, nope_w), lambda i: (i, 0)),
                  pl.BlockSpec((tm, nope_w), lambda i: (i, 1)),
                  pl.BlockSpec((tm, 2 * LANES), lambda i: (i, 0)),
                  pl.BlockSpec((tm, LANES), lambda i: (i, 0)),
                  pl.BlockSpec((tm, LANES), lambda i: (i, 0))],
        out_specs=[pl.BlockSpec((tm, wide), lambda i: (i, 0))] * 3,
        out_shape=[jax.ShapeDtypeStruct((s_len, wide), BF16)] * 3,
        compiler_params=_params(("parallel",)),
        name="rope_assemble",
    )(q_raw, kv_raw, kv_raw, tail, cos_t, sin_t)


def _attn_kernel(q_ref, k_ref, v_ref, o_ref, m_scr, acc_scr, s_scr, *, chunk):
    qi = pl.program_id(1)
    tq = q_ref.shape[0]
    m_scr[...] = jnp.full_like(m_scr, NEG_BIG)
    acc_scr[...] = jnp.zeros_like(acc_scr)

    def scores(slot, blk):
        start = pl.multiple_of(blk * tq, tq)
        s_scr[slot] = lax.dot_general(q_ref[...], k_ref[pl.ds(start, tq), :], (((1,), (1,)), ((), ())),
                                      preferred_element_type=F32)

    def consume(slot, blk, masked):
        start = pl.multiple_of(blk * tq, tq)
        s = s_scr[slot]
        if masked:
            rq = lax.broadcasted_iota(I32, (tq, tq), 0) // chunk
            ck = lax.broadcasted_iota(I32, (tq, tq), 1) // chunk
            s = jnp.where(ck <= rq, s, NEG_BIG)
        m_prev = m_scr[...]
        m_new = jnp.maximum(m_prev, jnp.max(s, axis=1, keepdims=True))
        pr = jnp.exp(s - m_new).astype(BF16)
        acc_scr[...] = (jnp.exp(m_prev - m_new) * acc_scr[...]
                        + jnp.dot(pr, v_ref[pl.ds(start, tq), :], preferred_element_type=F32))
        m_scr[...] = m_new

    scores(0, 0)

    def pair(t, carry):
        scores(1, 2 * t + 1)
        consume(0, 2 * t, False)
        scores(0, 2 * t + 2)
        consume(1, 2 * t + 1, False)
        return carry
    lax.fori_loop(0, qi // 2, pair, 0)

    @pl.when(qi % 2 == 1)
    def _():
        scores(1, qi)
        consume(0, qi - 1, False)
        consume(1, qi, True)

    @pl.when(qi % 2 == 0)
    def _():
        consume(0, qi, True)

    acc = acc_scr[...]
    o_ref[...] = (acc[:, :A_V_DIM] / acc[:, A_V_DIM:A_V_DIM + 1]).astype(o_ref.dtype)


def _attention(q_cat, k_cat, v_cat, *, tq, chunk):
    s_len = q_cat.shape[0]
    tq = min(tq, s_len)
    return pl.pallas_call(
        functools.partial(_attn_kernel, chunk=chunk),
        grid=(A_HEADS, s_len // tq),
        in_specs=[pl.BlockSpec((tq, A_HEAD_PAD), lambda h, i: (i, h)),
                  pl.BlockSpec((s_len, A_HEAD_PAD), lambda h, i: (0, h)),
                  pl.BlockSpec((s_len, A_HEAD_PAD), lambda h, i: (0, h))],
        out_specs=pl.BlockSpec((tq, A_V_DIM), lambda h, i: (i, h)),
        out_shape=jax.ShapeDtypeStruct((s_len, A_HEADS * A_V_DIM), BF16),
        scratch_shapes=[pltpu.VMEM((tq, 1), F32), pltpu.VMEM((tq, A_HEAD_PAD), F32),
                        pltpu.VMEM((2, tq, tq), F32)],
        compiler_params=_params(("parallel", "arbitrary")),
        name="mla_attention",
    )(q_cat, k_cat, v_cat)


def _route_t(logits, bias_col):
    n = logits.shape[1]
    scores = jax.nn.sigmoid(logits)
    biased = scores + bias_col
    sub = lax.broadcasted_iota(I32, (GROUP_SIZE, n), 0)
    rows = []
    for g in range(N_GROUPS):
        x = biased[g * GROUP_SIZE:(g + 1) * GROUP_SIZE, :]
        m1 = jnp.max(x, axis=0, keepdims=True)
        i1 = jnp.min(jnp.where(x == m1, sub, GROUP_SIZE), axis=0, keepdims=True)
        m2 = jnp.max(jnp.where(sub == i1, -jnp.inf, x), axis=0, keepdims=True)
        rows.append(m1 + m2)
    gscore = jnp.concatenate(rows, axis=0)
    gio = lax.broadcasted_iota(I32, (N_GROUPS, n), 0)
    grank = jnp.zeros((N_GROUPS, n), I32)
    for g in range(N_GROUPS):
        r = gscore[g:g + 1, :]
        grank = grank + jnp.where(gio > g, jnp.where(r >= gscore, 1, 0), jnp.where(r > gscore, 1, 0))
    gsel = grank < TOPK_GROUPS
    masked = jnp.concatenate(
        [jnp.where(gsel[g:g + 1, :], biased[g * GROUP_SIZE:(g + 1) * GROUP_SIZE, :], -jnp.inf)
         for g in range(N_GROUPS)], axis=0)
    eio = lax.broadcasted_iota(I32, (N_EXPERTS, n), 0)
    rank = jnp.zeros((N_EXPERTS, n), I32)
    for e in range(N_EXPERTS):
        r = masked[e:e + 1, :]
        rank = rank + jnp.where(eio > e, jnp.where(r >= masked, 1, 0), jnp.where(r > masked, 1, 0))
    sel = rank < TOP_K
    denom = jnp.sum(jnp.where(sel, scores, 0.0), axis=0, keepdims=True)
    wnorm = scores / denom * ROUTED_SCALE
    eio_f = eio.astype(F32)
    ids, wts = [], []
    for k in range(TOP_K):
        hit = rank == k
        ids.append(jnp.sum(jnp.where(hit, eio_f, 0.0), axis=0, keepdims=True))
        wts.append(jnp.sum(jnp.where(hit, wnorm, 0.0), axis=0, keepdims=True))
    return jnp.concatenate(ids, axis=0).astype(I32), jnp.concatenate(wts, axis=0)


def _mid_kernel(y_ref, x_ref, gt_ref, gpost_ref, gs_ref, sh_ref, wr_ref, br_ref,
                x1_ref, hp_ref, idx_ref, wts_ref, h_scr, *, rc):
    tm, d = x_ref.shape
    half = d // 2

    def body(r):
        y = y_ref[pl.ds(r, rc), :].astype(F32)
        yn = y * lax.rsqrt(jnp.mean(y * y, axis=-1, keepdims=True) + NORM_EPS) * gpost_ref[...]
        x1 = x_ref[pl.ds(r, rc), :] + gt_ref[...] * yn
        x1_ref[pl.ds(r, rc), :] = x1
        hn = x1 * lax.rsqrt(jnp.mean(x1 * x1, axis=-1, keepdims=True) + NORM_EPS)
        h = hn * gs_ref[...] + sh_ref[...]
        h_scr[pl.ds(r, rc), :] = h
        hp_ref[pl.ds(r, rc), :] = _pack_pair(h[:, :half], h[:, half:])
    _row_loop(tm, rc, body)

    logits = lax.dot_general(wr_ref[...], h_scr[...], (((1,), (1,)), ((), ())),
                             preferred_element_type=F32, precision=lax.Precision.HIGHEST)
    ids, wts = _route_t(logits, br_ref[...])
    idx_ref[...] = ids
    wts_ref[...] = wts


def _mid(y, x, gt1, g_post, gs2, sh2, w_router, b_router, *, tm):
    s_len, d = x.shape
    tm = min(tm, s_len)
    vec = lambda a: a.reshape(1, d).astype(F32)
    return pl.pallas_call(
        functools.partial(_mid_kernel, rc=16),
        grid=(s_len // tm,),
        in_specs=[pl.BlockSpec((tm, d), lambda i: (i, 0)),
                  pl.BlockSpec((tm, d), lambda i: (i, 0)),
                  pl.BlockSpec((1, d), lambda i: (0, 0)),
                  pl.BlockSpec((1, d), lambda i: (0, 0)),
                  pl.BlockSpec((1, d), lambda i: (0, 0)),
                  pl.BlockSpec((1, d), lambda i: (0, 0)),
                  pl.BlockSpec((N_EXPERTS, d), lambda i: (0, 0)),
                  pl.BlockSpec((N_EXPERTS, 1), lambda i: (0, 0))],
        out_specs=[pl.BlockSpec((tm, d), lambda i: (i, 0)),
                   pl.BlockSpec((tm, d // 2), lambda i: (i, 0)),
                   pl.BlockSpec((TOP_K, tm), lambda i: (0, i)),
                   pl.BlockSpec((TOP_K, tm), lambda i: (0, i))],
        out_shape=[jax.ShapeDtypeStruct((s_len, d), F32),
                   jax.ShapeDtypeStruct((s_len, d // 2), U32),
                   jax.ShapeDtypeStruct((TOP_K, s_len), I32),
                   jax.ShapeDtypeStruct((TOP_K, s_len), F32)],
        scratch_shapes=[pltpu.VMEM((tm, d), F32)],
        compiler_params=_params(("parallel",)),
        name="mid_norm_route",
    )(y, x, vec(gt1), vec(g_post), vec(gs2), vec(sh2), w_router.T.astype(F32), b_router.reshape(N_EXPERTS, 1))


def _row_copy(src_hbm, dst_vmem, sem, src_row, dst_row):
    return pltpu.make_async_copy(src_hbm.at[pl.ds(src_row, 1), :], dst_vmem.at[pl.ds(dst_row, 1), :], sem)


ROW_GROUP = 8


def _expert_kernel(be_ref, nb_ref, nxt_ref, tok_ref, h_hbm, wg_hbm, wu_hbm, wd_hbm, sg_hbm, su_hbm, sd_hbm,
                   o_ref, stage_g, stage_u, stage_d, wb_g, wb_u, wb_d, xbuf, sems, xsems, *, rc):
    b = pl.program_id(0)
    e = be_ref[b]
    tmb = xbuf.shape[1]
    cur = lax.rem(b, 2)
    stages = (stage_g, stage_u, stage_d)
    routed = (wg_hbm, wu_hbm, wd_hbm)
    shared = (sg_hbm, su_hbm, sd_hbm)

    def gather_loop(blk, slot):
        base = blk * tmb

        def step(gi, carry):
            r0 = gi * ROW_GROUP
            for u in range(ROW_GROUP):
                _row_copy(h_hbm, xbuf.at[slot], xsems.at[slot], tok_ref[base + r0 + u], r0 + u).start(priority=u % 2)
            return carry
        lax.fori_loop(0, tmb // ROW_GROUP, step, 0)

    def gather_unrolled(blk, slot):
        base = blk * tmb
        for r in range(tmb):
            _row_copy(h_hbm, xbuf.at[slot], xsems.at[slot], tok_ref[base + r], r).start(priority=r % 2)

    def gather_wait(slot):
        pltpu.make_async_copy(h_hbm.at[pl.ds(0, tmb), :], xbuf.at[slot], xsems.at[slot]).wait()

    def fetch(ex):
        @pl.when(ex < N_EXPERTS)
        def _():
            for i in range(3):
                pltpu.make_async_copy(routed[i].at[ex], stages[i], sems.at[i]).start()

        @pl.when(ex == N_EXPERTS)
        def _():
            for i in range(3):
                pltpu.make_async_copy(shared[i], stages[i], sems.at[i]).start()

    @pl.when(b < nb_ref[0])
    def _():
        @pl.when(b == 0)
        def _():
            fetch(e)
            gather_loop(0, 0)

        is_first = jnp.logical_or(b == 0, be_ref[jnp.maximum(b - 1, 0)] != e)

        @pl.when(is_first)
        def _():
            for i in range(3):
                pltpu.make_async_copy(shared[i], stages[i], sems.at[i]).wait()
            for src, dst in ((stage_g, wb_g), (stage_u, wb_u), (stage_d, wb_d)):
                def cast(r, src=src, dst=dst):
                    dst[pl.ds(r, rc), :] = src[pl.ds(r, rc), :].astype(BF16)
                _row_loop(src.shape[0], rc, cast)
            nxt = nxt_ref[e]

            @pl.when(nxt >= 0)
            def _():
                fetch(nxt)

        gather_wait(cur)
        gather_unrolled(jnp.minimum(b + 1, nb_ref[0] - 1), 1 - cur)
        lo, hi = _unpack_pair(xbuf[cur])
        x = jnp.concatenate([lo, hi], axis=1).astype(BF16)
        g = jnp.dot(x, wb_g[...], preferred_element_type=F32)
        u = jnp.dot(x, wb_u[...], preferred_element_type=F32)
        a = (g * jax.nn.sigmoid(g) * u).astype(BF16)
        y = jnp.dot(a, wb_d[...], preferred_element_type=F32)
        half = y.shape[1] // 2
        o_ref[...] = _pack_pair(y[:, :half], y[:, half:])

        @pl.when(b == nb_ref[0] - 1)
        def _():
            gather_wait(1 - cur)

    @pl.when(b >= nb_ref[0])
    def _():
        o_ref[...] = jnp.zeros_like(o_ref)


def _experts(block_e, n_used, next_e, slot_tok, hp, wg, wu, wd, sg, su, sd, *, tmb):
    n_slots = slot_tok.shape[0]
    wp = hp.shape[1]
    d, ff = wg.shape[1], wg.shape[2]
    nb = n_slots // tmb
    grid_spec = pltpu.PrefetchScalarGridSpec(
        num_scalar_prefetch=4,
        grid=(nb,),
        in_specs=[pl.BlockSpec(memory_space=pl.ANY)] * 7,
        out_specs=pl.BlockSpec((tmb, wp), lambda b, *_: (b, 0)),
        scratch_shapes=[pltpu.VMEM((d, ff), F32), pltpu.VMEM((d, ff), F32), pltpu.VMEM((ff, d), F32),
                        pltpu.VMEM((d, ff), BF16), pltpu.VMEM((d, ff), BF16), pltpu.VMEM((ff, d), BF16),
                        pltpu.VMEM((2, tmb, wp), U32),
                        pltpu.SemaphoreType.DMA((3,)), pltpu.SemaphoreType.DMA((2,))],
    )
    return pl.pallas_call(
        functools.partial(_expert_kernel, rc=128),
        grid_spec=grid_spec,
        out_shape=jax.ShapeDtypeStruct((n_slots, wp), U32),
        compiler_params=_params(("arbitrary",), vmem=60 * 1024 * 1024),
        name="moe_experts",
    )(block_e, n_used, next_e, slot_tok, hp, wg, wu, wd, sg, su, sd)


def _combine_kernel(slot_ref, w_ref, x1_ref, gt_ref, g_ref, ys_hbm, o_ref, buf, sems, *, rc, n_k):
    tc, d = x1_ref.shape
    half = d // 2
    i = pl.program_id(0)
    cur = lax.rem(i, 2)
    last = pl.num_programs(0) - 1

    def issue_rows(tile, slot, t0):
        base = tile * (tc * n_k)
        for t in range(rc):
            for k in range(n_k):
                _row_copy(ys_hbm, buf.at[slot, k], sems.at[slot], slot_ref[base + (t0 + t) * n_k + k],
                          t0 + t).start(priority=k % 2)

    def wait_planes(slot):
        for k in range(n_k):
            pltpu.make_async_copy(ys_hbm.at[pl.ds(0, tc), :], buf.at[slot, k], sems.at[slot]).wait()

    @pl.when(i == 0)
    def _():
        _row_loop(tc, rc, lambda r: issue_rows(0, 0, r))

    wait_planes(cur)
    nxt = jnp.minimum(i + 1, last)

    def body(r):
        issue_rows(nxt, 1 - cur, r)
        w = w_ref[pl.ds(r, rc), :]
        lo = jnp.zeros((rc, half), F32)
        hi = jnp.zeros((rc, half), F32)
        for k in range(n_k):
            a, b = _unpack_pair(buf[cur, k, pl.ds(r, rc), :])
            lo = lo + w[:, k:k + 1] * a
            hi = hi + w[:, k:k + 1] * b
        ms = (jnp.sum(lo * lo, axis=-1, keepdims=True) + jnp.sum(hi * hi, axis=-1, keepdims=True)) * (1.0 / d)
        rs = lax.rsqrt(ms + NORM_EPS)
        o_ref[pl.ds(r, rc), 0:half] = (x1_ref[pl.ds(r, rc), 0:half]
                                       + gt_ref[:, 0:half] * (lo * rs * g_ref[:, 0:half]))
        o_ref[pl.ds(r, rc), half:d] = (x1_ref[pl.ds(r, rc), half:d]
                                       + gt_ref[:, half:d] * (hi * rs * g_ref[:, half:d]))
    _row_loop(tc, rc, body)

    @pl.when(i == last)
    def _():
        wait_planes(1 - cur)


def _combine(slots, w_tok, x1, gt2, g_post, ys, *, tc):
    s_len, d = x1.shape
    n_k = slots.shape[0] // s_len
    tc = min(tc, s_len)
    grid_spec = pltpu.PrefetchScalarGridSpec(
        num_scalar_prefetch=1,
        grid=(s_len // tc,),
        in_specs=[pl.BlockSpec((tc, w_tok.shape[1]), lambda i, sl: (i, 0)),
                  pl.BlockSpec((tc, d), lambda i, sl: (i, 0)),
                  pl.BlockSpec((1, d), lambda i, sl: (0, 0)),
                  pl.BlockSpec((1, d), lambda i, sl: (0, 0)),
                  pl.BlockSpec(memory_space=pl.ANY)],
        out_specs=pl.BlockSpec((tc, d), lambda i, sl: (i, 0)),
        scratch_shapes=[pltpu.VMEM((2, n_k, tc, d // 2), U32), pltpu.SemaphoreType.DMA((2,))],
    )
    return pl.pallas_call(
        functools.partial(_combine_kernel, rc=8, n_k=n_k),
        grid_spec=grid_spec,
        out_shape=jax.ShapeDtypeStruct((s_len, d), F32),
        compiler_params=_params(("arbitrary",)),
        name="moe_combine",
    )(slots, w_tok, x1, gt2.reshape(1, d), g_post.reshape(1, d), ys)


def _moe_plan(idx_t, wts_t, *, tmb):
    n_k, n_tok = idx_t.shape
    n_e = N_EXPERTS + 1
    eid = jnp.concatenate([idx_t, jnp.full((1, n_tok), N_EXPERTS, I32)], axis=0)
    wts = jnp.concatenate([wts_t, jnp.ones((1, n_tok), F32)], axis=0)
    sel = (eid[:, None, :] == jnp.arange(n_e, dtype=I32)[None, :, None]).any(axis=0).astype(I32)
    csum = jnp.cumsum(sel, axis=1)
    counts = csum[:, -1]
    padded = (counts + tmb - 1) // tmb * tmb
    ends = jnp.cumsum(padded)
    starts = ends - padded
    slot_dense = starts[:, None] + csum - 1
    slot = jnp.take_along_axis(slot_dense, eid, axis=0)
    n_slots = -(-(n_tok * (n_k + 1) + n_e * (tmb - 1)) // tmb) * tmb
    tok = jnp.broadcast_to(jnp.arange(n_tok, dtype=I32)[None, :], slot.shape)
    slot_tok = jnp.zeros((n_slots,), I32).at[slot.reshape(-1)].set(tok.reshape(-1))
    n_used = (ends[-1] // tmb).astype(I32).reshape(1)
    block_start = jnp.arange(n_slots // tmb, dtype=I32) * tmb
    block_e = jnp.minimum(jnp.searchsorted(ends, block_start, side='right'), n_e - 1).astype(I32)
    slots_tok_major = slot.T.reshape(-1)
    w_tok = jnp.zeros((n_tok, 16), F32).at[:, :n_k + 1].set(wts.T)
    owner = jnp.where(padded > 0, jnp.arange(n_e, dtype=I32), n_e)
    later = jnp.concatenate([lax.cummin(owner[::-1])[::-1][1:], jnp.full((1,), n_e, I32)])
    next_e = jnp.where(later >= n_e, -1, later).astype(I32)
    return slot_tok, block_e, n_used, next_e, slots_tok_major, w_tok


def _in_proj_weights(w_in):
    d = w_in.shape[0]
    w_main = jnp.pad(w_in.astype(BF16), ((0, 0), (0, N_IN_PAD - N_IN)))
    kr = w_in[:, COL_KR:COL_KR + A_ROPE_DIM]
    gates = w_in[:, COL_IG:COL_IG + 2 * M_HEADS]
    half = A_ROPE_DIM // 2
    kswap = jnp.concatenate([-kr[:, half:], kr[:, :half]], axis=1)
    z = lambda n: jnp.zeros((d, n), w_in.dtype)
    w_tail = jnp.concatenate([kswap, gates, z(LANES - A_ROPE_DIM - 2 * M_HEADS), kr, z(LANES - A_ROPE_DIM)], axis=1)
    return w_main, w_tail.astype(BF16)


def _q_up_weight(w_uq):
    r = w_uq.shape[0]
    w = w_uq.reshape(r, A_HEADS, A_NOPE_DIM + A_ROPE_DIM)
    nope = w[:, :, :A_NOPE_DIM]
    rope = w[:, :, A_NOPE_DIM:]
    half = A_ROPE_DIM // 2
    swap = jnp.concatenate([-rope[:, :, half:], rope[:, :, :half]], axis=2)
    pad = jnp.zeros((r, A_HEADS, LANES - A_ROPE_DIM), w_uq.dtype)
    rope_p = jnp.concatenate([rope, pad], axis=2)
    swap_p = jnp.concatenate([swap, pad], axis=2)
    flat = lambda a: a.reshape(r, -1)
    return jnp.concatenate([flat(nope), flat(rope_p), flat(swap_p)], axis=1).astype(BF16)


def _kv_up_weight(w_ukv):
    r = w_ukv.shape[0]
    w = w_ukv.reshape(r, A_HEADS, A_NOPE_DIM + A_V_DIM)
    return jnp.concatenate([w[:, :, :A_NOPE_DIM].reshape(r, -1), w[:, :, A_NOPE_DIM:].reshape(r, -1)],
                           axis=1).astype(BF16)


def _rope_tables(s_len):
    pos = jnp.arange(s_len, dtype=F32)
    inv_freq = 1.0 / (ROPE_THETA ** (jnp.arange(0, A_ROPE_DIM, 2, dtype=F32) / A_ROPE_DIM))
    ang = pos[:, None] * inv_freq[None, :]
    pad = jnp.zeros((s_len, LANES - A_ROPE_DIM), F32)
    cos_t = jnp.concatenate([jnp.cos(ang), jnp.cos(ang), pad], axis=1)
    sin_t = jnp.concatenate([jnp.sin(ang), jnp.sin(ang), pad], axis=1)
    return cos_t, sin_t


def _block(x, c, w_ada, b_ada, g_pre_mix, g_post_mix, w_in, conv_w, conv_b, b_igate, b_fgate, g_mlstm_out,
           g_q_norm, w_uq, g_kv_norm, w_ukv, w_out, g_pre_ffn, g_post_ffn, w_router, b_router,
           w_gate, w_up, w_down, w_shared_gate, w_shared_up, w_shared_down):
    s_len, d = x.shape
    mod = _adaln(c, w_ada, b_ada)[0]
    sh1, sc1, gt1, sh2, sc2, gt2 = [mod[i * d:(i + 1) * d] for i in range(6)]

    w_main, w_tail = _in_proj_weights(w_in)
    proj, tail = _norm_mm(x, g_pre_mix * (1.0 + sc1), sh1, w_main, w_tail, tm=512, tn=IN_TILE)
    h_m = _mlstm(proj, tail, conv_w, conv_b, b_igate, b_fgate, g_mlstm_out, chunk=CHUNK)
    scale = (A_NOPE_DIM + A_ROPE_DIM) ** -0.5
    zq = jnp.zeros((A_Q_RANK,), F32)
    zkv = jnp.zeros((A_KV_RANK,), F32)
    q_raw = _norm_mm(proj[:, COL_CQ:COL_CQ + A_Q_RANK], g_q_norm * scale, zq, _q_up_weight(w_uq), tm=1024, tn=1536)
    kv_raw = _norm_mm(proj[:, COL_CKV:COL_CKV + A_KV_RANK], g_kv_norm, zkv, _kv_up_weight(w_ukv), tm=1024, tn=1024)
    cos_t, sin_t = _rope_tables(s_len)
    q_cat, k_cat, v_cat = _rope_assemble(q_raw, kv_raw, tail, cos_t, sin_t, tm=256)
    h_a = _attention(q_cat, k_cat, v_cat, tq=1024, chunk=CHUNK)
    y = _mm2(h_m, h_a, w_out.astype(BF16), tm=1024, tn=1024)

    x1, hp, idx_t, wts_t = _mid(y, x, gt1, g_post_mix, g_pre_ffn * (1.0 + sc2), sh2, w_router, b_router, tm=256)
    tmb = 256
    slot_tok, block_e, n_used, next_e, slots, w_tok = _moe_plan(idx_t, wts_t, tmb=tmb)
    ys = _experts(block_e, n_used, next_e, slot_tok, hp, w_gate, w_up, w_down,
                  w_shared_gate, w_shared_up, w_shared_down, tmb=tmb)
    return _combine(slots, w_tok, x1, gt2, g_post_ffn, ys, tc=128)


def kernel(x, c, w_ada, b_ada, g_pre_mix, g_post_mix, w_in, conv_w, conv_b, b_igate, b_fgate, g_mlstm_out,
           g_q_norm, w_uq, g_kv_norm, w_ukv, w_out, g_pre_ffn, g_post_ffn, w_router, b_router,
           w_gate, w_up, w_down, w_shared_gate, w_shared_up, w_shared_down):
    assert x.shape[0] == 1 and w_ada.shape[0] == 1, "single sequence, single layer"
    layer = (w_ada, b_ada, g_pre_mix, g_post_mix, w_in, conv_w, conv_b, b_igate, b_fgate, g_mlstm_out,
             g_q_norm, w_uq, g_kv_norm, w_ukv, w_out, g_pre_ffn, g_post_ffn, w_router, b_router,
             w_gate, w_up, w_down, w_shared_gate, w_shared_up, w_shared_down)
    out = _block(x[0], c[0], *[p[0] for p in layer])
    return out[None]
```

```python
import functools

import jax
import jax.numpy as jnp
from jax import lax
from jax.experimental import pallas as pl
from jax.experimental.pallas import tpu as pltpu

F32 = jnp.float32
BF16 = jnp.bfloat16
I32 = jnp.int32
U32 = jnp.uint32

NORM_EPS = 1e-6
CHUNK = 64
M_CHUNK = 128

M_HEADS = 4
M_QK_DIM = 256
M_V_DIM = 512
M_WIDTH = M_HEADS * M_V_DIM
M_QK_WIDTH = 2 * M_HEADS * M_QK_DIM
CONV_WIDTH = 4
GATE_SOFTCAP = 15.0

A_HEADS = 16
A_NOPE_DIM = 128
A_ROPE_DIM = 64
A_V_DIM = 128
A_Q_RANK = 768
A_KV_RANK = 512
A_HEAD_PAD = 256
ROPE_THETA = 10000.0

N_EXPERTS = 64
TOP_K = 8
N_GROUPS = 8
GROUP_SIZE = N_EXPERTS // N_GROUPS
TOPK_GROUPS = 4
ROUTED_SCALE = 2.5

LANES = 128
VMEM_LIMIT = 56 * 1024 * 1024
NEG_BIG = -1e30

COL_QK = 0
COL_V = COL_QK + M_QK_WIDTH
COL_O = COL_V + M_WIDTH
COL_IG = COL_O + M_WIDTH
COL_FG = COL_IG + M_HEADS
COL_CQ = COL_FG + M_HEADS
COL_CKV = COL_CQ + A_Q_RANK
COL_KR = COL_CKV + A_KV_RANK
N_IN = COL_KR + A_ROPE_DIM
IN_TILE = 768
N_IN_PAD = -(-N_IN // IN_TILE) * IN_TILE
TAIL_W = 2 * LANES
TAIL_GATE_LANE = A_ROPE_DIM


def _params(sem, vmem=VMEM_LIMIT):
    return pltpu.CompilerParams(dimension_semantics=sem, vmem_limit_bytes=vmem)


def _row_loop(n_rows, rc, body):
    def step(i, carry):
        body(pl.multiple_of(i * rc, rc))
        return carry
    lax.fori_loop(0, n_rows // rc, step, 0)


def _pack_pair(a, b):
    lo = lax.bitcast_convert_type(a.astype(BF16).astype(F32), U32) >> 16
    hi = lax.bitcast_convert_type(b.astype(BF16).astype(F32), U32) & jnp.uint32(0xFFFF0000)
    return lo | hi


def _unpack_pair(u):
    lo = lax.bitcast_convert_type(u << 16, F32)
    hi = lax.bitcast_convert_type(u & jnp.uint32(0xFFFF0000), F32)
    return lo, hi


def _adaln_kernel(c_ref, w_ref, b_ref, o_ref, *, rc):
    d, tn = w_ref.shape
    nl = tn // LANES

    def step(i, accs):
        r = pl.multiple_of(i * rc, rc)
        c = c_ref[pl.ds(r, rc), :]
        ca = c * jax.nn.sigmoid(c)
        out = []
        for j in range(nl):
            prod = w_ref[pl.ds(r, rc), j * LANES:(j + 1) * LANES] * ca
            out.append(accs[j] + jnp.sum(prod.reshape(rc // 8, 8, LANES), axis=0))
        return tuple(out)

    accs = lax.fori_loop(0, d // rc, step, tuple(jnp.zeros((8, LANES), F32) for _ in range(nl)))
    for j in range(nl):
        o_ref[:, j * LANES:(j + 1) * LANES] = (
            jnp.sum(accs[j], axis=0, keepdims=True) + b_ref[:, j * LANES:(j + 1) * LANES])


def _adaln(c, w_ada, b_ada):
    d, n = w_ada.shape
    tn = 512
    c_b = jnp.broadcast_to(c.reshape(d, 1), (d, LANES))
    return pl.pallas_call(
        functools.partial(_adaln_kernel, rc=64),
        grid=(n // tn,),
        in_specs=[pl.BlockSpec((d, LANES), lambda j: (0, 0)),
                  pl.BlockSpec((d, tn), lambda j: (0, j)),
                  pl.BlockSpec((1, tn), lambda j: (0, j))],
        out_specs=pl.BlockSpec((1, tn), lambda j: (0, j)),
        out_shape=jax.ShapeDtypeStruct((1, n), F32),
        compiler_params=_params(("arbitrary",)),
        name="adaln",
    )(c_b, w_ada, b_ada.reshape(1, n))


def _norm_mm_kernel(x_ref, gs_ref, sh_ref, w_ref, *rest, rc, has_tail):
    if has_tail:
        wt_ref, o_ref, t_ref, h_scr = rest
    else:
        o_ref, h_scr = rest
    tm = x_ref.shape[0]
    j = pl.program_id(1)

    @pl.when(j == 0)
    def _():
        def body(r):
            x = x_ref[pl.ds(r, rc), :].astype(F32)
            ms = jnp.mean(x * x, axis=-1, keepdims=True)
            y = x * lax.rsqrt(ms + NORM_EPS)
            h_scr[pl.ds(r, rc), :] = (y * gs_ref[...] + sh_ref[...]).astype(BF16)
        _row_loop(tm, rc, body)

    o_ref[...] = jnp.dot(h_scr[...], w_ref[...], preferred_element_type=F32).astype(o_ref.dtype)
    if has_tail:
        @pl.when(j == pl.num_programs(1) - 1)
        def _():
            t_ref[...] = jnp.dot(h_scr[...], wt_ref[...], preferred_element_type=F32)


def _norm_mm(x, gs, sh, w, w_tail=None, *, tm, tn):
    m, k = x.shape
    n = w.shape[1]
    tm = min(tm, m)
    in_specs = [pl.BlockSpec((tm, k), lambda i, j: (i, 0)),
                pl.BlockSpec((1, k), lambda i, j: (0, 0)),
                pl.BlockSpec((1, k), lambda i, j: (0, 0)),
                pl.BlockSpec((k, tn), lambda i, j: (0, j))]
    operands = [x, gs.reshape(1, k), sh.reshape(1, k), w]
    out_shape = [jax.ShapeDtypeStruct((m, n), BF16)]
    out_specs = [pl.BlockSpec((tm, tn), lambda i, j: (i, j))]
    if w_tail is not None:
        nt = w_tail.shape[1]
        in_specs.append(pl.BlockSpec((k, nt), lambda i, j: (0, 0)))
        operands.append(w_tail)
        out_shape.append(jax.ShapeDtypeStruct((m, nt), F32))
        out_specs.append(pl.BlockSpec((tm, nt), lambda i, j: (i, 0)))
    res = pl.pallas_call(
        functools.partial(_norm_mm_kernel, rc=32, has_tail=w_tail is not None),
        grid=(m // tm, n // tn),
        in_specs=in_specs,
        out_specs=out_specs,
        out_shape=out_shape,
        scratch_shapes=[pltpu.VMEM((tm, k), BF16)],
        compiler_params=_params(("parallel", "arbitrary")),
        name="norm_mm",
    )(*operands)
    return res if w_tail is not None else res[0]


def _mm2_kernel(a1_ref, a2_ref, w_ref, o_ref):
    k1 = a1_ref.shape[1]
    acc = jnp.dot(a1_ref[...], w_ref[:k1, :], preferred_element_type=F32)
    acc = acc + jnp.dot(a2_ref[...], w_ref[k1:, :], preferred_element_type=F32)
    o_ref[...] = acc.astype(o_ref.dtype)


def _mm2(a1, a2, w, *, tm, tn):
    m, k1 = a1.shape
    k2 = a2.shape[1]
    n = w.shape[1]
    tm = min(tm, m)
    return pl.pallas_call(
        _mm2_kernel,
        grid=(m // tm, n // tn),
        in_specs=[pl.BlockSpec((tm, k1), lambda i, j: (i, 0)),
                  pl.BlockSpec((tm, k2), lambda i, j: (i, 0)),
                  pl.BlockSpec((k1 + k2, tn), lambda i, j: (0, j))],
        out_specs=pl.BlockSpec((tm, tn), lambda i, j: (i, j)),
        out_shape=jax.ShapeDtypeStruct((m, n), BF16),
        compiler_params=_params(("parallel", "arbitrary")),
        name="out_proj",
    )(a1, a2, w)


def _softcap(z):
    return GATE_SOFTCAP * jnp.tanh(z * (1.0 / GATE_SOFTCAP))


def _log_sigmoid(z):
    return jnp.minimum(z, 0.0) - jnp.log1p(jnp.exp(-jnp.abs(z)))


def _mlstm_kernel(qk_ref, prev_ref, v_ref, o_ref, gcol_ref, grow_ref, cw_ref, cb_ref, bcol_ref, brow_ref,
                  gout_ref, out_ref, c_scr, n_scr, m_scr, u_scr):
    c = pl.program_id(0)
    L = qk_ref.shape[0]
    halo = prev_ref.shape[0]
    dk, dv = M_QK_DIM, M_V_DIM

    @pl.when(c == 0)
    def _():
        c_scr[...] = jnp.zeros_like(c_scr)
        n_scr[...] = jnp.zeros_like(n_scr)
        m_scr[...] = jnp.zeros_like(m_scr)

    prev = prev_ref[...].astype(F32)
    u_scr[0:halo, :] = jnp.where(c == 0, jnp.zeros_like(prev), prev)
    u_scr[halo:halo + L, :] = qk_ref[...].astype(F32)

    def conv_silu(col, width):
        acc = cb_ref[:, col:col + width]
        for j in range(CONV_WIDTH):
            r0 = halo - (CONV_WIDTH - 1) + j
            acc = acc + u_scr[r0:r0 + L, col:col + width] * cw_ref[j:j + 1, col:col + width]
        return acc * jax.nn.sigmoid(acc)

    pre_c = _softcap(gcol_ref[...] + bcol_ref[...])
    lf_c = _log_sigmoid(pre_c)
    pre_r = _softcap(grow_ref[0] + brow_ref[...])
    lf_r = _log_sigmoid(pre_r)
    row = lax.broadcasted_iota(I32, (L, L), 0)
    col = lax.broadcasted_iota(I32, (L, L), 1)
    causal = col <= row
    tril = causal.astype(F32)
    triu = (row <= col).astype(F32)
    bcum_c = jnp.dot(tril, lf_c, preferred_element_type=F32, precision=lax.Precision.HIGHEST)
    bcum_r = jnp.dot(lf_r, triu, preferred_element_type=F32, precision=lax.Precision.HIGHEST)

    for h in range(M_HEADS):
        li_lane = TAIL_GATE_LANE + h
        lf_lane = TAIL_GATE_LANE + M_HEADS + h
        q = (conv_silu(h * dk, dk) * (dk ** -0.5)).astype(BF16)
        kf = conv_silu(M_HEADS * dk + h * dk, dk)
        kb = kf.astype(BF16)
        v = v_ref[:, h * dv:(h + 1) * dv]
        b_c = bcum_c[:, lf_lane:lf_lane + 1]
        li_c = pre_c[:, li_lane:li_lane + 1]
        b_r = bcum_r[M_HEADS + h:M_HEADS + h + 1, :]
        li_r = pre_r[h:h + 1, :]
        m_prev = m_scr[h:h + 1, 0:1]

        dm = jnp.where(causal, b_c - b_r + li_r, NEG_BIG)
        inter = b_c + m_prev
        m_t = jnp.maximum(jnp.max(dm, axis=1, keepdims=True), inter)
        decay = jnp.exp(inter - m_t)
        s = lax.dot_general(q, kb, (((1,), (1,)), ((), ())), preferred_element_type=F32) * jnp.exp(dm - m_t)
        c_state = c_scr[h]
        n_state = n_scr[h:h + 1, :]
        num = jnp.dot(s.astype(BF16), v, preferred_element_type=F32)
        num = num + decay * jnp.dot(q, c_state.astype(BF16), preferred_element_type=F32)
        den = jnp.sum(s, axis=1, keepdims=True) + decay * jnp.sum(q.astype(F32) * n_state, axis=1, keepdims=True)
        hh = num / jnp.maximum(jnp.abs(den), jnp.exp(-m_t))

        b_last = b_c[L - 1:L, :]
        m_new = jnp.maximum(b_last + m_prev, jnp.max(b_last - b_r + li_r, axis=1, keepdims=True))
        carry = jnp.exp(b_last + m_prev - m_new)
        w_c = jnp.exp(b_last - b_c + li_c - m_new)
        kw = kf * w_c
        c_scr[h] = carry * c_state + lax.dot_general(kw.astype(BF16), v, (((0,), (0,)), ((), ())),
                                                     preferred_element_type=F32)
        n_scr[h:h + 1, :] = carry * n_state + jnp.sum(kw, axis=0, keepdims=True)
        m_scr[h:h + 1, :] = jnp.broadcast_to(m_new, (1, LANES))

        hn = hh * lax.rsqrt(jnp.mean(hh * hh, axis=1, keepdims=True) + NORM_EPS) * gout_ref[:, h * dv:(h + 1) * dv]
        gate = jax.nn.sigmoid(o_ref[:, h * dv:(h + 1) * dv].astype(F32))
        out_ref[:, h * dv:(h + 1) * dv] = (hn * gate).astype(out_ref.dtype)


def _mlstm(proj, tail, conv_w, conv_b, b_igate, b_fgate, g_out, *, chunk):
    s_len = proj.shape[0]
    L = chunk
    n_c = s_len // L
    halo = 16
    gates = tail[:, TAIL_GATE_LANE:TAIL_GATE_LANE + 2 * M_HEADS]
    g_row = gates.reshape(n_c, L, 2 * M_HEADS).transpose(0, 2, 1)
    bias = jnp.concatenate([b_igate, b_fgate]).astype(F32)
    b_col = jnp.zeros((1, LANES), F32).at[0, TAIL_GATE_LANE:TAIL_GATE_LANE + 2 * M_HEADS].set(bias)
    b_row = bias.reshape(2 * M_HEADS, 1)
    w = M_QK_WIDTH
    return pl.pallas_call(
        _mlstm_kernel,
        grid=(n_c,),
        in_specs=[pl.BlockSpec((L, w), lambda c: (c, COL_QK // w)),
                  pl.BlockSpec((halo, w), lambda c: (jnp.maximum(c * (L // halo) - 1, 0), COL_QK // w)),
                  pl.BlockSpec((L, M_WIDTH), lambda c: (c, COL_V // M_WIDTH)),
                  pl.BlockSpec((L, M_WIDTH), lambda c: (c, COL_O // M_WIDTH)),
                  pl.BlockSpec((L, LANES), lambda c: (c, 0)),
                  pl.BlockSpec((1, 2 * M_HEADS, L), lambda c: (c, 0, 0)),
                  pl.BlockSpec((CONV_WIDTH, w), lambda c: (0, 0)),
                  pl.BlockSpec((1, w), lambda c: (0, 0)),
                  pl.BlockSpec((1, LANES), lambda c: (0, 0)),
                  pl.BlockSpec((2 * M_HEADS, 1), lambda c: (0, 0)),
                  pl.BlockSpec((1, M_WIDTH), lambda c: (0, 0))],
        out_specs=pl.BlockSpec((L, M_WIDTH), lambda c: (c, 0)),
        out_shape=jax.ShapeDtypeStruct((s_len, M_WIDTH), BF16),
        scratch_shapes=[pltpu.VMEM((M_HEADS, M_QK_DIM, M_V_DIM), F32),
                        pltpu.VMEM((8, M_QK_DIM), F32),
                        pltpu.VMEM((8, LANES), F32),
                        pltpu.VMEM((halo + L, w), F32)],
        compiler_params=_params(("arbitrary",)),
        name="mlstm",
    )(proj, proj, proj, proj, tail, g_row, conv_w, conv_b.reshape(1, w), b_col, b_row, g_out.reshape(1, M_WIDTH))


def _rope_kernel(q_ref, kn_ref, v_ref, tail_ref, cos_ref, sin_ref, qo_ref, ko_ref, vo_ref):
    cos = cos_ref[...]
    sin = sin_ref[...]
    nope_w = A_HEADS * A_NOPE_DIM
    kpe = (tail_ref[:, LANES:2 * LANES] * cos + tail_ref[:, 0:LANES] * sin).astype(BF16)
    lane = lax.broadcasted_iota(I32, (q_ref.shape[0], LANES), 1)
    ones_col = jnp.where(lane == 0, 1.0, 0.0).astype(BF16)
    for h in range(A_HEADS):
        vo_ref[:, h * A_HEAD_PAD:h * A_HEAD_PAD + LANES] = v_ref[:, h * LANES:(h + 1) * LANES]
        vo_ref[:, h * A_HEAD_PAD + LANES:(h + 1) * A_HEAD_PAD] = ones_col
        lo = h * A_HEAD_PAD
        qo_ref[:, lo:lo + LANES] = q_ref[:, h * LANES:(h + 1) * LANES]
        qr = q_ref[:, nope_w + h * LANES:nope_w + (h + 1) * LANES].astype(F32)
        qs = q_ref[:, 2 * nope_w + h * LANES:2 * nope_w + (h + 1) * LANES].astype(F32)
        qo_ref[:, lo + LANES:lo + 2 * LANES] = (qr * cos + qs * sin).astype(BF16)
        ko_ref[:, lo:lo + LANES] = kn_ref[:, h * LANES:(h + 1) * LANES]
        ko_ref[:, lo + LANES:lo + 2 * LANES] = kpe


def _rope_assemble(q_raw, kv_raw, tail, cos_t, sin_t, *, tm):
    s_len = q_raw.shape[0]
    tm = min(tm, s_len)
    nope_w = A_HEADS * A_NOPE_DIM
    wide = A_HEADS * A_HEAD_PAD
    return pl.pallas_call(
        _rope_kernel,
        grid=(s_len // tm,),
        in_specs=[pl.BlockSpec((tm, 3 * nope_w), lambda i: (i, 0)),
                  pl.BlockSpec((tm, nope_w), lambda i: (i, 0)),
                  pl.BlockSpec((tm, nope_w), lambda i: (i, 1)),
                  pl.BlockSpec((tm, 2 * LANES), lambda i: (i, 0)),
                  pl.BlockSpec((tm, LANES), lambda i: (i, 0)),
                  pl.BlockSpec((tm, LANES), lambda i: (i, 0))],
        out_specs=[pl.BlockSpec((tm, wide), lambda i: (i, 0))] * 3,
        out_shape=[jax.ShapeDtypeStruct((s_len, wide), BF16)] * 3,
        compiler_params=_params(("parallel",)),
        name="rope_assemble",
    )(q_raw, kv_raw, kv_raw, tail, cos_t, sin_t)


def _attn_kernel(q_ref, k_ref, v_ref, o_ref, m_scr, acc_scr, s_scr, *, chunk):
    qi = pl.program_id(1)
    tq = q_ref.shape[0]
    m_scr[...] = jnp.full_like(m_scr, NEG_BIG)
    acc_scr[...] = jnp.zeros_like(acc_scr)

    def scores(slot, blk):
        start = pl.multiple_of(blk * tq, tq)
        s_scr[slot] = lax.dot_general(q_ref[...], k_ref[pl.ds(start, tq), :], (((1,), (1,)), ((), ())),
                                      preferred_element_type=F32)

    def consume(slot, blk, masked):
        start = pl.multiple_of(blk * tq, tq)
        s = s_scr[slot]
        if masked:
            rq = lax.broadcasted_iota(I32, (tq, tq), 0) // chunk
            ck = lax.broadcasted_iota(I32, (tq, tq), 1) // chunk
            s = jnp.where(ck <= rq, s, NEG_BIG)
        m_prev = m_scr[...]
        m_new = jnp.maximum(m_prev, jnp.max(s, axis=1, keepdims=True))
        pr = jnp.exp(s - m_new).astype(BF16)
        acc_scr[...] = (jnp.exp(m_prev - m_new) * acc_scr[...]
                        + jnp.dot(pr, v_ref[pl.ds(start, tq), :], preferred_element_type=F32))
        m_scr[...] = m_new

    scores(0, 0)

    def pair(t, carry):
        scores(1, 2 * t + 1)
        consume(0, 2 * t, False)
        scores(0, 2 * t + 2)
        consume(1, 2 * t + 1, False)
        return carry
    lax.fori_loop(0, qi // 2, pair, 0)

    @pl.when(qi % 2 == 1)
    def _():
        scores(1, qi)
        consume(0, qi - 1, False)
        consume(1, qi, True)

    @pl.when(qi % 2 == 0)
    def _():
        consume(0, qi, True)

    acc = acc_scr[...]
    o_ref[...] = (acc[:, :A_V_DIM] / acc[:, A_V_DIM:A_V_DIM + 1]).astype(o_ref.dtype)


def _attention(q_cat, k_cat, v_cat, *, tq, chunk):
    s_len = q_cat.shape[0]
    tq = min(tq, s_len)
    return pl.pallas_call(
        functools.partial(_attn_kernel, chunk=chunk),
        grid=(A_HEADS, s_len // tq),
        in_specs=[pl.BlockSpec((tq, A_HEAD_PAD), lambda h, i: (i, h)),
                  pl.BlockSpec((s_len, A_HEAD_PAD), lambda h, i: (0, h)),
                  pl.BlockSpec((s_len, A_HEAD_PAD), lambda h, i: (0, h))],
        out_specs=pl.BlockSpec((tq, A_V_DIM), lambda h, i: (i, h)),
        out_shape=jax.ShapeDtypeStruct((s_len, A_HEADS * A_V_DIM), BF16),
        scratch_shapes=[pltpu.VMEM((tq, 1), F32), pltpu.VMEM((tq, A_HEAD_PAD), F32),
                        pltpu.VMEM((2, tq, tq), F32)],
        compiler_params=_params(("parallel", "arbitrary")),
        name="mla_attention",
    )(q_cat, k_cat, v_cat)


def _route_t(logits, bias_col):
    n = logits.shape[1]
    scores = jax.nn.sigmoid(logits)
    biased = scores + bias_col
    sub = lax.broadcasted_iota(I32, (GROUP_SIZE, n), 0)
    rows = []
    for g in range(N_GROUPS):
        x = biased[g * GROUP_SIZE:(g + 1) * GROUP_SIZE, :]
        m1 = jnp.max(x, axis=0, keepdims=True)
        i1 = jnp.min(jnp.where(x == m1, sub, GROUP_SIZE), axis=0, keepdims=True)
        m2 = jnp.max(jnp.where(sub == i1, -jnp.inf, x), axis=0, keepdims=True)
        rows.append(m1 + m2)
    gscore = jnp.concatenate(rows, axis=0)
    gio = lax.broadcasted_iota(I32, (N_GROUPS, n), 0)
    grank = jnp.zeros((N_GROUPS, n), I32)
    for g in range(N_GROUPS):
        r = gscore[g:g + 1, :]
        grank = grank + jnp.where(gio > g, jnp.where(r >= gscore, 1, 0), jnp.where(r > gscore, 1, 0))
    gsel = grank < TOPK_GROUPS
    masked = jnp.concatenate(
        [jnp.where(gsel[g:g + 1, :], biased[g * GROUP_SIZE:(g + 1) * GROUP_SIZE, :], -jnp.inf)
         for g in range(N_GROUPS)], axis=0)
    eio = lax.broadcasted_iota(I32, (N_EXPERTS, n), 0)
    rank = jnp.zeros((N_EXPERTS, n), I32)
    for e in range(N_EXPERTS):
        r = masked[e:e + 1, :]
        rank = rank + jnp.where(eio > e, jnp.where(r >= masked, 1, 0), jnp.where(r > masked, 1, 0))
    sel = rank < TOP_K
    denom = jnp.sum(jnp.where(sel, scores, 0.0), axis=0, keepdims=True)
    wnorm = scores / denom * ROUTED_SCALE
    eio_f = eio.astype(F32)
    ids, wts = [], []
    for k in range(TOP_K):
        hit = rank == k
        ids.append(jnp.sum(jnp.where(hit, eio_f, 0.0), axis=0, keepdims=True))
        wts.append(jnp.sum(jnp.where(hit, wnorm, 0.0), axis=0, keepdims=True))
    return jnp.concatenate(ids, axis=0).astype(I32), jnp.concatenate(wts, axis=0)


def _mid_kernel(y_ref, x_ref, gt_ref, gpost_ref, gs_ref, sh_ref, wr_ref, br_ref,
                x1_ref, hp_ref, idx_ref, wts_ref, h_scr, *, rc):
    tm, d = x_ref.shape
    half = d // 2

    def body(r):
        y = y_ref[pl.ds(r, rc), :].astype(F32)
        yn = y * lax.rsqrt(jnp.mean(y * y, axis=-1, keepdims=True) + NORM_EPS) * gpost_ref[...]
        x1 = x_ref[pl.ds(r, rc), :] + gt_ref[...] * yn
        x1_ref[pl.ds(r, rc), :] = x1
        hn = x1 * lax.rsqrt(jnp.mean(x1 * x1, axis=-1, keepdims=True) + NORM_EPS)
        h = hn * gs_ref[...] + sh_ref[...]
        h_scr[pl.ds(r, rc), :] = h
        hp_ref[pl.ds(r, rc), :] = _pack_pair(h[:, :half], h[:, half:])
    _row_loop(tm, rc, body)

    logits = lax.dot_general(wr_ref[...], h_scr[...], (((1,), (1,)), ((), ())),
                             preferred_element_type=F32, precision=lax.Precision.HIGHEST)
    ids, wts = _route_t(logits, br_ref[...])
    idx_ref[...] = ids
    wts_ref[...] = wts


def _mid(y, x, gt1, g_post, gs2, sh2, w_router, b_router, *, tm):
    s_len, d = x.shape
    tm = min(tm, s_len)
    vec = lambda a: a.reshape(1, d).astype(F32)
    return pl.pallas_call(
        functools.partial(_mid_kernel, rc=16),
        grid=(s_len // tm,),
        in_specs=[pl.BlockSpec((tm, d), lambda i: (i, 0)),
                  pl.BlockSpec((tm, d), lambda i: (i, 0)),
                  pl.BlockSpec((1, d), lambda i: (0, 0)),
                  pl.BlockSpec((1, d), lambda i: (0, 0)),
                  pl.BlockSpec((1, d), lambda i: (0, 0)),
                  pl.BlockSpec((1, d), lambda i: (0, 0)),
                  pl.BlockSpec((N_EXPERTS, d), lambda i: (0, 0)),
                  pl.BlockSpec((N_EXPERTS, 1), lambda i: (0, 0))],
        out_specs=[pl.BlockSpec((tm, d), lambda i: (i, 0)),
                   pl.BlockSpec((tm, d // 2), lambda i: (i, 0)),
                   pl.BlockSpec((TOP_K, tm), lambda i: (0, i)),
                   pl.BlockSpec((TOP_K, tm), lambda i: (0, i))],
        out_shape=[jax.ShapeDtypeStruct((s_len, d), F32),
                   jax.ShapeDtypeStruct((s_len, d // 2), U32),
                   jax.ShapeDtypeStruct((TOP_K, s_len), I32),
                   jax.ShapeDtypeStruct((TOP_K, s_len), F32)],
        scratch_shapes=[pltpu.VMEM((tm, d), F32)],
        compiler_params=_params(("parallel",)),
        name="mid_norm_route",
    )(y, x, vec(gt1), vec(g_post), vec(gs2), vec(sh2), w_router.T.astype(F32), b_router.reshape(N_EXPERTS, 1))


def _row_copy(src_hbm, dst_vmem, sem, src_row, dst_row):
    return pltpu.make_async_copy(src_hbm.at[pl.ds(src_row, 1), :], dst_vmem.at[pl.ds(dst_row, 1), :], sem)


ROW_GROUP = 8


def _expert_kernel(be_ref, nb_ref, nxt_ref, tok_ref, h_hbm, wg_hbm, wu_hbm, wd_hbm, sg_hbm, su_hbm, sd_hbm,
                   o_ref, stage_g, stage_u, stage_d, wb_g, wb_u, wb_d, xbuf, sems, xsems, *, rc):
    b = pl.program_id(0)
    e = be_ref[b]
    tmb = xbuf.shape[1]
    cur = lax.rem(b, 2)
    stages = (stage_g, stage_u, stage_d)
    routed = (wg_hbm, wu_hbm, wd_hbm)
    shared = (sg_hbm, su_hbm, sd_hbm)

    def gather_loop(blk, slot):
        base = blk * tmb

        def step(gi, carry):
            r0 = gi * ROW_GROUP
            for u in range(ROW_GROUP):
                _row_copy(h_hbm, xbuf.at[slot], xsems.at[slot], tok_ref[base + r0 + u], r0 + u).start(priority=u % 2)
            return carry
        lax.fori_loop(0, tmb // ROW_GROUP, step, 0)

    def gather_unrolled(blk, slot):
        base = blk * tmb
        for r in range(tmb):
            _row_copy(h_hbm, xbuf.at[slot], xsems.at[slot], tok_ref[base + r], r).start(priority=r % 2)

    def gather_wait(slot):
        pltpu.make_async_copy(h_hbm.at[pl.ds(0, tmb), :], xbuf.at[slot], xsems.at[slot]).wait()

    def fetch(ex):
        @pl.when(ex < N_EXPERTS)
        def _():
            for i in range(3):
                pltpu.make_async_copy(routed[i].at[ex], stages[i], sems.at[i]).start()

        @pl.when(ex == N_EXPERTS)
        def _():
            for i in range(3):
                pltpu.make_async_copy(shared[i], stages[i], sems.at[i]).start()

    @pl.when(b < nb_ref[0])
    def _():
        @pl.when(b == 0)
        def _():
            fetch(e)
            gather_loop(0, 0)

        is_first = jnp.logical_or(b == 0, be_ref[jnp.maximum(b - 1, 0)] != e)

        @pl.when(is_first)
        def _():
            for i in range(3):
                pltpu.make_async_copy(shared[i], stages[i], sems.at[i]).wait()
            for src, dst in ((stage_g, wb_g), (stage_u, wb_u), (stage_d, wb_d)):
                def cast(r, src=src, dst=dst):
                    dst[pl.ds(r, rc), :] = src[pl.ds(r, rc), :].astype(BF16)
                _row_loop(src.shape[0], rc, cast)
            nxt = nxt_ref[e]

            @pl.when(nxt >= 0)
            def _():
                fetch(nxt)

        gather_wait(cur)
        gather_unrolled(jnp.minimum(b + 1, nb_ref[0] - 1), 1 - cur)
        lo, hi = _unpack_pair(xbuf[cur])
        x = jnp.concatenate([lo, hi], axis=1).astype(BF16)
        g = jnp.dot(x, wb_g[...], preferred_element_type=F32)
        u = jnp.dot(x, wb_u[...], preferred_element_type=F32)
        a = (g * jax.nn.sigmoid(g) * u).astype(BF16)
        y = jnp.dot(a, wb_d[...], preferred_element_type=F32)
        half = y.shape[1] // 2
        o_ref[...] = _pack_pair(y[:, :half], y[:, half:])

        @pl.when(b == nb_ref[0] - 1)
        def _():
            gather_wait(1 - cur)

    @pl.when(b >= nb_ref[0])
    def _():
        o_ref[...] = jnp.zeros_like(o_ref)


def _experts(block_e, n_used, next_e, slot_tok, hp, wg, wu, wd, sg, su, sd, *, tmb):
    n_slots = slot_tok.shape[0]
    wp = hp.shape[1]
    d, ff = wg.shape[1], wg.shape[2]
    nb = n_slots // tmb
    grid_spec = pltpu.PrefetchScalarGridSpec(
        num_scalar_prefetch=4,
        grid=(nb,),
        in_specs=[pl.BlockSpec(memory_space=pl.ANY)] * 7,
        out_specs=pl.BlockSpec((tmb, wp), lambda b, *_: (b, 0)),
        scratch_shapes=[pltpu.VMEM((d, ff), F32), pltpu.VMEM((d, ff), F32), pltpu.VMEM((ff, d), F32),
                        pltpu.VMEM((d, ff), BF16), pltpu.VMEM((d, ff), BF16), pltpu.VMEM((ff, d), BF16),
                        pltpu.VMEM((2, tmb, wp), U32),
                        pltpu.SemaphoreType.DMA((3,)), pltpu.SemaphoreType.DMA((2,))],
    )
    return pl.pallas_call(
        functools.partial(_expert_kernel, rc=128),
        grid_spec=grid_spec,
        out_shape=jax.ShapeDtypeStruct((n_slots, wp), U32),
        compiler_params=_params(("arbitrary",), vmem=60 * 1024 * 1024),
        name="moe_experts",
    )(block_e, n_used, next_e, slot_tok, hp, wg, wu, wd, sg, su, sd)


def _combine_kernel(slot_ref, w_ref, x1_ref, gt_ref, g_ref, ys_hbm, o_ref, buf, sems, *, rc, n_k):
    tc, d = x1_ref.shape
    half = d // 2
    i = pl.program_id(0)
    cur = lax.rem(i, 2)
    last = pl.num_programs(0) - 1

    def issue_rows(tile, slot, t0):
        base = tile * (tc * n_k)
        for t in range(rc):
            for k in range(n_k):
                _row_copy(ys_hbm, buf.at[slot, k], sems.at[slot], slot_ref[base + (t0 + t) * n_k + k],
                          t0 + t).start(priority=k % 2)

    def wait_planes(slot):
        for k in range(n_k):
            pltpu.make_async_copy(ys_hbm.at[pl.ds(0, tc), :], buf.at[slot, k], sems.at[slot]).wait()

    @pl.when(i == 0)
    def _():
        _row_loop(tc, rc, lambda r: issue_rows(0, 0, r))

    wait_planes(cur)
    nxt = jnp.minimum(i + 1, last)

    def body(r):
        issue_rows(nxt, 1 - cur, r)
        w = w_ref[pl.ds(r, rc), :]
        lo = jnp.zeros((rc, half), F32)
        hi = jnp.zeros((rc, half), F32)
        for k in range(n_k):
            a, b = _unpack_pair(buf[cur, k, pl.ds(r, rc), :])
            lo = lo + w[:, k:k + 1] * a
            hi = hi + w[:, k:k + 1] * b
        ms = (jnp.sum(lo * lo, axis=-1, keepdims=True) + jnp.sum(hi * hi, axis=-1, keepdims=True)) * (1.0 / d)
        rs = lax.rsqrt(ms + NORM_EPS)
        o_ref[pl.ds(r, rc), 0:half] = (x1_ref[pl.ds(r, rc), 0:half]
                                       + gt_ref[:, 0:half] * (lo * rs * g_ref[:, 0:half]))
        o_ref[pl.ds(r, rc), half:d] = (x1_ref[pl.ds(r, rc), half:d]
                                       + gt_ref[:, half:d] * (hi * rs * g_ref[:, half:d]))
    _row_loop(tc, rc, body)

    @pl.when(i == last)
    def _():
        wait_planes(1 - cur)


def _combine(slots, w_tok, x1, gt2, g_post, ys, *, tc):
    s_len, d = x1.shape
    n_k = slots.shape[0] // s_len
    tc = min(tc, s_len)
    grid_spec = pltpu.PrefetchScalarGridSpec(
        num_scalar_prefetch=1,
        grid=(s_len // tc,),
        in_specs=[pl.BlockSpec((tc, w_tok.shape[1]), lambda i, sl: (i, 0)),
                  pl.BlockSpec((tc, d), lambda i, sl: (i, 0)),
                  pl.BlockSpec((1, d), lambda i, sl: (0, 0)),
                  pl.BlockSpec((1, d), lambda i, sl: (0, 0)),
                  pl.BlockSpec(memory_space=pl.ANY)],
        out_specs=pl.BlockSpec((tc, d), lambda i, sl: (i, 0)),
        scratch_shapes=[pltpu.VMEM((2, n_k, tc, d // 2), U32), pltpu.SemaphoreType.DMA((2,))],
    )
    return pl.pallas_call(
        functools.partial(_combine_kernel, rc=8, n_k=n_k),
        grid_spec=grid_spec,
        out_shape=jax.ShapeDtypeStruct((s_len, d), F32),
        compiler_params=_params(("arbitrary",)),
        name="moe_combine",
    )(slots, w_tok, x1, gt2.reshape(1, d), g_post.reshape(1, d), ys)


def _xp_gather_kernel(tok_ref, h_hbm, o_ref, buf, sems, *, g, ring):
    i = pl.program_id(0)
    n = pl.num_programs(0)

    def issue(step, slot):
        base = step * g

        def body(gi, carry):
            r0 = gi * ROW_GROUP
            for u in range(ROW_GROUP):
                _row_copy(h_hbm, buf.at[slot], sems.at[slot], tok_ref[base + r0 + u], r0 + u).start(priority=u % 2)
            return carry
        lax.fori_loop(0, g // ROW_GROUP, body, 0)

    @pl.when(i == 0)
    def _():
        for s in range(ring - 1):
            issue(jnp.minimum(s, n - 1), s)

    issue(jnp.minimum(i + ring - 1, n - 1), lax.rem(i + ring - 1, ring))
    cur = lax.rem(i, ring)
    pltpu.make_async_copy(h_hbm.at[pl.ds(0, g), :], buf.at[cur], sems.at[cur]).wait()

    @pl.when(i == n - 1)
    def _():
        for s in range(1, ring):
            slot = lax.rem(i + s, ring)
            pltpu.make_async_copy(h_hbm.at[pl.ds(0, g), :], buf.at[slot], sems.at[slot]).wait()
        o_ref[...] = buf[0, 0:8, 0:LANES]


def _xp_gather(tok, hp, *, g, ring, name):
    n = tok.shape[0] // g
    grid_spec = pltpu.PrefetchScalarGridSpec(
        num_scalar_prefetch=1, grid=(n,),
        in_specs=[pl.BlockSpec(memory_space=pl.ANY)],
        out_specs=pl.BlockSpec((8, LANES), lambda i, t: (0, 0)),
        scratch_shapes=[pltpu.VMEM((ring, g, hp.shape[1]), hp.dtype), pltpu.SemaphoreType.DMA((ring,))])
    return pl.pallas_call(
        functools.partial(_xp_gather_kernel, g=g, ring=ring), grid_spec=grid_spec,
        out_shape=jax.ShapeDtypeStruct((8, LANES), hp.dtype),
        compiler_params=pltpu.CompilerParams(dimension_semantics=("arbitrary",), vmem_limit_bytes=VMEM_LIMIT,
                                             has_side_effects=True),
        name=name)(tok, hp)


def _moe_plan(idx_t, wts_t, *, tmb):
    n_k, n_tok = idx_t.shape
    n_e = N_EXPERTS + 1
    eid = jnp.concatenate([idx_t, jnp.full((1, n_tok), N_EXPERTS, I32)], axis=0)
    wts = jnp.concatenate([wts_t, jnp.ones((1, n_tok), F32)], axis=0)
    sel = (eid[:, None, :] == jnp.arange(n_e, dtype=I32)[None, :, None]).any(axis=0).astype(I32)
    csum = jnp.cumsum(sel, axis=1)
    counts = csum[:, -1]
    padded = (counts + tmb - 1) // tmb * tmb
    ends = jnp.cumsum(padded)
    starts = ends - padded
    slot_dense = starts[:, None] + csum - 1
    slot = jnp.take_along_axis(slot_dense, eid, axis=0)
    n_slots = -(-(n_tok * (n_k + 1) + n_e * (tmb - 1)) // tmb) * tmb
    tok = jnp.broadcast_to(jnp.arange(n_tok, dtype=I32)[None, :], slot.shape)
    slot_tok = jnp.zeros((n_slots,), I32).at[slot.reshape(-1)].set(tok.reshape(-1))
    n_used = (ends[-1] // tmb).astype(I32).reshape(1)
    block_start = jnp.arange(n_slots // tmb, dtype=I32) * tmb
    block_e = jnp.minimum(jnp.searchsorted(ends, block_start, side='right'), n_e - 1).astype(I32)
    slots_tok_major = slot.T.reshape(-1)
    w_tok = jnp.zeros((n_tok, 16), F32).at[:, :n_k + 1].set(wts.T)
    owner = jnp.where(padded > 0, jnp.arange(n_e, dtype=I32), n_e)
    later = jnp.concatenate([lax.cummin(owner[::-1])[::-1][1:], jnp.full((1,), n_e, I32)])
    next_e = jnp.where(later >= n_e, -1, later).astype(I32)
    return slot_tok, block_e, n_used, next_e, slots_tok_major, w_tok


def _in_proj_weights(w_in):
    d = w_in.shape[0]
    w_main = jnp.pad(w_in.astype(BF16), ((0, 0), (0, N_IN_PAD - N_IN)))
    kr = w_in[:, COL_KR:COL_KR + A_ROPE_DIM]
    gates = w_in[:, COL_IG:COL_IG + 2 * M_HEADS]
    half = A_ROPE_DIM // 2
    kswap = jnp.concatenate([-kr[:, half:], kr[:, :half]], axis=1)
    z = lambda n: jnp.zeros((d, n), w_in.dtype)
    w_tail = jnp.concatenate([kswap, gates, z(LANES - A_ROPE_DIM - 2 * M_HEADS), kr, z(LANES - A_ROPE_DIM)], axis=1)
    return w_main, w_tail.astype(BF16)


def _q_up_weight(w_uq):
    r = w_uq.shape[0]
    w = w_uq.reshape(r, A_HEADS, A_NOPE_DIM + A_ROPE_DIM)
    nope = w[:, :, :A_NOPE_DIM]
    rope = w[:, :, A_NOPE_DIM:]
    half = A_ROPE_DIM // 2
    swap = jnp.concatenate([-rope[:, :, half:], rope[:, :, :half]], axis=2)
    pad = jnp.zeros((r, A_HEADS, LANES - A_ROPE_DIM), w_uq.dtype)
    rope_p = jnp.concatenate([rope, pad], axis=2)
    swap_p = jnp.concatenate([swap, pad], axis=2)
    flat = lambda a: a.reshape(r, -1)
    return jnp.concatenate([flat(nope), flat(rope_p), flat(swap_p)], axis=1).astype(BF16)


def _kv_up_weight(w_ukv):
    r = w_ukv.shape[0]
    w = w_ukv.reshape(r, A_HEADS, A_NOPE_DIM + A_V_DIM)
    return jnp.concatenate([w[:, :, :A_NOPE_DIM].reshape(r, -1), w[:, :, A_NOPE_DIM:].reshape(r, -1)],
                           axis=1).astype(BF16)


def _rope_tables(s_len):
    pos = jnp.arange(s_len, dtype=F32)
    inv_freq = 1.0 / (ROPE_THETA ** (jnp.arange(0, A_ROPE_DIM, 2, dtype=F32) / A_ROPE_DIM))
    ang = pos[:, None] * inv_freq[None, :]
    pad = jnp.zeros((s_len, LANES - A_ROPE_DIM), F32)
    cos_t = jnp.concatenate([jnp.cos(ang), jnp.cos(ang), pad], axis=1)
    sin_t = jnp.concatenate([jnp.sin(ang), jnp.sin(ang), pad], axis=1)
    return cos_t, sin_t


def _block(x, c, w_ada, b_ada, g_pre_mix, g_post_mix, w_in, conv_w, conv_b, b_igate, b_fgate, g_mlstm_out,
           g_q_norm, w_uq, g_kv_norm, w_ukv, w_out, g_pre_ffn, g_post_ffn, w_router, b_router,
           w_gate, w_up, w_down, w_shared_gate, w_shared_up, w_shared_down):
    s_len, d = x.shape
    mod = _adaln(c, w_ada, b_ada)[0]
    sh1, sc1, gt1, sh2, sc2, gt2 = [mod[i * d:(i + 1) * d] for i in range(6)]

    w_main, w_tail = _in_proj_weights(w_in)
    proj, tail = _norm_mm(x, g_pre_mix * (1.0 + sc1), sh1, w_main, w_tail, tm=512, tn=IN_TILE)
    h_m = _mlstm(proj, tail, conv_w, conv_b, b_igate, b_fgate, g_mlstm_out, chunk=min(M_CHUNK, s_len))
    scale = (A_NOPE_DIM + A_ROPE_DIM) ** -0.5
    zq = jnp.zeros((A_Q_RANK,), F32)
    zkv = jnp.zeros((A_KV_RANK,), F32)
    q_raw = _norm_mm(proj[:, COL_CQ:COL_CQ + A_Q_RANK], g_q_norm * scale, zq, _q_up_weight(w_uq), tm=1024, tn=1536)
    kv_raw = _norm_mm(proj[:, COL_CKV:COL_CKV + A_KV_RANK], g_kv_norm, zkv, _kv_up_weight(w_ukv), tm=1024, tn=1024)
    cos_t, sin_t = _rope_tables(s_len)
    q_cat, k_cat, v_cat = _rope_assemble(q_raw, kv_raw, tail, cos_t, sin_t, tm=256)
    h_a = _attention(q_cat, k_cat, v_cat, tq=1024, chunk=CHUNK)
    y = _mm2(h_m, h_a, w_out.astype(BF16), tm=1024, tn=1024)

    x1, hp, idx_t, wts_t = _mid(y, x, gt1, g_post_mix, g_pre_ffn * (1.0 + sc2), sh2, w_router, b_router, tm=256)
    tmb = 256
    slot_tok, block_e, n_used, next_e, slots, w_tok = _moe_plan(idx_t, wts_t, tmb=tmb)
    ys = _experts(block_e, n_used, next_e, slot_tok, hp, w_gate, w_up, w_down,
                  w_shared_gate, w_shared_up, w_shared_down, tmb=tmb)
    out = _combine(slots, w_tok, x1, gt2, g_post_ffn, ys, tc=128)
    tok_rand = lax.rem(slot_tok * 40503 + jnp.arange(slot_tok.shape[0], dtype=I32), s_len)
    probes = [_xp_gather(slot_tok, hp, g=256, ring=1, name="xp_g256_r1"),
              _xp_gather(slot_tok, hp, g=256, ring=2, name="xp_g256_r2"),
              _xp_gather(slot_tok, hp, g=256, ring=4, name="xp_g256_r4"),
              _xp_gather(slot_tok[:88 * 1024], hp, g=1024, ring=2, name="xp_g1024_r2"),
              _xp_gather(tok_rand, hp, g=256, ring=2, name="xp_g256_r2_rand")]
    keep = probes[0][0:1, 0:1]
    for p in probes[1:]:
        keep = keep & p[0:1, 0:1]
    return out + (keep & jnp.uint32(0)).astype(F32)


def kernel(x, c, w_ada, b_ada, g_pre_mix, g_post_mix, w_in, conv_w, conv_b, b_igate, b_fgate, g_mlstm_out,
           g_q_norm, w_uq, g_kv_norm, w_ukv, w_out, g_pre_ffn, g_post_ffn, w_router, b_router,
           w_gate, w_up, w_down, w_shared_gate, w_shared_up, w_shared_down):
    assert x.shape[0] == 1 and w_ada.shape[0] == 1, "single sequence, single layer"
    layer = (w_ada, b_ada, g_pre_mix, g_post_mix, w_in, conv_w, conv_b, b_igate, b_fgate, g_mlstm_out,
             g_q_norm, w_uq, g_kv_norm, w_ukv, w_out, g_pre_ffn, g_post_ffn, w_router, b_router,
             w_gate, w_up, w_down, w_shared_gate, w_shared_up, w_shared_down)
    out = _block(x[0], c[0], *[p[0] for p in layer])
    return out[None]
```

```python
import functools
import math

import jax
import jax.numpy as jnp
from jax import lax
from jax.experimental import pallas as pl
from jax.experimental.pallas import tpu as pltpu

F32 = jnp.float32
BF16 = jnp.bfloat16
I32 = jnp.int32
U32 = jnp.uint32

NORM_EPS = 1e-6
CHUNK = 64
M_CHUNK = 128

M_HEADS = 4
M_QK_DIM = 256
M_V_DIM = 512
M_WIDTH = M_HEADS * M_V_DIM
M_QK_WIDTH = 2 * M_HEADS * M_QK_DIM
CONV_WIDTH = 4
GATE_SOFTCAP = 15.0

A_HEADS = 16
A_NOPE_DIM = 128
A_ROPE_DIM = 64
A_V_DIM = 128
A_Q_RANK = 768
A_KV_RANK = 512
A_HEAD_PAD = 256
ROPE_THETA = 10000.0

N_EXPERTS = 64
TOP_K = 8
N_GROUPS = 8
GROUP_SIZE = N_EXPERTS // N_GROUPS
TOPK_GROUPS = 4
ROUTED_SCALE = 2.5

LANES = 128
VMEM_LIMIT = 56 * 1024 * 1024
NEG_BIG = -1e30

COL_QK = 0
COL_V = COL_QK + M_QK_WIDTH
COL_O = COL_V + M_WIDTH
COL_IG = COL_O + M_WIDTH
COL_FG = COL_IG + M_HEADS
COL_CQ = COL_FG + M_HEADS
COL_CKV = COL_CQ + A_Q_RANK
COL_KR = COL_CKV + A_KV_RANK
N_IN = COL_KR + A_ROPE_DIM
IN_TILE = 768
N_IN_PAD = -(-N_IN // IN_TILE) * IN_TILE
TAIL_W = 2 * LANES
TAIL_GATE_LANE = A_ROPE_DIM


def _params(sem, vmem=VMEM_LIMIT):
    return pltpu.CompilerParams(dimension_semantics=sem, vmem_limit_bytes=vmem)


def _row_loop(n_rows, rc, body):
    def step(i, carry):
        body(pl.multiple_of(i * rc, rc))
        return carry
    lax.fori_loop(0, n_rows // rc, step, 0)


def _pack_pair(a, b):
    lo = lax.bitcast_convert_type(a.astype(BF16).astype(F32), U32) >> 16
    hi = lax.bitcast_convert_type(b.astype(BF16).astype(F32), U32) & jnp.uint32(0xFFFF0000)
    return lo | hi


def _unpack_pair(u):
    lo = lax.bitcast_convert_type(u << 16, F32)
    hi = lax.bitcast_convert_type(u & jnp.uint32(0xFFFF0000), F32)
    return lo, hi


def _adaln_kernel(c_ref, w_ref, b_ref, o_ref, *, rc):
    d, tn = w_ref.shape
    nl = tn // LANES

    def step(i, accs):
        r = pl.multiple_of(i * rc, rc)
        c = c_ref[pl.ds(r, rc), :]
        ca = c * jax.nn.sigmoid(c)
        out = []
        for j in range(nl):
            prod = w_ref[pl.ds(r, rc), j * LANES:(j + 1) * LANES] * ca
            out.append(accs[j] + jnp.sum(prod.reshape(rc // 8, 8, LANES), axis=0))
        return tuple(out)

    accs = lax.fori_loop(0, d // rc, step, tuple(jnp.zeros((8, LANES), F32) for _ in range(nl)))
    for j in range(nl):
        o_ref[:, j * LANES:(j + 1) * LANES] = (
            jnp.sum(accs[j], axis=0, keepdims=True) + b_ref[:, j * LANES:(j + 1) * LANES])


def _adaln(c, w_ada, b_ada):
    d, n = w_ada.shape
    tn = 512
    c_b = jnp.broadcast_to(c.reshape(d, 1), (d, LANES))
    return pl.pallas_call(
        functools.partial(_adaln_kernel, rc=64),
        grid=(n // tn,),
        in_specs=[pl.BlockSpec((d, LANES), lambda j: (0, 0)),
                  pl.BlockSpec((d, tn), lambda j: (0, j)),
                  pl.BlockSpec((1, tn), lambda j: (0, j))],
        out_specs=pl.BlockSpec((1, tn), lambda j: (0, j)),
        out_shape=jax.ShapeDtypeStruct((1, n), F32),
        compiler_params=_params(("arbitrary",)),
        name="adaln",
    )(c_b, w_ada, b_ada.reshape(1, n))


def _norm_mm_kernel(x_ref, gs_ref, sh_ref, w_ref, *rest, rc, has_tail):
    if has_tail:
        wt_ref, o_ref, t_ref, h_scr = rest
    else:
        o_ref, h_scr = rest
    tm = x_ref.shape[0]
    j = pl.program_id(1)

    @pl.when(j == 0)
    def _():
        def body(r):
            x = x_ref[pl.ds(r, rc), :].astype(F32)
            ms = jnp.mean(x * x, axis=-1, keepdims=True)
            y = x * lax.rsqrt(ms + NORM_EPS)
            h_scr[pl.ds(r, rc), :] = (y * gs_ref[...] + sh_ref[...]).astype(BF16)
        _row_loop(tm, rc, body)

    o_ref[...] = jnp.dot(h_scr[...], w_ref[...], preferred_element_type=F32).astype(o_ref.dtype)
    if has_tail:
        @pl.when(j == pl.num_programs(1) - 1)
        def _():
            t_ref[...] = jnp.dot(h_scr[...], wt_ref[...], preferred_element_type=F32)


def _norm_mm(x, gs, sh, w, w_tail=None, *, tm, tn):
    m, k = x.shape
    n = w.shape[1]
    tm = min(tm, m)
    in_specs = [pl.BlockSpec((tm, k), lambda i, j: (i, 0)),
                pl.BlockSpec((1, k), lambda i, j: (0, 0)),
                pl.BlockSpec((1, k), lambda i, j: (0, 0)),
                pl.BlockSpec((k, tn), lambda i, j: (0, j))]
    operands = [x, gs.reshape(1, k), sh.reshape(1, k), w]
    out_shape = [jax.ShapeDtypeStruct((m, n), BF16)]
    out_specs = [pl.BlockSpec((tm, tn), lambda i, j: (i, j))]
    if w_tail is not None:
        nt = w_tail.shape[1]
        in_specs.append(pl.BlockSpec((k, nt), lambda i, j: (0, 0)))
        operands.append(w_tail)
        out_shape.append(jax.ShapeDtypeStruct((m, nt), F32))
        out_specs.append(pl.BlockSpec((tm, nt), lambda i, j: (i, 0)))
    res = pl.pallas_call(
        functools.partial(_norm_mm_kernel, rc=32, has_tail=w_tail is not None),
        grid=(m // tm, n // tn),
        in_specs=in_specs,
        out_specs=out_specs,
        out_shape=out_shape,
        scratch_shapes=[pltpu.VMEM((tm, k), BF16)],
        compiler_params=_params(("parallel", "arbitrary")),
        name="norm_mm",
    )(*operands)
    return res if w_tail is not None else res[0]


def _mm2_kernel(a1_ref, a2_ref, w_ref, o_ref):
    k1 = a1_ref.shape[1]
    acc = jnp.dot(a1_ref[...], w_ref[:k1, :], preferred_element_type=F32)
    acc = acc + jnp.dot(a2_ref[...], w_ref[k1:, :], preferred_element_type=F32)
    o_ref[...] = acc.astype(o_ref.dtype)


def _mm2(a1, a2, w, *, tm, tn):
    m, k1 = a1.shape
    k2 = a2.shape[1]
    n = w.shape[1]
    tm = min(tm, m)
    return pl.pallas_call(
        _mm2_kernel,
        grid=(m // tm, n // tn),
        in_specs=[pl.BlockSpec((tm, k1), lambda i, j: (i, 0)),
                  pl.BlockSpec((tm, k2), lambda i, j: (i, 0)),
                  pl.BlockSpec((k1 + k2, tn), lambda i, j: (0, j))],
        out_specs=pl.BlockSpec((tm, tn), lambda i, j: (i, j)),
        out_shape=jax.ShapeDtypeStruct((m, n), BF16),
        compiler_params=_params(("parallel", "arbitrary")),
        name="out_proj",
    )(a1, a2, w)


def _softcap(z):
    return GATE_SOFTCAP * jnp.tanh(z * (1.0 / GATE_SOFTCAP))


def _log_sigmoid(z):
    return jnp.minimum(z, 0.0) - jnp.log1p(jnp.exp(-jnp.abs(z)))


def _mlstm_kernel(qk_ref, prev_ref, v_ref, o_ref, gcol_ref, grow_ref, cw_ref, cb_ref, bcol_ref, brow_ref,
                  gout_ref, out_ref, c_scr, n_scr, m_scr, u_scr):
    c = pl.program_id(0)
    L = qk_ref.shape[0]
    halo = prev_ref.shape[0]
    dk, dv = M_QK_DIM, M_V_DIM

    @pl.when(c == 0)
    def _():
        c_scr[...] = jnp.zeros_like(c_scr)
        n_scr[...] = jnp.zeros_like(n_scr)
        m_scr[...] = jnp.zeros_like(m_scr)

    prev = prev_ref[...].astype(F32)
    u_scr[0:halo, :] = jnp.where(c == 0, jnp.zeros_like(prev), prev)
    u_scr[halo:halo + L, :] = qk_ref[...].astype(F32)

    def conv_silu(col, width):
        acc = cb_ref[:, col:col + width]
        for j in range(CONV_WIDTH):
            r0 = halo - (CONV_WIDTH - 1) + j
            acc = acc + u_scr[r0:r0 + L, col:col + width] * cw_ref[j:j + 1, col:col + width]
        return acc * jax.nn.sigmoid(acc)

    pre_c = _softcap(gcol_ref[...] + bcol_ref[...])
    lf_c = _log_sigmoid(pre_c)
    pre_r = _softcap(grow_ref[0] + brow_ref[...])
    lf_r = _log_sigmoid(pre_r)
    row = lax.broadcasted_iota(I32, (L, L), 0)
    col = lax.broadcasted_iota(I32, (L, L), 1)
    causal = col <= row
    tril = causal.astype(F32)
    triu = (row <= col).astype(F32)
    bcum_c = jnp.dot(tril, lf_c, preferred_element_type=F32, precision=lax.Precision.HIGHEST)
    bcum_r = jnp.dot(lf_r, triu, preferred_element_type=F32, precision=lax.Precision.HIGHEST)

    for h in range(M_HEADS):
        li_lane = TAIL_GATE_LANE + h
        lf_lane = TAIL_GATE_LANE + M_HEADS + h
        q = (conv_silu(h * dk, dk) * (dk ** -0.5)).astype(BF16)
        kf = conv_silu(M_HEADS * dk + h * dk, dk)
        kb = kf.astype(BF16)
        v = v_ref[:, h * dv:(h + 1) * dv]
        b_c = bcum_c[:, lf_lane:lf_lane + 1]
        li_c = pre_c[:, li_lane:li_lane + 1]
        b_r = bcum_r[M_HEADS + h:M_HEADS + h + 1, :]
        li_r = pre_r[h:h + 1, :]
        m_prev = m_scr[h:h + 1, 0:1]

        dm = jnp.where(causal, b_c - b_r + li_r, NEG_BIG)
        inter = b_c + m_prev
        m_t = jnp.maximum(jnp.max(dm, axis=1, keepdims=True), inter)
        decay = jnp.exp(inter - m_t)
        s = lax.dot_general(q, kb, (((1,), (1,)), ((), ())), preferred_element_type=F32) * jnp.exp(dm - m_t)
        c_state = c_scr[h]
        n_state = n_scr[h:h + 1, :]
        num = jnp.dot(s.astype(BF16), v, preferred_element_type=F32)
        num = num + decay * jnp.dot(q, c_state.astype(BF16), preferred_element_type=F32)
        den = jnp.sum(s, axis=1, keepdims=True) + decay * jnp.sum(q.astype(F32) * n_state, axis=1, keepdims=True)
        hh = num / jnp.maximum(jnp.abs(den), jnp.exp(-m_t))

        b_last = b_c[L - 1:L, :]
        m_new = jnp.maximum(b_last + m_prev, jnp.max(b_last - b_r + li_r, axis=1, keepdims=True))
        carry = jnp.exp(b_last + m_prev - m_new)
        w_c = jnp.exp(b_last - b_c + li_c - m_new)
        kw = kf * w_c
        c_scr[h] = carry * c_state + lax.dot_general(kw.astype(BF16), v, (((0,), (0,)), ((), ())),
                                                     preferred_element_type=F32)
        n_scr[h:h + 1, :] = carry * n_state + jnp.sum(kw, axis=0, keepdims=True)
        m_scr[h:h + 1, :] = jnp.broadcast_to(m_new, (1, LANES))

        hn = hh * lax.rsqrt(jnp.mean(hh * hh, axis=1, keepdims=True) + NORM_EPS) * gout_ref[:, h * dv:(h + 1) * dv]
        gate = jax.nn.sigmoid(o_ref[:, h * dv:(h + 1) * dv].astype(F32))
        out_ref[:, h * dv:(h + 1) * dv] = (hn * gate).astype(out_ref.dtype)


def _mlstm(proj, tail, conv_w, conv_b, b_igate, b_fgate, g_out, *, chunk):
    s_len = proj.shape[0]
    L = chunk
    n_c = s_len // L
    halo = 16
    gates = tail[:, TAIL_GATE_LANE:TAIL_GATE_LANE + 2 * M_HEADS]
    g_row = gates.reshape(n_c, L, 2 * M_HEADS).transpose(0, 2, 1)
    bias = jnp.concatenate([b_igate, b_fgate]).astype(F32)
    b_col = jnp.zeros((1, LANES), F32).at[0, TAIL_GATE_LANE:TAIL_GATE_LANE + 2 * M_HEADS].set(bias)
    b_row = bias.reshape(2 * M_HEADS, 1)
    w = M_QK_WIDTH
    return pl.pallas_call(
        _mlstm_kernel,
        grid=(n_c,),
        in_specs=[pl.BlockSpec((L, w), lambda c: (c, COL_QK // w)),
                  pl.BlockSpec((halo, w), lambda c: (jnp.maximum(c * (L // halo) - 1, 0), COL_QK // w)),
                  pl.BlockSpec((L, M_WIDTH), lambda c: (c, COL_V // M_WIDTH)),
                  pl.BlockSpec((L, M_WIDTH), lambda c: (c, COL_O // M_WIDTH)),
                  pl.BlockSpec((L, LANES), lambda c: (c, 0)),
                  pl.BlockSpec((1, 2 * M_HEADS, L), lambda c: (c, 0, 0)),
                  pl.BlockSpec((CONV_WIDTH, w), lambda c: (0, 0)),
                  pl.BlockSpec((1, w), lambda c: (0, 0)),
                  pl.BlockSpec((1, LANES), lambda c: (0, 0)),
                  pl.BlockSpec((2 * M_HEADS, 1), lambda c: (0, 0)),
                  pl.BlockSpec((1, M_WIDTH), lambda c: (0, 0))],
        out_specs=pl.BlockSpec((L, M_WIDTH), lambda c: (c, 0)),
        out_shape=jax.ShapeDtypeStruct((s_len, M_WIDTH), BF16),
        scratch_shapes=[pltpu.VMEM((M_HEADS, M_QK_DIM, M_V_DIM), F32),
                        pltpu.VMEM((8, M_QK_DIM), F32),
                        pltpu.VMEM((8, LANES), F32),
                        pltpu.VMEM((halo + L, w), F32)],
        compiler_params=_params(("arbitrary",)),
        name="mlstm",
    )(proj, proj, proj, proj, tail, g_row, conv_w, conv_b.reshape(1, w), b_col, b_row, g_out.reshape(1, M_WIDTH))


def _rope_kernel(q_ref, kn_ref, v_ref, tail_ref, cos_ref, sin_ref, qo_ref, ko_ref, vo_ref):
    cos = cos_ref[...]
    sin = sin_ref[...]
    nope_w = A_HEADS * A_NOPE_DIM
    kpe = (tail_ref[:, LANES:2 * LANES] * cos + tail_ref[:, 0:LANES] * sin).astype(BF16)
    lane = lax.broadcasted_iota(I32, (q_ref.shape[0], LANES), 1)
    ones_col = jnp.where(lane == 0, 1.0, 0.0).astype(BF16)
    for h in range(A_HEADS):
        vo_ref[:, h * A_HEAD_PAD:h * A_HEAD_PAD + LANES] = v_ref[:, h * LANES:(h + 1) * LANES]
        vo_ref[:, h * A_HEAD_PAD + LANES:(h + 1) * A_HEAD_PAD] = ones_col
        lo = h * A_HEAD_PAD
        qo_ref[:, lo:lo + LANES] = q_ref[:, h * LANES:(h + 1) * LANES]
        qr = q_ref[:, nope_w + h * LANES:nope_w + (h + 1) * LANES].astype(F32)
        qs = q_ref[:, 2 * nope_w + h * LANES:2 * nope_w + (h + 1) * LANES].astype(F32)
        qo_ref[:, lo + LANES:lo + 2 * LANES] = (qr * cos + qs * sin).astype(BF16)
        ko_ref[:, lo:lo + LANES] = kn_ref[:, h * LANES:(h + 1) * LANES]
        ko_ref[:, lo + LANES:lo + 2 * LANES] = kpe


def _rope_assemble(q_raw, kv_raw, tail, cos_t, sin_t, *, tm):
    s_len = q_raw.shape[0]
    tm = min(tm, s_len)
    nope_w = A_HEADS * A_NOPE_DIM
    wide = A_HEADS * A_HEAD_PAD
    return pl.pallas_call(
        _rope_kernel,
        grid=(s_len // tm,),
        in_specs=[pl.BlockSpec((tm, 3 * nope_w), lambda i: (i, 0)),
                  pl.BlockSpec((tm, nope_w), lambda i: (i, 0)),
                  pl.BlockSpec((tm, nope_w), lambda i: (i, 1)),
                  pl.BlockSpec((tm, 2 * LANES), lambda i: (i, 0)),
                  pl.BlockSpec((tm, LANES), lambda i: (i, 0)),
                  pl.BlockSpec((tm, LANES), lambda i: (i, 0))],
        out_specs=[pl.BlockSpec((tm, wide), lambda i: (i, 0))] * 3,
        out_shape=[jax.ShapeDtypeStruct((s_len, wide), BF16)] * 3,
        compiler_params=_params(("parallel",)),
        name="rope_assemble",
    )(q_raw, kv_raw, kv_raw, tail, cos_t, sin_t)


def _attn_kernel(q_ref, k_ref, v_ref, o_ref, m_scr, acc_scr, s_scr, *, chunk):
    qi = pl.program_id(1)
    tq = q_ref.shape[0]
    m_scr[...] = jnp.full_like(m_scr, NEG_BIG)
    acc_scr[...] = jnp.zeros_like(acc_scr)

    def scores(slot, blk):
        start = pl.multiple_of(blk * tq, tq)
        s_scr[slot] = lax.dot_general(q_ref[...], k_ref[pl.ds(start, tq), :], (((1,), (1,)), ((), ())),
                                      preferred_element_type=F32)

    def consume(slot, blk, masked):
        start = pl.multiple_of(blk * tq, tq)
        s = s_scr[slot]
        if masked:
            rq = lax.broadcasted_iota(I32, (tq, tq), 0) // chunk
            ck = lax.broadcasted_iota(I32, (tq, tq), 1) // chunk
            s = jnp.where(ck <= rq, s, NEG_BIG)
        m_prev = m_scr[...]
        m_new = jnp.maximum(m_prev, jnp.max(s, axis=1, keepdims=True))
        pr = jnp.exp(s - m_new).astype(BF16)
        acc_scr[...] = (jnp.exp(m_prev - m_new) * acc_scr[...]
                        + jnp.dot(pr, v_ref[pl.ds(start, tq), :], preferred_element_type=F32))
        m_scr[...] = m_new

    scores(0, 0)

    def pair(t, carry):
        scores(1, 2 * t + 1)
        consume(0, 2 * t, False)
        scores(0, 2 * t + 2)
        consume(1, 2 * t + 1, False)
        return carry
    lax.fori_loop(0, qi // 2, pair, 0)

    @pl.when(qi % 2 == 1)
    def _():
        scores(1, qi)
        consume(0, qi - 1, False)
        consume(1, qi, True)

    @pl.when(qi % 2 == 0)
    def _():
        consume(0, qi, True)

    acc = acc_scr[...]
    o_ref[...] = (acc[:, :A_V_DIM] / acc[:, A_V_DIM:A_V_DIM + 1]).astype(o_ref.dtype)


def _attention(q_cat, k_cat, v_cat, *, tq, chunk):
    s_len = q_cat.shape[0]
    tq = min(tq, s_len)
    return pl.pallas_call(
        functools.partial(_attn_kernel, chunk=chunk),
        grid=(A_HEADS, s_len // tq),
        in_specs=[pl.BlockSpec((tq, A_HEAD_PAD), lambda h, i: (i, h)),
                  pl.BlockSpec((s_len, A_HEAD_PAD), lambda h, i: (0, h)),
                  pl.BlockSpec((s_len, A_HEAD_PAD), lambda h, i: (0, h))],
        out_specs=pl.BlockSpec((tq, A_V_DIM), lambda h, i: (i, h)),
        out_shape=jax.ShapeDtypeStruct((s_len, A_HEADS * A_V_DIM), BF16),
        scratch_shapes=[pltpu.VMEM((tq, 1), F32), pltpu.VMEM((tq, A_HEAD_PAD), F32),
                        pltpu.VMEM((2, tq, tq), F32)],
        compiler_params=_params(("parallel", "arbitrary")),
        name="mla_attention",
    )(q_cat, k_cat, v_cat)


def _route_t(logits, bias_col):
    n = logits.shape[1]
    scores = jax.nn.sigmoid(logits)
    biased = scores + bias_col
    sub = lax.broadcasted_iota(I32, (GROUP_SIZE, n), 0)
    rows = []
    for g in range(N_GROUPS):
        x = biased[g * GROUP_SIZE:(g + 1) * GROUP_SIZE, :]
        m1 = jnp.max(x, axis=0, keepdims=True)
        i1 = jnp.min(jnp.where(x == m1, sub, GROUP_SIZE), axis=0, keepdims=True)
        m2 = jnp.max(jnp.where(sub == i1, -jnp.inf, x), axis=0, keepdims=True)
        rows.append(m1 + m2)
    gscore = jnp.concatenate(rows, axis=0)
    gio = lax.broadcasted_iota(I32, (N_GROUPS, n), 0)
    grank = jnp.zeros((N_GROUPS, n), I32)
    for g in range(N_GROUPS):
        r = gscore[g:g + 1, :]
        grank = grank + jnp.where(gio > g, jnp.where(r >= gscore, 1, 0), jnp.where(r > gscore, 1, 0))
    gsel = grank < TOPK_GROUPS
    masked = jnp.concatenate(
        [jnp.where(gsel[g:g + 1, :], biased[g * GROUP_SIZE:(g + 1) * GROUP_SIZE, :], -jnp.inf)
         for g in range(N_GROUPS)], axis=0)
    eio = lax.broadcasted_iota(I32, (N_EXPERTS, n), 0)
    rank = jnp.zeros((N_EXPERTS, n), I32)
    for e in range(N_EXPERTS):
        r = masked[e:e + 1, :]
        rank = rank + jnp.where(eio > e, jnp.where(r >= masked, 1, 0), jnp.where(r > masked, 1, 0))
    sel = rank < TOP_K
    denom = jnp.sum(jnp.where(sel, scores, 0.0), axis=0, keepdims=True)
    wnorm = scores / denom * ROUTED_SCALE
    eio_f = eio.astype(F32)
    ids, wts = [], []
    for k in range(TOP_K):
        hit = rank == k
        ids.append(jnp.sum(jnp.where(hit, eio_f, 0.0), axis=0, keepdims=True))
        wts.append(jnp.sum(jnp.where(hit, wnorm, 0.0), axis=0, keepdims=True))
    return jnp.concatenate(ids, axis=0).astype(I32), jnp.concatenate(wts, axis=0)


def _mid_kernel(y_ref, x_ref, gt_ref, gpost_ref, gs_ref, sh_ref, wr_ref, br_ref,
                x1_ref, hp_ref, idx_ref, wts_ref, h_scr, *, rc):
    tm, d = x_ref.shape
    half = d // 2

    def body(r):
        y = y_ref[pl.ds(r, rc), :].astype(F32)
        yn = y * lax.rsqrt(jnp.mean(y * y, axis=-1, keepdims=True) + NORM_EPS) * gpost_ref[...]
        x1 = x_ref[pl.ds(r, rc), :] + gt_ref[...] * yn
        x1_ref[pl.ds(r, rc), :] = x1
        hn = x1 * lax.rsqrt(jnp.mean(x1 * x1, axis=-1, keepdims=True) + NORM_EPS)
        h = hn * gs_ref[...] + sh_ref[...]
        h_scr[pl.ds(r, rc), :] = h
        hp_ref[pl.ds(r, rc), :] = _pack_pair(h[:, :half], h[:, half:])
    _row_loop(tm, rc, body)

    logits = lax.dot_general(wr_ref[...], h_scr[...], (((1,), (1,)), ((), ())),
                             preferred_element_type=F32, precision=lax.Precision.HIGHEST)
    ids, wts = _route_t(logits, br_ref[...])
    idx_ref[...] = ids
    wts_ref[...] = wts


def _mid(y, x, gt1, g_post, gs2, sh2, w_router, b_router, *, tm):
    s_len, d = x.shape
    tm = min(tm, s_len)
    vec = lambda a: a.reshape(1, d).astype(F32)
    return pl.pallas_call(
        functools.partial(_mid_kernel, rc=16),
        grid=(s_len // tm,),
        in_specs=[pl.BlockSpec((tm, d), lambda i: (i, 0)),
                  pl.BlockSpec((tm, d), lambda i: (i, 0)),
                  pl.BlockSpec((1, d), lambda i: (0, 0)),
                  pl.BlockSpec((1, d), lambda i: (0, 0)),
                  pl.BlockSpec((1, d), lambda i: (0, 0)),
                  pl.BlockSpec((1, d), lambda i: (0, 0)),
                  pl.BlockSpec((N_EXPERTS, d), lambda i: (0, 0)),
                  pl.BlockSpec((N_EXPERTS, 1), lambda i: (0, 0))],
        out_specs=[pl.BlockSpec((tm, d), lambda i: (i, 0)),
                   pl.BlockSpec((tm, d // 2), lambda i: (i, 0)),
                   pl.BlockSpec((TOP_K, tm), lambda i: (0, i)),
                   pl.BlockSpec((TOP_K, tm), lambda i: (0, i))],
        out_shape=[jax.ShapeDtypeStruct((s_len, d), F32),
                   jax.ShapeDtypeStruct((s_len, d // 2), U32),
                   jax.ShapeDtypeStruct((TOP_K, s_len), I32),
                   jax.ShapeDtypeStruct((TOP_K, s_len), F32)],
        scratch_shapes=[pltpu.VMEM((tm, d), F32)],
        compiler_params=_params(("parallel",)),
        name="mid_norm_route",
    )(y, x, vec(gt1), vec(g_post), vec(gs2), vec(sh2), w_router.T.astype(F32), b_router.reshape(N_EXPERTS, 1))


def _row_copy(src_hbm, dst_vmem, sem, src_row, dst_row):
    return pltpu.make_async_copy(src_hbm.at[pl.ds(src_row, 1), :], dst_vmem.at[pl.ds(dst_row, 1), :], sem)


ROW_GROUP = 8


def _expert_kernel(be_ref, nb_ref, nxt_ref, tok_ref, h_hbm, wg_hbm, wu_hbm, wd_hbm, sg_hbm, su_hbm, sd_hbm,
                   o_ref, stage_g, stage_u, stage_d, wb_g, wb_u, wb_d, xbuf, sems, xsems, *, rc):
    b = pl.program_id(0)
    e = be_ref[b]
    tmb = xbuf.shape[1]
    cur = lax.rem(b, 2)
    stages = (stage_g, stage_u, stage_d)
    routed = (wg_hbm, wu_hbm, wd_hbm)
    shared = (sg_hbm, su_hbm, sd_hbm)

    def gather_loop(blk, slot):
        base = blk * tmb

        def step(gi, carry):
            r0 = gi * ROW_GROUP
            for u in range(ROW_GROUP):
                _row_copy(h_hbm, xbuf.at[slot], xsems.at[slot], tok_ref[base + r0 + u], r0 + u).start(priority=u % 2)
            return carry
        lax.fori_loop(0, tmb // ROW_GROUP, step, 0)

    def gather_unrolled(blk, slot):
        base = blk * tmb
        for r in range(tmb):
            _row_copy(h_hbm, xbuf.at[slot], xsems.at[slot], tok_ref[base + r], r).start(priority=r % 2)

    def gather_wait(slot):
        pltpu.make_async_copy(h_hbm.at[pl.ds(0, tmb), :], xbuf.at[slot], xsems.at[slot]).wait()

    def fetch(ex):
        @pl.when(ex < N_EXPERTS)
        def _():
            for i in range(3):
                pltpu.make_async_copy(routed[i].at[ex], stages[i], sems.at[i]).start()

        @pl.when(ex == N_EXPERTS)
        def _():
            for i in range(3):
                pltpu.make_async_copy(shared[i], stages[i], sems.at[i]).start()

    @pl.when(b < nb_ref[0])
    def _():
        @pl.when(b == 0)
        def _():
            fetch(e)
            gather_loop(0, 0)

        is_first = jnp.logical_or(b == 0, be_ref[jnp.maximum(b - 1, 0)] != e)

        @pl.when(is_first)
        def _():
            for i in range(3):
                pltpu.make_async_copy(shared[i], stages[i], sems.at[i]).wait()
            for src, dst in ((stage_g, wb_g), (stage_u, wb_u), (stage_d, wb_d)):
                def cast(r, src=src, dst=dst):
                    dst[pl.ds(r, rc), :] = src[pl.ds(r, rc), :].astype(BF16)
                _row_loop(src.shape[0], rc, cast)
            nxt = nxt_ref[e]

            @pl.when(nxt >= 0)
            def _():
                fetch(nxt)

        gather_wait(cur)
        gather_unrolled(jnp.minimum(b + 1, nb_ref[0] - 1), 1 - cur)
        lo, hi = _unpack_pair(xbuf[cur])
        x = jnp.concatenate([lo, hi], axis=1).astype(BF16)
        g = jnp.dot(x, wb_g[...], preferred_element_type=F32)
        u = jnp.dot(x, wb_u[...], preferred_element_type=F32)
        a = (g * jax.nn.sigmoid(g) * u).astype(BF16)
        y = jnp.dot(a, wb_d[...], preferred_element_type=F32)
        half = y.shape[1] // 2
        o_ref[...] = _pack_pair(y[:, :half], y[:, half:])

        @pl.when(b == nb_ref[0] - 1)
        def _():
            gather_wait(1 - cur)

    @pl.when(b >= nb_ref[0])
    def _():
        o_ref[...] = jnp.zeros_like(o_ref)


def _experts(block_e, n_used, next_e, slot_tok, hp, wg, wu, wd, sg, su, sd, *, tmb):
    n_slots = slot_tok.shape[0]
    wp = hp.shape[1]
    d, ff = wg.shape[1], wg.shape[2]
    nb = n_slots // tmb
    grid_spec = pltpu.PrefetchScalarGridSpec(
        num_scalar_prefetch=4,
        grid=(nb,),
        in_specs=[pl.BlockSpec(memory_space=pl.ANY)] * 7,
        out_specs=pl.BlockSpec((tmb, wp), lambda b, *_: (b, 0)),
        scratch_shapes=[pltpu.VMEM((d, ff), F32), pltpu.VMEM((d, ff), F32), pltpu.VMEM((ff, d), F32),
                        pltpu.VMEM((d, ff), BF16), pltpu.VMEM((d, ff), BF16), pltpu.VMEM((ff, d), BF16),
                        pltpu.VMEM((2, tmb, wp), U32),
                        pltpu.SemaphoreType.DMA((3,)), pltpu.SemaphoreType.DMA((2,))],
    )
    return pl.pallas_call(
        functools.partial(_expert_kernel, rc=128),
        grid_spec=grid_spec,
        out_shape=jax.ShapeDtypeStruct((n_slots, wp), U32),
        compiler_params=_params(("arbitrary",), vmem=60 * 1024 * 1024),
        name="moe_experts",
    )(block_e, n_used, next_e, slot_tok, hp, wg, wu, wd, sg, su, sd)


def _combine_kernel(slot_ref, w_ref, x1_ref, gt_ref, g_ref, ys_hbm, o_ref, buf, sems, *, rc, n_k):
    tc, d = x1_ref.shape
    half = d // 2
    i = pl.program_id(0)
    cur = lax.rem(i, 2)
    last = pl.num_programs(0) - 1

    def issue_rows(tile, slot, t0):
        base = tile * (tc * n_k)
        for t in range(rc):
            for k in range(n_k):
                _row_copy(ys_hbm, buf.at[slot, k], sems.at[slot], slot_ref[base + (t0 + t) * n_k + k],
                          t0 + t).start(priority=k % 2)

    def wait_planes(slot):
        for k in range(n_k):
            pltpu.make_async_copy(ys_hbm.at[pl.ds(0, tc), :], buf.at[slot, k], sems.at[slot]).wait()

    @pl.when(i == 0)
    def _():
        _row_loop(tc, rc, lambda r: issue_rows(0, 0, r))

    wait_planes(cur)
    nxt = jnp.minimum(i + 1, last)

    def body(r):
        issue_rows(nxt, 1 - cur, r)
        w = w_ref[pl.ds(r, rc), :]
        lo = jnp.zeros((rc, half), F32)
        hi = jnp.zeros((rc, half), F32)
        for k in range(n_k):
            a, b = _unpack_pair(buf[cur, k, pl.ds(r, rc), :])
            lo = lo + w[:, k:k + 1] * a
            hi = hi + w[:, k:k + 1] * b
        ms = (jnp.sum(lo * lo, axis=-1, keepdims=True) + jnp.sum(hi * hi, axis=-1, keepdims=True)) * (1.0 / d)
        rs = lax.rsqrt(ms + NORM_EPS)
        o_ref[pl.ds(r, rc), 0:half] = (x1_ref[pl.ds(r, rc), 0:half]
                                       + gt_ref[:, 0:half] * (lo * rs * g_ref[:, 0:half]))
        o_ref[pl.ds(r, rc), half:d] = (x1_ref[pl.ds(r, rc), half:d]
                                       + gt_ref[:, half:d] * (hi * rs * g_ref[:, half:d]))
    _row_loop(tc, rc, body)

    @pl.when(i == last)
    def _():
        wait_planes(1 - cur)


def _combine(slots, w_tok, x1, gt2, g_post, ys, *, tc):
    s_len, d = x1.shape
    n_k = slots.shape[0] // s_len
    tc = min(tc, s_len)
    grid_spec = pltpu.PrefetchScalarGridSpec(
        num_scalar_prefetch=1,
        grid=(s_len // tc,),
        in_specs=[pl.BlockSpec((tc, w_tok.shape[1]), lambda i, sl: (i, 0)),
                  pl.BlockSpec((tc, d), lambda i, sl: (i, 0)),
                  pl.BlockSpec((1, d), lambda i, sl: (0, 0)),
                  pl.BlockSpec((1, d), lambda i, sl: (0, 0)),
                  pl.BlockSpec(memory_space=pl.ANY)],
        out_specs=pl.BlockSpec((tc, d), lambda i, sl: (i, 0)),
        scratch_shapes=[pltpu.VMEM((2, n_k, tc, d // 2), U32), pltpu.SemaphoreType.DMA((2,))],
    )
    return pl.pallas_call(
        functools.partial(_combine_kernel, rc=8, n_k=n_k),
        grid_spec=grid_spec,
        out_shape=jax.ShapeDtypeStruct((s_len, d), F32),
        compiler_params=_params(("arbitrary",)),
        name="moe_combine",
    )(slots, w_tok, x1, gt2.reshape(1, d), g_post.reshape(1, d), ys)


def _spread_stride(n):
    m = int(n * 0.6180339887) | 1
    while math.gcd(m, n) != 1:
        m += 2
    return m


def _moe_plan(idx_t, wts_t, *, tmb):
    n_k, n_tok = idx_t.shape
    n_e = N_EXPERTS + 1
    eid = jnp.concatenate([idx_t, jnp.full((1, n_tok), N_EXPERTS, I32)], axis=0)
    wts = jnp.concatenate([wts_t, jnp.ones((1, n_tok), F32)], axis=0)
    sel = (eid[:, None, :] == jnp.arange(n_e, dtype=I32)[None, :, None]).any(axis=0).astype(I32)
    stride = _spread_stride(n_tok)
    visit = lax.rem(jnp.arange(n_tok, dtype=I32) * stride, n_tok)
    where = lax.rem(jnp.arange(n_tok, dtype=I32) * pow(stride, -1, n_tok), n_tok)
    csum = jnp.cumsum(sel[:, visit], axis=1)
    counts = csum[:, -1]
    padded = (counts + tmb - 1) // tmb * tmb
    ends = jnp.cumsum(padded)
    starts = ends - padded
    slot_dense = (starts[:, None] + csum - 1)[:, where]
    slot = jnp.take_along_axis(slot_dense, eid, axis=0)
    n_slots = -(-(n_tok * (n_k + 1) + n_e * (tmb - 1)) // tmb) * tmb
    tok = jnp.broadcast_to(jnp.arange(n_tok, dtype=I32)[None, :], slot.shape)
    filler = lax.rem(jnp.arange(n_slots, dtype=I32) * stride, n_tok)
    slot_tok = filler.at[slot.reshape(-1)].set(tok.reshape(-1))
    n_used = (ends[-1] // tmb).astype(I32).reshape(1)
    block_start = jnp.arange(n_slots // tmb, dtype=I32) * tmb
    block_e = jnp.minimum(jnp.searchsorted(ends, block_start, side='right'), n_e - 1).astype(I32)
    slots_tok_major = slot.T.reshape(-1)
    w_tok = jnp.zeros((n_tok, 16), F32).at[:, :n_k + 1].set(wts.T)
    owner = jnp.where(padded > 0, jnp.arange(n_e, dtype=I32), n_e)
    later = jnp.concatenate([lax.cummin(owner[::-1])[::-1][1:], jnp.full((1,), n_e, I32)])
    next_e = jnp.where(later >= n_e, -1, later).astype(I32)
    return slot_tok, block_e, n_used, next_e, slots_tok_major, w_tok


def _in_proj_weights(w_in):
    d = w_in.shape[0]
    w_main = jnp.pad(w_in.astype(BF16), ((0, 0), (0, N_IN_PAD - N_IN)))
    kr = w_in[:, COL_KR:COL_KR + A_ROPE_DIM]
    gates = w_in[:, COL_IG:COL_IG + 2 * M_HEADS]
    half = A_ROPE_DIM // 2
    kswap = jnp.concatenate([-kr[:, half:], kr[:, :half]], axis=1)
    z = lambda n: jnp.zeros((d, n), w_in.dtype)
    w_tail = jnp.concatenate([kswap, gates, z(LANES - A_ROPE_DIM - 2 * M_HEADS), kr, z(LANES - A_ROPE_DIM)], axis=1)
    return w_main, w_tail.astype(BF16)


def _q_up_weight(w_uq):
    r = w_uq.shape[0]
    w = w_uq.reshape(r, A_HEADS, A_NOPE_DIM + A_ROPE_DIM)
    nope = w[:, :, :A_NOPE_DIM]
    rope = w[:, :, A_NOPE_DIM:]
    half = A_ROPE_DIM // 2
    swap = jnp.concatenate([-rope[:, :, half:], rope[:, :, :half]], axis=2)
    pad = jnp.zeros((r, A_HEADS, LANES - A_ROPE_DIM), w_uq.dtype)
    rope_p = jnp.concatenate([rope, pad], axis=2)
    swap_p = jnp.concatenate([swap, pad], axis=2)
    flat = lambda a: a.reshape(r, -1)
    return jnp.concatenate([flat(nope), flat(rope_p), flat(swap_p)], axis=1).astype(BF16)


def _kv_up_weight(w_ukv):
    r = w_ukv.shape[0]
    w = w_ukv.reshape(r, A_HEADS, A_NOPE_DIM + A_V_DIM)
    return jnp.concatenate([w[:, :, :A_NOPE_DIM].reshape(r, -1), w[:, :, A_NOPE_DIM:].reshape(r, -1)],
                           axis=1).astype(BF16)


def _rope_tables(s_len):
    pos = jnp.arange(s_len, dtype=F32)
    inv_freq = 1.0 / (ROPE_THETA ** (jnp.arange(0, A_ROPE_DIM, 2, dtype=F32) / A_ROPE_DIM))
    ang = pos[:, None] * inv_freq[None, :]
    pad = jnp.zeros((s_len, LANES - A_ROPE_DIM), F32)
    cos_t = jnp.concatenate([jnp.cos(ang), jnp.cos(ang), pad], axis=1)
    sin_t = jnp.concatenate([jnp.sin(ang), jnp.sin(ang), pad], axis=1)
    return cos_t, sin_t


def _block(x, c, w_ada, b_ada, g_pre_mix, g_post_mix, w_in, conv_w, conv_b, b_igate, b_fgate, g_mlstm_out,
           g_q_norm, w_uq, g_kv_norm, w_ukv, w_out, g_pre_ffn, g_post_ffn, w_router, b_router,
           w_gate, w_up, w_down, w_shared_gate, w_shared_up, w_shared_down):
    s_len, d = x.shape
    mod = _adaln(c, w_ada, b_ada)[0]
    sh1, sc1, gt1, sh2, sc2, gt2 = [mod[i * d:(i + 1) * d] for i in range(6)]

    w_main, w_tail = _in_proj_weights(w_in)
    proj, tail = _norm_mm(x, g_pre_mix * (1.0 + sc1), sh1, w_main, w_tail, tm=512, tn=IN_TILE)
    h_m = _mlstm(proj, tail, conv_w, conv_b, b_igate, b_fgate, g_mlstm_out, chunk=min(M_CHUNK, s_len))
    scale = (A_NOPE_DIM + A_ROPE_DIM) ** -0.5
    zq = jnp.zeros((A_Q_RANK,), F32)
    zkv = jnp.zeros((A_KV_RANK,), F32)
    q_raw = _norm_mm(proj[:, COL_CQ:COL_CQ + A_Q_RANK], g_q_norm * scale, zq, _q_up_weight(w_uq), tm=1024, tn=1536)
    kv_raw = _norm_mm(proj[:, COL_CKV:COL_CKV + A_KV_RANK], g_kv_norm, zkv, _kv_up_weight(w_ukv), tm=1024, tn=1024)
    cos_t, sin_t = _rope_tables(s_len)
    q_cat, k_cat, v_cat = _rope_assemble(q_raw, kv_raw, tail, cos_t, sin_t, tm=256)
    h_a = _attention(q_cat, k_cat, v_cat, tq=1024, chunk=CHUNK)
    y = _mm2(h_m, h_a, w_out.astype(BF16), tm=1024, tn=1024)

    x1, hp, idx_t, wts_t = _mid(y, x, gt1, g_post_mix, g_pre_ffn * (1.0 + sc2), sh2, w_router, b_router, tm=256)
    tmb = 256
    slot_tok, block_e, n_used, next_e, slots, w_tok = _moe_plan(idx_t, wts_t, tmb=tmb)
    ys = _experts(block_e, n_used, next_e, slot_tok, hp, w_gate, w_up, w_down,
                  w_shared_gate, w_shared_up, w_shared_down, tmb=tmb)
    return _combine(slots, w_tok, x1, gt2, g_post_ffn, ys, tc=128)


def kernel(x, c, w_ada, b_ada, g_pre_mix, g_post_mix, w_in, conv_w, conv_b, b_igate, b_fgate, g_mlstm_out,
           g_q_norm, w_uq, g_kv_norm, w_ukv, w_out, g_pre_ffn, g_post_ffn, w_router, b_router,
           w_gate, w_up, w_down, w_shared_gate, w_shared_up, w_shared_down):
    assert x.shape[0] == 1 and w_ada.shape[0] == 1, "single sequence, single layer"
    layer = (w_ada, b_ada, g_pre_mix, g_post_mix, w_in, conv_w, conv_b, b_igate, b_fgate, g_mlstm_out,
             g_q_norm, w_uq, g_kv_norm, w_ukv, w_out, g_pre_ffn, g_post_ffn, w_router, b_router,
             w_gate, w_up, w_down, w_shared_gate, w_shared_up, w_shared_down)
    out = _block(x[0], c[0], *[p[0] for p in layer])
    return out[None]
```

```python
import functools
import math

import jax
import jax.numpy as jnp
from jax import lax
from jax.experimental import pallas as pl
from jax.experimental.pallas import tpu as pltpu

F32 = jnp.float32
BF16 = jnp.bfloat16
I32 = jnp.int32
U32 = jnp.uint32

NORM_EPS = 1e-6
CHUNK = 64
M_CHUNK = 128

M_HEADS = 4
M_QK_DIM = 256
M_V_DIM = 512
M_WIDTH = M_HEADS * M_V_DIM
M_QK_WIDTH = 2 * M_HEADS * M_QK_DIM
CONV_WIDTH = 4
GATE_SOFTCAP = 15.0

A_HEADS = 16
A_NOPE_DIM = 128
A_ROPE_DIM = 64
A_V_DIM = 128
A_Q_RANK = 768
A_KV_RANK = 512
A_HEAD_PAD = 256
ROPE_THETA = 10000.0

N_EXPERTS = 64
TOP_K = 8
N_GROUPS = 8
GROUP_SIZE = N_EXPERTS // N_GROUPS
TOPK_GROUPS = 4
ROUTED_SCALE = 2.5

LANES = 128
VMEM_LIMIT = 56 * 1024 * 1024
NEG_BIG = -1e30

COL_QK = 0
COL_V = COL_QK + M_QK_WIDTH
COL_O = COL_V + M_WIDTH
COL_IG = COL_O + M_WIDTH
COL_FG = COL_IG + M_HEADS
COL_CQ = COL_FG + M_HEADS
COL_CKV = COL_CQ + A_Q_RANK
COL_KR = COL_CKV + A_KV_RANK
N_IN = COL_KR + A_ROPE_DIM
IN_TILE = 1536
N_IN_PAD = -(-N_IN // IN_TILE) * IN_TILE
TAIL_W = 2 * LANES
TAIL_GATE_LANE = A_ROPE_DIM


def _params(sem, vmem=VMEM_LIMIT):
    return pltpu.CompilerParams(dimension_semantics=sem, vmem_limit_bytes=vmem)


def _row_loop(n_rows, rc, body):
    def step(i, carry):
        body(pl.multiple_of(i * rc, rc))
        return carry
    lax.fori_loop(0, n_rows // rc, step, 0)


def _pack_pair(a, b):
    lo = lax.bitcast_convert_type(a.astype(BF16).astype(F32), U32) >> 16
    hi = lax.bitcast_convert_type(b.astype(BF16).astype(F32), U32) & jnp.uint32(0xFFFF0000)
    return lo | hi


def _unpack_pair(u):
    lo = lax.bitcast_convert_type(u << 16, F32)
    hi = lax.bitcast_convert_type(u & jnp.uint32(0xFFFF0000), F32)
    return lo, hi


def _adaln_kernel(c_ref, w_ref, b_ref, o_ref, *, rc):
    d, tn = w_ref.shape
    nl = tn // LANES

    def step(i, accs):
        r = pl.multiple_of(i * rc, rc)
        c = c_ref[pl.ds(r, rc), :]
        ca = c * jax.nn.sigmoid(c)
        out = []
        for j in range(nl):
            prod = w_ref[pl.ds(r, rc), j * LANES:(j + 1) * LANES] * ca
            out.append(accs[j] + jnp.sum(prod.reshape(rc // 8, 8, LANES), axis=0))
        return tuple(out)

    accs = lax.fori_loop(0, d // rc, step, tuple(jnp.zeros((8, LANES), F32) for _ in range(nl)))
    for j in range(nl):
        o_ref[:, j * LANES:(j + 1) * LANES] = (
            jnp.sum(accs[j], axis=0, keepdims=True) + b_ref[:, j * LANES:(j + 1) * LANES])


def _adaln(c, w_ada, b_ada):
    d, n = w_ada.shape
    tn = 512
    c_b = jnp.broadcast_to(c.reshape(d, 1), (d, LANES))
    return pl.pallas_call(
        functools.partial(_adaln_kernel, rc=64),
        grid=(n // tn,),
        in_specs=[pl.BlockSpec((d, LANES), lambda j: (0, 0)),
                  pl.BlockSpec((d, tn), lambda j: (0, j)),
                  pl.BlockSpec((1, tn), lambda j: (0, j))],
        out_specs=pl.BlockSpec((1, tn), lambda j: (0, j)),
        out_shape=jax.ShapeDtypeStruct((1, n), F32),
        compiler_params=_params(("arbitrary",)),
        name="adaln",
    )(c_b, w_ada, b_ada.reshape(1, n))


def _norm_mm_kernel(x_ref, g_ref, w_ref, o_ref, h_scr, *, rc):
    tm = x_ref.shape[0]

    @pl.when(pl.program_id(1) == 0)
    def _():
        def body(r):
            x = x_ref[pl.ds(r, rc), :].astype(F32)
            y = x * lax.rsqrt(jnp.mean(x * x, axis=-1, keepdims=True) + NORM_EPS)
            h_scr[pl.ds(r, rc), :] = (y * g_ref[...]).astype(BF16)
        _row_loop(tm, rc, body)

    o_ref[...] = jnp.dot(h_scr[...], w_ref[...], preferred_element_type=F32).astype(o_ref.dtype)


def _norm_mm(x, g, w, *, tm, tn):
    m, k = x.shape
    n = w.shape[1]
    tm = min(tm, m)
    return pl.pallas_call(
        functools.partial(_norm_mm_kernel, rc=32),
        grid=(m // tm, n // tn),
        in_specs=[pl.BlockSpec((tm, k), lambda i, j: (i, 0)),
                  pl.BlockSpec((1, k), lambda i, j: (0, 0)),
                  pl.BlockSpec((k, tn), lambda i, j: (0, j))],
        out_specs=pl.BlockSpec((tm, tn), lambda i, j: (i, j)),
        out_shape=jax.ShapeDtypeStruct((m, n), BF16),
        scratch_shapes=[pltpu.VMEM((tm, k), BF16)],
        compiler_params=_params(("parallel", "arbitrary")),
        name="norm_mm",
    )(x, g.reshape(1, k), w)


def _prenorm_kernel(x_ref, gs_ref, sh_ref, o_ref, *, rc):
    def body(r):
        x = x_ref[pl.ds(r, rc), :]
        y = x * lax.rsqrt(jnp.mean(x * x, axis=-1, keepdims=True) + NORM_EPS)
        o_ref[pl.ds(r, rc), :] = (y * gs_ref[...] + sh_ref[...]).astype(o_ref.dtype)
    _row_loop(x_ref.shape[0], rc, body)


def _prenorm(x, gs, sh, *, tm):
    m, k = x.shape
    tm = min(tm, m)
    return pl.pallas_call(
        functools.partial(_prenorm_kernel, rc=32),
        grid=(m // tm,),
        in_specs=[pl.BlockSpec((tm, k), lambda i: (i, 0)),
                  pl.BlockSpec((1, k), lambda i: (0, 0)),
                  pl.BlockSpec((1, k), lambda i: (0, 0))],
        out_specs=pl.BlockSpec((tm, k), lambda i: (i, 0)),
        out_shape=jax.ShapeDtypeStruct((m, k), BF16),
        compiler_params=_params(("parallel",)),
        name="prenorm",
    )(x, gs.reshape(1, k), sh.reshape(1, k))


def _mm_tail_kernel(a_ref, w_ref, wt_ref, o_ref, t_ref):
    o_ref[...] = jnp.dot(a_ref[...], w_ref[...], preferred_element_type=F32).astype(o_ref.dtype)

    @pl.when(pl.program_id(1) == pl.num_programs(1) - 1)
    def _():
        t_ref[...] = jnp.dot(a_ref[...], wt_ref[...], preferred_element_type=F32)


def _mm_tail(a, w, w_tail, *, tm, tn):
    m, k = a.shape
    n, nt = w.shape[1], w_tail.shape[1]
    tm = min(tm, m)
    return pl.pallas_call(
        _mm_tail_kernel,
        grid=(m // tm, n // tn),
        in_specs=[pl.BlockSpec((tm, k), lambda i, j: (i, 0)),
                  pl.BlockSpec((k, tn), lambda i, j: (0, j)),
                  pl.BlockSpec((k, nt), lambda i, j: (0, 0))],
        out_specs=[pl.BlockSpec((tm, tn), lambda i, j: (i, j)),
                   pl.BlockSpec((tm, nt), lambda i, j: (i, 0))],
        out_shape=[jax.ShapeDtypeStruct((m, n), BF16), jax.ShapeDtypeStruct((m, nt), F32)],
        compiler_params=_params(("parallel", "arbitrary"), vmem=60 * 1024 * 1024),
        name="in_proj",
    )(a, w, w_tail)


def _mm2_kernel(a1_ref, a2_ref, w_ref, o_ref):
    k1 = a1_ref.shape[1]
    acc = jnp.dot(a1_ref[...], w_ref[:k1, :], preferred_element_type=F32)
    acc = acc + jnp.dot(a2_ref[...], w_ref[k1:, :], preferred_element_type=F32)
    o_ref[...] = acc.astype(o_ref.dtype)


def _mm2(a1, a2, w, *, tm, tn):
    m, k1 = a1.shape
    k2 = a2.shape[1]
    n = w.shape[1]
    tm = min(tm, m)
    return pl.pallas_call(
        _mm2_kernel,
        grid=(m // tm, n // tn),
        in_specs=[pl.BlockSpec((tm, k1), lambda i, j: (i, 0)),
                  pl.BlockSpec((tm, k2), lambda i, j: (i, 0)),
                  pl.BlockSpec((k1 + k2, tn), lambda i, j: (0, j))],
        out_specs=pl.BlockSpec((tm, tn), lambda i, j: (i, j)),
        out_shape=jax.ShapeDtypeStruct((m, n), BF16),
        compiler_params=_params(("parallel", "arbitrary")),
        name="out_proj",
    )(a1, a2, w)


def _softcap(z):
    return GATE_SOFTCAP * jnp.tanh(z * (1.0 / GATE_SOFTCAP))


def _log_sigmoid(z):
    return jnp.minimum(z, 0.0) - jnp.log1p(jnp.exp(-jnp.abs(z)))


def _mlstm_kernel(qk_ref, prev_ref, v_ref, o_ref, gcol_ref, grow_ref, cw_ref, cb_ref, bcol_ref, brow_ref,
                  gout_ref, out_ref, c_scr, n_scr, m_scr, u_scr):
    c = pl.program_id(0)
    L = qk_ref.shape[0]
    halo = prev_ref.shape[0]
    dk, dv = M_QK_DIM, M_V_DIM

    @pl.when(c == 0)
    def _():
        c_scr[...] = jnp.zeros_like(c_scr)
        n_scr[...] = jnp.zeros_like(n_scr)
        m_scr[...] = jnp.zeros_like(m_scr)

    prev = prev_ref[...].astype(F32)
    u_scr[0:halo, :] = jnp.where(c == 0, jnp.zeros_like(prev), prev)
    u_scr[halo:halo + L, :] = qk_ref[...].astype(F32)

    def conv_silu(col, width):
        acc = cb_ref[:, col:col + width]
        for j in range(CONV_WIDTH):
            r0 = halo - (CONV_WIDTH - 1) + j
            acc = acc + u_scr[r0:r0 + L, col:col + width] * cw_ref[j:j + 1, col:col + width]
        return acc * jax.nn.sigmoid(acc)

    pre_c = _softcap(gcol_ref[...] + bcol_ref[...])
    lf_c = _log_sigmoid(pre_c)
    pre_r = _softcap(grow_ref[0] + brow_ref[...])
    lf_r = _log_sigmoid(pre_r)
    row = lax.broadcasted_iota(I32, (L, L), 0)
    col = lax.broadcasted_iota(I32, (L, L), 1)
    causal = col <= row
    tril = causal.astype(F32)
    triu = (row <= col).astype(F32)
    bcum_c = jnp.dot(tril, lf_c, preferred_element_type=F32, precision=lax.Precision.HIGHEST)
    bcum_r = jnp.dot(lf_r, triu, preferred_element_type=F32, precision=lax.Precision.HIGHEST)

    for h in range(M_HEADS):
        li_lane = TAIL_GATE_LANE + h
        lf_lane = TAIL_GATE_LANE + M_HEADS + h
        q = (conv_silu(h * dk, dk) * (dk ** -0.5)).astype(BF16)
        kf = conv_silu(M_HEADS * dk + h * dk, dk)
        kb = kf.astype(BF16)
        v = v_ref[:, h * dv:(h + 1) * dv]
        b_c = bcum_c[:, lf_lane:lf_lane + 1]
        li_c = pre_c[:, li_lane:li_lane + 1]
        b_r = bcum_r[M_HEADS + h:M_HEADS + h + 1, :]
        li_r = pre_r[h:h + 1, :]
        m_prev = m_scr[h:h + 1, 0:1]

        dm = jnp.where(causal, b_c - b_r + li_r, NEG_BIG)
        inter = b_c + m_prev
        m_t = jnp.maximum(jnp.max(dm, axis=1, keepdims=True), inter)
        decay = jnp.exp(inter - m_t)
        s = lax.dot_general(q, kb, (((1,), (1,)), ((), ())), preferred_element_type=F32) * jnp.exp(dm - m_t)
        c_state = c_scr[h]
        n_state = n_scr[h:h + 1, :]
        num = jnp.dot(s.astype(BF16), v, preferred_element_type=F32)
        num = num + decay * jnp.dot(q, c_state.astype(BF16), preferred_element_type=F32)
        den = jnp.sum(s, axis=1, keepdims=True) + decay * jnp.sum(q.astype(F32) * n_state, axis=1, keepdims=True)
        hh = num / jnp.maximum(jnp.abs(den), jnp.exp(-m_t))

        b_last = b_c[L - 1:L, :]
        m_new = jnp.maximum(b_last + m_prev, jnp.max(b_last - b_r + li_r, axis=1, keepdims=True))
        carry = jnp.exp(b_last + m_prev - m_new)
        w_c = jnp.exp(b_last - b_c + li_c - m_new)
        kw = kf * w_c
        c_scr[h] = carry * c_state + lax.dot_general(kw.astype(BF16), v, (((0,), (0,)), ((), ())),
                                                     preferred_element_type=F32)
        n_scr[h:h + 1, :] = carry * n_state + jnp.sum(kw, axis=0, keepdims=True)
        m_scr[h:h + 1, :] = jnp.broadcast_to(m_new, (1, LANES))

        hn = hh * lax.rsqrt(jnp.mean(hh * hh, axis=1, keepdims=True) + NORM_EPS) * gout_ref[:, h * dv:(h + 1) * dv]
        gate = jax.nn.sigmoid(o_ref[:, h * dv:(h + 1) * dv].astype(F32))
        out_ref[:, h * dv:(h + 1) * dv] = (hn * gate).astype(out_ref.dtype)


def _mlstm(proj, tail, conv_w, conv_b, b_igate, b_fgate, g_out, *, chunk):
    s_len = proj.shape[0]
    L = chunk
    n_c = s_len // L
    halo = 16
    gates = tail[:, TAIL_GATE_LANE:TAIL_GATE_LANE + 2 * M_HEADS]
    g_row = gates.reshape(n_c, L, 2 * M_HEADS).transpose(0, 2, 1)
    bias = jnp.concatenate([b_igate, b_fgate]).astype(F32)
    b_col = jnp.zeros((1, LANES), F32).at[0, TAIL_GATE_LANE:TAIL_GATE_LANE + 2 * M_HEADS].set(bias)
    b_row = bias.reshape(2 * M_HEADS, 1)
    w = M_QK_WIDTH
    return pl.pallas_call(
        _mlstm_kernel,
        grid=(n_c,),
        in_specs=[pl.BlockSpec((L, w), lambda c: (c, COL_QK // w)),
                  pl.BlockSpec((halo, w), lambda c: (jnp.maximum(c * (L // halo) - 1, 0), COL_QK // w)),
                  pl.BlockSpec((L, M_WIDTH), lambda c: (c, COL_V // M_WIDTH)),
                  pl.BlockSpec((L, M_WIDTH), lambda c: (c, COL_O // M_WIDTH)),
                  pl.BlockSpec((L, LANES), lambda c: (c, 0)),
                  pl.BlockSpec((1, 2 * M_HEADS, L), lambda c: (c, 0, 0)),
                  pl.BlockSpec((CONV_WIDTH, w), lambda c: (0, 0)),
                  pl.BlockSpec((1, w), lambda c: (0, 0)),
                  pl.BlockSpec((1, LANES), lambda c: (0, 0)),
                  pl.BlockSpec((2 * M_HEADS, 1), lambda c: (0, 0)),
                  pl.BlockSpec((1, M_WIDTH), lambda c: (0, 0))],
        out_specs=pl.BlockSpec((L, M_WIDTH), lambda c: (c, 0)),
        out_shape=jax.ShapeDtypeStruct((s_len, M_WIDTH), BF16),
        scratch_shapes=[pltpu.VMEM((M_HEADS, M_QK_DIM, M_V_DIM), F32),
                        pltpu.VMEM((8, M_QK_DIM), F32),
                        pltpu.VMEM((8, LANES), F32),
                        pltpu.VMEM((halo + L, w), F32)],
        compiler_params=_params(("arbitrary",)),
        name="mlstm",
    )(proj, proj, proj, proj, tail, g_row, conv_w, conv_b.reshape(1, w), b_col, b_row, g_out.reshape(1, M_WIDTH))


def _rope_kernel(q_ref, kn_ref, v_ref, tail_ref, cos_ref, sin_ref, qo_ref, ko_ref, vo_ref):
    cos = cos_ref[...]
    sin = sin_ref[...]
    nope_w = A_HEADS * A_NOPE_DIM
    kpe = (tail_ref[:, LANES:2 * LANES] * cos + tail_ref[:, 0:LANES] * sin).astype(BF16)
    lane = lax.broadcasted_iota(I32, (q_ref.shape[0], LANES), 1)
    ones_col = jnp.where(lane == 0, 1.0, 0.0).astype(BF16)
    for h in range(A_HEADS):
        vo_ref[:, h * A_HEAD_PAD:h * A_HEAD_PAD + LANES] = v_ref[:, h * LANES:(h + 1) * LANES]
        vo_ref[:, h * A_HEAD_PAD + LANES:(h + 1) * A_HEAD_PAD] = ones_col
        lo = h * A_HEAD_PAD
        qo_ref[:, lo:lo + LANES] = q_ref[:, h * LANES:(h + 1) * LANES]
        qr = q_ref[:, nope_w + h * LANES:nope_w + (h + 1) * LANES].astype(F32)
        qs = q_ref[:, 2 * nope_w + h * LANES:2 * nope_w + (h + 1) * LANES].astype(F32)
        qo_ref[:, lo + LANES:lo + 2 * LANES] = (qr * cos + qs * sin).astype(BF16)
        ko_ref[:, lo:lo + LANES] = kn_ref[:, h * LANES:(h + 1) * LANES]
        ko_ref[:, lo + LANES:lo + 2 * LANES] = kpe


def _rope_assemble(q_raw, kv_raw, tail, cos_t, sin_t, *, tm):
    s_len = q_raw.shape[0]
    tm = min(tm, s_len)
    nope_w = A_HEADS * A_NOPE_DIM
    wide = A_HEADS * A_HEAD_PAD
    return pl.pallas_call(
        _rope_kernel,
        grid=(s_len // tm,),
        in_specs=[pl.BlockSpec((tm, 3 * nope_w), lambda i: (i, 0)),
                  pl.BlockSpec((tm, nope_w), lambda i: (i, 0)),
                  pl.BlockSpec((tm, nope_w), lambda i: (i, 1)),
                  pl.BlockSpec((tm, 2 * LANES), lambda i: (i, 0)),
                  pl.BlockSpec((tm, LANES), lambda i: (i, 0)),
                  pl.BlockSpec((tm, LANES), lambda i: (i, 0))],
        out_specs=[pl.BlockSpec((tm, wide), lambda i: (i, 0))] * 3,
        out_shape=[jax.ShapeDtypeStruct((s_len, wide), BF16)] * 3,
        compiler_params=_params(("parallel",)),
        name="rope_assemble",
    )(q_raw, kv_raw, kv_raw, tail, cos_t, sin_t)


def _attn_kernel(q_ref, k_ref, v_ref, o_ref, m_scr, acc_scr, s_scr, *, chunk):
    qi = pl.program_id(1)
    tq = q_ref.shape[0]
    m_scr[...] = jnp.full_like(m_scr, NEG_BIG)
    acc_scr[...] = jnp.zeros_like(acc_scr)

    def scores(slot, blk):
        start = pl.multiple_of(blk * tq, tq)
        s_scr[slot] = lax.dot_general(q_ref[...], k_ref[pl.ds(start, tq), :], (((1,), (1,)), ((), ())),
                                      preferred_element_type=F32)

    def consume(slot, blk, masked):
        start = pl.multiple_of(blk * tq, tq)
        s = s_scr[slot]
        if masked:
            rq = lax.broadcasted_iota(I32, (tq, tq), 0) // chunk
            ck = lax.broadcasted_iota(I32, (tq, tq), 1) // chunk
            s = jnp.where(ck <= rq, s, NEG_BIG)
        m_prev = m_scr[...]
        m_new = jnp.maximum(m_prev, jnp.max(s, axis=1, keepdims=True))
        pr = jnp.exp(s - m_new).astype(BF16)
        acc_scr[...] = (jnp.exp(m_prev - m_new) * acc_scr[...]
                        + jnp.dot(pr, v_ref[pl.ds(start, tq), :], preferred_element_type=F32))
        m_scr[...] = m_new

    scores(0, 0)

    def pair(t, carry):
        scores(1, 2 * t + 1)
        consume(0, 2 * t, False)
        scores(0, 2 * t + 2)
        consume(1, 2 * t + 1, False)
        return carry
    lax.fori_loop(0, qi // 2, pair, 0)

    @pl.when(qi % 2 == 1)
    def _():
        scores(1, qi)
        consume(0, qi - 1, False)
        consume(1, qi, True)

    @pl.when(qi % 2 == 0)
    def _():
        consume(0, qi, True)

    acc = acc_scr[...]
    o_ref[...] = (acc[:, :A_V_DIM] / acc[:, A_V_DIM:A_V_DIM + 1]).astype(o_ref.dtype)


def _attention(q_cat, k_cat, v_cat, *, tq, chunk):
    s_len = q_cat.shape[0]
    tq = min(tq, s_len)
    return pl.pallas_call(
        functools.partial(_attn_kernel, chunk=chunk),
        grid=(A_HEADS, s_len // tq),
        in_specs=[pl.BlockSpec((tq, A_HEAD_PAD), lambda h, i: (i, h)),
                  pl.BlockSpec((s_len, A_HEAD_PAD), lambda h, i: (0, h)),
                  pl.BlockSpec((s_len, A_HEAD_PAD), lambda h, i: (0, h))],
        out_specs=pl.BlockSpec((tq, A_V_DIM), lambda h, i: (i, h)),
        out_shape=jax.ShapeDtypeStruct((s_len, A_HEADS * A_V_DIM), BF16),
        scratch_shapes=[pltpu.VMEM((tq, 1), F32), pltpu.VMEM((tq, A_HEAD_PAD), F32),
                        pltpu.VMEM((2, tq, tq), F32)],
        compiler_params=_params(("parallel", "arbitrary")),
        name="mla_attention",
    )(q_cat, k_cat, v_cat)


def _route_t(logits, bias_col):
    n = logits.shape[1]
    scores = jax.nn.sigmoid(logits)
    biased = scores + bias_col
    sub = lax.broadcasted_iota(I32, (GROUP_SIZE, n), 0)
    rows = []
    for g in range(N_GROUPS):
        x = biased[g * GROUP_SIZE:(g + 1) * GROUP_SIZE, :]
        m1 = jnp.max(x, axis=0, keepdims=True)
        i1 = jnp.min(jnp.where(x == m1, sub, GROUP_SIZE), axis=0, keepdims=True)
        m2 = jnp.max(jnp.where(sub == i1, -jnp.inf, x), axis=0, keepdims=True)
        rows.append(m1 + m2)
    gscore = jnp.concatenate(rows, axis=0)
    gio = lax.broadcasted_iota(I32, (N_GROUPS, n), 0)
    grank = jnp.zeros((N_GROUPS, n), I32)
    for g in range(N_GROUPS):
        r = gscore[g:g + 1, :]
        grank = grank + jnp.where(gio > g, jnp.where(r >= gscore, 1, 0), jnp.where(r > gscore, 1, 0))
    gsel = grank < TOPK_GROUPS
    masked = jnp.concatenate(
        [jnp.where(gsel[g:g + 1, :], biased[g * GROUP_SIZE:(g + 1) * GROUP_SIZE, :], -jnp.inf)
         for g in range(N_GROUPS)], axis=0)
    eio = lax.broadcasted_iota(I32, (N_EXPERTS, n), 0)
    rank = jnp.zeros((N_EXPERTS, n), I32)
    for e in range(N_EXPERTS):
        r = masked[e:e + 1, :]
        rank = rank + jnp.where(eio > e, jnp.where(r >= masked, 1, 0), jnp.where(r > masked, 1, 0))
    sel = rank < TOP_K
    denom = jnp.sum(jnp.where(sel, scores, 0.0), axis=0, keepdims=True)
    wnorm = scores / denom * ROUTED_SCALE
    eio_f = eio.astype(F32)
    ids, wts = [], []
    for k in range(TOP_K):
        hit = rank == k
        ids.append(jnp.sum(jnp.where(hit, eio_f, 0.0), axis=0, keepdims=True))
        wts.append(jnp.sum(jnp.where(hit, wnorm, 0.0), axis=0, keepdims=True))
    return jnp.concatenate(ids, axis=0).astype(I32), jnp.concatenate(wts, axis=0)


def _mid_kernel(y_ref, x_ref, gt_ref, gpost_ref, gs_ref, sh_ref, wr_ref, br_ref,
                x1_ref, hp_ref, idx_ref, wts_ref, h_scr, *, rc):
    tm, d = x_ref.shape
    half = d // 2

    def body(r):
        y = y_ref[pl.ds(r, rc), :].astype(F32)
        yn = y * lax.rsqrt(jnp.mean(y * y, axis=-1, keepdims=True) + NORM_EPS) * gpost_ref[...]
        x1 = x_ref[pl.ds(r, rc), :] + gt_ref[...] * yn
        x1_ref[pl.ds(r, rc), :] = x1
        hn = x1 * lax.rsqrt(jnp.mean(x1 * x1, axis=-1, keepdims=True) + NORM_EPS)
        h = hn * gs_ref[...] + sh_ref[...]
        h_scr[pl.ds(r, rc), :] = h
        hp_ref[pl.ds(r, rc), :] = _pack_pair(h[:, :half], h[:, half:])
    _row_loop(tm, rc, body)

    logits = lax.dot_general(wr_ref[...], h_scr[...], (((1,), (1,)), ((), ())),
                             preferred_element_type=F32, precision=lax.Precision.HIGHEST)
    ids, wts = _route_t(logits, br_ref[...])
    idx_ref[...] = ids
    wts_ref[...] = wts


def _mid(y, x, gt1, g_post, gs2, sh2, w_router, b_router, *, tm):
    s_len, d = x.shape
    tm = min(tm, s_len)
    vec = lambda a: a.reshape(1, d).astype(F32)
    return pl.pallas_call(
        functools.partial(_mid_kernel, rc=16),
        grid=(s_len // tm,),
        in_specs=[pl.BlockSpec((tm, d), lambda i: (i, 0)),
                  pl.BlockSpec((tm, d), lambda i: (i, 0)),
                  pl.BlockSpec((1, d), lambda i: (0, 0)),
                  pl.BlockSpec((1, d), lambda i: (0, 0)),
                  pl.BlockSpec((1, d), lambda i: (0, 0)),
                  pl.BlockSpec((1, d), lambda i: (0, 0)),
                  pl.BlockSpec((N_EXPERTS, d), lambda i: (0, 0)),
                  pl.BlockSpec((N_EXPERTS, 1), lambda i: (0, 0))],
        out_specs=[pl.BlockSpec((tm, d), lambda i: (i, 0)),
                   pl.BlockSpec((tm, d // 2), lambda i: (i, 0)),
                   pl.BlockSpec((TOP_K, tm), lambda i: (0, i)),
                   pl.BlockSpec((TOP_K, tm), lambda i: (0, i))],
        out_shape=[jax.ShapeDtypeStruct((s_len, d), F32),
                   jax.ShapeDtypeStruct((s_len, d // 2), U32),
                   jax.ShapeDtypeStruct((TOP_K, s_len), I32),
                   jax.ShapeDtypeStruct((TOP_K, s_len), F32)],
        scratch_shapes=[pltpu.VMEM((tm, d), F32)],
        compiler_params=_params(("parallel",)),
        name="mid_norm_route",
    )(y, x, vec(gt1), vec(g_post), vec(gs2), vec(sh2), w_router.T.astype(F32), b_router.reshape(N_EXPERTS, 1))


def _row_copy(src_hbm, dst_vmem, sem, src_row, dst_row):
    return pltpu.make_async_copy(src_hbm.at[pl.ds(src_row, 1), :], dst_vmem.at[pl.ds(dst_row, 1), :], sem)


ROW_GROUP = 8
ROW_DMA_PRIORITY = 0
WEIGHT_DMA_PRIORITY = 1


def _expert_kernel(be_ref, nb_ref, nxt_ref, tok_ref, h_hbm, wg_hbm, wu_hbm, wd_hbm, sg_hbm, su_hbm, sd_hbm,
                   o_ref, stage_g, stage_u, stage_d, wb_g, wb_u, wb_d, xbuf, sems, xsems, *, rc):
    b = pl.program_id(0)
    e = be_ref[b]
    tmb = xbuf.shape[1]
    cur = lax.rem(b, 2)
    stages = (stage_g, stage_u, stage_d)
    routed = (wg_hbm, wu_hbm, wd_hbm)
    shared = (sg_hbm, su_hbm, sd_hbm)

    def gather_loop(blk, slot):
        base = blk * tmb

        def step(gi, carry):
            r0 = gi * ROW_GROUP
            for u in range(ROW_GROUP):
                _row_copy(h_hbm, xbuf.at[slot], xsems.at[slot], tok_ref[base + r0 + u], r0 + u).start(
                    priority=ROW_DMA_PRIORITY)
            return carry
        lax.fori_loop(0, tmb // ROW_GROUP, step, 0)

    def gather_unrolled(blk, slot):
        base = blk * tmb
        for r in range(tmb):
            _row_copy(h_hbm, xbuf.at[slot], xsems.at[slot], tok_ref[base + r], r).start(priority=ROW_DMA_PRIORITY)

    def gather_wait(slot):
        pltpu.make_async_copy(h_hbm.at[pl.ds(0, tmb), :], xbuf.at[slot], xsems.at[slot]).wait()

    def fetch(ex):
        @pl.when(ex < N_EXPERTS)
        def _():
            for i in range(3):
                pltpu.make_async_copy(routed[i].at[ex], stages[i], sems.at[i]).start(priority=WEIGHT_DMA_PRIORITY)

        @pl.when(ex == N_EXPERTS)
        def _():
            for i in range(3):
                pltpu.make_async_copy(shared[i], stages[i], sems.at[i]).start(priority=WEIGHT_DMA_PRIORITY)

    @pl.when(b < nb_ref[0])
    def _():
        @pl.when(b == 0)
        def _():
            fetch(e)
            gather_loop(0, 0)

        is_first = jnp.logical_or(b == 0, be_ref[jnp.maximum(b - 1, 0)] != e)

        @pl.when(is_first)
        def _():
            for i in range(3):
                pltpu.make_async_copy(shared[i], stages[i], sems.at[i]).wait()
            for src, dst in ((stage_g, wb_g), (stage_u, wb_u), (stage_d, wb_d)):
                def cast(r, src=src, dst=dst):
                    dst[pl.ds(r, rc), :] = src[pl.ds(r, rc), :].astype(BF16)
                _row_loop(src.shape[0], rc, cast)
            nxt = nxt_ref[e]

            @pl.when(nxt >= 0)
            def _():
                fetch(nxt)

        gather_wait(cur)
        gather_unrolled(jnp.minimum(b + 1, nb_ref[0] - 1), 1 - cur)
        lo, hi = _unpack_pair(xbuf[cur])
        x = jnp.concatenate([lo, hi], axis=1).astype(BF16)
        g = jnp.dot(x, wb_g[...], preferred_element_type=F32)
        u = jnp.dot(x, wb_u[...], preferred_element_type=F32)
        a = (g * jax.nn.sigmoid(g) * u).astype(BF16)
        y = jnp.dot(a, wb_d[...], preferred_element_type=F32)
        half = y.shape[1] // 2
        o_ref[...] = _pack_pair(y[:, :half], y[:, half:])

        @pl.when(b == nb_ref[0] - 1)
        def _():
            gather_wait(1 - cur)

    @pl.when(b >= nb_ref[0])
    def _():
        o_ref[...] = jnp.zeros_like(o_ref)


def _experts(block_e, n_used, next_e, slot_tok, hp, wg, wu, wd, sg, su, sd, *, tmb):
    n_slots = slot_tok.shape[0]
    wp = hp.shape[1]
    d, ff = wg.shape[1], wg.shape[2]
    nb = n_slots // tmb
    grid_spec = pltpu.PrefetchScalarGridSpec(
        num_scalar_prefetch=4,
        grid=(nb,),
        in_specs=[pl.BlockSpec(memory_space=pl.ANY)] * 7,
        out_specs=pl.BlockSpec((tmb, wp), lambda b, *_: (b, 0)),
        scratch_shapes=[pltpu.VMEM((d, ff), F32), pltpu.VMEM((d, ff), F32), pltpu.VMEM((ff, d), F32),
                        pltpu.VMEM((d, ff), BF16), pltpu.VMEM((d, ff), BF16), pltpu.VMEM((ff, d), BF16),
                        pltpu.VMEM((2, tmb, wp), U32),
                        pltpu.SemaphoreType.DMA((3,)), pltpu.SemaphoreType.DMA((2,))],
    )
    return pl.pallas_call(
        functools.partial(_expert_kernel, rc=128),
        grid_spec=grid_spec,
        out_shape=jax.ShapeDtypeStruct((n_slots, wp), U32),
        compiler_params=_params(("arbitrary",), vmem=60 * 1024 * 1024),
        name="moe_experts",
    )(block_e, n_used, next_e, slot_tok, hp, wg, wu, wd, sg, su, sd)


def _combine_kernel(slot_ref, w_ref, x1_ref, gt_ref, g_ref, ys_hbm, o_ref, buf, sems, *, rc, n_k):
    tc, d = x1_ref.shape
    half = d // 2
    i = pl.program_id(0)
    cur = lax.rem(i, 2)
    last = pl.num_programs(0) - 1

    def issue_rows(tile, slot, t0):
        base = tile * (tc * n_k)
        for t in range(rc):
            for k in range(n_k):
                _row_copy(ys_hbm, buf.at[slot, k], sems.at[slot], slot_ref[base + (t0 + t) * n_k + k],
                          t0 + t).start(priority=k % 2)

    def wait_planes(slot):
        for k in range(n_k):
            pltpu.make_async_copy(ys_hbm.at[pl.ds(0, tc), :], buf.at[slot, k], sems.at[slot]).wait()

    @pl.when(i == 0)
    def _():
        _row_loop(tc, rc, lambda r: issue_rows(0, 0, r))

    wait_planes(cur)
    nxt = jnp.minimum(i + 1, last)

    def body(r):
        issue_rows(nxt, 1 - cur, r)
        w = w_ref[pl.ds(r, rc), :]
        lo = jnp.zeros((rc, half), F32)
        hi = jnp.zeros((rc, half), F32)
        for k in range(n_k):
            a, b = _unpack_pair(buf[cur, k, pl.ds(r, rc), :])
            lo = lo + w[:, k:k + 1] * a
            hi = hi + w[:, k:k + 1] * b
        ms = (jnp.sum(lo * lo, axis=-1, keepdims=True) + jnp.sum(hi * hi, axis=-1, keepdims=True)) * (1.0 / d)
        rs = lax.rsqrt(ms + NORM_EPS)
        o_ref[pl.ds(r, rc), 0:half] = (x1_ref[pl.ds(r, rc), 0:half]
                                       + gt_ref[:, 0:half] * (lo * rs * g_ref[:, 0:half]))
        o_ref[pl.ds(r, rc), half:d] = (x1_ref[pl.ds(r, rc), half:d]
                                       + gt_ref[:, half:d] * (hi * rs * g_ref[:, half:d]))
    _row_loop(tc, rc, body)

    @pl.when(i == last)
    def _():
        wait_planes(1 - cur)


def _combine(slots, w_tok, x1, gt2, g_post, ys, *, tc):
    s_len, d = x1.shape
    n_k = slots.shape[0] // s_len
    tc = min(tc, s_len)
    grid_spec = pltpu.PrefetchScalarGridSpec(
        num_scalar_prefetch=1,
        grid=(s_len // tc,),
        in_specs=[pl.BlockSpec((tc, w_tok.shape[1]), lambda i, sl: (i, 0)),
                  pl.BlockSpec((tc, d), lambda i, sl: (i, 0)),
                  pl.BlockSpec((1, d), lambda i, sl: (0, 0)),
                  pl.BlockSpec((1, d), lambda i, sl: (0, 0)),
                  pl.BlockSpec(memory_space=pl.ANY)],
        out_specs=pl.BlockSpec((tc, d), lambda i, sl: (i, 0)),
        scratch_shapes=[pltpu.VMEM((2, n_k, tc, d // 2), U32), pltpu.SemaphoreType.DMA((2,))],
    )
    return pl.pallas_call(
        functools.partial(_combine_kernel, rc=8, n_k=n_k),
        grid_spec=grid_spec,
        out_shape=jax.ShapeDtypeStruct((s_len, d), F32),
        compiler_params=_params(("arbitrary",)),
        name="moe_combine",
    )(slots, w_tok, x1, gt2.reshape(1, d), g_post.reshape(1, d), ys)


def _spread_stride(n):
    m = int(n * 0.6180339887) | 1
    while math.gcd(m, n) != 1:
        m += 2
    return m


def _moe_plan(idx_t, wts_t, *, tmb):
    n_k, n_tok = idx_t.shape
    n_e = N_EXPERTS + 1
    eid = jnp.concatenate([idx_t, jnp.full((1, n_tok), N_EXPERTS, I32)], axis=0)
    wts = jnp.concatenate([wts_t, jnp.ones((1, n_tok), F32)], axis=0)
    sel = (eid[:, None, :] == jnp.arange(n_e, dtype=I32)[None, :, None]).any(axis=0).astype(I32)
    stride = _spread_stride(n_tok)
    visit = lax.rem(jnp.arange(n_tok, dtype=I32) * stride, n_tok)
    where = lax.rem(jnp.arange(n_tok, dtype=I32) * pow(stride, -1, n_tok), n_tok)
    csum = jnp.cumsum(sel[:, visit], axis=1)
    counts = csum[:, -1]
    padded = (counts + tmb - 1) // tmb * tmb
    ends = jnp.cumsum(padded)
    starts = ends - padded
    slot_dense = (starts[:, None] + csum - 1)[:, where]
    slot = jnp.take_along_axis(slot_dense, eid, axis=0)
    n_slots = -(-(n_tok * (n_k + 1) + n_e * (tmb - 1)) // tmb) * tmb
    tok = jnp.broadcast_to(jnp.arange(n_tok, dtype=I32)[None, :], slot.shape)
    filler = lax.rem(jnp.arange(n_slots, dtype=I32) * stride, n_tok)
    slot_tok = filler.at[slot.reshape(-1)].set(tok.reshape(-1))
    n_used = (ends[-1] // tmb).astype(I32).reshape(1)
    block_start = jnp.arange(n_slots // tmb, dtype=I32) * tmb
    block_e = jnp.minimum(jnp.searchsorted(ends, block_start, side='right'), n_e - 1).astype(I32)
    slots_tok_major = slot.T.reshape(-1)
    w_tok = jnp.zeros((n_tok, 16), F32).at[:, :n_k + 1].set(wts.T)
    owner = jnp.where(padded > 0, jnp.arange(n_e, dtype=I32), n_e)
    later = jnp.concatenate([lax.cummin(owner[::-1])[::-1][1:], jnp.full((1,), n_e, I32)])
    next_e = jnp.where(later >= n_e, -1, later).astype(I32)
    return slot_tok, block_e, n_used, next_e, slots_tok_major, w_tok


def _in_proj_weights(w_in):
    d = w_in.shape[0]
    w_main = jnp.pad(w_in.astype(BF16), ((0, 0), (0, N_IN_PAD - N_IN)))
    kr = w_in[:, COL_KR:COL_KR + A_ROPE_DIM]
    gates = w_in[:, COL_IG:COL_IG + 2 * M_HEADS]
    half = A_ROPE_DIM // 2
    kswap = jnp.concatenate([-kr[:, half:], kr[:, :half]], axis=1)
    z = lambda n: jnp.zeros((d, n), w_in.dtype)
    w_tail = jnp.concatenate([kswap, gates, z(LANES - A_ROPE_DIM - 2 * M_HEADS), kr, z(LANES - A_ROPE_DIM)], axis=1)
    return w_main, w_tail.astype(BF16)


def _q_up_weight(w_uq):
    r = w_uq.shape[0]
    w = w_uq.reshape(r, A_HEADS, A_NOPE_DIM + A_ROPE_DIM)
    nope = w[:, :, :A_NOPE_DIM]
    rope = w[:, :, A_NOPE_DIM:]
    half = A_ROPE_DIM // 2
    swap = jnp.concatenate([-rope[:, :, half:], rope[:, :, :half]], axis=2)
    pad = jnp.zeros((r, A_HEADS, LANES - A_ROPE_DIM), w_uq.dtype)
    rope_p = jnp.concatenate([rope, pad], axis=2)
    swap_p = jnp.concatenate([swap, pad], axis=2)
    flat = lambda a: a.reshape(r, -1)
    return jnp.concatenate([flat(nope), flat(rope_p), flat(swap_p)], axis=1).astype(BF16)


def _kv_up_weight(w_ukv):
    r = w_ukv.shape[0]
    w = w_ukv.reshape(r, A_HEADS, A_NOPE_DIM + A_V_DIM)
    return jnp.concatenate([w[:, :, :A_NOPE_DIM].reshape(r, -1), w[:, :, A_NOPE_DIM:].reshape(r, -1)],
                           axis=1).astype(BF16)


def _rope_tables(s_len):
    pos = jnp.arange(s_len, dtype=F32)
    inv_freq = 1.0 / (ROPE_THETA ** (jnp.arange(0, A_ROPE_DIM, 2, dtype=F32) / A_ROPE_DIM))
    ang = pos[:, None] * inv_freq[None, :]
    pad = jnp.zeros((s_len, LANES - A_ROPE_DIM), F32)
    cos_t = jnp.concatenate([jnp.cos(ang), jnp.cos(ang), pad], axis=1)
    sin_t = jnp.concatenate([jnp.sin(ang), jnp.sin(ang), pad], axis=1)
    return cos_t, sin_t


def _block(x, c, w_ada, b_ada, g_pre_mix, g_post_mix, w_in, conv_w, conv_b, b_igate, b_fgate, g_mlstm_out,
           g_q_norm, w_uq, g_kv_norm, w_ukv, w_out, g_pre_ffn, g_post_ffn, w_router, b_router,
           w_gate, w_up, w_down, w_shared_gate, w_shared_up, w_shared_down):
    s_len, d = x.shape
    mod = _adaln(c, w_ada, b_ada)[0]
    sh1, sc1, gt1, sh2, sc2, gt2 = [mod[i * d:(i + 1) * d] for i in range(6)]

    w_main, w_tail = _in_proj_weights(w_in)
    h1 = _prenorm(x, g_pre_mix * (1.0 + sc1), sh1, tm=256)
    proj, tail = _mm_tail(h1, w_main, w_tail, tm=1024, tn=IN_TILE)
    h_m = _mlstm(proj, tail, conv_w, conv_b, b_igate, b_fgate, g_mlstm_out, chunk=min(M_CHUNK, s_len))
    scale = (A_NOPE_DIM + A_ROPE_DIM) ** -0.5
    q_raw = _norm_mm(proj[:, COL_CQ:COL_CQ + A_Q_RANK], g_q_norm * scale, _q_up_weight(w_uq), tm=1024, tn=1536)
    kv_raw = _norm_mm(proj[:, COL_CKV:COL_CKV + A_KV_RANK], g_kv_norm, _kv_up_weight(w_ukv), tm=1024, tn=1024)
    cos_t, sin_t = _rope_tables(s_len)
    q_cat, k_cat, v_cat = _rope_assemble(q_raw, kv_raw, tail, cos_t, sin_t, tm=256)
    h_a = _attention(q_cat, k_cat, v_cat, tq=1024, chunk=CHUNK)
    y = _mm2(h_m, h_a, w_out.astype(BF16), tm=1024, tn=1024)

    x1, hp, idx_t, wts_t = _mid(y, x, gt1, g_post_mix, g_pre_ffn * (1.0 + sc2), sh2, w_router, b_router, tm=256)
    tmb = 256
    slot_tok, block_e, n_used, next_e, slots, w_tok = _moe_plan(idx_t, wts_t, tmb=tmb)
    ys = _experts(block_e, n_used, next_e, slot_tok, hp, w_gate, w_up, w_down,
                  w_shared_gate, w_shared_up, w_shared_down, tmb=tmb)
    return _combine(slots, w_tok, x1, gt2, g_post_ffn, ys, tc=128)


def kernel(x, c, w_ada, b_ada, g_pre_mix, g_post_mix, w_in, conv_w, conv_b, b_igate, b_fgate, g_mlstm_out,
           g_q_norm, w_uq, g_kv_norm, w_ukv, w_out, g_pre_ffn, g_post_ffn, w_router, b_router,
           w_gate, w_up, w_down, w_shared_gate, w_shared_up, w_shared_down):
    assert x.shape[0] == 1 and w_ada.shape[0] == 1, "single sequence, single layer"
    layer = (w_ada, b_ada, g_pre_mix, g_post_mix, w_in, conv_w, conv_b, b_igate, b_fgate, g_mlstm_out,
             g_q_norm, w_uq, g_kv_norm, w_ukv, w_out, g_pre_ffn, g_post_ffn, w_router, b_router,
             w_gate, w_up, w_down, w_shared_gate, w_shared_up, w_shared_down)
    out = _block(x[0], c[0], *[p[0] for p in layer])
    return out[None]
```

```python
import functools
import math

import jax
import jax.numpy as jnp
from jax import lax
from jax.experimental import pallas as pl
from jax.experimental.pallas import tpu as pltpu

F32 = jnp.float32
BF16 = jnp.bfloat16
I32 = jnp.int32
U32 = jnp.uint32

NORM_EPS = 1e-6
CHUNK = 64
M_CHUNK = 128

M_HEADS = 4
M_QK_DIM = 256
M_V_DIM = 512
M_WIDTH = M_HEADS * M_V_DIM
M_QK_WIDTH = 2 * M_HEADS * M_QK_DIM
CONV_WIDTH = 4
GATE_SOFTCAP = 15.0

A_HEADS = 16
A_NOPE_DIM = 128
A_ROPE_DIM = 64
A_V_DIM = 128
A_Q_RANK = 768
A_KV_RANK = 512
A_HEAD_PAD = 256
ROPE_THETA = 10000.0

N_EXPERTS = 64
TOP_K = 8
N_GROUPS = 8
GROUP_SIZE = N_EXPERTS // N_GROUPS
TOPK_GROUPS = 4
ROUTED_SCALE = 2.5

LANES = 128
VMEM_LIMIT = 56 * 1024 * 1024
NEG_BIG = -1e30

COL_QK = 0
COL_V = COL_QK + M_QK_WIDTH
COL_O = COL_V + M_WIDTH
COL_IG = COL_O + M_WIDTH
COL_FG = COL_IG + M_HEADS
COL_CQ = COL_FG + M_HEADS
COL_CKV = COL_CQ + A_Q_RANK
COL_KR = COL_CKV + A_KV_RANK
N_IN = COL_KR + A_ROPE_DIM
IN_TILE = 1536
TAIL_W = 2 * LANES
TAIL_GATE_LANE = A_ROPE_DIM


def _params(sem, vmem=VMEM_LIMIT):
    return pltpu.CompilerParams(dimension_semantics=sem, vmem_limit_bytes=vmem)


def _row_loop(n_rows, rc, body):
    def step(i, carry):
        body(pl.multiple_of(i * rc, rc))
        return carry
    lax.fori_loop(0, n_rows // rc, step, 0)


def _pack_pair(a, b):
    lo = lax.bitcast_convert_type(a.astype(BF16).astype(F32), U32) >> 16
    hi = lax.bitcast_convert_type(b.astype(BF16).astype(F32), U32) & jnp.uint32(0xFFFF0000)
    return lo | hi


def _unpack_pair(u):
    lo = lax.bitcast_convert_type(u << 16, F32)
    hi = lax.bitcast_convert_type(u & jnp.uint32(0xFFFF0000), F32)
    return lo, hi


def _adaln_kernel(c_ref, w_ref, b_ref, o_ref, *, rc):
    d, tn = w_ref.shape
    nl = tn // LANES

    def step(i, accs):
        r = pl.multiple_of(i * rc, rc)
        c = c_ref[pl.ds(r, rc), :]
        ca = c * jax.nn.sigmoid(c)
        out = []
        for j in range(nl):
            prod = w_ref[pl.ds(r, rc), j * LANES:(j + 1) * LANES] * ca
            out.append(accs[j] + jnp.sum(prod.reshape(rc // 8, 8, LANES), axis=0))
        return tuple(out)

    accs = lax.fori_loop(0, d // rc, step, tuple(jnp.zeros((8, LANES), F32) for _ in range(nl)))
    for j in range(nl):
        o_ref[:, j * LANES:(j + 1) * LANES] = (
            jnp.sum(accs[j], axis=0, keepdims=True) + b_ref[:, j * LANES:(j + 1) * LANES])


def _adaln(c, w_ada, b_ada):
    d, n = w_ada.shape
    tn = 512
    c_b = jnp.broadcast_to(c.reshape(d, 1), (d, LANES))
    return pl.pallas_call(
        functools.partial(_adaln_kernel, rc=64),
        grid=(n // tn,),
        in_specs=[pl.BlockSpec((d, LANES), lambda j: (0, 0)),
                  pl.BlockSpec((d, tn), lambda j: (0, j)),
                  pl.BlockSpec((1, tn), lambda j: (0, j))],
        out_specs=pl.BlockSpec((1, tn), lambda j: (0, j)),
        out_shape=jax.ShapeDtypeStruct((1, n), F32),
        compiler_params=_params(("arbitrary",)),
        name="adaln",
    )(c_b, w_ada, b_ada.reshape(1, n))


def _norm_mm_kernel(x_ref, g_ref, w_ref, o_ref, h_scr, *, rc):
    tm = x_ref.shape[0]

    @pl.when(pl.program_id(1) == 0)
    def _():
        def body(r):
            x = x_ref[pl.ds(r, rc), :].astype(F32)
            y = x * lax.rsqrt(jnp.mean(x * x, axis=-1, keepdims=True) + NORM_EPS)
            h_scr[pl.ds(r, rc), :] = (y * g_ref[...]).astype(BF16)
        _row_loop(tm, rc, body)

    o_ref[...] = jnp.dot(h_scr[...], w_ref[...], preferred_element_type=F32).astype(o_ref.dtype)


def _norm_mm(x, g, w, *, tm, tn):
    m, k = x.shape
    n = w.shape[1]
    tm = min(tm, m)
    return pl.pallas_call(
        functools.partial(_norm_mm_kernel, rc=32),
        grid=(m // tm, n // tn),
        in_specs=[pl.BlockSpec((tm, k), lambda i, j: (i, 0)),
                  pl.BlockSpec((1, k), lambda i, j: (0, 0)),
                  pl.BlockSpec((k, tn), lambda i, j: (0, j))],
        out_specs=pl.BlockSpec((tm, tn), lambda i, j: (i, j)),
        out_shape=jax.ShapeDtypeStruct((m, n), BF16),
        scratch_shapes=[pltpu.VMEM((tm, k), BF16)],
        compiler_params=_params(("parallel", "arbitrary")),
        name="norm_mm",
    )(x, g.reshape(1, k), w)


def _prenorm_kernel(x_ref, gs_ref, sh_ref, o_ref, *, rc):
    def body(r):
        x = x_ref[pl.ds(r, rc), :]
        y = x * lax.rsqrt(jnp.mean(x * x, axis=-1, keepdims=True) + NORM_EPS)
        o_ref[pl.ds(r, rc), :] = (y * gs_ref[...] + sh_ref[...]).astype(o_ref.dtype)
    _row_loop(x_ref.shape[0], rc, body)


def _prenorm(x, gs, sh, *, tm):
    m, k = x.shape
    tm = min(tm, m)
    return pl.pallas_call(
        functools.partial(_prenorm_kernel, rc=32),
        grid=(m // tm,),
        in_specs=[pl.BlockSpec((tm, k), lambda i: (i, 0)),
                  pl.BlockSpec((1, k), lambda i: (0, 0)),
                  pl.BlockSpec((1, k), lambda i: (0, 0))],
        out_specs=pl.BlockSpec((tm, k), lambda i: (i, 0)),
        out_shape=jax.ShapeDtypeStruct((m, k), BF16),
        compiler_params=_params(("parallel",)),
        name="prenorm",
    )(x, gs.reshape(1, k), sh.reshape(1, k))


def _mm_tail_kernel(a_ref, wt_ref, wtail_ref, o_ref, t_ref):
    res = lax.dot_general(a_ref[...], wt_ref[...], (((1,), (1,)), ((), ())), preferred_element_type=F32)
    o_ref[...] = res.astype(o_ref.dtype)

    @pl.when(pl.program_id(1) == pl.num_programs(1) - 1)
    def _():
        t_ref[...] = jnp.dot(a_ref[...], wtail_ref[...], preferred_element_type=F32)


def _mm_tail(a, w_t, w_tail, *, tm, tn):
    m, k = a.shape
    n, nt = w_t.shape[0], w_tail.shape[1]
    tm = min(tm, m)
    return pl.pallas_call(
        _mm_tail_kernel,
        grid=(m // tm, pl.cdiv(n, tn)),
        in_specs=[pl.BlockSpec((tm, k), lambda i, j: (i, 0)),
                  pl.BlockSpec((tn, k), lambda i, j: (j, 0)),
                  pl.BlockSpec((k, nt), lambda i, j: (0, 0))],
        out_specs=[pl.BlockSpec((tm, tn), lambda i, j: (i, j)),
                   pl.BlockSpec((tm, nt), lambda i, j: (i, 0))],
        out_shape=[jax.ShapeDtypeStruct((m, n), BF16), jax.ShapeDtypeStruct((m, nt), F32)],
        compiler_params=_params(("parallel", "arbitrary"), vmem=60 * 1024 * 1024),
        name="in_proj",
    )(a, w_t, w_tail)


def _mm2_kernel(a1_ref, a2_ref, w_ref, o_ref):
    k1 = a1_ref.shape[1]
    acc = jnp.dot(a1_ref[...], w_ref[:k1, :], preferred_element_type=F32)
    acc = acc + jnp.dot(a2_ref[...], w_ref[k1:, :], preferred_element_type=F32)
    o_ref[...] = acc.astype(o_ref.dtype)


def _mm2(a1, a2, w, *, tm, tn):
    m, k1 = a1.shape
    k2 = a2.shape[1]
    n = w.shape[1]
    tm = min(tm, m)
    return pl.pallas_call(
        _mm2_kernel,
        grid=(m // tm, n // tn),
        in_specs=[pl.BlockSpec((tm, k1), lambda i, j: (i, 0)),
                  pl.BlockSpec((tm, k2), lambda i, j: (i, 0)),
                  pl.BlockSpec((k1 + k2, tn), lambda i, j: (0, j))],
        out_specs=pl.BlockSpec((tm, tn), lambda i, j: (i, j)),
        out_shape=jax.ShapeDtypeStruct((m, n), BF16),
        compiler_params=_params(("parallel", "arbitrary")),
        name="out_proj",
    )(a1, a2, w)


def _softcap(z):
    return GATE_SOFTCAP * jnp.tanh(z * (1.0 / GATE_SOFTCAP))


def _log_sigmoid(z):
    return jnp.minimum(z, 0.0) - jnp.log1p(jnp.exp(-jnp.abs(z)))


def _mlstm_kernel(qk_ref, prev_ref, v_ref, o_ref, gcol_ref, grow_ref, cw_ref, cb_ref, bcol_ref, brow_ref,
                  gout_ref, out_ref, c_scr, n_scr, m_scr, u_scr):
    c = pl.program_id(0)
    L = qk_ref.shape[0]
    halo = prev_ref.shape[0]
    dk, dv = M_QK_DIM, M_V_DIM

    @pl.when(c == 0)
    def _():
        c_scr[...] = jnp.zeros_like(c_scr)
        n_scr[...] = jnp.zeros_like(n_scr)
        m_scr[...] = jnp.zeros_like(m_scr)

    prev = prev_ref[...].astype(F32)
    u_scr[0:halo, :] = jnp.where(c == 0, jnp.zeros_like(prev), prev)
    u_scr[halo:halo + L, :] = qk_ref[...].astype(F32)

    def conv_silu(col, width):
        acc = cb_ref[:, col:col + width]
        for j in range(CONV_WIDTH):
            r0 = halo - (CONV_WIDTH - 1) + j
            acc = acc + u_scr[r0:r0 + L, col:col + width] * cw_ref[j:j + 1, col:col + width]
        return acc * jax.nn.sigmoid(acc)

    pre_c = _softcap(gcol_ref[...] + bcol_ref[...])
    lf_c = _log_sigmoid(pre_c)
    pre_r = _softcap(grow_ref[0] + brow_ref[...])
    lf_r = _log_sigmoid(pre_r)
    row = lax.broadcasted_iota(I32, (L, L), 0)
    col = lax.broadcasted_iota(I32, (L, L), 1)
    causal = col <= row
    tril = causal.astype(F32)
    triu = (row <= col).astype(F32)
    bcum_c = jnp.dot(tril, lf_c, preferred_element_type=F32, precision=lax.Precision.HIGHEST)
    bcum_r = jnp.dot(lf_r, triu, preferred_element_type=F32, precision=lax.Precision.HIGHEST)

    for h in range(M_HEADS):
        li_lane = TAIL_GATE_LANE + h
        lf_lane = TAIL_GATE_LANE + M_HEADS + h
        q = (conv_silu(h * dk, dk) * (dk ** -0.5)).astype(BF16)
        kf = conv_silu(M_HEADS * dk + h * dk, dk)
        kb = kf.astype(BF16)
        v = v_ref[:, h * dv:(h + 1) * dv]
        b_c = bcum_c[:, lf_lane:lf_lane + 1]
        li_c = pre_c[:, li_lane:li_lane + 1]
        b_r = bcum_r[M_HEADS + h:M_HEADS + h + 1, :]
        li_r = pre_r[h:h + 1, :]
        m_prev = m_scr[h:h + 1, 0:1]

        dm = jnp.where(causal, b_c - b_r + li_r, NEG_BIG)
        inter = b_c + m_prev
        m_t = jnp.maximum(jnp.max(dm, axis=1, keepdims=True), inter)
        decay = jnp.exp(inter - m_t)
        s = lax.dot_general(q, kb, (((1,), (1,)), ((), ())), preferred_element_type=F32) * jnp.exp(dm - m_t)
        c_state = c_scr[h]
        n_state = n_scr[h:h + 1, :]
        num = jnp.dot(s.astype(BF16), v, preferred_element_type=F32)
        num = num + decay * jnp.dot(q, c_state.astype(BF16), preferred_element_type=F32)
        den = jnp.sum(s, axis=1, keepdims=True) + decay * jnp.sum(q.astype(F32) * n_state, axis=1, keepdims=True)
        hh = num / jnp.maximum(jnp.abs(den), jnp.exp(-m_t))

        b_last = b_c[L - 1:L, :]
        m_new = jnp.maximum(b_last + m_prev, jnp.max(b_last - b_r + li_r, axis=1, keepdims=True))
        carry = jnp.exp(b_last + m_prev - m_new)
        w_c = jnp.exp(b_last - b_c + li_c - m_new)
        kw = kf * w_c
        c_scr[h] = carry * c_state + lax.dot_general(kw.astype(BF16), v, (((0,), (0,)), ((), ())),
                                                     preferred_element_type=F32)
        n_scr[h:h + 1, :] = carry * n_state + jnp.sum(kw, axis=0, keepdims=True)
        m_scr[h:h + 1, :] = jnp.broadcast_to(m_new, (1, LANES))

        hn = hh * lax.rsqrt(jnp.mean(hh * hh, axis=1, keepdims=True) + NORM_EPS) * gout_ref[:, h * dv:(h + 1) * dv]
        gate = jax.nn.sigmoid(o_ref[:, h * dv:(h + 1) * dv].astype(F32))
        out_ref[:, h * dv:(h + 1) * dv] = (hn * gate).astype(out_ref.dtype)


def _mlstm(proj, tail, conv_w, conv_b, b_igate, b_fgate, g_out, *, chunk):
    s_len = proj.shape[0]
    L = chunk
    n_c = s_len // L
    halo = 16
    gates = tail[:, TAIL_GATE_LANE:TAIL_GATE_LANE + 2 * M_HEADS]
    g_row = gates.reshape(n_c, L, 2 * M_HEADS).transpose(0, 2, 1)
    bias = jnp.concatenate([b_igate, b_fgate]).astype(F32)
    b_col = jnp.zeros((1, LANES), F32).at[0, TAIL_GATE_LANE:TAIL_GATE_LANE + 2 * M_HEADS].set(bias)
    b_row = bias.reshape(2 * M_HEADS, 1)
    w = M_QK_WIDTH
    return pl.pallas_call(
        _mlstm_kernel,
        grid=(n_c,),
        in_specs=[pl.BlockSpec((L, w), lambda c: (c, COL_QK // w)),
                  pl.BlockSpec((halo, w), lambda c: (jnp.maximum(c * (L // halo) - 1, 0), COL_QK // w)),
                  pl.BlockSpec((L, M_WIDTH), lambda c: (c, COL_V // M_WIDTH)),
                  pl.BlockSpec((L, M_WIDTH), lambda c: (c, COL_O // M_WIDTH)),
                  pl.BlockSpec((L, LANES), lambda c: (c, 0)),
                  pl.BlockSpec((1, 2 * M_HEADS, L), lambda c: (c, 0, 0)),
                  pl.BlockSpec((CONV_WIDTH, w), lambda c: (0, 0)),
                  pl.BlockSpec((1, w), lambda c: (0, 0)),
                  pl.BlockSpec((1, LANES), lambda c: (0, 0)),
                  pl.BlockSpec((2 * M_HEADS, 1), lambda c: (0, 0)),
                  pl.BlockSpec((1, M_WIDTH), lambda c: (0, 0))],
        out_specs=pl.BlockSpec((L, M_WIDTH), lambda c: (c, 0)),
        out_shape=jax.ShapeDtypeStruct((s_len, M_WIDTH), BF16),
        scratch_shapes=[pltpu.VMEM((M_HEADS, M_QK_DIM, M_V_DIM), F32),
                        pltpu.VMEM((8, M_QK_DIM), F32),
                        pltpu.VMEM((8, LANES), F32),
                        pltpu.VMEM((halo + L, w), F32)],
        compiler_params=_params(("arbitrary",)),
        name="mlstm",
    )(proj, proj, proj, proj, tail, g_row, conv_w, conv_b.reshape(1, w), b_col, b_row, g_out.reshape(1, M_WIDTH))


def _rope_kernel(q_ref, kn_ref, v_ref, tail_ref, cos_ref, sin_ref, qo_ref, ko_ref, vo_ref):
    cos = cos_ref[...]
    sin = sin_ref[...]
    nope_w = A_HEADS * A_NOPE_DIM
    kpe = (tail_ref[:, LANES:2 * LANES] * cos + tail_ref[:, 0:LANES] * sin).astype(BF16)
    lane = lax.broadcasted_iota(I32, (q_ref.shape[0], LANES), 1)
    ones_col = jnp.where(lane == 0, 1.0, 0.0).astype(BF16)
    for h in range(A_HEADS):
        vo_ref[:, h * A_HEAD_PAD:h * A_HEAD_PAD + LANES] = v_ref[:, h * LANES:(h + 1) * LANES]
        vo_ref[:, h * A_HEAD_PAD + LANES:(h + 1) * A_HEAD_PAD] = ones_col
        lo = h * A_HEAD_PAD
        qo_ref[:, lo:lo + LANES] = q_ref[:, h * LANES:(h + 1) * LANES]
        qr = q_ref[:, nope_w + h * LANES:nope_w + (h + 1) * LANES].astype(F32)
        qs = q_ref[:, 2 * nope_w + h * LANES:2 * nope_w + (h + 1) * LANES].astype(F32)
        qo_ref[:, lo + LANES:lo + 2 * LANES] = (qr * cos + qs * sin).astype(BF16)
        ko_ref[:, lo:lo + LANES] = kn_ref[:, h * LANES:(h + 1) * LANES]
        ko_ref[:, lo + LANES:lo + 2 * LANES] = kpe


def _rope_assemble(q_raw, kv_raw, tail, cos_t, sin_t, *, tm):
    s_len = q_raw.shape[0]
    tm = min(tm, s_len)
    nope_w = A_HEADS * A_NOPE_DIM
    wide = A_HEADS * A_HEAD_PAD
    return pl.pallas_call(
        _rope_kernel,
        grid=(s_len // tm,),
        in_specs=[pl.BlockSpec((tm, 3 * nope_w), lambda i: (i, 0)),
                  pl.BlockSpec((tm, nope_w), lambda i: (i, 0)),
                  pl.BlockSpec((tm, nope_w), lambda i: (i, 1)),
                  pl.BlockSpec((tm, 2 * LANES), lambda i: (i, 0)),
                  pl.BlockSpec((tm, LANES), lambda i: (i, 0)),
                  pl.BlockSpec((tm, LANES), lambda i: (i, 0))],
        out_specs=[pl.BlockSpec((tm, wide), lambda i: (i, 0))] * 3,
        out_shape=[jax.ShapeDtypeStruct((s_len, wide), BF16)] * 3,
        compiler_params=_params(("parallel",)),
        name="rope_assemble",
    )(q_raw, kv_raw, kv_raw, tail, cos_t, sin_t)


def _attn_kernel(q_ref, k_ref, v_ref, o_ref, m_scr, acc_scr, s_scr, *, chunk):
    qi = pl.program_id(1)
    tq = q_ref.shape[0]
    m_scr[...] = jnp.full_like(m_scr, NEG_BIG)
    acc_scr[...] = jnp.zeros_like(acc_scr)

    def scores(slot, blk):
        start = pl.multiple_of(blk * tq, tq)
        s_scr[slot] = lax.dot_general(q_ref[...], k_ref[pl.ds(start, tq), :], (((1,), (1,)), ((), ())),
                                      preferred_element_type=F32)

    def consume(slot, blk, masked):
        start = pl.multiple_of(blk * tq, tq)
        s = s_scr[slot]
        if masked:
            rq = lax.broadcasted_iota(I32, (tq, tq), 0) // chunk
            ck = lax.broadcasted_iota(I32, (tq, tq), 1) // chunk
            s = jnp.where(ck <= rq, s, NEG_BIG)
        m_prev = m_scr[...]
        m_new = jnp.maximum(m_prev, jnp.max(s, axis=1, keepdims=True))
        pr = jnp.exp(s - m_new).astype(BF16)
        acc_scr[...] = (jnp.exp(m_prev - m_new) * acc_scr[...]
                        + jnp.dot(pr, v_ref[pl.ds(start, tq), :], preferred_element_type=F32))
        m_scr[...] = m_new

    scores(0, 0)

    def pair(t, carry):
        scores(1, 2 * t + 1)
        consume(0, 2 * t, False)
        scores(0, 2 * t + 2)
        consume(1, 2 * t + 1, False)
        return carry
    lax.fori_loop(0, qi // 2, pair, 0)

    @pl.when(qi % 2 == 1)
    def _():
        scores(1, qi)
        consume(0, qi - 1, False)
        consume(1, qi, True)

    @pl.when(qi % 2 == 0)
    def _():
        consume(0, qi, True)

    acc = acc_scr[...]
    o_ref[...] = (acc[:, :A_V_DIM] / acc[:, A_V_DIM:A_V_DIM + 1]).astype(o_ref.dtype)


def _attention(q_cat, k_cat, v_cat, *, tq, chunk):
    s_len = q_cat.shape[0]
    tq = min(tq, s_len)
    return pl.pallas_call(
        functools.partial(_attn_kernel, chunk=chunk),
        grid=(A_HEADS, s_len // tq),
        in_specs=[pl.BlockSpec((tq, A_HEAD_PAD), lambda h, i: (i, h)),
                  pl.BlockSpec((s_len, A_HEAD_PAD), lambda h, i: (0, h)),
                  pl.BlockSpec((s_len, A_HEAD_PAD), lambda h, i: (0, h))],
        out_specs=pl.BlockSpec((tq, A_V_DIM), lambda h, i: (i, h)),
        out_shape=jax.ShapeDtypeStruct((s_len, A_HEADS * A_V_DIM), BF16),
        scratch_shapes=[pltpu.VMEM((tq, 1), F32), pltpu.VMEM((tq, A_HEAD_PAD), F32),
                        pltpu.VMEM((2, tq, tq), F32)],
        compiler_params=_params(("parallel", "arbitrary")),
        name="mla_attention",
    )(q_cat, k_cat, v_cat)


def _route_t(logits, bias_col):
    n = logits.shape[1]
    scores = jax.nn.sigmoid(logits)
    biased = scores + bias_col
    sub = lax.broadcasted_iota(I32, (GROUP_SIZE, n), 0)
    rows = []
    for g in range(N_GROUPS):
        x = biased[g * GROUP_SIZE:(g + 1) * GROUP_SIZE, :]
        m1 = jnp.max(x, axis=0, keepdims=True)
        i1 = jnp.min(jnp.where(x == m1, sub, GROUP_SIZE), axis=0, keepdims=True)
        m2 = jnp.max(jnp.where(sub == i1, -jnp.inf, x), axis=0, keepdims=True)
        rows.append(m1 + m2)
    gscore = jnp.concatenate(rows, axis=0)
    gio = lax.broadcasted_iota(I32, (N_GROUPS, n), 0)
    grank = jnp.zeros((N_GROUPS, n), I32)
    for g in range(N_GROUPS):
        r = gscore[g:g + 1, :]
        grank = grank + jnp.where(gio > g, jnp.where(r >= gscore, 1, 0), jnp.where(r > gscore, 1, 0))
    gsel = grank < TOPK_GROUPS
    masked = jnp.concatenate(
        [jnp.where(gsel[g:g + 1, :], biased[g * GROUP_SIZE:(g + 1) * GROUP_SIZE, :], -jnp.inf)
         for g in range(N_GROUPS)], axis=0)
    eio = lax.broadcasted_iota(I32, (N_EXPERTS, n), 0)
    rank = jnp.zeros((N_EXPERTS, n), I32)
    for e in range(N_EXPERTS):
        r = masked[e:e + 1, :]
        rank = rank + jnp.where(eio > e, jnp.where(r >= masked, 1, 0), jnp.where(r > masked, 1, 0))
    sel = rank < TOP_K
    denom = jnp.sum(jnp.where(sel, scores, 0.0), axis=0, keepdims=True)
    wnorm = scores / denom * ROUTED_SCALE
    eio_f = eio.astype(F32)
    ids, wts = [], []
    for k in range(TOP_K):
        hit = rank == k
        ids.append(jnp.sum(jnp.where(hit, eio_f, 0.0), axis=0, keepdims=True))
        wts.append(jnp.sum(jnp.where(hit, wnorm, 0.0), axis=0, keepdims=True))
    return jnp.concatenate(ids, axis=0).astype(I32), jnp.concatenate(wts, axis=0)


def _mid_kernel(y_ref, x_ref, gt_ref, gpost_ref, gs_ref, sh_ref, wr_ref, br_ref,
                x1_ref, hp_ref, idx_ref, wts_ref, h_scr, *, rc):
    tm, d = x_ref.shape
    half = d // 2

    def body(r):
        y = y_ref[pl.ds(r, rc), :].astype(F32)
        yn = y * lax.rsqrt(jnp.mean(y * y, axis=-1, keepdims=True) + NORM_EPS) * gpost_ref[...]
        x1 = x_ref[pl.ds(r, rc), :] + gt_ref[...] * yn
        x1_ref[pl.ds(r, rc), :] = x1
        hn = x1 * lax.rsqrt(jnp.mean(x1 * x1, axis=-1, keepdims=True) + NORM_EPS)
        h = hn * gs_ref[...] + sh_ref[...]
        h_scr[pl.ds(r, rc), :] = h
        hp_ref[pl.ds(r, rc), :] = _pack_pair(h[:, :half], h[:, half:])
    _row_loop(tm, rc, body)

    logits = lax.dot_general(wr_ref[...], h_scr[...], (((1,), (1,)), ((), ())),
                             preferred_element_type=F32, precision=lax.Precision.HIGHEST)
    ids, wts = _route_t(logits, br_ref[...])
    idx_ref[...] = ids
    wts_ref[...] = wts


def _mid(y, x, gt1, g_post, gs2, sh2, w_router, b_router, *, tm):
    s_len, d = x.shape
    tm = min(tm, s_len)
    vec = lambda a: a.reshape(1, d).astype(F32)
    return pl.pallas_call(
        functools.partial(_mid_kernel, rc=16),
        grid=(s_len // tm,),
        in_specs=[pl.BlockSpec((tm, d), lambda i: (i, 0)),
                  pl.BlockSpec((tm, d), lambda i: (i, 0)),
                  pl.BlockSpec((1, d), lambda i: (0, 0)),
                  pl.BlockSpec((1, d), lambda i: (0, 0)),
                  pl.BlockSpec((1, d), lambda i: (0, 0)),
                  pl.BlockSpec((1, d), lambda i: (0, 0)),
                  pl.BlockSpec((N_EXPERTS, d), lambda i: (0, 0)),
                  pl.BlockSpec((N_EXPERTS, 1), lambda i: (0, 0))],
        out_specs=[pl.BlockSpec((tm, d), lambda i: (i, 0)),
                   pl.BlockSpec((tm, d // 2), lambda i: (i, 0)),
                   pl.BlockSpec((TOP_K, tm), lambda i: (0, i)),
                   pl.BlockSpec((TOP_K, tm), lambda i: (0, i))],
        out_shape=[jax.ShapeDtypeStruct((s_len, d), F32),
                   jax.ShapeDtypeStruct((s_len, d // 2), U32),
                   jax.ShapeDtypeStruct((TOP_K, s_len), I32),
                   jax.ShapeDtypeStruct((TOP_K, s_len), F32)],
        scratch_shapes=[pltpu.VMEM((tm, d), F32)],
        compiler_params=_params(("parallel",)),
        name="mid_norm_route",
    )(y, x, vec(gt1), vec(g_post), vec(gs2), vec(sh2), w_router.T.astype(F32), b_router.reshape(N_EXPERTS, 1))


def _row_copy(src_hbm, dst_vmem, sem, src_row, dst_row):
    return pltpu.make_async_copy(src_hbm.at[pl.ds(src_row, 1), :], dst_vmem.at[pl.ds(dst_row, 1), :], sem)


ROW_GROUP = 8
ROW_DMA_PRIORITY = 0
WEIGHT_DMA_PRIORITY = 1


def _expert_kernel(be_ref, nb_ref, nxt_ref, tok_ref, h_hbm, wg_hbm, wu_hbm, wd_hbm, sg_hbm, su_hbm, sd_hbm,
                   o_ref, stage_g, stage_u, stage_d, wb_g, wb_u, wb_d, xbuf, sems, xsems, *, rc):
    b = pl.program_id(0)
    e = be_ref[b]
    tmb = xbuf.shape[1]
    cur = lax.rem(b, 2)
    stages = (stage_g, stage_u, stage_d)
    routed = (wg_hbm, wu_hbm, wd_hbm)
    shared = (sg_hbm, su_hbm, sd_hbm)

    def gather_loop(blk, slot):
        base = blk * tmb

        def step(gi, carry):
            r0 = gi * ROW_GROUP
            for u in range(ROW_GROUP):
                _row_copy(h_hbm, xbuf.at[slot], xsems.at[slot], tok_ref[base + r0 + u], r0 + u).start(
                    priority=ROW_DMA_PRIORITY)
            return carry
        lax.fori_loop(0, tmb // ROW_GROUP, step, 0)

    def gather_unrolled(blk, slot):
        base = blk * tmb
        for r in range(tmb):
            _row_copy(h_hbm, xbuf.at[slot], xsems.at[slot], tok_ref[base + r], r).start(priority=ROW_DMA_PRIORITY)

    def gather_wait(slot):
        pltpu.make_async_copy(h_hbm.at[pl.ds(0, tmb), :], xbuf.at[slot], xsems.at[slot]).wait()

    def fetch(ex):
        @pl.when(ex < N_EXPERTS)
        def _():
            for i in range(3):
                pltpu.make_async_copy(routed[i].at[ex], stages[i], sems.at[i]).start(priority=WEIGHT_DMA_PRIORITY)

        @pl.when(ex == N_EXPERTS)
        def _():
            for i in range(3):
                pltpu.make_async_copy(shared[i], stages[i], sems.at[i]).start(priority=WEIGHT_DMA_PRIORITY)

    @pl.when(b < nb_ref[0])
    def _():
        @pl.when(b == 0)
        def _():
            fetch(e)
            gather_loop(0, 0)

        is_first = jnp.logical_or(b == 0, be_ref[jnp.maximum(b - 1, 0)] != e)

        @pl.when(is_first)
        def _():
            for i in range(3):
                pltpu.make_async_copy(shared[i], stages[i], sems.at[i]).wait()
            for src, dst in ((stage_g, wb_g), (stage_u, wb_u), (stage_d, wb_d)):
                def cast(r, src=src, dst=dst):
                    dst[pl.ds(r, rc), :] = src[pl.ds(r, rc), :].astype(BF16)
                _row_loop(src.shape[0], rc, cast)
            nxt = nxt_ref[e]

            @pl.when(nxt >= 0)
            def _():
                fetch(nxt)

        gather_wait(cur)
        gather_unrolled(jnp.minimum(b + 1, nb_ref[0] - 1), 1 - cur)
        lo, hi = _unpack_pair(xbuf[cur])
        x = jnp.concatenate([lo, hi], axis=1).astype(BF16)
        g = jnp.dot(x, wb_g[...], preferred_element_type=F32)
        u = jnp.dot(x, wb_u[...], preferred_element_type=F32)
        a = (g * jax.nn.sigmoid(g) * u).astype(BF16)
        y = jnp.dot(a, wb_d[...], preferred_element_type=F32)
        half = y.shape[1] // 2
        o_ref[...] = _pack_pair(y[:, :half], y[:, half:])

        @pl.when(b == nb_ref[0] - 1)
        def _():
            gather_wait(1 - cur)

    @pl.when(b >= nb_ref[0])
    def _():
        o_ref[...] = jnp.zeros_like(o_ref)


def _experts(block_e, n_used, next_e, slot_tok, hp, wg, wu, wd, sg, su, sd, *, tmb):
    n_slots = slot_tok.shape[0]
    wp = hp.shape[1]
    d, ff = wg.shape[1], wg.shape[2]
    nb = n_slots // tmb
    grid_spec = pltpu.PrefetchScalarGridSpec(
        num_scalar_prefetch=4,
        grid=(nb,),
        in_specs=[pl.BlockSpec(memory_space=pl.ANY)] * 7,
        out_specs=pl.BlockSpec((tmb, wp), lambda b, *_: (b, 0)),
        scratch_shapes=[pltpu.VMEM((d, ff), F32), pltpu.VMEM((d, ff), F32), pltpu.VMEM((ff, d), F32),
                        pltpu.VMEM((d, ff), BF16), pltpu.VMEM((d, ff), BF16), pltpu.VMEM((ff, d), BF16),
                        pltpu.VMEM((2, tmb, wp), U32),
                        pltpu.SemaphoreType.DMA((3,)), pltpu.SemaphoreType.DMA((2,))],
    )
    return pl.pallas_call(
        functools.partial(_expert_kernel, rc=128),
        grid_spec=grid_spec,
        out_shape=jax.ShapeDtypeStruct((n_slots, wp), U32),
        compiler_params=_params(("arbitrary",), vmem=60 * 1024 * 1024),
        name="moe_experts",
    )(block_e, n_used, next_e, slot_tok, hp, wg, wu, wd, sg, su, sd)


def _combine_kernel(slot_ref, w_ref, x1_ref, gt_ref, g_ref, ys_hbm, o_ref, buf, sems, *, rc, n_k):
    tc, d = x1_ref.shape
    half = d // 2
    i = pl.program_id(0)
    cur = lax.rem(i, 2)
    last = pl.num_programs(0) - 1

    def issue_rows(tile, slot, t0):
        base = tile * (tc * n_k)
        for t in range(rc):
            for k in range(n_k):
                _row_copy(ys_hbm, buf.at[slot, k], sems.at[slot], slot_ref[base + (t0 + t) * n_k + k],
                          t0 + t).start(priority=k % 2)

    def wait_planes(slot):
        for k in range(n_k):
            pltpu.make_async_copy(ys_hbm.at[pl.ds(0, tc), :], buf.at[slot, k], sems.at[slot]).wait()

    @pl.when(i == 0)
    def _():
        _row_loop(tc, rc, lambda r: issue_rows(0, 0, r))

    wait_planes(cur)
    nxt = jnp.minimum(i + 1, last)

    def body(r):
        issue_rows(nxt, 1 - cur, r)
        w = w_ref[pl.ds(r, rc), :]
        lo = jnp.zeros((rc, half), F32)
        hi = jnp.zeros((rc, half), F32)
        for k in range(n_k):
            a, b = _unpack_pair(buf[cur, k, pl.ds(r, rc), :])
            lo = lo + w[:, k:k + 1] * a
            hi = hi + w[:, k:k + 1] * b
        ms = (jnp.sum(lo * lo, axis=-1, keepdims=True) + jnp.sum(hi * hi, axis=-1, keepdims=True)) * (1.0 / d)
        rs = lax.rsqrt(ms + NORM_EPS)
        o_ref[pl.ds(r, rc), 0:half] = (x1_ref[pl.ds(r, rc), 0:half]
                                       + gt_ref[:, 0:half] * (lo * rs * g_ref[:, 0:half]))
        o_ref[pl.ds(r, rc), half:d] = (x1_ref[pl.ds(r, rc), half:d]
                                       + gt_ref[:, half:d] * (hi * rs * g_ref[:, half:d]))
    _row_loop(tc, rc, body)

    @pl.when(i == last)
    def _():
        wait_planes(1 - cur)


def _combine(slots, w_tok, x1, gt2, g_post, ys, *, tc):
    s_len, d = x1.shape
    n_k = slots.shape[0] // s_len
    tc = min(tc, s_len)
    grid_spec = pltpu.PrefetchScalarGridSpec(
        num_scalar_prefetch=1,
        grid=(s_len // tc,),
        in_specs=[pl.BlockSpec((tc, w_tok.shape[1]), lambda i, sl: (i, 0)),
                  pl.BlockSpec((tc, d), lambda i, sl: (i, 0)),
                  pl.BlockSpec((1, d), lambda i, sl: (0, 0)),
                  pl.BlockSpec((1, d), lambda i, sl: (0, 0)),
                  pl.BlockSpec(memory_space=pl.ANY)],
        out_specs=pl.BlockSpec((tc, d), lambda i, sl: (i, 0)),
        scratch_shapes=[pltpu.VMEM((2, n_k, tc, d // 2), U32), pltpu.SemaphoreType.DMA((2,))],
    )
    return pl.pallas_call(
        functools.partial(_combine_kernel, rc=8, n_k=n_k),
        grid_spec=grid_spec,
        out_shape=jax.ShapeDtypeStruct((s_len, d), F32),
        compiler_params=_params(("arbitrary",)),
        name="moe_combine",
    )(slots, w_tok, x1, gt2.reshape(1, d), g_post.reshape(1, d), ys)


def _invert_kernel(slots_ref, lo_ref, hi_ref, o_ref, *, n_k, n_tok, stride):
    n_slots = o_ref.shape[0]
    n_e = lo_ref.shape[0]
    mask = (1 << (n_tok.bit_length() - 1)) - 1

    def fill(lo, hi):
        def body(j, carry):
            o_ref[j] = (j * stride) & mask
            return carry
        lax.fori_loop(lo, hi, body, 0)

    def per_expert(e, carry):
        fill(lo_ref[e], hi_ref[e])
        return carry
    lax.fori_loop(0, n_e, per_expert, 0)
    fill(hi_ref[n_e - 1], n_slots)

    def put(t, carry):
        for k in range(n_k):
            o_ref[slots_ref[t * n_k + k]] = t
        return carry
    lax.fori_loop(0, n_tok, put, 0)


def _invert_slots(slots, pad_lo, pad_hi, *, n_slots, n_k, n_tok, stride):
    smem = pl.BlockSpec(memory_space=pltpu.SMEM)
    return pl.pallas_call(
        functools.partial(_invert_kernel, n_k=n_k, n_tok=n_tok, stride=stride),
        in_specs=[smem, smem, smem],
        out_specs=smem,
        out_shape=jax.ShapeDtypeStruct((n_slots,), I32),
        name="moe_invert_slots",
    )(slots, pad_lo, pad_hi)


def _spread_stride(n):
    m = int(n * 0.6180339887) | 1
    while math.gcd(m, n) != 1:
        m += 2
    return m


def _moe_plan(idx_t, wts_t, *, tmb):
    n_k, n_tok = idx_t.shape
    n_e = N_EXPERTS + 1
    eid = jnp.concatenate([idx_t, jnp.full((1, n_tok), N_EXPERTS, I32)], axis=0)
    wts = jnp.concatenate([wts_t, jnp.ones((1, n_tok), F32)], axis=0)
    picks = eid[:, None, :] == jnp.arange(n_e, dtype=I32)[None, :, None]
    sel = picks.any(axis=0).astype(I32)
    stride = _spread_stride(n_tok)
    visit = lax.rem(jnp.arange(n_tok, dtype=I32) * stride, n_tok)
    where = lax.rem(jnp.arange(n_tok, dtype=I32) * pow(stride, -1, n_tok), n_tok)
    csum = jnp.cumsum(sel[:, visit], axis=1)
    counts = csum[:, -1]
    padded = (counts + tmb - 1) // tmb * tmb
    ends = jnp.cumsum(padded)
    starts = ends - padded
    slot_dense = (starts[:, None] + csum - 1)[:, where]
    slot = jnp.sum(jnp.where(picks, slot_dense[None], 0), axis=1)
    n_slots = -(-(n_tok * (n_k + 1) + n_e * (tmb - 1)) // tmb) * tmb
    slots_tok_major = slot.T.reshape(-1)
    slot_tok = _invert_slots(slots_tok_major, (starts + counts).astype(I32), ends.astype(I32),
                             n_slots=n_slots, n_k=n_k + 1, n_tok=n_tok, stride=stride)
    n_used = (ends[-1] // tmb).astype(I32).reshape(1)
    block_start = jnp.arange(n_slots // tmb, dtype=I32) * tmb
    block_e = jnp.minimum(jnp.sum(ends[None, :] <= block_start[:, None], axis=1), n_e - 1).astype(I32)
    w_tok = jnp.zeros((n_tok, 16), F32).at[:, :n_k + 1].set(wts.T)
    owner = jnp.where(padded > 0, jnp.arange(n_e, dtype=I32), n_e)
    later = jnp.concatenate([lax.cummin(owner[::-1])[::-1][1:], jnp.full((1,), n_e, I32)])
    next_e = jnp.where(later >= n_e, -1, later).astype(I32)
    return slot_tok, block_e, n_used, next_e, slots_tok_major, w_tok


def _in_proj_weights(w_in):
    d = w_in.shape[0]
    w_main = w_in.T.astype(BF16)
    kr = w_in[:, COL_KR:COL_KR + A_ROPE_DIM]
    gates = w_in[:, COL_IG:COL_IG + 2 * M_HEADS]
    half = A_ROPE_DIM // 2
    kswap = jnp.concatenate([-kr[:, half:], kr[:, :half]], axis=1)
    z = lambda n: jnp.zeros((d, n), w_in.dtype)
    w_tail = jnp.concatenate([kswap, gates, z(LANES - A_ROPE_DIM - 2 * M_HEADS), kr, z(LANES - A_ROPE_DIM)], axis=1)
    return w_main, w_tail.astype(BF16)


def _q_up_weight(w_uq):
    r = w_uq.shape[0]
    w = w_uq.reshape(r, A_HEADS, A_NOPE_DIM + A_ROPE_DIM)
    nope = w[:, :, :A_NOPE_DIM]
    rope = w[:, :, A_NOPE_DIM:]
    half = A_ROPE_DIM // 2
    swap = jnp.concatenate([-rope[:, :, half:], rope[:, :, :half]], axis=2)
    pad = jnp.zeros((r, A_HEADS, LANES - A_ROPE_DIM), w_uq.dtype)
    rope_p = jnp.concatenate([rope, pad], axis=2)
    swap_p = jnp.concatenate([swap, pad], axis=2)
    flat = lambda a: a.reshape(r, -1)
    return jnp.concatenate([flat(nope), flat(rope_p), flat(swap_p)], axis=1).astype(BF16)


def _kv_up_weight(w_ukv):
    r = w_ukv.shape[0]
    w = w_ukv.reshape(r, A_HEADS, A_NOPE_DIM + A_V_DIM)
    return jnp.concatenate([w[:, :, :A_NOPE_DIM].reshape(r, -1), w[:, :, A_NOPE_DIM:].reshape(r, -1)],
                           axis=1).astype(BF16)


def _rope_tables(s_len):
    pos = jnp.arange(s_len, dtype=F32)
    inv_freq = 1.0 / (ROPE_THETA ** (jnp.arange(0, A_ROPE_DIM, 2, dtype=F32) / A_ROPE_DIM))
    ang = pos[:, None] * inv_freq[None, :]
    pad = jnp.zeros((s_len, LANES - A_ROPE_DIM), F32)
    cos_t = jnp.concatenate([jnp.cos(ang), jnp.cos(ang), pad], axis=1)
    sin_t = jnp.concatenate([jnp.sin(ang), jnp.sin(ang), pad], axis=1)
    return cos_t, sin_t


def _block(x, c, w_ada, b_ada, g_pre_mix, g_post_mix, w_in, conv_w, conv_b, b_igate, b_fgate, g_mlstm_out,
           g_q_norm, w_uq, g_kv_norm, w_ukv, w_out, g_pre_ffn, g_post_ffn, w_router, b_router,
           w_gate, w_up, w_down, w_shared_gate, w_shared_up, w_shared_down):
    s_len, d = x.shape
    mod = _adaln(c, w_ada, b_ada)[0]
    sh1, sc1, gt1, sh2, sc2, gt2 = [mod[i * d:(i + 1) * d] for i in range(6)]

    w_main, w_tail = _in_proj_weights(w_in)
    h1 = _prenorm(x, g_pre_mix * (1.0 + sc1), sh1, tm=256)
    proj, tail = _mm_tail(h1, w_main, w_tail, tm=1024, tn=IN_TILE)
    h_m = _mlstm(proj, tail, conv_w, conv_b, b_igate, b_fgate, g_mlstm_out, chunk=min(M_CHUNK, s_len))
    scale = (A_NOPE_DIM + A_ROPE_DIM) ** -0.5
    q_raw = _norm_mm(proj[:, COL_CQ:COL_CQ + A_Q_RANK], g_q_norm * scale, _q_up_weight(w_uq), tm=1024, tn=1536)
    kv_raw = _norm_mm(proj[:, COL_CKV:COL_CKV + A_KV_RANK], g_kv_norm, _kv_up_weight(w_ukv), tm=1024, tn=1024)
    cos_t, sin_t = _rope_tables(s_len)
    q_cat, k_cat, v_cat = _rope_assemble(q_raw, kv_raw, tail, cos_t, sin_t, tm=256)
    h_a = _attention(q_cat, k_cat, v_cat, tq=1024, chunk=CHUNK)
    y = _mm2(h_m, h_a, w_out.astype(BF16), tm=1024, tn=1024)

    x1, hp, idx_t, wts_t = _mid(y, x, gt1, g_post_mix, g_pre_ffn * (1.0 + sc2), sh2, w_router, b_router, tm=256)
    tmb = 256
    slot_tok, block_e, n_used, next_e, slots, w_tok = _moe_plan(idx_t, wts_t, tmb=tmb)
    ys = _experts(block_e, n_used, next_e, slot_tok, hp, w_gate, w_up, w_down,
                  w_shared_gate, w_shared_up, w_shared_down, tmb=tmb)
    return _combine(slots, w_tok, x1, gt2, g_post_ffn, ys, tc=128)


def kernel(x, c, w_ada, b_ada, g_pre_mix, g_post_mix, w_in, conv_w, conv_b, b_igate, b_fgate, g_mlstm_out,
           g_q_norm, w_uq, g_kv_norm, w_ukv, w_out, g_pre_ffn, g_post_ffn, w_router, b_router,
           w_gate, w_up, w_down, w_shared_gate, w_shared_up, w_shared_down):
    assert x.shape[0] == 1 and w_ada.shape[0] == 1, "single sequence, single layer"
    layer = (w_ada, b_ada, g_pre_mix, g_post_mix, w_in, conv_w, conv_b, b_igate, b_fgate, g_mlstm_out,
             g_q_norm, w_uq, g_kv_norm, w_ukv, w_out, g_pre_ffn, g_post_ffn, w_router, b_router,
             w_gate, w_up, w_down, w_shared_gate, w_shared_up, w_shared_down)
    out = _block(x[0], c[0], *[p[0] for p in layer])
    return out[None]
```

```python
import functools
import math

import jax
import jax.numpy as jnp
from jax import lax
from jax.experimental import pallas as pl
from jax.experimental.pallas import tpu as pltpu

F32 = jnp.float32
BF16 = jnp.bfloat16
I32 = jnp.int32
U32 = jnp.uint32

NORM_EPS = 1e-6
CHUNK = 64
M_CHUNK = 128

M_HEADS = 4
M_QK_DIM = 256
M_V_DIM = 512
M_WIDTH = M_HEADS * M_V_DIM
M_QK_WIDTH = 2 * M_HEADS * M_QK_DIM
CONV_WIDTH = 4
GATE_SOFTCAP = 15.0

A_HEADS = 16
A_NOPE_DIM = 128
A_ROPE_DIM = 64
A_V_DIM = 128
A_Q_RANK = 768
A_KV_RANK = 512
A_HEAD_PAD = 256
ROPE_THETA = 10000.0

N_EXPERTS = 64
TOP_K = 8
N_GROUPS = 8
GROUP_SIZE = N_EXPERTS // N_GROUPS
TOPK_GROUPS = 4
ROUTED_SCALE = 2.5

LANES = 128
VMEM_LIMIT = 56 * 1024 * 1024
NEG_BIG = -1e30

COL_QK = 0
COL_V = COL_QK + M_QK_WIDTH
COL_O = COL_V + M_WIDTH
COL_IG = COL_O + M_WIDTH
COL_FG = COL_IG + M_HEADS
COL_CQ = COL_FG + M_HEADS
COL_CKV = COL_CQ + A_Q_RANK
COL_KR = COL_CKV + A_KV_RANK
N_IN = COL_KR + A_ROPE_DIM
IN_TILE = 1536
TAIL_W = 2 * LANES
TAIL_GATE_LANE = A_ROPE_DIM


def _params(sem, vmem=VMEM_LIMIT):
    return pltpu.CompilerParams(dimension_semantics=sem, vmem_limit_bytes=vmem)


def _row_loop(n_rows, rc, body):
    def step(i, carry):
        body(pl.multiple_of(i * rc, rc))
        return carry
    lax.fori_loop(0, n_rows // rc, step, 0)


def _pack_pair(a, b):
    lo = lax.bitcast_convert_type(a.astype(BF16).astype(F32), U32) >> 16
    hi = lax.bitcast_convert_type(b.astype(BF16).astype(F32), U32) & jnp.uint32(0xFFFF0000)
    return lo | hi


def _unpack_pair(u):
    lo = lax.bitcast_convert_type(u << 16, F32)
    hi = lax.bitcast_convert_type(u & jnp.uint32(0xFFFF0000), F32)
    return lo, hi


def _adaln_kernel(c_ref, w_ref, b_ref, o_ref, *, rc):
    d, tn = w_ref.shape
    nl = tn // LANES

    def step(i, accs):
        r = pl.multiple_of(i * rc, rc)
        c = c_ref[pl.ds(r, rc), :]
        ca = c * jax.nn.sigmoid(c)
        out = []
        for j in range(nl):
            prod = w_ref[pl.ds(r, rc), j * LANES:(j + 1) * LANES] * ca
            out.append(accs[j] + jnp.sum(prod.reshape(rc // 8, 8, LANES), axis=0))
        return tuple(out)

    accs = lax.fori_loop(0, d // rc, step, tuple(jnp.zeros((8, LANES), F32) for _ in range(nl)))
    for j in range(nl):
        o_ref[:, j * LANES:(j + 1) * LANES] = (
            jnp.sum(accs[j], axis=0, keepdims=True) + b_ref[:, j * LANES:(j + 1) * LANES])


def _adaln(c, w_ada, b_ada):
    d, n = w_ada.shape
    tn = 512
    c_b = jnp.broadcast_to(c.reshape(d, 1), (d, LANES))
    return pl.pallas_call(
        functools.partial(_adaln_kernel, rc=64),
        grid=(n // tn,),
        in_specs=[pl.BlockSpec((d, LANES), lambda j: (0, 0)),
                  pl.BlockSpec((d, tn), lambda j: (0, j)),
                  pl.BlockSpec((1, tn), lambda j: (0, j))],
        out_specs=pl.BlockSpec((1, tn), lambda j: (0, j)),
        out_shape=jax.ShapeDtypeStruct((1, n), F32),
        compiler_params=_params(("arbitrary",)),
        name="adaln",
    )(c_b, w_ada, b_ada.reshape(1, n))


def _norm_mm_kernel(x_ref, g_ref, w_ref, o_ref, h_scr, *, rc):
    tm = x_ref.shape[0]

    @pl.when(pl.program_id(1) == 0)
    def _():
        def body(r):
            x = x_ref[pl.ds(r, rc), :].astype(F32)
            y = x * lax.rsqrt(jnp.mean(x * x, axis=-1, keepdims=True) + NORM_EPS)
            h_scr[pl.ds(r, rc), :] = (y * g_ref[...]).astype(BF16)
        _row_loop(tm, rc, body)

    o_ref[...] = jnp.dot(h_scr[...], w_ref[...], preferred_element_type=F32).astype(o_ref.dtype)


def _norm_mm(x, g, w, *, tm, tn):
    m, k = x.shape
    n = w.shape[1]
    tm = min(tm, m)
    return pl.pallas_call(
        functools.partial(_norm_mm_kernel, rc=32),
        grid=(m // tm, n // tn),
        in_specs=[pl.BlockSpec((tm, k), lambda i, j: (i, 0)),
                  pl.BlockSpec((1, k), lambda i, j: (0, 0)),
                  pl.BlockSpec((k, tn), lambda i, j: (0, j))],
        out_specs=pl.BlockSpec((tm, tn), lambda i, j: (i, j)),
        out_shape=jax.ShapeDtypeStruct((m, n), BF16),
        scratch_shapes=[pltpu.VMEM((tm, k), BF16)],
        compiler_params=_params(("parallel", "arbitrary")),
        name="norm_mm",
    )(x, g.reshape(1, k), w)


def _prenorm_kernel(x_ref, gs_ref, sh_ref, o_ref, *, rc):
    def body(r):
        x = x_ref[pl.ds(r, rc), :]
        y = x * lax.rsqrt(jnp.mean(x * x, axis=-1, keepdims=True) + NORM_EPS)
        o_ref[pl.ds(r, rc), :] = (y * gs_ref[...] + sh_ref[...]).astype(o_ref.dtype)
    _row_loop(x_ref.shape[0], rc, body)


def _prenorm(x, gs, sh, *, tm):
    m, k = x.shape
    tm = min(tm, m)
    return pl.pallas_call(
        functools.partial(_prenorm_kernel, rc=32),
        grid=(m // tm,),
        in_specs=[pl.BlockSpec((tm, k), lambda i: (i, 0)),
                  pl.BlockSpec((1, k), lambda i: (0, 0)),
                  pl.BlockSpec((1, k), lambda i: (0, 0))],
        out_specs=pl.BlockSpec((tm, k), lambda i: (i, 0)),
        out_shape=jax.ShapeDtypeStruct((m, k), BF16),
        compiler_params=_params(("parallel",)),
        name="prenorm",
    )(x, gs.reshape(1, k), sh.reshape(1, k))


def _mm_tail_kernel(a_ref, wt_ref, tailt_ref, o_ref, t_ref):
    nt_dims = (((1,), (1,)), ((), ()))
    o_ref[...] = lax.dot_general(a_ref[...], wt_ref[...], nt_dims, preferred_element_type=F32).astype(o_ref.dtype)

    @pl.when(pl.program_id(1) == pl.num_programs(1) - 1)
    def _():
        t_ref[...] = lax.dot_general(a_ref[...], tailt_ref[...], nt_dims, preferred_element_type=F32)


def _mm_tail(a, w_t, tail_t, *, tm, tn):
    m, k = a.shape
    n, nt = w_t.shape[0], tail_t.shape[0]
    tm = min(tm, m)
    return pl.pallas_call(
        _mm_tail_kernel,
        grid=(m // tm, pl.cdiv(n, tn)),
        in_specs=[pl.BlockSpec((tm, k), lambda i, j: (i, 0)),
                  pl.BlockSpec((tn, k), lambda i, j: (j, 0)),
                  pl.BlockSpec((nt, k), lambda i, j: (0, 0))],
        out_specs=[pl.BlockSpec((tm, tn), lambda i, j: (i, j)),
                   pl.BlockSpec((tm, nt), lambda i, j: (i, 0))],
        out_shape=[jax.ShapeDtypeStruct((m, n), BF16), jax.ShapeDtypeStruct((m, nt), F32)],
        compiler_params=_params(("parallel", "arbitrary"), vmem=60 * 1024 * 1024),
        name="in_proj",
    )(a, w_t, tail_t)


def _mm2_kernel(a1_ref, a2_ref, w_ref, o_ref):
    k1 = a1_ref.shape[1]
    acc = jnp.dot(a1_ref[...], w_ref[:k1, :], preferred_element_type=F32)
    acc = acc + jnp.dot(a2_ref[...], w_ref[k1:, :], preferred_element_type=F32)
    o_ref[...] = acc.astype(o_ref.dtype)


def _mm2(a1, a2, w, *, tm, tn):
    m, k1 = a1.shape
    k2 = a2.shape[1]
    n = w.shape[1]
    tm = min(tm, m)
    return pl.pallas_call(
        _mm2_kernel,
        grid=(m // tm, n // tn),
        in_specs=[pl.BlockSpec((tm, k1), lambda i, j: (i, 0)),
                  pl.BlockSpec((tm, k2), lambda i, j: (i, 0)),
                  pl.BlockSpec((k1 + k2, tn), lambda i, j: (0, j))],
        out_specs=pl.BlockSpec((tm, tn), lambda i, j: (i, j)),
        out_shape=jax.ShapeDtypeStruct((m, n), BF16),
        compiler_params=_params(("parallel", "arbitrary")),
        name="out_proj",
    )(a1, a2, w)


def _softcap(z):
    return GATE_SOFTCAP * jnp.tanh(z * (1.0 / GATE_SOFTCAP))


def _log_sigmoid(z):
    return jnp.minimum(z, 0.0) - jnp.log1p(jnp.exp(-jnp.abs(z)))


def _mlstm_kernel(qk_ref, prev_ref, v_ref, o_ref, gcol_ref, grow_ref, cw_ref, cb_ref, bcol_ref, brow_ref,
                  gout_ref, out_ref, c_scr, n_scr, m_scr, u_scr):
    c = pl.program_id(0)
    L = qk_ref.shape[0]
    halo = prev_ref.shape[0]
    dk, dv = M_QK_DIM, M_V_DIM

    @pl.when(c == 0)
    def _():
        c_scr[...] = jnp.zeros_like(c_scr)
        n_scr[...] = jnp.zeros_like(n_scr)
        m_scr[...] = jnp.zeros_like(m_scr)

    prev = prev_ref[...].astype(F32)
    u_scr[0:halo, :] = jnp.where(c == 0, jnp.zeros_like(prev), prev)
    u_scr[halo:halo + L, :] = qk_ref[...].astype(F32)

    def conv_silu(col, width):
        acc = cb_ref[:, col:col + width]
        for j in range(CONV_WIDTH):
            r0 = halo - (CONV_WIDTH - 1) + j
            acc = acc + u_scr[r0:r0 + L, col:col + width] * cw_ref[j:j + 1, col:col + width]
        return acc * jax.nn.sigmoid(acc)

    pre_c = _softcap(gcol_ref[...] + bcol_ref[...])
    lf_c = _log_sigmoid(pre_c)
    pre_r = _softcap(grow_ref[0] + brow_ref[...])
    lf_r = _log_sigmoid(pre_r)
    row = lax.broadcasted_iota(I32, (L, L), 0)
    col = lax.broadcasted_iota(I32, (L, L), 1)
    causal = col <= row
    tril = causal.astype(F32)
    triu = (row <= col).astype(F32)
    bcum_c = jnp.dot(tril, lf_c, preferred_element_type=F32, precision=lax.Precision.HIGHEST)
    bcum_r = jnp.dot(lf_r, triu, preferred_element_type=F32, precision=lax.Precision.HIGHEST)

    for h in range(M_HEADS):
        li_lane = TAIL_GATE_LANE + h
        lf_lane = TAIL_GATE_LANE + M_HEADS + h
        q = (conv_silu(h * dk, dk) * (dk ** -0.5)).astype(BF16)
        kf = conv_silu(M_HEADS * dk + h * dk, dk)
        kb = kf.astype(BF16)
        v = v_ref[:, h * dv:(h + 1) * dv]
        b_c = bcum_c[:, lf_lane:lf_lane + 1]
        li_c = pre_c[:, li_lane:li_lane + 1]
        b_r = bcum_r[M_HEADS + h:M_HEADS + h + 1, :]
        li_r = pre_r[h:h + 1, :]
        m_prev = m_scr[h:h + 1, 0:1]

        dm = jnp.where(causal, b_c - b_r + li_r, NEG_BIG)
        inter = b_c + m_prev
        m_t = jnp.maximum(jnp.max(dm, axis=1, keepdims=True), inter)
        decay = jnp.exp(inter - m_t)
        s = lax.dot_general(q, kb, (((1,), (1,)), ((), ())), preferred_element_type=F32) * jnp.exp(dm - m_t)
        c_state = c_scr[h]
        n_state = n_scr[h:h + 1, :]
        num = jnp.dot(s.astype(BF16), v, preferred_element_type=F32)
        num = num + decay * jnp.dot(q, c_state.astype(BF16), preferred_element_type=F32)
        den = jnp.sum(s, axis=1, keepdims=True) + decay * jnp.sum(q.astype(F32) * n_state, axis=1, keepdims=True)
        hh = num / jnp.maximum(jnp.abs(den), jnp.exp(-m_t))

        b_last = b_c[L - 1:L, :]
        m_new = jnp.maximum(b_last + m_prev, jnp.max(b_last - b_r + li_r, axis=1, keepdims=True))
        carry = jnp.exp(b_last + m_prev - m_new)
        w_c = jnp.exp(b_last - b_c + li_c - m_new)
        kw = kf * w_c
        c_scr[h] = carry * c_state + lax.dot_general(kw.astype(BF16), v, (((0,), (0,)), ((), ())),
                                                     preferred_element_type=F32)
        n_scr[h:h + 1, :] = carry * n_state + jnp.sum(kw, axis=0, keepdims=True)
        m_scr[h:h + 1, :] = jnp.broadcast_to(m_new, (1, LANES))

        hn = hh * lax.rsqrt(jnp.mean(hh * hh, axis=1, keepdims=True) + NORM_EPS) * gout_ref[:, h * dv:(h + 1) * dv]
        gate = jax.nn.sigmoid(o_ref[:, h * dv:(h + 1) * dv].astype(F32))
        out_ref[:, h * dv:(h + 1) * dv] = (hn * gate).astype(out_ref.dtype)


def _mlstm(proj, tail, conv_w, conv_b, b_igate, b_fgate, g_out, *, chunk):
    s_len = proj.shape[0]
    L = chunk
    n_c = s_len // L
    halo = 16
    gates = tail[:, TAIL_GATE_LANE:TAIL_GATE_LANE + 2 * M_HEADS]
    g_row = gates.reshape(n_c, L, 2 * M_HEADS).transpose(0, 2, 1)
    bias = jnp.concatenate([b_igate, b_fgate]).astype(F32)
    b_col = jnp.zeros((1, LANES), F32).at[0, TAIL_GATE_LANE:TAIL_GATE_LANE + 2 * M_HEADS].set(bias)
    b_row = bias.reshape(2 * M_HEADS, 1)
    w = M_QK_WIDTH
    return pl.pallas_call(
        _mlstm_kernel,
        grid=(n_c,),
        in_specs=[pl.BlockSpec((L, w), lambda c: (c, COL_QK // w)),
                  pl.BlockSpec((halo, w), lambda c: (jnp.maximum(c * (L // halo) - 1, 0), COL_QK // w)),
                  pl.BlockSpec((L, M_WIDTH), lambda c: (c, COL_V // M_WIDTH)),
                  pl.BlockSpec((L, M_WIDTH), lambda c: (c, COL_O // M_WIDTH)),
                  pl.BlockSpec((L, LANES), lambda c: (c, 0)),
                  pl.BlockSpec((1, 2 * M_HEADS, L), lambda c: (c, 0, 0)),
                  pl.BlockSpec((CONV_WIDTH, w), lambda c: (0, 0)),
                  pl.BlockSpec((1, w), lambda c: (0, 0)),
                  pl.BlockSpec((1, LANES), lambda c: (0, 0)),
                  pl.BlockSpec((2 * M_HEADS, 1), lambda c: (0, 0)),
                  pl.BlockSpec((1, M_WIDTH), lambda c: (0, 0))],
        out_specs=pl.BlockSpec((L, M_WIDTH), lambda c: (c, 0)),
        out_shape=jax.ShapeDtypeStruct((s_len, M_WIDTH), BF16),
        scratch_shapes=[pltpu.VMEM((M_HEADS, M_QK_DIM, M_V_DIM), F32),
                        pltpu.VMEM((8, M_QK_DIM), F32),
                        pltpu.VMEM((8, LANES), F32),
                        pltpu.VMEM((halo + L, w), F32)],
        compiler_params=_params(("arbitrary",)),
        name="mlstm",
    )(proj, proj, proj, proj, tail, g_row, conv_w, conv_b.reshape(1, w), b_col, b_row, g_out.reshape(1, M_WIDTH))


def _rope_kernel(q_ref, kn_ref, v_ref, tail_ref, cos_ref, sin_ref, qo_ref, ko_ref, vo_ref):
    cos = cos_ref[...]
    sin = sin_ref[...]
    nope_w = A_HEADS * A_NOPE_DIM
    kpe = (tail_ref[:, LANES:2 * LANES] * cos + tail_ref[:, 0:LANES] * sin).astype(BF16)
    lane = lax.broadcasted_iota(I32, (q_ref.shape[0], LANES), 1)
    ones_col = jnp.where(lane == 0, 1.0, 0.0).astype(BF16)
    for h in range(A_HEADS):
        vo_ref[:, h * A_HEAD_PAD:h * A_HEAD_PAD + LANES] = v_ref[:, h * LANES:(h + 1) * LANES]
        vo_ref[:, h * A_HEAD_PAD + LANES:(h + 1) * A_HEAD_PAD] = ones_col
        lo = h * A_HEAD_PAD
        qo_ref[:, lo:lo + LANES] = q_ref[:, h * LANES:(h + 1) * LANES]
        qr = q_ref[:, nope_w + h * LANES:nope_w + (h + 1) * LANES].astype(F32)
        qs = q_ref[:, 2 * nope_w + h * LANES:2 * nope_w + (h + 1) * LANES].astype(F32)
        qo_ref[:, lo + LANES:lo + 2 * LANES] = (qr * cos + qs * sin).astype(BF16)
        ko_ref[:, lo:lo + LANES] = kn_ref[:, h * LANES:(h + 1) * LANES]
        ko_ref[:, lo + LANES:lo + 2 * LANES] = kpe


def _rope_assemble(q_raw, kv_raw, tail, cos_t, sin_t, *, tm):
    s_len = q_raw.shape[0]
    tm = min(tm, s_len)
    nope_w = A_HEADS * A_NOPE_DIM
    wide = A_HEADS * A_HEAD_PAD
    return pl.pallas_call(
        _rope_kernel,
        grid=(s_len // tm,),
        in_specs=[pl.BlockSpec((tm, 3 * nope_w), lambda i: (i, 0)),
                  pl.BlockSpec((tm, nope_w), lambda i: (i, 0)),
                  pl.BlockSpec((tm, nope_w), lambda i: (i, 1)),
                  pl.BlockSpec((tm, 2 * LANES), lambda i: (i, 0)),
                  pl.BlockSpec((tm, LANES), lambda i: (i, 0)),
                  pl.BlockSpec((tm, LANES), lambda i: (i, 0))],
        out_specs=[pl.BlockSpec((tm, wide), lambda i: (i, 0))] * 3,
        out_shape=[jax.ShapeDtypeStruct((s_len, wide), BF16)] * 3,
        compiler_params=_params(("parallel",)),
        name="rope_assemble",
    )(q_raw, kv_raw, kv_raw, tail, cos_t, sin_t)


def _attn_kernel(q_ref, k_ref, v_ref, o_ref, m_scr, acc_scr, s_scr, *, chunk):
    qi = pl.program_id(1)
    tq = q_ref.shape[0]
    m_scr[...] = jnp.full_like(m_scr, NEG_BIG)
    acc_scr[...] = jnp.zeros_like(acc_scr)

    def scores(slot, blk):
        start = pl.multiple_of(blk * tq, tq)
        s_scr[slot] = lax.dot_general(q_ref[...], k_ref[pl.ds(start, tq), :], (((1,), (1,)), ((), ())),
                                      preferred_element_type=F32)

    def consume(slot, blk, masked):
        start = pl.multiple_of(blk * tq, tq)
        s = s_scr[slot]
        if masked:
            rq = lax.broadcasted_iota(I32, (tq, tq), 0) // chunk
            ck = lax.broadcasted_iota(I32, (tq, tq), 1) // chunk
            s = jnp.where(ck <= rq, s, NEG_BIG)
        m_prev = m_scr[...]
        m_new = jnp.maximum(m_prev, jnp.max(s, axis=1, keepdims=True))
        pr = jnp.exp(s - m_new).astype(BF16)
        acc_scr[...] = (jnp.exp(m_prev - m_new) * acc_scr[...]
                        + jnp.dot(pr, v_ref[pl.ds(start, tq), :], preferred_element_type=F32))
        m_scr[...] = m_new

    scores(0, 0)

    def pair(t, carry):
        scores(1, 2 * t + 1)
        consume(0, 2 * t, False)
        scores(0, 2 * t + 2)
        consume(1, 2 * t + 1, False)
        return carry
    lax.fori_loop(0, qi // 2, pair, 0)

    @pl.when(qi % 2 == 1)
    def _():
        scores(1, qi)
        consume(0, qi - 1, False)
        consume(1, qi, True)

    @pl.when(qi % 2 == 0)
    def _():
        consume(0, qi, True)

    acc = acc_scr[...]
    o_ref[...] = (acc[:, :A_V_DIM] / acc[:, A_V_DIM:A_V_DIM + 1]).astype(o_ref.dtype)


def _attention(q_cat, k_cat, v_cat, *, tq, chunk):
    s_len = q_cat.shape[0]
    tq = min(tq, s_len)
    return pl.pallas_call(
        functools.partial(_attn_kernel, chunk=chunk),
        grid=(A_HEADS, s_len // tq),
        in_specs=[pl.BlockSpec((tq, A_HEAD_PAD), lambda h, i: (i, h)),
                  pl.BlockSpec((s_len, A_HEAD_PAD), lambda h, i: (0, h)),
                  pl.BlockSpec((s_len, A_HEAD_PAD), lambda h, i: (0, h))],
        out_specs=pl.BlockSpec((tq, A_V_DIM), lambda h, i: (i, h)),
        out_shape=jax.ShapeDtypeStruct((s_len, A_HEADS * A_V_DIM), BF16),
        scratch_shapes=[pltpu.VMEM((tq, 1), F32), pltpu.VMEM((tq, A_HEAD_PAD), F32),
                        pltpu.VMEM((2, tq, tq), F32)],
        compiler_params=_params(("parallel", "arbitrary")),
        name="mla_attention",
    )(q_cat, k_cat, v_cat)


def _route_t(logits, bias_col):
    n = logits.shape[1]
    scores = jax.nn.sigmoid(logits)
    biased = scores + bias_col
    sub = lax.broadcasted_iota(I32, (GROUP_SIZE, n), 0)
    rows = []
    for g in range(N_GROUPS):
        x = biased[g * GROUP_SIZE:(g + 1) * GROUP_SIZE, :]
        m1 = jnp.max(x, axis=0, keepdims=True)
        i1 = jnp.min(jnp.where(x == m1, sub, GROUP_SIZE), axis=0, keepdims=True)
        m2 = jnp.max(jnp.where(sub == i1, -jnp.inf, x), axis=0, keepdims=True)
        rows.append(m1 + m2)
    gscore = jnp.concatenate(rows, axis=0)
    gio = lax.broadcasted_iota(I32, (N_GROUPS, n), 0)
    grank = jnp.zeros((N_GROUPS, n), I32)
    for g in range(N_GROUPS):
        r = gscore[g:g + 1, :]
        grank = grank + jnp.where(gio > g, jnp.where(r >= gscore, 1, 0), jnp.where(r > gscore, 1, 0))
    gsel = grank < TOPK_GROUPS
    masked = jnp.concatenate(
        [jnp.where(gsel[g:g + 1, :], biased[g * GROUP_SIZE:(g + 1) * GROUP_SIZE, :], -jnp.inf)
         for g in range(N_GROUPS)], axis=0)
    eio = lax.broadcasted_iota(I32, (N_EXPERTS, n), 0)
    rank = jnp.zeros((N_EXPERTS, n), I32)
    for e in range(N_EXPERTS):
        r = masked[e:e + 1, :]
        rank = rank + jnp.where(eio > e, jnp.where(r >= masked, 1, 0), jnp.where(r > masked, 1, 0))
    sel = rank < TOP_K
    denom = jnp.sum(jnp.where(sel, scores, 0.0), axis=0, keepdims=True)
    wnorm = scores / denom * ROUTED_SCALE
    eio_f = eio.astype(F32)
    ids, wts = [], []
    for k in range(TOP_K):
        hit = rank == k
        ids.append(jnp.sum(jnp.where(hit, eio_f, 0.0), axis=0, keepdims=True))
        wts.append(jnp.sum(jnp.where(hit, wnorm, 0.0), axis=0, keepdims=True))
    return jnp.concatenate(ids, axis=0).astype(I32), jnp.concatenate(wts, axis=0)


def _mid_kernel(y_ref, x_ref, gt_ref, gpost_ref, gs_ref, sh_ref, wr_ref, br_ref,
                x1_ref, hp_ref, idx_ref, wts_ref, h_scr, *, rc):
    tm, d = x_ref.shape
    half = d // 2

    def body(r):
        y = y_ref[pl.ds(r, rc), :].astype(F32)
        yn = y * lax.rsqrt(jnp.mean(y * y, axis=-1, keepdims=True) + NORM_EPS) * gpost_ref[...]
        x1 = x_ref[pl.ds(r, rc), :] + gt_ref[...] * yn
        x1_ref[pl.ds(r, rc), :] = x1
        hn = x1 * lax.rsqrt(jnp.mean(x1 * x1, axis=-1, keepdims=True) + NORM_EPS)
        h = hn * gs_ref[...] + sh_ref[...]
        h_scr[pl.ds(r, rc), :] = h
        hp_ref[pl.ds(r, rc), :] = _pack_pair(h[:, :half], h[:, half:])
    _row_loop(tm, rc, body)

    logits = lax.dot_general(wr_ref[...], h_scr[...], (((1,), (1,)), ((), ())),
                             preferred_element_type=F32, precision=lax.Precision.HIGHEST)
    ids, wts = _route_t(logits, br_ref[...])
    idx_ref[...] = ids
    wts_ref[...] = wts


def _mid(y, x, gt1, g_post, gs2, sh2, w_router, b_router, *, tm):
    s_len, d = x.shape
    tm = min(tm, s_len)
    vec = lambda a: a.reshape(1, d).astype(F32)
    return pl.pallas_call(
        functools.partial(_mid_kernel, rc=16),
        grid=(s_len // tm,),
        in_specs=[pl.BlockSpec((tm, d), lambda i: (i, 0)),
                  pl.BlockSpec((tm, d), lambda i: (i, 0)),
                  pl.BlockSpec((1, d), lambda i: (0, 0)),
                  pl.BlockSpec((1, d), lambda i: (0, 0)),
                  pl.BlockSpec((1, d), lambda i: (0, 0)),
                  pl.BlockSpec((1, d), lambda i: (0, 0)),
                  pl.BlockSpec((N_EXPERTS, d), lambda i: (0, 0)),
                  pl.BlockSpec((N_EXPERTS, 1), lambda i: (0, 0))],
        out_specs=[pl.BlockSpec((tm, d), lambda i: (i, 0)),
                   pl.BlockSpec((tm, d // 2), lambda i: (i, 0)),
                   pl.BlockSpec((TOP_K, tm), lambda i: (0, i)),
                   pl.BlockSpec((TOP_K, tm), lambda i: (0, i))],
        out_shape=[jax.ShapeDtypeStruct((s_len, d), F32),
                   jax.ShapeDtypeStruct((s_len, d // 2), U32),
                   jax.ShapeDtypeStruct((TOP_K, s_len), I32),
                   jax.ShapeDtypeStruct((TOP_K, s_len), F32)],
        scratch_shapes=[pltpu.VMEM((tm, d), F32)],
        compiler_params=_params(("parallel",)),
        name="mid_norm_route",
    )(y, x, vec(gt1), vec(g_post), vec(gs2), vec(sh2), w_router.T.astype(F32), b_router.reshape(N_EXPERTS, 1))


def _row_copy(src_hbm, dst_vmem, sem, src_row, dst_row):
    return pltpu.make_async_copy(src_hbm.at[pl.ds(src_row, 1), :], dst_vmem.at[pl.ds(dst_row, 1), :], sem)


ROW_GROUP = 8
ROW_DMA_PRIORITY = 0
WEIGHT_DMA_PRIORITY = 1


def _expert_kernel(be_ref, nb_ref, nxt_ref, tok_ref, h_hbm, wg_hbm, wu_hbm, wd_hbm, sg_hbm, su_hbm, sd_hbm,
                   o_ref, stage_g, stage_u, stage_d, wb_g, wb_u, wb_d, xbuf, sems, xsems, *, rc):
    b = pl.program_id(0)
    e = be_ref[b]
    tmb = xbuf.shape[1]
    cur = lax.rem(b, 2)
    stages = (stage_g, stage_u, stage_d)
    routed = (wg_hbm, wu_hbm, wd_hbm)
    shared = (sg_hbm, su_hbm, sd_hbm)

    def gather_loop(blk, slot):
        base = blk * tmb

        def step(gi, carry):
            r0 = gi * ROW_GROUP
            for u in range(ROW_GROUP):
                _row_copy(h_hbm, xbuf.at[slot], xsems.at[slot], tok_ref[base + r0 + u], r0 + u).start(
                    priority=ROW_DMA_PRIORITY)
            return carry
        lax.fori_loop(0, tmb // ROW_GROUP, step, 0)

    def gather_unrolled(blk, slot):
        base = blk * tmb
        for r in range(tmb):
            _row_copy(h_hbm, xbuf.at[slot], xsems.at[slot], tok_ref[base + r], r).start(priority=ROW_DMA_PRIORITY)

    def gather_wait(slot):
        pltpu.make_async_copy(h_hbm.at[pl.ds(0, tmb), :], xbuf.at[slot], xsems.at[slot]).wait()

    def fetch(ex):
        @pl.when(ex < N_EXPERTS)
        def _():
            for i in range(3):
                pltpu.make_async_copy(routed[i].at[ex], stages[i], sems.at[i]).start(priority=WEIGHT_DMA_PRIORITY)

        @pl.when(ex == N_EXPERTS)
        def _():
            for i in range(3):
                pltpu.make_async_copy(shared[i], stages[i], sems.at[i]).start(priority=WEIGHT_DMA_PRIORITY)

    @pl.when(b < nb_ref[0])
    def _():
        @pl.when(b == 0)
        def _():
            fetch(e)
            gather_loop(0, 0)

        is_first = jnp.logical_or(b == 0, be_ref[jnp.maximum(b - 1, 0)] != e)

        @pl.when(is_first)
        def _():
            for i in range(3):
                pltpu.make_async_copy(shared[i], stages[i], sems.at[i]).wait()
            for src, dst in ((stage_g, wb_g), (stage_u, wb_u), (stage_d, wb_d)):
                def cast(r, src=src, dst=dst):
                    dst[pl.ds(r, rc), :] = src[pl.ds(r, rc), :].astype(BF16)
                _row_loop(src.shape[0], rc, cast)
            nxt = nxt_ref[e]

            @pl.when(nxt >= 0)
            def _():
                fetch(nxt)

        gather_wait(cur)
        gather_unrolled(jnp.minimum(b + 1, nb_ref[0] - 1), 1 - cur)
        lo, hi = _unpack_pair(xbuf[cur])
        x = jnp.concatenate([lo, hi], axis=1).astype(BF16)
        g = jnp.dot(x, wb_g[...], preferred_element_type=F32)
        u = jnp.dot(x, wb_u[...], preferred_element_type=F32)
        a = (g * jax.nn.sigmoid(g) * u).astype(BF16)
        y = jnp.dot(a, wb_d[...], preferred_element_type=F32)
        half = y.shape[1] // 2
        o_ref[...] = _pack_pair(y[:, :half], y[:, half:])

        @pl.when(b == nb_ref[0] - 1)
        def _():
            gather_wait(1 - cur)

    @pl.when(b >= nb_ref[0])
    def _():
        o_ref[...] = jnp.zeros_like(o_ref)


def _experts(block_e, n_used, next_e, slot_tok, hp, wg, wu, wd, sg, su, sd, *, tmb):
    n_slots = slot_tok.shape[0]
    wp = hp.shape[1]
    d, ff = wg.shape[1], wg.shape[2]
    nb = n_slots // tmb
    grid_spec = pltpu.PrefetchScalarGridSpec(
        num_scalar_prefetch=4,
        grid=(nb,),
        in_specs=[pl.BlockSpec(memory_space=pl.ANY)] * 7,
        out_specs=pl.BlockSpec((tmb, wp), lambda b, *_: (b, 0)),
        scratch_shapes=[pltpu.VMEM((d, ff), F32), pltpu.VMEM((d, ff), F32), pltpu.VMEM((ff, d), F32),
                        pltpu.VMEM((d, ff), BF16), pltpu.VMEM((d, ff), BF16), pltpu.VMEM((ff, d), BF16),
                        pltpu.VMEM((2, tmb, wp), U32),
                        pltpu.SemaphoreType.DMA((3,)), pltpu.SemaphoreType.DMA((2,))],
    )
    return pl.pallas_call(
        functools.partial(_expert_kernel, rc=128),
        grid_spec=grid_spec,
        out_shape=jax.ShapeDtypeStruct((n_slots, wp), U32),
        compiler_params=_params(("arbitrary",), vmem=60 * 1024 * 1024),
        name="moe_experts",
    )(block_e, n_used, next_e, slot_tok, hp, wg, wu, wd, sg, su, sd)


def _combine_kernel(slot_ref, w_ref, x1_ref, gt_ref, g_ref, ys_hbm, o_ref, buf, sems, *, rc, n_k):
    tc, d = x1_ref.shape
    half = d // 2
    i = pl.program_id(0)
    cur = lax.rem(i, 2)
    last = pl.num_programs(0) - 1

    def issue_rows(tile, slot, t0):
        base = tile * (tc * n_k)
        for t in range(rc):
            for k in range(n_k):
                _row_copy(ys_hbm, buf.at[slot, k], sems.at[slot], slot_ref[base + (t0 + t) * n_k + k],
                          t0 + t).start(priority=k % 2)

    def wait_planes(slot):
        for k in range(n_k):
            pltpu.make_async_copy(ys_hbm.at[pl.ds(0, tc), :], buf.at[slot, k], sems.at[slot]).wait()

    @pl.when(i == 0)
    def _():
        _row_loop(tc, rc, lambda r: issue_rows(0, 0, r))

    wait_planes(cur)
    nxt = jnp.minimum(i + 1, last)

    def body(r):
        issue_rows(nxt, 1 - cur, r)
        w = w_ref[pl.ds(r, rc), :]
        lo = jnp.zeros((rc, half), F32)
        hi = jnp.zeros((rc, half), F32)
        for k in range(n_k):
            a, b = _unpack_pair(buf[cur, k, pl.ds(r, rc), :])
            lo = lo + w[:, k:k + 1] * a
            hi = hi + w[:, k:k + 1] * b
        ms = (jnp.sum(lo * lo, axis=-1, keepdims=True) + jnp.sum(hi * hi, axis=-1, keepdims=True)) * (1.0 / d)
        rs = lax.rsqrt(ms + NORM_EPS)
        o_ref[pl.ds(r, rc), 0:half] = (x1_ref[pl.ds(r, rc), 0:half]
                                       + gt_ref[:, 0:half] * (lo * rs * g_ref[:, 0:half]))
        o_ref[pl.ds(r, rc), half:d] = (x1_ref[pl.ds(r, rc), half:d]
                                       + gt_ref[:, half:d] * (hi * rs * g_ref[:, half:d]))
    _row_loop(tc, rc, body)

    @pl.when(i == last)
    def _():
        wait_planes(1 - cur)


def _combine(slots, w_tok, x1, gt2, g_post, ys, *, tc):
    s_len, d = x1.shape
    n_k = slots.shape[0] // s_len
    tc = min(tc, s_len)
    grid_spec = pltpu.PrefetchScalarGridSpec(
        num_scalar_prefetch=1,
        grid=(s_len // tc,),
        in_specs=[pl.BlockSpec((tc, w_tok.shape[1]), lambda i, sl: (i, 0)),
                  pl.BlockSpec((tc, d), lambda i, sl: (i, 0)),
                  pl.BlockSpec((1, d), lambda i, sl: (0, 0)),
                  pl.BlockSpec((1, d), lambda i, sl: (0, 0)),
                  pl.BlockSpec(memory_space=pl.ANY)],
        out_specs=pl.BlockSpec((tc, d), lambda i, sl: (i, 0)),
        scratch_shapes=[pltpu.VMEM((2, n_k, tc, d // 2), U32), pltpu.SemaphoreType.DMA((2,))],
    )
    return pl.pallas_call(
        functools.partial(_combine_kernel, rc=8, n_k=n_k),
        grid_spec=grid_spec,
        out_shape=jax.ShapeDtypeStruct((s_len, d), F32),
        compiler_params=_params(("arbitrary",)),
        name="moe_combine",
    )(slots, w_tok, x1, gt2.reshape(1, d), g_post.reshape(1, d), ys)


def _invert_kernel(slots_ref, lo_ref, hi_ref, o_ref, *, n_k, n_tok, stride):
    n_slots = o_ref.shape[0]
    n_e = lo_ref.shape[0]
    mask = (1 << (n_tok.bit_length() - 1)) - 1

    def fill(lo, hi):
        def body(j, carry):
            o_ref[j] = (j * stride) & mask
            return carry
        lax.fori_loop(lo, hi, body, 0)

    def per_expert(e, carry):
        fill(lo_ref[e], hi_ref[e])
        return carry
    lax.fori_loop(0, n_e, per_expert, 0)
    fill(hi_ref[n_e - 1], n_slots)

    def put(t, carry):
        for k in range(n_k):
            o_ref[slots_ref[t * n_k + k]] = t
        return carry
    lax.fori_loop(0, n_tok, put, 0)


def _invert_slots(slots, pad_lo, pad_hi, *, n_slots, n_k, n_tok, stride):
    smem = pl.BlockSpec(memory_space=pltpu.SMEM)
    return pl.pallas_call(
        functools.partial(_invert_kernel, n_k=n_k, n_tok=n_tok, stride=stride),
        in_specs=[smem, smem, smem],
        out_specs=smem,
        out_shape=jax.ShapeDtypeStruct((n_slots,), I32),
        name="moe_invert_slots",
    )(slots, pad_lo, pad_hi)


def _spread_stride(n):
    m = int(n * 0.6180339887) | 1
    while math.gcd(m, n) != 1:
        m += 2
    return m


def _moe_plan(idx_t, wts_t, *, tmb):
    n_k, n_tok = idx_t.shape
    n_e = N_EXPERTS + 1
    eid = jnp.concatenate([idx_t, jnp.full((1, n_tok), N_EXPERTS, I32)], axis=0)
    wts = jnp.concatenate([wts_t, jnp.ones((1, n_tok), F32)], axis=0)
    picks = eid[:, None, :] == jnp.arange(n_e, dtype=I32)[None, :, None]
    sel = picks.any(axis=0).astype(I32)
    stride = _spread_stride(n_tok)
    visit = lax.rem(jnp.arange(n_tok, dtype=I32) * stride, n_tok)
    where = lax.rem(jnp.arange(n_tok, dtype=I32) * pow(stride, -1, n_tok), n_tok)
    csum = jnp.cumsum(sel[:, visit], axis=1)
    counts = csum[:, -1]
    padded = (counts + tmb - 1) // tmb * tmb
    ends = jnp.cumsum(padded)
    starts = ends - padded
    slot_dense = (starts[:, None] + csum - 1)[:, where]
    slot = jnp.sum(jnp.where(picks, slot_dense[None], 0), axis=1)
    n_slots = -(-(n_tok * (n_k + 1) + n_e * (tmb - 1)) // tmb) * tmb
    slots_tok_major = slot.T.reshape(-1)
    slot_tok = _invert_slots(slots_tok_major, (starts + counts).astype(I32), ends.astype(I32),
                             n_slots=n_slots, n_k=n_k + 1, n_tok=n_tok, stride=stride)
    n_used = (ends[-1] // tmb).astype(I32).reshape(1)
    block_start = jnp.arange(n_slots // tmb, dtype=I32) * tmb
    block_e = jnp.minimum(jnp.sum(ends[None, :] <= block_start[:, None], axis=1), n_e - 1).astype(I32)
    w_tok = jnp.zeros((n_tok, 16), F32).at[:, :n_k + 1].set(wts.T)
    owner = jnp.where(padded > 0, jnp.arange(n_e, dtype=I32), n_e)
    later = jnp.concatenate([lax.cummin(owner[::-1])[::-1][1:], jnp.full((1,), n_e, I32)])
    next_e = jnp.where(later >= n_e, -1, later).astype(I32)
    return slot_tok, block_e, n_used, next_e, slots_tok_major, w_tok


def _in_proj_weights(w_in):
    w_t = w_in.T
    d = w_t.shape[1]
    kr = w_t[COL_KR:COL_KR + A_ROPE_DIM]
    gates = w_t[COL_IG:COL_IG + 2 * M_HEADS]
    half = A_ROPE_DIM // 2
    z = lambda n: jnp.zeros((n, d), w_t.dtype)
    tail_t = jnp.concatenate([-kr[half:], kr[:half], gates, z(LANES - A_ROPE_DIM - 2 * M_HEADS),
                              kr, z(LANES - A_ROPE_DIM)], axis=0)
    return w_t.astype(BF16), tail_t.astype(BF16)


def _q_up_weight(w_uq):
    r = w_uq.shape[0]
    w = w_uq.reshape(r, A_HEADS, A_NOPE_DIM + A_ROPE_DIM)
    nope = w[:, :, :A_NOPE_DIM]
    rope = w[:, :, A_NOPE_DIM:]
    half = A_ROPE_DIM // 2
    swap = jnp.concatenate([-rope[:, :, half:], rope[:, :, :half]], axis=2)
    pad = jnp.zeros((r, A_HEADS, LANES - A_ROPE_DIM), w_uq.dtype)
    rope_p = jnp.concatenate([rope, pad], axis=2)
    swap_p = jnp.concatenate([swap, pad], axis=2)
    flat = lambda a: a.reshape(r, -1)
    return jnp.concatenate([flat(nope), flat(rope_p), flat(swap_p)], axis=1).astype(BF16)


def _kv_up_weight(w_ukv):
    r = w_ukv.shape[0]
    w = w_ukv.reshape(r, A_HEADS, A_NOPE_DIM + A_V_DIM)
    return jnp.concatenate([w[:, :, :A_NOPE_DIM].reshape(r, -1), w[:, :, A_NOPE_DIM:].reshape(r, -1)],
                           axis=1).astype(BF16)


def _rope_tables(s_len):
    pos = jnp.arange(s_len, dtype=F32)
    inv_freq = 1.0 / (ROPE_THETA ** (jnp.arange(0, A_ROPE_DIM, 2, dtype=F32) / A_ROPE_DIM))
    ang = pos[:, None] * inv_freq[None, :]
    pad = jnp.zeros((s_len, LANES - A_ROPE_DIM), F32)
    cos_t = jnp.concatenate([jnp.cos(ang), jnp.cos(ang), pad], axis=1)
    sin_t = jnp.concatenate([jnp.sin(ang), jnp.sin(ang), pad], axis=1)
    return cos_t, sin_t


def _block(x, c, w_ada, b_ada, g_pre_mix, g_post_mix, w_in, conv_w, conv_b, b_igate, b_fgate, g_mlstm_out,
           g_q_norm, w_uq, g_kv_norm, w_ukv, w_out, g_pre_ffn, g_post_ffn, w_router, b_router,
           w_gate, w_up, w_down, w_shared_gate, w_shared_up, w_shared_down):
    s_len, d = x.shape
    mod = _adaln(c, w_ada, b_ada)[0]
    sh1, sc1, gt1, sh2, sc2, gt2 = [mod[i * d:(i + 1) * d] for i in range(6)]

    w_main, w_tail = _in_proj_weights(w_in)
    h1 = _prenorm(x, g_pre_mix * (1.0 + sc1), sh1, tm=256)
    proj, tail = _mm_tail(h1, w_main, w_tail, tm=1024, tn=IN_TILE)
    h_m = _mlstm(proj, tail, conv_w, conv_b, b_igate, b_fgate, g_mlstm_out, chunk=min(M_CHUNK, s_len))
    scale = (A_NOPE_DIM + A_ROPE_DIM) ** -0.5
    q_raw = _norm_mm(proj[:, COL_CQ:COL_CQ + A_Q_RANK], g_q_norm * scale, _q_up_weight(w_uq), tm=1024, tn=1536)
    kv_raw = _norm_mm(proj[:, COL_CKV:COL_CKV + A_KV_RANK], g_kv_norm, _kv_up_weight(w_ukv), tm=1024, tn=1024)
    cos_t, sin_t = _rope_tables(s_len)
    q_cat, k_cat, v_cat = _rope_assemble(q_raw, kv_raw, tail, cos_t, sin_t, tm=256)
    h_a = _attention(q_cat, k_cat, v_cat, tq=1024, chunk=CHUNK)
    y = _mm2(h_m, h_a, w_out.astype(BF16), tm=1024, tn=1024)

    x1, hp, idx_t, wts_t = _mid(y, x, gt1, g_post_mix, g_pre_ffn * (1.0 + sc2), sh2, w_router, b_router, tm=256)
    tmb = 256
    slot_tok, block_e, n_used, next_e, slots, w_tok = _moe_plan(idx_t, wts_t, tmb=tmb)
    ys = _experts(block_e, n_used, next_e, slot_tok, hp, w_gate, w_up, w_down,
                  w_shared_gate, w_shared_up, w_shared_down, tmb=tmb)
    return _combine(slots, w_tok, x1, gt2, g_post_ffn, ys, tc=128)


def kernel(x, c, w_ada, b_ada, g_pre_mix, g_post_mix, w_in, conv_w, conv_b, b_igate, b_fgate, g_mlstm_out,
           g_q_norm, w_uq, g_kv_norm, w_ukv, w_out, g_pre_ffn, g_post_ffn, w_router, b_router,
           w_gate, w_up, w_down, w_shared_gate, w_shared_up, w_shared_down):
    assert x.shape[0] == 1 and w_ada.shape[0] == 1, "single sequence, single layer"
    layer = (w_ada, b_ada, g_pre_mix, g_post_mix, w_in, conv_w, conv_b, b_igate, b_fgate, g_mlstm_out,
             g_q_norm, w_uq, g_kv_norm, w_ukv, w_out, g_pre_ffn, g_post_ffn, w_router, b_router,
             w_gate, w_up, w_down, w_shared_gate, w_shared_up, w_shared_down)
    out = _block(x[0], c[0], *[p[0] for p in layer])
    return out[None]
```

```python
import functools
import math

import jax
import jax.numpy as jnp
from jax import lax
from jax.experimental import pallas as pl
from jax.experimental.pallas import tpu as pltpu

F32 = jnp.float32
BF16 = jnp.bfloat16
I32 = jnp.int32
U32 = jnp.uint32

NORM_EPS = 1e-6
CHUNK = 64
M_CHUNK = 128

M_HEADS = 4
M_QK_DIM = 256
M_V_DIM = 512
M_WIDTH = M_HEADS * M_V_DIM
M_QK_WIDTH = 2 * M_HEADS * M_QK_DIM
CONV_WIDTH = 4
GATE_SOFTCAP = 15.0

A_HEADS = 16
A_NOPE_DIM = 128
A_ROPE_DIM = 64
A_V_DIM = 128
A_Q_RANK = 768
A_KV_RANK = 512
A_HEAD_PAD = 256
ROPE_THETA = 10000.0

N_EXPERTS = 64
TOP_K = 8
N_GROUPS = 8
GROUP_SIZE = N_EXPERTS // N_GROUPS
TOPK_GROUPS = 4
ROUTED_SCALE = 2.5

LANES = 128
VMEM_LIMIT = 56 * 1024 * 1024
NEG_BIG = -1e30

COL_QK = 0
COL_V = COL_QK + M_QK_WIDTH
COL_O = COL_V + M_WIDTH
COL_IG = COL_O + M_WIDTH
COL_FG = COL_IG + M_HEADS
COL_CQ = COL_FG + M_HEADS
COL_CKV = COL_CQ + A_Q_RANK
COL_KR = COL_CKV + A_KV_RANK
N_IN = COL_KR + A_ROPE_DIM
IN_TILE = 1536
TAIL_W = 2 * LANES
TAIL_GATE_LANE = A_ROPE_DIM


def _params(sem, vmem=VMEM_LIMIT):
    return pltpu.CompilerParams(dimension_semantics=sem, vmem_limit_bytes=vmem)


def _row_loop(n_rows, rc, body):
    def step(i, carry):
        body(pl.multiple_of(i * rc, rc))
        return carry
    lax.fori_loop(0, n_rows // rc, step, 0)


def _pack_pair(a, b):
    lo = lax.bitcast_convert_type(a.astype(BF16).astype(F32), U32) >> 16
    hi = lax.bitcast_convert_type(b.astype(BF16).astype(F32), U32) & jnp.uint32(0xFFFF0000)
    return lo | hi


def _unpack_pair(u):
    lo = lax.bitcast_convert_type(u << 16, F32)
    hi = lax.bitcast_convert_type(u & jnp.uint32(0xFFFF0000), F32)
    return lo, hi


def _adaln_kernel(c_ref, w_ref, b_ref, o_ref, *, rc):
    d, tn = w_ref.shape
    nl = tn // LANES

    def step(i, accs):
        r = pl.multiple_of(i * rc, rc)
        c = c_ref[pl.ds(r, rc), :]
        ca = c * jax.nn.sigmoid(c)
        out = []
        for j in range(nl):
            prod = w_ref[pl.ds(r, rc), j * LANES:(j + 1) * LANES] * ca
            out.append(accs[j] + jnp.sum(prod.reshape(rc // 8, 8, LANES), axis=0))
        return tuple(out)

    accs = lax.fori_loop(0, d // rc, step, tuple(jnp.zeros((8, LANES), F32) for _ in range(nl)))
    for j in range(nl):
        o_ref[:, j * LANES:(j + 1) * LANES] = (
            jnp.sum(accs[j], axis=0, keepdims=True) + b_ref[:, j * LANES:(j + 1) * LANES])


def _adaln(c, w_ada, b_ada):
    d, n = w_ada.shape
    tn = 512
    c_b = jnp.broadcast_to(c.reshape(d, 1), (d, LANES))
    return pl.pallas_call(
        functools.partial(_adaln_kernel, rc=64),
        grid=(n // tn,),
        in_specs=[pl.BlockSpec((d, LANES), lambda j: (0, 0)),
                  pl.BlockSpec((d, tn), lambda j: (0, j)),
                  pl.BlockSpec((1, tn), lambda j: (0, j))],
        out_specs=pl.BlockSpec((1, tn), lambda j: (0, j)),
        out_shape=jax.ShapeDtypeStruct((1, n), F32),
        compiler_params=_params(("arbitrary",)),
        name="adaln",
    )(c_b, w_ada, b_ada.reshape(1, n))


def _norm_mm_kernel(x_ref, g_ref, w_ref, o_ref, h_scr, *, rc):
    tm = x_ref.shape[0]

    @pl.when(pl.program_id(1) == 0)
    def _():
        def body(r):
            x = x_ref[pl.ds(r, rc), :].astype(F32)
            y = x * lax.rsqrt(jnp.mean(x * x, axis=-1, keepdims=True) + NORM_EPS)
            h_scr[pl.ds(r, rc), :] = (y * g_ref[...]).astype(BF16)
        _row_loop(tm, rc, body)

    o_ref[...] = jnp.dot(h_scr[...], w_ref[...], preferred_element_type=F32).astype(o_ref.dtype)


def _norm_mm(x, g, w, *, tm, tn):
    m, k = x.shape
    n = w.shape[1]
    tm = min(tm, m)
    return pl.pallas_call(
        functools.partial(_norm_mm_kernel, rc=32),
        grid=(m // tm, n // tn),
        in_specs=[pl.BlockSpec((tm, k), lambda i, j: (i, 0)),
                  pl.BlockSpec((1, k), lambda i, j: (0, 0)),
                  pl.BlockSpec((k, tn), lambda i, j: (0, j))],
        out_specs=pl.BlockSpec((tm, tn), lambda i, j: (i, j)),
        out_shape=jax.ShapeDtypeStruct((m, n), BF16),
        scratch_shapes=[pltpu.VMEM((tm, k), BF16)],
        compiler_params=_params(("parallel", "arbitrary")),
        name="norm_mm",
    )(x, g.reshape(1, k), w)


def _prenorm_kernel(x_ref, gs_ref, sh_ref, o_ref, *, rc):
    def body(r):
        x = x_ref[pl.ds(r, rc), :]
        y = x * lax.rsqrt(jnp.mean(x * x, axis=-1, keepdims=True) + NORM_EPS)
        o_ref[pl.ds(r, rc), :] = (y * gs_ref[...] + sh_ref[...]).astype(o_ref.dtype)
    _row_loop(x_ref.shape[0], rc, body)


def _prenorm(x, gs, sh, *, tm):
    m, k = x.shape
    tm = min(tm, m)
    return pl.pallas_call(
        functools.partial(_prenorm_kernel, rc=32),
        grid=(m // tm,),
        in_specs=[pl.BlockSpec((tm, k), lambda i: (i, 0)),
                  pl.BlockSpec((1, k), lambda i: (0, 0)),
                  pl.BlockSpec((1, k), lambda i: (0, 0))],
        out_specs=pl.BlockSpec((tm, k), lambda i: (i, 0)),
        out_shape=jax.ShapeDtypeStruct((m, k), BF16),
        compiler_params=_params(("parallel",)),
        name="prenorm",
    )(x, gs.reshape(1, k), sh.reshape(1, k))


def _mm_tail_kernel(a_ref, wt_ref, tailt_ref, o_ref, t_ref):
    nt_dims = (((1,), (1,)), ((), ()))
    o_ref[...] = lax.dot_general(a_ref[...], wt_ref[...], nt_dims, preferred_element_type=F32).astype(o_ref.dtype)

    @pl.when(pl.program_id(1) == pl.num_programs(1) - 1)
    def _():
        t_ref[...] = lax.dot_general(a_ref[...], tailt_ref[...], nt_dims, preferred_element_type=F32)


def _mm_tail(a, w_t, tail_t, *, tm, tn):
    m, k = a.shape
    n, nt = w_t.shape[0], tail_t.shape[0]
    tm = min(tm, m)
    return pl.pallas_call(
        _mm_tail_kernel,
        grid=(m // tm, pl.cdiv(n, tn)),
        in_specs=[pl.BlockSpec((tm, k), lambda i, j: (i, 0)),
                  pl.BlockSpec((tn, k), lambda i, j: (j, 0)),
                  pl.BlockSpec((nt, k), lambda i, j: (0, 0))],
        out_specs=[pl.BlockSpec((tm, tn), lambda i, j: (i, j)),
                   pl.BlockSpec((tm, nt), lambda i, j: (i, 0))],
        out_shape=[jax.ShapeDtypeStruct((m, n), BF16), jax.ShapeDtypeStruct((m, nt), F32)],
        compiler_params=_params(("parallel", "arbitrary"), vmem=60 * 1024 * 1024),
        name="in_proj",
    )(a, w_t, tail_t)


def _mm2_kernel(a1_ref, a2_ref, w_ref, o_ref):
    k1 = a1_ref.shape[1]
    acc = jnp.dot(a1_ref[...], w_ref[:k1, :], preferred_element_type=F32)
    acc = acc + jnp.dot(a2_ref[...], w_ref[k1:, :], preferred_element_type=F32)
    o_ref[...] = acc.astype(o_ref.dtype)


def _mm2(a1, a2, w, *, tm, tn):
    m, k1 = a1.shape
    k2 = a2.shape[1]
    n = w.shape[1]
    tm = min(tm, m)
    return pl.pallas_call(
        _mm2_kernel,
        grid=(m // tm, n // tn),
        in_specs=[pl.BlockSpec((tm, k1), lambda i, j: (i, 0)),
                  pl.BlockSpec((tm, k2), lambda i, j: (i, 0)),
                  pl.BlockSpec((k1 + k2, tn), lambda i, j: (0, j))],
        out_specs=pl.BlockSpec((tm, tn), lambda i, j: (i, j)),
        out_shape=jax.ShapeDtypeStruct((m, n), BF16),
        compiler_params=_params(("parallel", "arbitrary")),
        name="out_proj",
    )(a1, a2, w)


def _softcap(z):
    return GATE_SOFTCAP * jnp.tanh(z * (1.0 / GATE_SOFTCAP))


def _log_sigmoid(z):
    return jnp.minimum(z, 0.0) - jnp.log1p(jnp.exp(-jnp.abs(z)))


def _mlstm_kernel(qk_ref, prev_ref, v_ref, o_ref, gcol_ref, grow_ref, cw_ref, cb_ref, bcol_ref, brow_ref,
                  gout_ref, out_ref, c_scr, n_scr, m_scr, u_scr):
    c = pl.program_id(0)
    L = qk_ref.shape[0]
    halo = prev_ref.shape[0]
    dk, dv = M_QK_DIM, M_V_DIM

    @pl.when(c == 0)
    def _():
        c_scr[...] = jnp.zeros_like(c_scr)
        n_scr[...] = jnp.zeros_like(n_scr)
        m_scr[...] = jnp.zeros_like(m_scr)

    prev = prev_ref[...].astype(F32)
    u_scr[0:halo, :] = jnp.where(c == 0, jnp.zeros_like(prev), prev)
    u_scr[halo:halo + L, :] = qk_ref[...].astype(F32)

    def conv_silu(col, width):
        acc = cb_ref[:, col:col + width]
        for j in range(CONV_WIDTH):
            r0 = halo - (CONV_WIDTH - 1) + j
            acc = acc + u_scr[r0:r0 + L, col:col + width] * cw_ref[j:j + 1, col:col + width]
        return acc * jax.nn.sigmoid(acc)

    pre_c = _softcap(gcol_ref[...] + bcol_ref[...])
    lf_c = _log_sigmoid(pre_c)
    pre_r = _softcap(grow_ref[0] + brow_ref[...])
    lf_r = _log_sigmoid(pre_r)
    row = lax.broadcasted_iota(I32, (L, L), 0)
    col = lax.broadcasted_iota(I32, (L, L), 1)
    causal = col <= row
    tril = causal.astype(F32)
    triu = (row <= col).astype(F32)
    bcum_c = jnp.dot(tril, lf_c, preferred_element_type=F32, precision=lax.Precision.HIGHEST)
    bcum_r = jnp.dot(lf_r, triu, preferred_element_type=F32, precision=lax.Precision.HIGHEST)

    for h in range(M_HEADS):
        li_lane = TAIL_GATE_LANE + h
        lf_lane = TAIL_GATE_LANE + M_HEADS + h
        q = (conv_silu(h * dk, dk) * (dk ** -0.5)).astype(BF16)
        kf = conv_silu(M_HEADS * dk + h * dk, dk)
        kb = kf.astype(BF16)
        v = v_ref[:, h * dv:(h + 1) * dv]
        b_c = bcum_c[:, lf_lane:lf_lane + 1]
        li_c = pre_c[:, li_lane:li_lane + 1]
        b_r = bcum_r[M_HEADS + h:M_HEADS + h + 1, :]
        li_r = pre_r[h:h + 1, :]
        m_prev = m_scr[h:h + 1, 0:1]

        dm = jnp.where(causal, b_c - b_r + li_r, NEG_BIG)
        inter = b_c + m_prev
        m_t = jnp.maximum(jnp.max(dm, axis=1, keepdims=True), inter)
        decay = jnp.exp(inter - m_t)
        s = lax.dot_general(q, kb, (((1,), (1,)), ((), ())), preferred_element_type=F32) * jnp.exp(dm - m_t)
        c_state = c_scr[h]
        n_state = n_scr[h:h + 1, :]
        num = jnp.dot(s.astype(BF16), v, preferred_element_type=F32)
        num = num + decay * jnp.dot(q, c_state.astype(BF16), preferred_element_type=F32)
        den = jnp.sum(s, axis=1, keepdims=True) + decay * jnp.sum(q.astype(F32) * n_state, axis=1, keepdims=True)
        hh = num / jnp.maximum(jnp.abs(den), jnp.exp(-m_t))

        b_last = b_c[L - 1:L, :]
        m_new = jnp.maximum(b_last + m_prev, jnp.max(b_last - b_r + li_r, axis=1, keepdims=True))
        carry = jnp.exp(b_last + m_prev - m_new)
        w_c = jnp.exp(b_last - b_c + li_c - m_new)
        kw = kf * w_c
        c_scr[h] = carry * c_state + lax.dot_general(kw.astype(BF16), v, (((0,), (0,)), ((), ())),
                                                     preferred_element_type=F32)
        n_scr[h:h + 1, :] = carry * n_state + jnp.sum(kw, axis=0, keepdims=True)
        m_scr[h:h + 1, :] = jnp.broadcast_to(m_new, (1, LANES))

        hn = hh * lax.rsqrt(jnp.mean(hh * hh, axis=1, keepdims=True) + NORM_EPS) * gout_ref[:, h * dv:(h + 1) * dv]
        gate = jax.nn.sigmoid(o_ref[:, h * dv:(h + 1) * dv].astype(F32))
        out_ref[:, h * dv:(h + 1) * dv] = (hn * gate).astype(out_ref.dtype)


def _mlstm(proj, tail, conv_w, conv_b, b_igate, b_fgate, g_out, *, chunk):
    s_len = proj.shape[0]
    L = chunk
    n_c = s_len // L
    halo = 16
    gates = tail[:, TAIL_GATE_LANE:TAIL_GATE_LANE + 2 * M_HEADS]
    g_row = gates.reshape(n_c, L, 2 * M_HEADS).transpose(0, 2, 1)
    bias = jnp.concatenate([b_igate, b_fgate]).astype(F32)
    b_col = jnp.zeros((1, LANES), F32).at[0, TAIL_GATE_LANE:TAIL_GATE_LANE + 2 * M_HEADS].set(bias)
    b_row = bias.reshape(2 * M_HEADS, 1)
    w = M_QK_WIDTH
    return pl.pallas_call(
        _mlstm_kernel,
        grid=(n_c,),
        in_specs=[pl.BlockSpec((L, w), lambda c: (c, COL_QK // w)),
                  pl.BlockSpec((halo, w), lambda c: (jnp.maximum(c * (L // halo) - 1, 0), COL_QK // w)),
                  pl.BlockSpec((L, M_WIDTH), lambda c: (c, COL_V // M_WIDTH)),
                  pl.BlockSpec((L, M_WIDTH), lambda c: (c, COL_O // M_WIDTH)),
                  pl.BlockSpec((L, LANES), lambda c: (c, 0)),
                  pl.BlockSpec((1, 2 * M_HEADS, L), lambda c: (c, 0, 0)),
                  pl.BlockSpec((CONV_WIDTH, w), lambda c: (0, 0)),
                  pl.BlockSpec((1, w), lambda c: (0, 0)),
                  pl.BlockSpec((1, LANES), lambda c: (0, 0)),
                  pl.BlockSpec((2 * M_HEADS, 1), lambda c: (0, 0)),
                  pl.BlockSpec((1, M_WIDTH), lambda c: (0, 0))],
        out_specs=pl.BlockSpec((L, M_WIDTH), lambda c: (c, 0)),
        out_shape=jax.ShapeDtypeStruct((s_len, M_WIDTH), BF16),
        scratch_shapes=[pltpu.VMEM((M_HEADS, M_QK_DIM, M_V_DIM), F32),
                        pltpu.VMEM((8, M_QK_DIM), F32),
                        pltpu.VMEM((8, LANES), F32),
                        pltpu.VMEM((halo + L, w), F32)],
        compiler_params=_params(("arbitrary",)),
        name="mlstm",
    )(proj, proj, proj, proj, tail, g_row, conv_w, conv_b.reshape(1, w), b_col, b_row, g_out.reshape(1, M_WIDTH))


def _rope_kernel(q_ref, kn_ref, v_ref, tail_ref, cos_ref, sin_ref, qo_ref, ko_ref, vo_ref):
    cos = cos_ref[...]
    sin = sin_ref[...]
    nope_w = A_HEADS * A_NOPE_DIM
    kpe = (tail_ref[:, LANES:2 * LANES] * cos + tail_ref[:, 0:LANES] * sin).astype(BF16)
    lane = lax.broadcasted_iota(I32, (q_ref.shape[0], LANES), 1)
    ones_col = jnp.where(lane == 0, 1.0, 0.0).astype(BF16)
    for h in range(A_HEADS):
        vo_ref[:, h * A_HEAD_PAD:h * A_HEAD_PAD + LANES] = v_ref[:, h * LANES:(h + 1) * LANES]
        vo_ref[:, h * A_HEAD_PAD + LANES:(h + 1) * A_HEAD_PAD] = ones_col
        lo = h * A_HEAD_PAD
        qo_ref[:, lo:lo + LANES] = q_ref[:, h * LANES:(h + 1) * LANES]
        qr = q_ref[:, nope_w + h * LANES:nope_w + (h + 1) * LANES].astype(F32)
        qs = q_ref[:, 2 * nope_w + h * LANES:2 * nope_w + (h + 1) * LANES].astype(F32)
        qo_ref[:, lo + LANES:lo + 2 * LANES] = (qr * cos + qs * sin).astype(BF16)
        ko_ref[:, lo:lo + LANES] = kn_ref[:, h * LANES:(h + 1) * LANES]
        ko_ref[:, lo + LANES:lo + 2 * LANES] = kpe


def _rope_assemble(q_raw, kv_raw, tail, cos_t, sin_t, *, tm):
    s_len = q_raw.shape[0]
    tm = min(tm, s_len)
    nope_w = A_HEADS * A_NOPE_DIM
    wide = A_HEADS * A_HEAD_PAD
    return pl.pallas_call(
        _rope_kernel,
        grid=(s_len // tm,),
        in_specs=[pl.BlockSpec((tm, 3 * nope_w), lambda i: (i, 0)),
                  pl.BlockSpec((tm, nope_w), lambda i: (i, 0)),
                  pl.BlockSpec((tm, nope_w), lambda i: (i, 1)),
                  pl.BlockSpec((tm, 2 * LANES), lambda i: (i, 0)),
                  pl.BlockSpec((tm, LANES), lambda i: (i, 0)),
                  pl.BlockSpec((tm, LANES), lambda i: (i, 0))],
        out_specs=[pl.BlockSpec((tm, wide), lambda i: (i, 0))] * 3,
        out_shape=[jax.ShapeDtypeStruct((s_len, wide), BF16)] * 3,
        compiler_params=_params(("parallel",)),
        name="rope_assemble",
    )(q_raw, kv_raw, kv_raw, tail, cos_t, sin_t)


def _attn_kernel(q_ref, k_ref, v_ref, o_ref, m_scr, acc_scr, s_scr, *, chunk):
    qi = pl.program_id(1)
    tq = q_ref.shape[0]
    m_scr[...] = jnp.full_like(m_scr, NEG_BIG)
    acc_scr[...] = jnp.zeros_like(acc_scr)

    def scores(slot, blk):
        start = pl.multiple_of(blk * tq, tq)
        s_scr[slot] = lax.dot_general(q_ref[...], k_ref[pl.ds(start, tq), :], (((1,), (1,)), ((), ())),
                                      preferred_element_type=F32)

    def consume(slot, blk, masked):
        start = pl.multiple_of(blk * tq, tq)
        s = s_scr[slot]
        if masked:
            rq = lax.broadcasted_iota(I32, (tq, tq), 0) // chunk
            ck = lax.broadcasted_iota(I32, (tq, tq), 1) // chunk
            s = jnp.where(ck <= rq, s, NEG_BIG)
        m_prev = m_scr[...]
        m_new = jnp.maximum(m_prev, jnp.max(s, axis=1, keepdims=True))
        pr = jnp.exp(s - m_new).astype(BF16)
        acc_scr[...] = (jnp.exp(m_prev - m_new) * acc_scr[...]
                        + jnp.dot(pr, v_ref[pl.ds(start, tq), :], preferred_element_type=F32))
        m_scr[...] = m_new

    scores(0, 0)

    def pair(t, carry):
        scores(1, 2 * t + 1)
        consume(0, 2 * t, False)
        scores(0, 2 * t + 2)
        consume(1, 2 * t + 1, False)
        return carry
    lax.fori_loop(0, qi // 2, pair, 0)

    @pl.when(qi % 2 == 1)
    def _():
        scores(1, qi)
        consume(0, qi - 1, False)
        consume(1, qi, True)

    @pl.when(qi % 2 == 0)
    def _():
        consume(0, qi, True)

    acc = acc_scr[...]
    o_ref[...] = (acc[:, :A_V_DIM] / acc[:, A_V_DIM:A_V_DIM + 1]).astype(o_ref.dtype)


def _attention(q_cat, k_cat, v_cat, *, tq, chunk):
    s_len = q_cat.shape[0]
    tq = min(tq, s_len)
    return pl.pallas_call(
        functools.partial(_attn_kernel, chunk=chunk),
        grid=(A_HEADS, s_len // tq),
        in_specs=[pl.BlockSpec((tq, A_HEAD_PAD), lambda h, i: (i, h)),
                  pl.BlockSpec((s_len, A_HEAD_PAD), lambda h, i: (0, h)),
                  pl.BlockSpec((s_len, A_HEAD_PAD), lambda h, i: (0, h))],
        out_specs=pl.BlockSpec((tq, A_V_DIM), lambda h, i: (i, h)),
        out_shape=jax.ShapeDtypeStruct((s_len, A_HEADS * A_V_DIM), BF16),
        scratch_shapes=[pltpu.VMEM((tq, 1), F32), pltpu.VMEM((tq, A_HEAD_PAD), F32),
                        pltpu.VMEM((2, tq, tq), F32)],
        compiler_params=_params(("parallel", "arbitrary")),
        name="mla_attention",
    )(q_cat, k_cat, v_cat)


def _route_t(logits, bias_col):
    n = logits.shape[1]
    scores = jax.nn.sigmoid(logits)
    biased = scores + bias_col
    sub = lax.broadcasted_iota(I32, (GROUP_SIZE, n), 0)
    rows = []
    for g in range(N_GROUPS):
        x = biased[g * GROUP_SIZE:(g + 1) * GROUP_SIZE, :]
        m1 = jnp.max(x, axis=0, keepdims=True)
        i1 = jnp.min(jnp.where(x == m1, sub, GROUP_SIZE), axis=0, keepdims=True)
        m2 = jnp.max(jnp.where(sub == i1, -jnp.inf, x), axis=0, keepdims=True)
        rows.append(m1 + m2)
    gscore = jnp.concatenate(rows, axis=0)
    gio = lax.broadcasted_iota(I32, (N_GROUPS, n), 0)
    grank = jnp.zeros((N_GROUPS, n), I32)
    for g in range(N_GROUPS):
        r = gscore[g:g + 1, :]
        grank = grank + jnp.where(gio > g, jnp.where(r >= gscore, 1, 0), jnp.where(r > gscore, 1, 0))
    gsel = grank < TOPK_GROUPS
    masked = jnp.concatenate(
        [jnp.where(gsel[g:g + 1, :], biased[g * GROUP_SIZE:(g + 1) * GROUP_SIZE, :], -jnp.inf)
         for g in range(N_GROUPS)], axis=0)
    eio = lax.broadcasted_iota(I32, (N_EXPERTS, n), 0)
    rank = jnp.zeros((N_EXPERTS, n), I32)
    for e in range(N_EXPERTS):
        r = masked[e:e + 1, :]
        rank = rank + jnp.where(eio > e, jnp.where(r >= masked, 1, 0), jnp.where(r > masked, 1, 0))
    sel = rank < TOP_K
    denom = jnp.sum(jnp.where(sel, scores, 0.0), axis=0, keepdims=True)
    wnorm = scores / denom * ROUTED_SCALE
    eio_f = eio.astype(F32)
    ids, wts = [], []
    for k in range(TOP_K):
        hit = rank == k
        ids.append(jnp.sum(jnp.where(hit, eio_f, 0.0), axis=0, keepdims=True))
        wts.append(jnp.sum(jnp.where(hit, wnorm, 0.0), axis=0, keepdims=True))
    return jnp.concatenate(ids, axis=0).astype(I32), jnp.concatenate(wts, axis=0)


def _mid_kernel(y_ref, x_ref, gt_ref, gpost_ref, gs_ref, sh_ref, wr_ref, br_ref,
                x1_ref, hp_ref, idx_ref, wts_ref, h_scr, *, rc):
    tm, d = x_ref.shape
    half = d // 2

    def body(r):
        y = y_ref[pl.ds(r, rc), :].astype(F32)
        yn = y * lax.rsqrt(jnp.mean(y * y, axis=-1, keepdims=True) + NORM_EPS) * gpost_ref[...]
        x1 = x_ref[pl.ds(r, rc), :] + gt_ref[...] * yn
        x1_ref[pl.ds(r, rc), :] = x1
        hn = x1 * lax.rsqrt(jnp.mean(x1 * x1, axis=-1, keepdims=True) + NORM_EPS)
        h = hn * gs_ref[...] + sh_ref[...]
        h_scr[pl.ds(r, rc), :] = h
        hp_ref[pl.ds(r, rc), :] = _pack_pair(h[:, :half], h[:, half:])
    _row_loop(tm, rc, body)

    logits = lax.dot_general(wr_ref[...], h_scr[...], (((1,), (1,)), ((), ())),
                             preferred_element_type=F32, precision=lax.Precision.HIGHEST)
    ids, wts = _route_t(logits, br_ref[...])
    idx_ref[...] = ids
    wts_ref[...] = wts


def _mid(y, x, gt1, g_post, gs2, sh2, w_router, b_router, *, tm):
    s_len, d = x.shape
    tm = min(tm, s_len)
    vec = lambda a: a.reshape(1, d).astype(F32)
    return pl.pallas_call(
        functools.partial(_mid_kernel, rc=16),
        grid=(s_len // tm,),
        in_specs=[pl.BlockSpec((tm, d), lambda i: (i, 0)),
                  pl.BlockSpec((tm, d), lambda i: (i, 0)),
                  pl.BlockSpec((1, d), lambda i: (0, 0)),
                  pl.BlockSpec((1, d), lambda i: (0, 0)),
                  pl.BlockSpec((1, d), lambda i: (0, 0)),
                  pl.BlockSpec((1, d), lambda i: (0, 0)),
                  pl.BlockSpec((N_EXPERTS, d), lambda i: (0, 0)),
                  pl.BlockSpec((N_EXPERTS, 1), lambda i: (0, 0))],
        out_specs=[pl.BlockSpec((tm, d), lambda i: (i, 0)),
                   pl.BlockSpec((tm, d // 2), lambda i: (i, 0)),
                   pl.BlockSpec((TOP_K, tm), lambda i: (0, i)),
                   pl.BlockSpec((TOP_K, tm), lambda i: (0, i))],
        out_shape=[jax.ShapeDtypeStruct((s_len, d), F32),
                   jax.ShapeDtypeStruct((s_len, d // 2), U32),
                   jax.ShapeDtypeStruct((TOP_K, s_len), I32),
                   jax.ShapeDtypeStruct((TOP_K, s_len), F32)],
        scratch_shapes=[pltpu.VMEM((tm, d), F32)],
        compiler_params=_params(("parallel",)),
        name="mid_norm_route",
    )(y, x, vec(gt1), vec(g_post), vec(gs2), vec(sh2), w_router.T.astype(F32), b_router.reshape(N_EXPERTS, 1))


def _row_copy(src_hbm, dst_vmem, sem, src_row, dst_row):
    return pltpu.make_async_copy(src_hbm.at[pl.ds(src_row, 1), :], dst_vmem.at[pl.ds(dst_row, 1), :], sem)


ROW_GROUP = 8
ROW_DMA_PRIORITY = 0
WEIGHT_DMA_PRIORITY = 1


def _expert_kernel(be_ref, nb_ref, nxt_ref, tok_ref, h_hbm, wg_hbm, wu_hbm, wd_hbm, sg_hbm, su_hbm, sd_hbm,
                   o_ref, stage_g, stage_u, stage_d, wb_g, wb_u, wb_d, xbuf, sems, xsems, *, rc):
    b = pl.program_id(0)
    e = be_ref[b]
    tmb = xbuf.shape[1]
    cur = lax.rem(b, 2)
    stages = (stage_g, stage_u, stage_d)
    routed = (wg_hbm, wu_hbm, wd_hbm)
    shared = (sg_hbm, su_hbm, sd_hbm)

    def gather_loop(blk, slot):
        base = blk * tmb

        def step(gi, carry):
            r0 = gi * ROW_GROUP
            for u in range(ROW_GROUP):
                _row_copy(h_hbm, xbuf.at[slot], xsems.at[slot], tok_ref[base + r0 + u], r0 + u).start(
                    priority=ROW_DMA_PRIORITY)
            return carry
        lax.fori_loop(0, tmb // ROW_GROUP, step, 0)

    def gather_unrolled(blk, slot, part, n_parts):
        base = blk * tmb
        for r in range(part * tmb // n_parts, (part + 1) * tmb // n_parts):
            _row_copy(h_hbm, xbuf.at[slot], xsems.at[slot], tok_ref[base + r], r).start(priority=ROW_DMA_PRIORITY)

    def gather_wait(slot):
        pltpu.make_async_copy(h_hbm.at[pl.ds(0, tmb), :], xbuf.at[slot], xsems.at[slot]).wait()

    def fetch(ex):
        @pl.when(ex < N_EXPERTS)
        def _():
            for i in range(3):
                pltpu.make_async_copy(routed[i].at[ex], stages[i], sems.at[i]).start(priority=WEIGHT_DMA_PRIORITY)

        @pl.when(ex == N_EXPERTS)
        def _():
            for i in range(3):
                pltpu.make_async_copy(shared[i], stages[i], sems.at[i]).start(priority=WEIGHT_DMA_PRIORITY)

    @pl.when(b < nb_ref[0])
    def _():
        @pl.when(b == 0)
        def _():
            fetch(e)
            gather_loop(0, 0)

        is_first = jnp.logical_or(b == 0, be_ref[jnp.maximum(b - 1, 0)] != e)

        @pl.when(is_first)
        def _():
            for i in range(3):
                pltpu.make_async_copy(shared[i], stages[i], sems.at[i]).wait()
            for src, dst in ((stage_g, wb_g), (stage_u, wb_u), (stage_d, wb_d)):
                def cast(r, src=src, dst=dst):
                    dst[pl.ds(r, rc), :] = src[pl.ds(r, rc), :].astype(BF16)
                _row_loop(src.shape[0], rc, cast)
            nxt = nxt_ref[e]

            @pl.when(nxt >= 0)
            def _():
                fetch(nxt)

        gather_wait(cur)
        nxt_blk = jnp.minimum(b + 1, nb_ref[0] - 1)
        lo, hi = _unpack_pair(xbuf[cur])
        x = jnp.concatenate([lo, hi], axis=1).astype(BF16)
        gather_unrolled(nxt_blk, 1 - cur, 0, 3)
        g = jnp.dot(x, wb_g[...], preferred_element_type=F32)
        gather_unrolled(nxt_blk, 1 - cur, 1, 3)
        u = jnp.dot(x, wb_u[...], preferred_element_type=F32)
        a = (g * jax.nn.sigmoid(g) * u).astype(BF16)
        gather_unrolled(nxt_blk, 1 - cur, 2, 3)
        y = jnp.dot(a, wb_d[...], preferred_element_type=F32)
        half = y.shape[1] // 2
        o_ref[...] = _pack_pair(y[:, :half], y[:, half:])

        @pl.when(b == nb_ref[0] - 1)
        def _():
            gather_wait(1 - cur)

    @pl.when(b >= nb_ref[0])
    def _():
        o_ref[...] = jnp.zeros_like(o_ref)


def _experts(block_e, n_used, next_e, slot_tok, hp, wg, wu, wd, sg, su, sd, *, tmb):
    n_slots = slot_tok.shape[0]
    wp = hp.shape[1]
    d, ff = wg.shape[1], wg.shape[2]
    nb = n_slots // tmb
    grid_spec = pltpu.PrefetchScalarGridSpec(
        num_scalar_prefetch=4,
        grid=(nb,),
        in_specs=[pl.BlockSpec(memory_space=pl.ANY)] * 7,
        out_specs=pl.BlockSpec((tmb, wp), lambda b, *_: (b, 0)),
        scratch_shapes=[pltpu.VMEM((d, ff), F32), pltpu.VMEM((d, ff), F32), pltpu.VMEM((ff, d), F32),
                        pltpu.VMEM((d, ff), BF16), pltpu.VMEM((d, ff), BF16), pltpu.VMEM((ff, d), BF16),
                        pltpu.VMEM((2, tmb, wp), U32),
                        pltpu.SemaphoreType.DMA((3,)), pltpu.SemaphoreType.DMA((2,))],
    )
    return pl.pallas_call(
        functools.partial(_expert_kernel, rc=128),
        grid_spec=grid_spec,
        out_shape=jax.ShapeDtypeStruct((n_slots, wp), U32),
        compiler_params=_params(("arbitrary",), vmem=60 * 1024 * 1024),
        name="moe_experts",
    )(block_e, n_used, next_e, slot_tok, hp, wg, wu, wd, sg, su, sd)


def _combine_kernel(slot_ref, w_ref, x1_ref, gt_ref, g_ref, ys_hbm, o_ref, buf, sems, *, rc, n_k):
    tc, d = x1_ref.shape
    half = d // 2
    i = pl.program_id(0)
    cur = lax.rem(i, 2)
    last = pl.num_programs(0) - 1

    def issue_plane(tile, slot, t0, k):
        base = tile * (tc * n_k)
        for t in range(rc):
            _row_copy(ys_hbm, buf.at[slot, k], sems.at[slot], slot_ref[base + (t0 + t) * n_k + k],
                      t0 + t).start(priority=k % 2)

    def issue_rows(tile, slot, t0):
        for k in range(n_k):
            issue_plane(tile, slot, t0, k)

    def wait_planes(slot):
        for k in range(n_k):
            pltpu.make_async_copy(ys_hbm.at[pl.ds(0, tc), :], buf.at[slot, k], sems.at[slot]).wait()

    @pl.when(i == 0)
    def _():
        _row_loop(tc, rc, lambda r: issue_rows(0, 0, r))

    wait_planes(cur)
    nxt = jnp.minimum(i + 1, last)

    def body(r):
        w = w_ref[pl.ds(r, rc), :]
        lo = jnp.zeros((rc, half), F32)
        hi = jnp.zeros((rc, half), F32)
        for k in range(n_k):
            issue_plane(nxt, 1 - cur, r, k)
            a, b = _unpack_pair(buf[cur, k, pl.ds(r, rc), :])
            lo = lo + w[:, k:k + 1] * a
            hi = hi + w[:, k:k + 1] * b
        ms = (jnp.sum(lo * lo, axis=-1, keepdims=True) + jnp.sum(hi * hi, axis=-1, keepdims=True)) * (1.0 / d)
        rs = lax.rsqrt(ms + NORM_EPS)
        o_ref[pl.ds(r, rc), 0:half] = (x1_ref[pl.ds(r, rc), 0:half]
                                       + gt_ref[:, 0:half] * (lo * rs * g_ref[:, 0:half]))
        o_ref[pl.ds(r, rc), half:d] = (x1_ref[pl.ds(r, rc), half:d]
                                       + gt_ref[:, half:d] * (hi * rs * g_ref[:, half:d]))
    _row_loop(tc, rc, body)

    @pl.when(i == last)
    def _():
        wait_planes(1 - cur)


def _combine(slots, w_tok, x1, gt2, g_post, ys, *, tc):
    s_len, d = x1.shape
    n_k = slots.shape[0] // s_len
    tc = min(tc, s_len)
    grid_spec = pltpu.PrefetchScalarGridSpec(
        num_scalar_prefetch=1,
        grid=(s_len // tc,),
        in_specs=[pl.BlockSpec((tc, w_tok.shape[1]), lambda i, sl: (i, 0)),
                  pl.BlockSpec((tc, d), lambda i, sl: (i, 0)),
                  pl.BlockSpec((1, d), lambda i, sl: (0, 0)),
                  pl.BlockSpec((1, d), lambda i, sl: (0, 0)),
                  pl.BlockSpec(memory_space=pl.ANY)],
        out_specs=pl.BlockSpec((tc, d), lambda i, sl: (i, 0)),
        scratch_shapes=[pltpu.VMEM((2, n_k, tc, d // 2), U32), pltpu.SemaphoreType.DMA((2,))],
    )
    return pl.pallas_call(
        functools.partial(_combine_kernel, rc=8, n_k=n_k),
        grid_spec=grid_spec,
        out_shape=jax.ShapeDtypeStruct((s_len, d), F32),
        compiler_params=_params(("arbitrary",)),
        name="moe_combine",
    )(slots, w_tok, x1, gt2.reshape(1, d), g_post.reshape(1, d), ys)


def _invert_kernel(slots_ref, lo_ref, hi_ref, o_ref, *, n_k, n_tok, stride):
    n_slots = o_ref.shape[0]
    n_e = lo_ref.shape[0]
    mask = (1 << (n_tok.bit_length() - 1)) - 1

    def fill(lo, hi):
        def body(j, carry):
            o_ref[j] = (j * stride) & mask
            return carry
        lax.fori_loop(lo, hi, body, 0)

    def per_expert(e, carry):
        fill(lo_ref[e], hi_ref[e])
        return carry
    lax.fori_loop(0, n_e, per_expert, 0)
    fill(hi_ref[n_e - 1], n_slots)

    def put(t, carry):
        for k in range(n_k):
            o_ref[slots_ref[t * n_k + k]] = t
        return carry
    lax.fori_loop(0, n_tok, put, 0)


def _invert_slots(slots, pad_lo, pad_hi, *, n_slots, n_k, n_tok, stride):
    smem = pl.BlockSpec(memory_space=pltpu.SMEM)
    return pl.pallas_call(
        functools.partial(_invert_kernel, n_k=n_k, n_tok=n_tok, stride=stride),
        in_specs=[smem, smem, smem],
        out_specs=smem,
        out_shape=jax.ShapeDtypeStruct((n_slots,), I32),
        name="moe_invert_slots",
    )(slots, pad_lo, pad_hi)


def _spread_stride(n):
    m = int(n * 0.6180339887) | 1
    while math.gcd(m, n) != 1:
        m += 2
    return m


def _moe_plan(idx_t, wts_t, *, tmb):
    n_k, n_tok = idx_t.shape
    n_e = N_EXPERTS + 1
    eid = jnp.concatenate([idx_t, jnp.full((1, n_tok), N_EXPERTS, I32)], axis=0)
    wts = jnp.concatenate([wts_t, jnp.ones((1, n_tok), F32)], axis=0)
    picks = eid[:, None, :] == jnp.arange(n_e, dtype=I32)[None, :, None]
    sel = picks.any(axis=0).astype(I32)
    stride = _spread_stride(n_tok)
    visit = lax.rem(jnp.arange(n_tok, dtype=I32) * stride, n_tok)
    where = lax.rem(jnp.arange(n_tok, dtype=I32) * pow(stride, -1, n_tok), n_tok)
    csum = jnp.cumsum(sel[:, visit], axis=1)
    counts = csum[:, -1]
    padded = (counts + tmb - 1) // tmb * tmb
    ends = jnp.cumsum(padded)
    starts = ends - padded
    slot_dense = (starts[:, None] + csum - 1)[:, where]
    slot = jnp.sum(jnp.where(picks, slot_dense[None], 0), axis=1)
    n_slots = -(-(n_tok * (n_k + 1) + n_e * (tmb - 1)) // tmb) * tmb
    slots_tok_major = slot.T.reshape(-1)
    slot_tok = _invert_slots(slots_tok_major, (starts + counts).astype(I32), ends.astype(I32),
                             n_slots=n_slots, n_k=n_k + 1, n_tok=n_tok, stride=stride)
    n_used = (ends[-1] // tmb).astype(I32).reshape(1)
    block_start = jnp.arange(n_slots // tmb, dtype=I32) * tmb
    block_e = jnp.minimum(jnp.sum(ends[None, :] <= block_start[:, None], axis=1), n_e - 1).astype(I32)
    w_tok = jnp.zeros((n_tok, 16), F32).at[:, :n_k + 1].set(wts.T)
    owner = jnp.where(padded > 0, jnp.arange(n_e, dtype=I32), n_e)
    later = jnp.concatenate([lax.cummin(owner[::-1])[::-1][1:], jnp.full((1,), n_e, I32)])
    next_e = jnp.where(later >= n_e, -1, later).astype(I32)
    return slot_tok, block_e, n_used, next_e, slots_tok_major, w_tok


def _in_proj_weights(w_in):
    w_t = w_in.T
    d = w_t.shape[1]
    kr = w_t[COL_KR:COL_KR + A_ROPE_DIM]
    gates = w_t[COL_IG:COL_IG + 2 * M_HEADS]
    half = A_ROPE_DIM // 2
    z = lambda n: jnp.zeros((n, d), w_t.dtype)
    tail_t = jnp.concatenate([-kr[half:], kr[:half], gates, z(LANES - A_ROPE_DIM - 2 * M_HEADS),
                              kr, z(LANES - A_ROPE_DIM)], axis=0)
    return w_t.astype(BF16), tail_t.astype(BF16)


def _q_up_weight(w_uq):
    r = w_uq.shape[0]
    w = w_uq.reshape(r, A_HEADS, A_NOPE_DIM + A_ROPE_DIM)
    nope = w[:, :, :A_NOPE_DIM]
    rope = w[:, :, A_NOPE_DIM:]
    half = A_ROPE_DIM // 2
    swap = jnp.concatenate([-rope[:, :, half:], rope[:, :, :half]], axis=2)
    pad = jnp.zeros((r, A_HEADS, LANES - A_ROPE_DIM), w_uq.dtype)
    rope_p = jnp.concatenate([rope, pad], axis=2)
    swap_p = jnp.concatenate([swap, pad], axis=2)
    flat = lambda a: a.reshape(r, -1)
    return jnp.concatenate([flat(nope), flat(rope_p), flat(swap_p)], axis=1).astype(BF16)


def _kv_up_weight(w_ukv):
    r = w_ukv.shape[0]
    w = w_ukv.reshape(r, A_HEADS, A_NOPE_DIM + A_V_DIM)
    return jnp.concatenate([w[:, :, :A_NOPE_DIM].reshape(r, -1), w[:, :, A_NOPE_DIM:].reshape(r, -1)],
                           axis=1).astype(BF16)


def _rope_tables(s_len):
    pos = jnp.arange(s_len, dtype=F32)
    inv_freq = 1.0 / (ROPE_THETA ** (jnp.arange(0, A_ROPE_DIM, 2, dtype=F32) / A_ROPE_DIM))
    ang = pos[:, None] * inv_freq[None, :]
    pad = jnp.zeros((s_len, LANES - A_ROPE_DIM), F32)
    cos_t = jnp.concatenate([jnp.cos(ang), jnp.cos(ang), pad], axis=1)
    sin_t = jnp.concatenate([jnp.sin(ang), jnp.sin(ang), pad], axis=1)
    return cos_t, sin_t


def _block(x, c, w_ada, b_ada, g_pre_mix, g_post_mix, w_in, conv_w, conv_b, b_igate, b_fgate, g_mlstm_out,
           g_q_norm, w_uq, g_kv_norm, w_ukv, w_out, g_pre_ffn, g_post_ffn, w_router, b_router,
           w_gate, w_up, w_down, w_shared_gate, w_shared_up, w_shared_down):
    s_len, d = x.shape
    mod = _adaln(c, w_ada, b_ada)[0]
    sh1, sc1, gt1, sh2, sc2, gt2 = [mod[i * d:(i + 1) * d] for i in range(6)]

    w_main, w_tail = _in_proj_weights(w_in)
    h1 = _prenorm(x, g_pre_mix * (1.0 + sc1), sh1, tm=256)
    proj, tail = _mm_tail(h1, w_main, w_tail, tm=1024, tn=IN_TILE)
    h_m = _mlstm(proj, tail, conv_w, conv_b, b_igate, b_fgate, g_mlstm_out, chunk=min(M_CHUNK, s_len))
    scale = (A_NOPE_DIM + A_ROPE_DIM) ** -0.5
    q_raw = _norm_mm(proj[:, COL_CQ:COL_CQ + A_Q_RANK], g_q_norm * scale, _q_up_weight(w_uq), tm=1024, tn=1536)
    kv_raw = _norm_mm(proj[:, COL_CKV:COL_CKV + A_KV_RANK], g_kv_norm, _kv_up_weight(w_ukv), tm=1024, tn=1024)
    cos_t, sin_t = _rope_tables(s_len)
    q_cat, k_cat, v_cat = _rope_assemble(q_raw, kv_raw, tail, cos_t, sin_t, tm=256)
    h_a = _attention(q_cat, k_cat, v_cat, tq=1024, chunk=CHUNK)
    y = _mm2(h_m, h_a, w_out.astype(BF16), tm=1024, tn=1024)

    x1, hp, idx_t, wts_t = _mid(y, x, gt1, g_post_mix, g_pre_ffn * (1.0 + sc2), sh2, w_router, b_router, tm=256)
    tmb = 256
    slot_tok, block_e, n_used, next_e, slots, w_tok = _moe_plan(idx_t, wts_t, tmb=tmb)
    ys = _experts(block_e, n_used, next_e, slot_tok, hp, w_gate, w_up, w_down,
                  w_shared_gate, w_shared_up, w_shared_down, tmb=tmb)
    return _combine(slots, w_tok, x1, gt2, g_post_ffn, ys, tc=128)


def kernel(x, c, w_ada, b_ada, g_pre_mix, g_post_mix, w_in, conv_w, conv_b, b_igate, b_fgate, g_mlstm_out,
           g_q_norm, w_uq, g_kv_norm, w_ukv, w_out, g_pre_ffn, g_post_ffn, w_router, b_router,
           w_gate, w_up, w_down, w_shared_gate, w_shared_up, w_shared_down):
    assert x.shape[0] == 1 and w_ada.shape[0] == 1, "single sequence, single layer"
    layer = (w_ada, b_ada, g_pre_mix, g_post_mix, w_in, conv_w, conv_b, b_igate, b_fgate, g_mlstm_out,
             g_q_norm, w_uq, g_kv_norm, w_ukv, w_out, g_pre_ffn, g_post_ffn, w_router, b_router,
             w_gate, w_up, w_down, w_shared_gate, w_shared_up, w_shared_down)
    out = _block(x[0], c[0], *[p[0] for p in layer])
    return out[None]
```

```python
import functools
import math

import jax
import jax.numpy as jnp
from jax import lax
from jax.experimental import pallas as pl
from jax.experimental.pallas import tpu as pltpu

F32 = jnp.float32
BF16 = jnp.bfloat16
I32 = jnp.int32
U32 = jnp.uint32

NORM_EPS = 1e-6
CHUNK = 64
M_CHUNK = 128

M_HEADS = 4
M_QK_DIM = 256
M_V_DIM = 512
M_WIDTH = M_HEADS * M_V_DIM
M_QK_WIDTH = 2 * M_HEADS * M_QK_DIM
CONV_WIDTH = 4
GATE_SOFTCAP = 15.0

A_HEADS = 16
A_NOPE_DIM = 128
A_ROPE_DIM = 64
A_V_DIM = 128
A_Q_RANK = 768
A_KV_RANK = 512
A_HEAD_PAD = 256
ROPE_THETA = 10000.0

N_EXPERTS = 64
TOP_K = 8
N_GROUPS = 8
GROUP_SIZE = N_EXPERTS // N_GROUPS
TOPK_GROUPS = 4
ROUTED_SCALE = 2.5

LANES = 128
VMEM_LIMIT = 56 * 1024 * 1024
NEG_BIG = -1e30

COL_QK = 0
COL_V = COL_QK + M_QK_WIDTH
COL_O = COL_V + M_WIDTH
COL_IG = COL_O + M_WIDTH
COL_FG = COL_IG + M_HEADS
COL_CQ = COL_FG + M_HEADS
COL_CKV = COL_CQ + A_Q_RANK
COL_KR = COL_CKV + A_KV_RANK
N_IN = COL_KR + A_ROPE_DIM
IN_TILE = 1536
TAIL_W = 2 * LANES
TAIL_GATE_LANE = A_ROPE_DIM


def _params(sem, vmem=VMEM_LIMIT):
    return pltpu.CompilerParams(dimension_semantics=sem, vmem_limit_bytes=vmem)


def _row_loop(n_rows, rc, body):
    def step(i, carry):
        body(pl.multiple_of(i * rc, rc))
        return carry
    lax.fori_loop(0, n_rows // rc, step, 0)


def _pack_pair(a, b):
    lo = lax.bitcast_convert_type(a.astype(BF16).astype(F32), U32) >> 16
    hi = lax.bitcast_convert_type(b.astype(BF16).astype(F32), U32) & jnp.uint32(0xFFFF0000)
    return lo | hi


def _unpack_pair(u):
    lo = lax.bitcast_convert_type(u << 16, F32)
    hi = lax.bitcast_convert_type(u & jnp.uint32(0xFFFF0000), F32)
    return lo, hi


def _adaln_kernel(c_ref, w_ref, b_ref, o_ref, *, rc):
    d, tn = w_ref.shape
    nl = tn // LANES

    def step(i, accs):
        r = pl.multiple_of(i * rc, rc)
        c = c_ref[pl.ds(r, rc), :]
        ca = c * jax.nn.sigmoid(c)
        out = []
        for j in range(nl):
            prod = w_ref[pl.ds(r, rc), j * LANES:(j + 1) * LANES] * ca
            out.append(accs[j] + jnp.sum(prod.reshape(rc // 8, 8, LANES), axis=0))
        return tuple(out)

    accs = lax.fori_loop(0, d // rc, step, tuple(jnp.zeros((8, LANES), F32) for _ in range(nl)))
    for j in range(nl):
        o_ref[:, j * LANES:(j + 1) * LANES] = (
            jnp.sum(accs[j], axis=0, keepdims=True) + b_ref[:, j * LANES:(j + 1) * LANES])


def _adaln(c, w_ada, b_ada):
    d, n = w_ada.shape
    tn = 512
    c_b = jnp.broadcast_to(c.reshape(d, 1), (d, LANES))
    return pl.pallas_call(
        functools.partial(_adaln_kernel, rc=64),
        grid=(n // tn,),
        in_specs=[pl.BlockSpec((d, LANES), lambda j: (0, 0)),
                  pl.BlockSpec((d, tn), lambda j: (0, j)),
                  pl.BlockSpec((1, tn), lambda j: (0, j))],
        out_specs=pl.BlockSpec((1, tn), lambda j: (0, j)),
        out_shape=jax.ShapeDtypeStruct((1, n), F32),
        compiler_params=_params(("arbitrary",)),
        name="adaln",
    )(c_b, w_ada, b_ada.reshape(1, n))


def _norm_mm_kernel(x_ref, g_ref, w_ref, o_ref, h_scr, *, rc):
    tm = x_ref.shape[0]

    @pl.when(pl.program_id(1) == 0)
    def _():
        def body(r):
            x = x_ref[pl.ds(r, rc), :].astype(F32)
            y = x * lax.rsqrt(jnp.mean(x * x, axis=-1, keepdims=True) + NORM_EPS)
            h_scr[pl.ds(r, rc), :] = (y * g_ref[...]).astype(BF16)
        _row_loop(tm, rc, body)

    o_ref[...] = jnp.dot(h_scr[...], w_ref[...], preferred_element_type=F32).astype(o_ref.dtype)


def _norm_mm(x, g, w, *, tm, tn):
    m, k = x.shape
    n = w.shape[1]
    tm = min(tm, m)
    return pl.pallas_call(
        functools.partial(_norm_mm_kernel, rc=32),
        grid=(m // tm, n // tn),
        in_specs=[pl.BlockSpec((tm, k), lambda i, j: (i, 0)),
                  pl.BlockSpec((1, k), lambda i, j: (0, 0)),
                  pl.BlockSpec((k, tn), lambda i, j: (0, j))],
        out_specs=pl.BlockSpec((tm, tn), lambda i, j: (i, j)),
        out_shape=jax.ShapeDtypeStruct((m, n), BF16),
        scratch_shapes=[pltpu.VMEM((tm, k), BF16)],
        compiler_params=_params(("parallel", "arbitrary")),
        name="norm_mm",
    )(x, g.reshape(1, k), w)


def _prenorm_kernel(x_ref, gs_ref, sh_ref, o_ref, *, rc):
    def body(r):
        x = x_ref[pl.ds(r, rc), :]
        y = x * lax.rsqrt(jnp.mean(x * x, axis=-1, keepdims=True) + NORM_EPS)
        o_ref[pl.ds(r, rc), :] = (y * gs_ref[...] + sh_ref[...]).astype(o_ref.dtype)
    _row_loop(x_ref.shape[0], rc, body)


def _prenorm(x, gs, sh, *, tm):
    m, k = x.shape
    tm = min(tm, m)
    return pl.pallas_call(
        functools.partial(_prenorm_kernel, rc=32),
        grid=(m // tm,),
        in_specs=[pl.BlockSpec((tm, k), lambda i: (i, 0)),
                  pl.BlockSpec((1, k), lambda i: (0, 0)),
                  pl.BlockSpec((1, k), lambda i: (0, 0))],
        out_specs=pl.BlockSpec((tm, k), lambda i: (i, 0)),
        out_shape=jax.ShapeDtypeStruct((m, k), BF16),
        compiler_params=_params(("parallel",)),
        name="prenorm",
    )(x, gs.reshape(1, k), sh.reshape(1, k))


def _mm_tail_kernel(a_ref, wt_ref, tailt_ref, o_ref, t_ref):
    nt_dims = (((1,), (1,)), ((), ()))
    o_ref[...] = lax.dot_general(a_ref[...], wt_ref[...], nt_dims, preferred_element_type=F32).astype(o_ref.dtype)

    @pl.when(pl.program_id(1) == pl.num_programs(1) - 1)
    def _():
        t_ref[...] = lax.dot_general(a_ref[...], tailt_ref[...], nt_dims, preferred_element_type=F32)


def _mm_tail(a, w_t, tail_t, *, tm, tn):
    m, k = a.shape
    n, nt = w_t.shape[0], tail_t.shape[0]
    tm = min(tm, m)
    return pl.pallas_call(
        _mm_tail_kernel,
        grid=(m // tm, pl.cdiv(n, tn)),
        in_specs=[pl.BlockSpec((tm, k), lambda i, j: (i, 0)),
                  pl.BlockSpec((tn, k), lambda i, j: (j, 0)),
                  pl.BlockSpec((nt, k), lambda i, j: (0, 0))],
        out_specs=[pl.BlockSpec((tm, tn), lambda i, j: (i, j)),
                   pl.BlockSpec((tm, nt), lambda i, j: (i, 0))],
        out_shape=[jax.ShapeDtypeStruct((m, n), BF16), jax.ShapeDtypeStruct((m, nt), F32)],
        compiler_params=_params(("parallel", "arbitrary"), vmem=60 * 1024 * 1024),
        name="in_proj",
    )(a, w_t, tail_t)


def _mm2_kernel(a1_ref, a2_ref, w_ref, o_ref):
    k1 = a1_ref.shape[1]
    acc = jnp.dot(a1_ref[...], w_ref[:k1, :], preferred_element_type=F32)
    acc = acc + jnp.dot(a2_ref[...], w_ref[k1:, :], preferred_element_type=F32)
    o_ref[...] = acc.astype(o_ref.dtype)


def _mm2(a1, a2, w, *, tm, tn):
    m, k1 = a1.shape
    k2 = a2.shape[1]
    n = w.shape[1]
    tm = min(tm, m)
    return pl.pallas_call(
        _mm2_kernel,
        grid=(m // tm, n // tn),
        in_specs=[pl.BlockSpec((tm, k1), lambda i, j: (i, 0)),
                  pl.BlockSpec((tm, k2), lambda i, j: (i, 0)),
                  pl.BlockSpec((k1 + k2, tn), lambda i, j: (0, j))],
        out_specs=pl.BlockSpec((tm, tn), lambda i, j: (i, j)),
        out_shape=jax.ShapeDtypeStruct((m, n), BF16),
        compiler_params=_params(("parallel", "arbitrary")),
        name="out_proj",
    )(a1, a2, w)


def _softcap(z):
    return GATE_SOFTCAP * jnp.tanh(z * (1.0 / GATE_SOFTCAP))


def _log_sigmoid(z):
    return jnp.minimum(z, 0.0) - jnp.log1p(jnp.exp(-jnp.abs(z)))


def _mlstm_kernel(qk_ref, prev_ref, v_ref, o_ref, gcol_ref, grow_ref, cw_ref, cb_ref, bcol_ref, brow_ref,
                  gout_ref, out_ref, c_scr, n_scr, m_scr, u_scr):
    c = pl.program_id(0)
    L = qk_ref.shape[0]
    halo = prev_ref.shape[0]
    dk, dv = M_QK_DIM, M_V_DIM

    @pl.when(c == 0)
    def _():
        c_scr[...] = jnp.zeros_like(c_scr)
        n_scr[...] = jnp.zeros_like(n_scr)
        m_scr[...] = jnp.zeros_like(m_scr)

    prev = prev_ref[...].astype(F32)
    u_scr[0:halo, :] = jnp.where(c == 0, jnp.zeros_like(prev), prev)
    u_scr[halo:halo + L, :] = qk_ref[...].astype(F32)

    def conv_silu(col, width):
        acc = cb_ref[:, col:col + width]
        for j in range(CONV_WIDTH):
            r0 = halo - (CONV_WIDTH - 1) + j
            acc = acc + u_scr[r0:r0 + L, col:col + width] * cw_ref[j:j + 1, col:col + width]
        return acc * jax.nn.sigmoid(acc)

    pre_c = _softcap(gcol_ref[...] + bcol_ref[...])
    lf_c = _log_sigmoid(pre_c)
    pre_r = _softcap(grow_ref[0] + brow_ref[...])
    lf_r = _log_sigmoid(pre_r)
    row = lax.broadcasted_iota(I32, (L, L), 0)
    col = lax.broadcasted_iota(I32, (L, L), 1)
    causal = col <= row
    tril = causal.astype(F32)
    triu = (row <= col).astype(F32)
    bcum_c = jnp.dot(tril, lf_c, preferred_element_type=F32, precision=lax.Precision.HIGHEST)
    bcum_r = jnp.dot(lf_r, triu, preferred_element_type=F32, precision=lax.Precision.HIGHEST)

    for h in range(M_HEADS):
        li_lane = TAIL_GATE_LANE + h
        lf_lane = TAIL_GATE_LANE + M_HEADS + h
        q = (conv_silu(h * dk, dk) * (dk ** -0.5)).astype(BF16)
        kf = conv_silu(M_HEADS * dk + h * dk, dk)
        kb = kf.astype(BF16)
        v = v_ref[:, h * dv:(h + 1) * dv]
        b_c = bcum_c[:, lf_lane:lf_lane + 1]
        li_c = pre_c[:, li_lane:li_lane + 1]
        b_r = bcum_r[M_HEADS + h:M_HEADS + h + 1, :]
        li_r = pre_r[h:h + 1, :]
        m_prev = m_scr[h:h + 1, 0:1]

        dm = jnp.where(causal, b_c - b_r + li_r, NEG_BIG)
        inter = b_c + m_prev
        m_t = jnp.maximum(jnp.max(dm, axis=1, keepdims=True), inter)
        decay = jnp.exp(inter - m_t)
        s = lax.dot_general(q, kb, (((1,), (1,)), ((), ())), preferred_element_type=F32) * jnp.exp(dm - m_t)
        c_state = c_scr[h]
        n_state = n_scr[h:h + 1, :]
        num = jnp.dot(s.astype(BF16), v, preferred_element_type=F32)
        num = num + decay * jnp.dot(q, c_state.astype(BF16), preferred_element_type=F32)
        den = jnp.sum(s, axis=1, keepdims=True) + decay * jnp.sum(q.astype(F32) * n_state, axis=1, keepdims=True)
        hh = num / jnp.maximum(jnp.abs(den), jnp.exp(-m_t))

        b_last = b_c[L - 1:L, :]
        m_new = jnp.maximum(b_last + m_prev, jnp.max(b_last - b_r + li_r, axis=1, keepdims=True))
        carry = jnp.exp(b_last + m_prev - m_new)
        w_c = jnp.exp(b_last - b_c + li_c - m_new)
        kw = kf * w_c
        c_scr[h] = carry * c_state + lax.dot_general(kw.astype(BF16), v, (((0,), (0,)), ((), ())),
                                                     preferred_element_type=F32)
        n_scr[h:h + 1, :] = carry * n_state + jnp.sum(kw, axis=0, keepdims=True)
        m_scr[h:h + 1, :] = jnp.broadcast_to(m_new, (1, LANES))

        hn = hh * lax.rsqrt(jnp.mean(hh * hh, axis=1, keepdims=True) + NORM_EPS) * gout_ref[:, h * dv:(h + 1) * dv]
        gate = jax.nn.sigmoid(o_ref[:, h * dv:(h + 1) * dv].astype(F32))
        out_ref[:, h * dv:(h + 1) * dv] = (hn * gate).astype(out_ref.dtype)


def _mlstm(proj, tail, conv_w, conv_b, b_igate, b_fgate, g_out, *, chunk):
    s_len = proj.shape[0]
    L = chunk
    n_c = s_len // L
    halo = 16
    gates = tail[:, TAIL_GATE_LANE:TAIL_GATE_LANE + 2 * M_HEADS]
    g_row = gates.reshape(n_c, L, 2 * M_HEADS).transpose(0, 2, 1)
    bias = jnp.concatenate([b_igate, b_fgate]).astype(F32)
    b_col = jnp.zeros((1, LANES), F32).at[0, TAIL_GATE_LANE:TAIL_GATE_LANE + 2 * M_HEADS].set(bias)
    b_row = bias.reshape(2 * M_HEADS, 1)
    w = M_QK_WIDTH
    return pl.pallas_call(
        _mlstm_kernel,
        grid=(n_c,),
        in_specs=[pl.BlockSpec((L, w), lambda c: (c, COL_QK // w)),
                  pl.BlockSpec((halo, w), lambda c: (jnp.maximum(c * (L // halo) - 1, 0), COL_QK // w)),
                  pl.BlockSpec((L, M_WIDTH), lambda c: (c, COL_V // M_WIDTH)),
                  pl.BlockSpec((L, M_WIDTH), lambda c: (c, COL_O // M_WIDTH)),
                  pl.BlockSpec((L, LANES), lambda c: (c, 0)),
                  pl.BlockSpec((1, 2 * M_HEADS, L), lambda c: (c, 0, 0)),
                  pl.BlockSpec((CONV_WIDTH, w), lambda c: (0, 0)),
                  pl.BlockSpec((1, w), lambda c: (0, 0)),
                  pl.BlockSpec((1, LANES), lambda c: (0, 0)),
                  pl.BlockSpec((2 * M_HEADS, 1), lambda c: (0, 0)),
                  pl.BlockSpec((1, M_WIDTH), lambda c: (0, 0))],
        out_specs=pl.BlockSpec((L, M_WIDTH), lambda c: (c, 0)),
        out_shape=jax.ShapeDtypeStruct((s_len, M_WIDTH), BF16),
        scratch_shapes=[pltpu.VMEM((M_HEADS, M_QK_DIM, M_V_DIM), F32),
                        pltpu.VMEM((8, M_QK_DIM), F32),
                        pltpu.VMEM((8, LANES), F32),
                        pltpu.VMEM((halo + L, w), F32)],
        compiler_params=_params(("arbitrary",)),
        name="mlstm",
    )(proj, proj, proj, proj, tail, g_row, conv_w, conv_b.reshape(1, w), b_col, b_row, g_out.reshape(1, M_WIDTH))


def _rope_kernel(q_ref, kn_ref, v_ref, tail_ref, cos_ref, sin_ref, qo_ref, ko_ref, vo_ref):
    cos = cos_ref[...]
    sin = sin_ref[...]
    nope_w = A_HEADS * A_NOPE_DIM
    kpe = (tail_ref[:, LANES:2 * LANES] * cos + tail_ref[:, 0:LANES] * sin).astype(BF16)
    lane = lax.broadcasted_iota(I32, (q_ref.shape[0], LANES), 1)
    ones_col = jnp.where(lane == 0, 1.0, 0.0).astype(BF16)
    for h in range(A_HEADS):
        vo_ref[:, h * A_HEAD_PAD:h * A_HEAD_PAD + LANES] = v_ref[:, h * LANES:(h + 1) * LANES]
        vo_ref[:, h * A_HEAD_PAD + LANES:(h + 1) * A_HEAD_PAD] = ones_col
        lo = h * A_HEAD_PAD
        qo_ref[:, lo:lo + LANES] = q_ref[:, h * LANES:(h + 1) * LANES]
        qr = q_ref[:, nope_w + h * LANES:nope_w + (h + 1) * LANES].astype(F32)
        qs = q_ref[:, 2 * nope_w + h * LANES:2 * nope_w + (h + 1) * LANES].astype(F32)
        qo_ref[:, lo + LANES:lo + 2 * LANES] = (qr * cos + qs * sin).astype(BF16)
        ko_ref[:, lo:lo + LANES] = kn_ref[:, h * LANES:(h + 1) * LANES]
        ko_ref[:, lo + LANES:lo + 2 * LANES] = kpe


def _rope_assemble(q_raw, kv_raw, tail, cos_t, sin_t, *, tm):
    s_len = q_raw.shape[0]
    tm = min(tm, s_len)
    nope_w = A_HEADS * A_NOPE_DIM
    wide = A_HEADS * A_HEAD_PAD
    return pl.pallas_call(
        _rope_kernel,
        grid=(s_len // tm,),
        in_specs=[pl.BlockSpec((tm, 3 * nope_w), lambda i: (i, 0)),
                  pl.BlockSpec((tm, nope_w), lambda i: (i, 0)),
                  pl.BlockSpec((tm, nope_w), lambda i: (i, 1)),
                  pl.BlockSpec((tm, 2 * LANES), lambda i: (i, 0)),
                  pl.BlockSpec((tm, LANES), lambda i: (i, 0)),
                  pl.BlockSpec((tm, LANES), lambda i: (i, 0))],
        out_specs=[pl.BlockSpec((tm, wide), lambda i: (i, 0))] * 3,
        out_shape=[jax.ShapeDtypeStruct((s_len, wide), BF16)] * 3,
        compiler_params=_params(("parallel",)),
        name="rope_assemble",
    )(q_raw, kv_raw, kv_raw, tail, cos_t, sin_t)


def _attn_kernel(q_ref, k_ref, v_ref, o_ref, m_scr, acc_scr, s_scr, *, chunk):
    qi = pl.program_id(1)
    tq = q_ref.shape[0]
    m_scr[...] = jnp.full_like(m_scr, NEG_BIG)
    acc_scr[...] = jnp.zeros_like(acc_scr)

    def scores(slot, blk):
        start = pl.multiple_of(blk * tq, tq)
        s_scr[slot] = lax.dot_general(q_ref[...], k_ref[pl.ds(start, tq), :], (((1,), (1,)), ((), ())),
                                      preferred_element_type=F32)

    def consume(slot, blk, masked):
        start = pl.multiple_of(blk * tq, tq)
        s = s_scr[slot]
        if masked:
            rq = lax.broadcasted_iota(I32, (tq, tq), 0) // chunk
            ck = lax.broadcasted_iota(I32, (tq, tq), 1) // chunk
            s = jnp.where(ck <= rq, s, NEG_BIG)
        m_prev = m_scr[...]
        m_new = jnp.maximum(m_prev, jnp.max(s, axis=1, keepdims=True))
        pr = jnp.exp(s - m_new).astype(BF16)
        acc_scr[...] = (jnp.exp(m_prev - m_new) * acc_scr[...]
                        + jnp.dot(pr, v_ref[pl.ds(start, tq), :], preferred_element_type=F32))
        m_scr[...] = m_new

    scores(0, 0)

    def pair(t, carry):
        scores(1, 2 * t + 1)
        consume(0, 2 * t, False)
        scores(0, 2 * t + 2)
        consume(1, 2 * t + 1, False)
        return carry
    lax.fori_loop(0, qi // 2, pair, 0)

    @pl.when(qi % 2 == 1)
    def _():
        scores(1, qi)
        consume(0, qi - 1, False)
        consume(1, qi, True)

    @pl.when(qi % 2 == 0)
    def _():
        consume(0, qi, True)

    acc = acc_scr[...]
    o_ref[...] = (acc[:, :A_V_DIM] / acc[:, A_V_DIM:A_V_DIM + 1]).astype(o_ref.dtype)


def _attention(q_cat, k_cat, v_cat, *, tq, chunk):
    s_len = q_cat.shape[0]
    tq = min(tq, s_len)
    return pl.pallas_call(
        functools.partial(_attn_kernel, chunk=chunk),
        grid=(A_HEADS, s_len // tq),
        in_specs=[pl.BlockSpec((tq, A_HEAD_PAD), lambda h, i: (i, h)),
                  pl.BlockSpec((s_len, A_HEAD_PAD), lambda h, i: (0, h)),
                  pl.BlockSpec((s_len, A_HEAD_PAD), lambda h, i: (0, h))],
        out_specs=pl.BlockSpec((tq, A_V_DIM), lambda h, i: (i, h)),
        out_shape=jax.ShapeDtypeStruct((s_len, A_HEADS * A_V_DIM), BF16),
        scratch_shapes=[pltpu.VMEM((tq, 1), F32), pltpu.VMEM((tq, A_HEAD_PAD), F32),
                        pltpu.VMEM((2, tq, tq), F32)],
        compiler_params=_params(("parallel", "arbitrary")),
        name="mla_attention",
    )(q_cat, k_cat, v_cat)


def _route_t(logits, bias_col):
    n = logits.shape[1]
    scores = jax.nn.sigmoid(logits)
    biased = scores + bias_col
    sub = lax.broadcasted_iota(I32, (GROUP_SIZE, n), 0)
    rows = []
    for g in range(N_GROUPS):
        x = biased[g * GROUP_SIZE:(g + 1) * GROUP_SIZE, :]
        m1 = jnp.max(x, axis=0, keepdims=True)
        i1 = jnp.min(jnp.where(x == m1, sub, GROUP_SIZE), axis=0, keepdims=True)
        m2 = jnp.max(jnp.where(sub == i1, -jnp.inf, x), axis=0, keepdims=True)
        rows.append(m1 + m2)
    gscore = jnp.concatenate(rows, axis=0)
    gio = lax.broadcasted_iota(I32, (N_GROUPS, n), 0)
    grank = jnp.zeros((N_GROUPS, n), I32)
    for g in range(N_GROUPS):
        r = gscore[g:g + 1, :]
        grank = grank + jnp.where(gio > g, jnp.where(r >= gscore, 1, 0), jnp.where(r > gscore, 1, 0))
    gsel = grank < TOPK_GROUPS
    masked = jnp.concatenate(
        [jnp.where(gsel[g:g + 1, :], biased[g * GROUP_SIZE:(g + 1) * GROUP_SIZE, :], -jnp.inf)
         for g in range(N_GROUPS)], axis=0)
    eio = lax.broadcasted_iota(I32, (N_EXPERTS, n), 0)
    rank = jnp.zeros((N_EXPERTS, n), I32)
    for e in range(N_EXPERTS):
        r = masked[e:e + 1, :]
        rank = rank + jnp.where(eio > e, jnp.where(r >= masked, 1, 0), jnp.where(r > masked, 1, 0))
    sel = rank < TOP_K
    denom = jnp.sum(jnp.where(sel, scores, 0.0), axis=0, keepdims=True)
    wnorm = scores / denom * ROUTED_SCALE
    eio_f = eio.astype(F32)
    ids, wts = [], []
    for k in range(TOP_K):
        hit = rank == k
        ids.append(jnp.sum(jnp.where(hit, eio_f, 0.0), axis=0, keepdims=True))
        wts.append(jnp.sum(jnp.where(hit, wnorm, 0.0), axis=0, keepdims=True))
    return jnp.concatenate(ids, axis=0).astype(I32), jnp.concatenate(wts, axis=0)


def _mid_kernel(y_ref, x_ref, gt_ref, gpost_ref, gs_ref, sh_ref, wr_ref, br_ref,
                x1_ref, hp_ref, idx_ref, wts_ref, h_scr, *, rc):
    tm, d = x_ref.shape
    half = d // 2

    def body(r):
        y = y_ref[pl.ds(r, rc), :].astype(F32)
        yn = y * lax.rsqrt(jnp.mean(y * y, axis=-1, keepdims=True) + NORM_EPS) * gpost_ref[...]
        x1 = x_ref[pl.ds(r, rc), :] + gt_ref[...] * yn
        x1_ref[pl.ds(r, rc), :] = x1
        hn = x1 * lax.rsqrt(jnp.mean(x1 * x1, axis=-1, keepdims=True) + NORM_EPS)
        h = hn * gs_ref[...] + sh_ref[...]
        h_scr[pl.ds(r, rc), :] = h
        hp_ref[pl.ds(r, rc), :] = _pack_pair(h[:, :half], h[:, half:])
    _row_loop(tm, rc, body)

    logits = lax.dot_general(wr_ref[...], h_scr[...], (((1,), (1,)), ((), ())),
                             preferred_element_type=F32, precision=lax.Precision.HIGHEST)
    ids, wts = _route_t(logits, br_ref[...])
    idx_ref[...] = ids
    wts_ref[...] = wts


def _mid(y, x, gt1, g_post, gs2, sh2, w_router, b_router, *, tm):
    s_len, d = x.shape
    tm = min(tm, s_len)
    vec = lambda a: a.reshape(1, d).astype(F32)
    return pl.pallas_call(
        functools.partial(_mid_kernel, rc=16),
        grid=(s_len // tm,),
        in_specs=[pl.BlockSpec((tm, d), lambda i: (i, 0)),
                  pl.BlockSpec((tm, d), lambda i: (i, 0)),
                  pl.BlockSpec((1, d), lambda i: (0, 0)),
                  pl.BlockSpec((1, d), lambda i: (0, 0)),
                  pl.BlockSpec((1, d), lambda i: (0, 0)),
                  pl.BlockSpec((1, d), lambda i: (0, 0)),
                  pl.BlockSpec((N_EXPERTS, d), lambda i: (0, 0)),
                  pl.BlockSpec((N_EXPERTS, 1), lambda i: (0, 0))],
        out_specs=[pl.BlockSpec((tm, d), lambda i: (i, 0)),
                   pl.BlockSpec((tm, d // 2), lambda i: (i, 0)),
                   pl.BlockSpec((TOP_K, tm), lambda i: (0, i)),
                   pl.BlockSpec((TOP_K, tm), lambda i: (0, i))],
        out_shape=[jax.ShapeDtypeStruct((s_len, d), F32),
                   jax.ShapeDtypeStruct((s_len, d // 2), U32),
                   jax.ShapeDtypeStruct((TOP_K, s_len), I32),
                   jax.ShapeDtypeStruct((TOP_K, s_len), F32)],
        scratch_shapes=[pltpu.VMEM((tm, d), F32)],
        compiler_params=_params(("parallel",)),
        name="mid_norm_route",
    )(y, x, vec(gt1), vec(g_post), vec(gs2), vec(sh2), w_router.T.astype(F32), b_router.reshape(N_EXPERTS, 1))


def _row_copy(src_hbm, dst_vmem, sem, src_row, dst_row):
    return pltpu.make_async_copy(src_hbm.at[pl.ds(src_row, 1), :], dst_vmem.at[pl.ds(dst_row, 1), :], sem)


ROW_GROUP = 8
ROW_DMA_PRIORITY = 0
WEIGHT_DMA_PRIORITY = 1
GATHER_RING = 3


def _expert_kernel(be_ref, nb_ref, nxt_ref, tok_ref, h_hbm, wg_hbm, wu_hbm, wd_hbm, sg_hbm, su_hbm, sd_hbm,
                   o_ref, stage_g, stage_u, stage_d, wb_g, wb_u, wb_d, xbuf, sems, xsems, *, rc):
    b = pl.program_id(0)
    e = be_ref[b]
    tmb = xbuf.shape[1]
    ring = xbuf.shape[0]
    cur = lax.rem(b, ring)
    ahead = lax.rem(b + ring - 1, ring)
    last = nb_ref[0] - 1
    stages = (stage_g, stage_u, stage_d)
    routed = (wg_hbm, wu_hbm, wd_hbm)
    shared = (sg_hbm, su_hbm, sd_hbm)

    def gather_loop(blk, slot):
        base = blk * tmb

        def step(gi, carry):
            r0 = gi * ROW_GROUP
            for u in range(ROW_GROUP):
                _row_copy(h_hbm, xbuf.at[slot], xsems.at[slot], tok_ref[base + r0 + u], r0 + u).start(
                    priority=ROW_DMA_PRIORITY)
            return carry
        lax.fori_loop(0, tmb // ROW_GROUP, step, 0)

    def gather_unrolled(blk, slot, part, n_parts):
        base = blk * tmb
        for r in range(part * tmb // n_parts, (part + 1) * tmb // n_parts):
            _row_copy(h_hbm, xbuf.at[slot], xsems.at[slot], tok_ref[base + r], r).start(priority=ROW_DMA_PRIORITY)

    def gather_wait(slot):
        pltpu.make_async_copy(h_hbm.at[pl.ds(0, tmb), :], xbuf.at[slot], xsems.at[slot]).wait()

    def fetch(ex):
        @pl.when(ex < N_EXPERTS)
        def _():
            for i in range(3):
                pltpu.make_async_copy(routed[i].at[ex], stages[i], sems.at[i]).start(priority=WEIGHT_DMA_PRIORITY)

        @pl.when(ex == N_EXPERTS)
        def _():
            for i in range(3):
                pltpu.make_async_copy(shared[i], stages[i], sems.at[i]).start(priority=WEIGHT_DMA_PRIORITY)

    @pl.when(b < nb_ref[0])
    def _():
        @pl.when(b == 0)
        def _():
            fetch(e)
            for s in range(ring - 1):
                gather_loop(jnp.minimum(s, last), s)

        is_first = jnp.logical_or(b == 0, be_ref[jnp.maximum(b - 1, 0)] != e)

        @pl.when(is_first)
        def _():
            for i in range(3):
                pltpu.make_async_copy(shared[i], stages[i], sems.at[i]).wait()
            for src, dst in ((stage_g, wb_g), (stage_u, wb_u), (stage_d, wb_d)):
                def cast(r, src=src, dst=dst):
                    dst[pl.ds(r, rc), :] = src[pl.ds(r, rc), :].astype(BF16)
                _row_loop(src.shape[0], rc, cast)
            nxt = nxt_ref[e]

            @pl.when(nxt >= 0)
            def _():
                fetch(nxt)

        gather_wait(cur)
        nxt_blk = jnp.minimum(b + ring - 1, last)
        lo, hi = _unpack_pair(xbuf[cur])
        x = jnp.concatenate([lo, hi], axis=1).astype(BF16)
        gather_unrolled(nxt_blk, ahead, 0, 3)
        g = jnp.dot(x, wb_g[...], preferred_element_type=F32)
        gather_unrolled(nxt_blk, ahead, 1, 3)
        u = jnp.dot(x, wb_u[...], preferred_element_type=F32)
        a = (g * jax.nn.sigmoid(g) * u).astype(BF16)
        gather_unrolled(nxt_blk, ahead, 2, 3)
        y = jnp.dot(a, wb_d[...], preferred_element_type=F32)
        half = y.shape[1] // 2
        o_ref[...] = _pack_pair(y[:, :half], y[:, half:])

        @pl.when(b == last)
        def _():
            for s in range(1, ring):
                gather_wait(lax.rem(b + s, ring))

    @pl.when(b >= nb_ref[0])
    def _():
        o_ref[...] = jnp.zeros_like(o_ref)


def _experts(block_e, n_used, next_e, slot_tok, hp, wg, wu, wd, sg, su, sd, *, tmb):
    n_slots = slot_tok.shape[0]
    wp = hp.shape[1]
    d, ff = wg.shape[1], wg.shape[2]
    nb = n_slots // tmb
    grid_spec = pltpu.PrefetchScalarGridSpec(
        num_scalar_prefetch=4,
        grid=(nb,),
        in_specs=[pl.BlockSpec(memory_space=pl.ANY)] * 7,
        out_specs=pl.BlockSpec((tmb, wp), lambda b, *_: (b, 0)),
        scratch_shapes=[pltpu.VMEM((d, ff), F32), pltpu.VMEM((d, ff), F32), pltpu.VMEM((ff, d), F32),
                        pltpu.VMEM((d, ff), BF16), pltpu.VMEM((d, ff), BF16), pltpu.VMEM((ff, d), BF16),
                        pltpu.VMEM((GATHER_RING, tmb, wp), U32),
                        pltpu.SemaphoreType.DMA((3,)), pltpu.SemaphoreType.DMA((GATHER_RING,))],
    )
    return pl.pallas_call(
        functools.partial(_expert_kernel, rc=128),
        grid_spec=grid_spec,
        out_shape=jax.ShapeDtypeStruct((n_slots, wp), U32),
        compiler_params=_params(("arbitrary",), vmem=60 * 1024 * 1024),
        name="moe_experts",
    )(block_e, n_used, next_e, slot_tok, hp, wg, wu, wd, sg, su, sd)


def _combine_kernel(slot_ref, w_ref, x1_ref, gt_ref, g_ref, ys_hbm, o_ref, buf, sems, *, rc, n_k):
    tc, d = x1_ref.shape
    half = d // 2
    i = pl.program_id(0)
    cur = lax.rem(i, 2)
    last = pl.num_programs(0) - 1

    def issue_plane(tile, slot, t0, k):
        base = tile * (tc * n_k)
        for t in range(rc):
            _row_copy(ys_hbm, buf.at[slot, k], sems.at[slot], slot_ref[base + (t0 + t) * n_k + k],
                      t0 + t).start(priority=k % 2)

    def issue_rows(tile, slot, t0):
        for k in range(n_k):
            issue_plane(tile, slot, t0, k)

    def wait_planes(slot):
        for k in range(n_k):
            pltpu.make_async_copy(ys_hbm.at[pl.ds(0, tc), :], buf.at[slot, k], sems.at[slot]).wait()

    @pl.when(i == 0)
    def _():
        _row_loop(tc, rc, lambda r: issue_rows(0, 0, r))

    wait_planes(cur)
    nxt = jnp.minimum(i + 1, last)

    def body(r):
        w = w_ref[pl.ds(r, rc), :]
        lo = jnp.zeros((rc, half), F32)
        hi = jnp.zeros((rc, half), F32)
        for k in range(n_k):
            issue_plane(nxt, 1 - cur, r, k)
            a, b = _unpack_pair(buf[cur, k, pl.ds(r, rc), :])
            lo = lo + w[:, k:k + 1] * a
            hi = hi + w[:, k:k + 1] * b
        ms = (jnp.sum(lo * lo, axis=-1, keepdims=True) + jnp.sum(hi * hi, axis=-1, keepdims=True)) * (1.0 / d)
        rs = lax.rsqrt(ms + NORM_EPS)
        o_ref[pl.ds(r, rc), 0:half] = (x1_ref[pl.ds(r, rc), 0:half]
                                       + gt_ref[:, 0:half] * (lo * rs * g_ref[:, 0:half]))
        o_ref[pl.ds(r, rc), half:d] = (x1_ref[pl.ds(r, rc), half:d]
                                       + gt_ref[:, half:d] * (hi * rs * g_ref[:, half:d]))
    _row_loop(tc, rc, body)

    @pl.when(i == last)
    def _():
        wait_planes(1 - cur)


def _combine(slots, w_tok, x1, gt2, g_post, ys, *, tc):
    s_len, d = x1.shape
    n_k = slots.shape[0] // s_len
    tc = min(tc, s_len)
    grid_spec = pltpu.PrefetchScalarGridSpec(
        num_scalar_prefetch=1,
        grid=(s_len // tc,),
        in_specs=[pl.BlockSpec((tc, w_tok.shape[1]), lambda i, sl: (i, 0)),
                  pl.BlockSpec((tc, d), lambda i, sl: (i, 0)),
                  pl.BlockSpec((1, d), lambda i, sl: (0, 0)),
                  pl.BlockSpec((1, d), lambda i, sl: (0, 0)),
                  pl.BlockSpec(memory_space=pl.ANY)],
        out_specs=pl.BlockSpec((tc, d), lambda i, sl: (i, 0)),
        scratch_shapes=[pltpu.VMEM((2, n_k, tc, d // 2), U32), pltpu.SemaphoreType.DMA((2,))],
    )
    return pl.pallas_call(
        functools.partial(_combine_kernel, rc=8, n_k=n_k),
        grid_spec=grid_spec,
        out_shape=jax.ShapeDtypeStruct((s_len, d), F32),
        compiler_params=_params(("arbitrary",)),
        name="moe_combine",
    )(slots, w_tok, x1, gt2.reshape(1, d), g_post.reshape(1, d), ys)


def _invert_kernel(slots_ref, lo_ref, hi_ref, o_ref, *, n_k, n_tok, stride):
    n_slots = o_ref.shape[0]
    n_e = lo_ref.shape[0]
    mask = (1 << (n_tok.bit_length() - 1)) - 1

    def fill(lo, hi):
        def body(j, carry):
            o_ref[j] = (j * stride) & mask
            return carry
        lax.fori_loop(lo, hi, body, 0)

    def per_expert(e, carry):
        fill(lo_ref[e], hi_ref[e])
        return carry
    lax.fori_loop(0, n_e, per_expert, 0)
    fill(hi_ref[n_e - 1], n_slots)

    def put(t, carry):
        for k in range(n_k):
            o_ref[slots_ref[t * n_k + k]] = t
        return carry
    lax.fori_loop(0, n_tok, put, 0)


def _invert_slots(slots, pad_lo, pad_hi, *, n_slots, n_k, n_tok, stride):
    smem = pl.BlockSpec(memory_space=pltpu.SMEM)
    return pl.pallas_call(
        functools.partial(_invert_kernel, n_k=n_k, n_tok=n_tok, stride=stride),
        in_specs=[smem, smem, smem],
        out_specs=smem,
        out_shape=jax.ShapeDtypeStruct((n_slots,), I32),
        name="moe_invert_slots",
    )(slots, pad_lo, pad_hi)


def _spread_stride(n):
    m = int(n * 0.6180339887) | 1
    while math.gcd(m, n) != 1:
        m += 2
    return m


def _moe_plan(idx_t, wts_t, *, tmb):
    n_k, n_tok = idx_t.shape
    n_e = N_EXPERTS + 1
    eid = jnp.concatenate([idx_t, jnp.full((1, n_tok), N_EXPERTS, I32)], axis=0)
    wts = jnp.concatenate([wts_t, jnp.ones((1, n_tok), F32)], axis=0)
    picks = eid[:, None, :] == jnp.arange(n_e, dtype=I32)[None, :, None]
    sel = picks.any(axis=0).astype(I32)
    stride = _spread_stride(n_tok)
    visit = lax.rem(jnp.arange(n_tok, dtype=I32) * stride, n_tok)
    where = lax.rem(jnp.arange(n_tok, dtype=I32) * pow(stride, -1, n_tok), n_tok)
    csum = jnp.cumsum(sel[:, visit], axis=1)
    counts = csum[:, -1]
    padded = (counts + tmb - 1) // tmb * tmb
    ends = jnp.cumsum(padded)
    starts = ends - padded
    slot_dense = (starts[:, None] + csum - 1)[:, where]
    slot = jnp.sum(jnp.where(picks, slot_dense[None], 0), axis=1)
    n_slots = -(-(n_tok * (n_k + 1) + n_e * (tmb - 1)) // tmb) * tmb
    slots_tok_major = slot.T.reshape(-1)
    slot_tok = _invert_slots(slots_tok_major, (starts + counts).astype(I32), ends.astype(I32),
                             n_slots=n_slots, n_k=n_k + 1, n_tok=n_tok, stride=stride)
    n_used = (ends[-1] // tmb).astype(I32).reshape(1)
    block_start = jnp.arange(n_slots // tmb, dtype=I32) * tmb
    block_e = jnp.minimum(jnp.sum(ends[None, :] <= block_start[:, None], axis=1), n_e - 1).astype(I32)
    w_tok = jnp.zeros((n_tok, 16), F32).at[:, :n_k + 1].set(wts.T)
    owner = jnp.where(padded > 0, jnp.arange(n_e, dtype=I32), n_e)
    later = jnp.concatenate([lax.cummin(owner[::-1])[::-1][1:], jnp.full((1,), n_e, I32)])
    next_e = jnp.where(later >= n_e, -1, later).astype(I32)
    return slot_tok, block_e, n_used, next_e, slots_tok_major, w_tok


def _in_proj_weights(w_in):
    w_t = w_in.T
    d = w_t.shape[1]
    kr = w_t[COL_KR:COL_KR + A_ROPE_DIM]
    gates = w_t[COL_IG:COL_IG + 2 * M_HEADS]
    half = A_ROPE_DIM // 2
    z = lambda n: jnp.zeros((n, d), w_t.dtype)
    tail_t = jnp.concatenate([-kr[half:], kr[:half], gates, z(LANES - A_ROPE_DIM - 2 * M_HEADS),
                              kr, z(LANES - A_ROPE_DIM)], axis=0)
    return w_t.astype(BF16), tail_t.astype(BF16)


def _q_up_weight(w_uq):
    r = w_uq.shape[0]
    w = w_uq.reshape(r, A_HEADS, A_NOPE_DIM + A_ROPE_DIM)
    nope = w[:, :, :A_NOPE_DIM]
    rope = w[:, :, A_NOPE_DIM:]
    half = A_ROPE_DIM // 2
    swap = jnp.concatenate([-rope[:, :, half:], rope[:, :, :half]], axis=2)
    pad = jnp.zeros((r, A_HEADS, LANES - A_ROPE_DIM), w_uq.dtype)
    rope_p = jnp.concatenate([rope, pad], axis=2)
    swap_p = jnp.concatenate([swap, pad], axis=2)
    flat = lambda a: a.reshape(r, -1)
    return jnp.concatenate([flat(nope), flat(rope_p), flat(swap_p)], axis=1).astype(BF16)


def _kv_up_weight(w_ukv):
    r = w_ukv.shape[0]
    w = w_ukv.reshape(r, A_HEADS, A_NOPE_DIM + A_V_DIM)
    return jnp.concatenate([w[:, :, :A_NOPE_DIM].reshape(r, -1), w[:, :, A_NOPE_DIM:].reshape(r, -1)],
                           axis=1).astype(BF16)


def _rope_tables(s_len):
    pos = jnp.arange(s_len, dtype=F32)
    inv_freq = 1.0 / (ROPE_THETA ** (jnp.arange(0, A_ROPE_DIM, 2, dtype=F32) / A_ROPE_DIM))
    ang = pos[:, None] * inv_freq[None, :]
    pad = jnp.zeros((s_len, LANES - A_ROPE_DIM), F32)
    cos_t = jnp.concatenate([jnp.cos(ang), jnp.cos(ang), pad], axis=1)
    sin_t = jnp.concatenate([jnp.sin(ang), jnp.sin(ang), pad], axis=1)
    return cos_t, sin_t


def _block(x, c, w_ada, b_ada, g_pre_mix, g_post_mix, w_in, conv_w, conv_b, b_igate, b_fgate, g_mlstm_out,
           g_q_norm, w_uq, g_kv_norm, w_ukv, w_out, g_pre_ffn, g_post_ffn, w_router, b_router,
           w_gate, w_up, w_down, w_shared_gate, w_shared_up, w_shared_down):
    s_len, d = x.shape
    mod = _adaln(c, w_ada, b_ada)[0]
    sh1, sc1, gt1, sh2, sc2, gt2 = [mod[i * d:(i + 1) * d] for i in range(6)]

    w_main, w_tail = _in_proj_weights(w_in)
    h1 = _prenorm(x, g_pre_mix * (1.0 + sc1), sh1, tm=256)
    proj, tail = _mm_tail(h1, w_main, w_tail, tm=1024, tn=IN_TILE)
    h_m = _mlstm(proj, tail, conv_w, conv_b, b_igate, b_fgate, g_mlstm_out, chunk=min(M_CHUNK, s_len))
    scale = (A_NOPE_DIM + A_ROPE_DIM) ** -0.5
    q_raw = _norm_mm(proj[:, COL_CQ:COL_CQ + A_Q_RANK], g_q_norm * scale, _q_up_weight(w_uq), tm=1024, tn=1536)
    kv_raw = _norm_mm(proj[:, COL_CKV:COL_CKV + A_KV_RANK], g_kv_norm, _kv_up_weight(w_ukv), tm=1024, tn=1024)
    cos_t, sin_t = _rope_tables(s_len)
    q_cat, k_cat, v_cat = _rope_assemble(q_raw, kv_raw, tail, cos_t, sin_t, tm=256)
    h_a = _attention(q_cat, k_cat, v_cat, tq=1024, chunk=CHUNK)
    y = _mm2(h_m, h_a, w_out.astype(BF16), tm=1024, tn=1024)

    x1, hp, idx_t, wts_t = _mid(y, x, gt1, g_post_mix, g_pre_ffn * (1.0 + sc2), sh2, w_router, b_router, tm=256)
    tmb = 256
    slot_tok, block_e, n_used, next_e, slots, w_tok = _moe_plan(idx_t, wts_t, tmb=tmb)
    ys = _experts(block_e, n_used, next_e, slot_tok, hp, w_gate, w_up, w_down,
                  w_shared_gate, w_shared_up, w_shared_down, tmb=tmb)
    return _combine(slots, w_tok, x1, gt2, g_post_ffn, ys, tc=128)


def kernel(x, c, w_ada, b_ada, g_pre_mix, g_post_mix, w_in, conv_w, conv_b, b_igate, b_fgate, g_mlstm_out,
           g_q_norm, w_uq, g_kv_norm, w_ukv, w_out, g_pre_ffn, g_post_ffn, w_router, b_router,
           w_gate, w_up, w_down, w_shared_gate, w_shared_up, w_shared_down):
    assert x.shape[0] == 1 and w_ada.shape[0] == 1, "single sequence, single layer"
    layer = (w_ada, b_ada, g_pre_mix, g_post_mix, w_in, conv_w, conv_b, b_igate, b_fgate, g_mlstm_out,
             g_q_norm, w_uq, g_kv_norm, w_ukv, w_out, g_pre_ffn, g_post_ffn, w_router, b_router,
             w_gate, w_up, w_down, w_shared_gate, w_shared_up, w_shared_down)
    out = _block(x[0], c[0], *[p[0] for p in layer])
    return out[None]
```

```python
import functools
import math

import jax
import jax.numpy as jnp
from jax import lax
from jax.experimental import pallas as pl
from jax.experimental.pallas import tpu as pltpu

F32 = jnp.float32
BF16 = jnp.bfloat16
I32 = jnp.int32
U32 = jnp.uint32

NORM_EPS = 1e-6
CHUNK = 64
M_CHUNK = 128

M_HEADS = 4
M_QK_DIM = 256
M_V_DIM = 512
M_WIDTH = M_HEADS * M_V_DIM
M_QK_WIDTH = 2 * M_HEADS * M_QK_DIM
CONV_WIDTH = 4
GATE_SOFTCAP = 15.0

A_HEADS = 16
A_NOPE_DIM = 128
A_ROPE_DIM = 64
A_V_DIM = 128
A_Q_RANK = 768
A_KV_RANK = 512
A_HEAD_PAD = 256
ROPE_THETA = 10000.0

N_EXPERTS = 64
TOP_K = 8
N_GROUPS = 8
GROUP_SIZE = N_EXPERTS // N_GROUPS
TOPK_GROUPS = 4
ROUTED_SCALE = 2.5

LANES = 128
VMEM_LIMIT = 56 * 1024 * 1024
NEG_BIG = -1e30

COL_QK = 0
COL_V = COL_QK + M_QK_WIDTH
COL_O = COL_V + M_WIDTH
COL_IG = COL_O + M_WIDTH
COL_FG = COL_IG + M_HEADS
COL_CQ = COL_FG + M_HEADS
COL_CKV = COL_CQ + A_Q_RANK
COL_KR = COL_CKV + A_KV_RANK
N_IN = COL_KR + A_ROPE_DIM
IN_TILE = 1536
TAIL_W = 2 * LANES
TAIL_GATE_LANE = A_ROPE_DIM


def _params(sem, vmem=VMEM_LIMIT):
    return pltpu.CompilerParams(dimension_semantics=sem, vmem_limit_bytes=vmem)


def _row_loop(n_rows, rc, body):
    def step(i, carry):
        body(pl.multiple_of(i * rc, rc))
        return carry
    lax.fori_loop(0, n_rows // rc, step, 0)


def _pack_pair(a, b):
    lo = lax.bitcast_convert_type(a.astype(BF16).astype(F32), U32) >> 16
    hi = lax.bitcast_convert_type(b.astype(BF16).astype(F32), U32) & jnp.uint32(0xFFFF0000)
    return lo | hi


def _unpack_pair(u):
    lo = lax.bitcast_convert_type(u << 16, F32)
    hi = lax.bitcast_convert_type(u & jnp.uint32(0xFFFF0000), F32)
    return lo, hi


def _adaln_kernel(c_ref, w_ref, b_ref, o_ref, *, rc):
    d, tn = w_ref.shape
    nl = tn // LANES

    def step(i, accs):
        r = pl.multiple_of(i * rc, rc)
        c = c_ref[pl.ds(r, rc), :]
        ca = c * jax.nn.sigmoid(c)
        out = []
        for j in range(nl):
            prod = w_ref[pl.ds(r, rc), j * LANES:(j + 1) * LANES] * ca
            out.append(accs[j] + jnp.sum(prod.reshape(rc // 8, 8, LANES), axis=0))
        return tuple(out)

    accs = lax.fori_loop(0, d // rc, step, tuple(jnp.zeros((8, LANES), F32) for _ in range(nl)))
    for j in range(nl):
        o_ref[:, j * LANES:(j + 1) * LANES] = (
            jnp.sum(accs[j], axis=0, keepdims=True) + b_ref[:, j * LANES:(j + 1) * LANES])


def _adaln(c, w_ada, b_ada):
    d, n = w_ada.shape
    tn = 512
    c_b = jnp.broadcast_to(c.reshape(d, 1), (d, LANES))
    return pl.pallas_call(
        functools.partial(_adaln_kernel, rc=64),
        grid=(n // tn,),
        in_specs=[pl.BlockSpec((d, LANES), lambda j: (0, 0)),
                  pl.BlockSpec((d, tn), lambda j: (0, j)),
                  pl.BlockSpec((1, tn), lambda j: (0, j))],
        out_specs=pl.BlockSpec((1, tn), lambda j: (0, j)),
        out_shape=jax.ShapeDtypeStruct((1, n), F32),
        compiler_params=_params(("arbitrary",)),
        name="adaln",
    )(c_b, w_ada, b_ada.reshape(1, n))


def _norm_mm_kernel(x_ref, g_ref, w_ref, o_ref, h_scr, *, rc):
    tm = x_ref.shape[0]

    @pl.when(pl.program_id(1) == 0)
    def _():
        def body(r):
            x = x_ref[pl.ds(r, rc), :].astype(F32)
            y = x * lax.rsqrt(jnp.mean(x * x, axis=-1, keepdims=True) + NORM_EPS)
            h_scr[pl.ds(r, rc), :] = (y * g_ref[...]).astype(BF16)
        _row_loop(tm, rc, body)

    o_ref[...] = jnp.dot(h_scr[...], w_ref[...], preferred_element_type=F32).astype(o_ref.dtype)


def _norm_mm(x, g, w, *, tm, tn):
    m, k = x.shape
    n = w.shape[1]
    tm = min(tm, m)
    return pl.pallas_call(
        functools.partial(_norm_mm_kernel, rc=32),
        grid=(m // tm, n // tn),
        in_specs=[pl.BlockSpec((tm, k), lambda i, j: (i, 0)),
                  pl.BlockSpec((1, k), lambda i, j: (0, 0)),
                  pl.BlockSpec((k, tn), lambda i, j: (0, j))],
        out_specs=pl.BlockSpec((tm, tn), lambda i, j: (i, j)),
        out_shape=jax.ShapeDtypeStruct((m, n), BF16),
        scratch_shapes=[pltpu.VMEM((tm, k), BF16)],
        compiler_params=_params(("parallel", "arbitrary")),
        name="norm_mm",
    )(x, g.reshape(1, k), w)


def _prenorm_kernel(x_ref, gs_ref, sh_ref, o_ref, *, rc):
    def body(r):
        x = x_ref[pl.ds(r, rc), :]
        y = x * lax.rsqrt(jnp.mean(x * x, axis=-1, keepdims=True) + NORM_EPS)
        o_ref[pl.ds(r, rc), :] = (y * gs_ref[...] + sh_ref[...]).astype(o_ref.dtype)
    _row_loop(x_ref.shape[0], rc, body)


def _prenorm(x, gs, sh, *, tm):
    m, k = x.shape
    tm = min(tm, m)
    return pl.pallas_call(
        functools.partial(_prenorm_kernel, rc=32),
        grid=(m // tm,),
        in_specs=[pl.BlockSpec((tm, k), lambda i: (i, 0)),
                  pl.BlockSpec((1, k), lambda i: (0, 0)),
                  pl.BlockSpec((1, k), lambda i: (0, 0))],
        out_specs=pl.BlockSpec((tm, k), lambda i: (i, 0)),
        out_shape=jax.ShapeDtypeStruct((m, k), BF16),
        compiler_params=_params(("parallel",)),
        name="prenorm",
    )(x, gs.reshape(1, k), sh.reshape(1, k))


def _mm_tail_kernel(a_ref, wt_ref, tailt_ref, o_ref, t_ref):
    nt_dims = (((1,), (1,)), ((), ()))
    o_ref[...] = lax.dot_general(a_ref[...], wt_ref[...], nt_dims, preferred_element_type=F32).astype(o_ref.dtype)

    @pl.when(pl.program_id(1) == pl.num_programs(1) - 1)
    def _():
        t_ref[...] = lax.dot_general(a_ref[...], tailt_ref[...], nt_dims, preferred_element_type=F32)


def _mm_tail(a, w_t, tail_t, *, tm, tn):
    m, k = a.shape
    n, nt = w_t.shape[0], tail_t.shape[0]
    tm = min(tm, m)
    return pl.pallas_call(
        _mm_tail_kernel,
        grid=(m // tm, pl.cdiv(n, tn)),
        in_specs=[pl.BlockSpec((tm, k), lambda i, j: (i, 0)),
                  pl.BlockSpec((tn, k), lambda i, j: (j, 0)),
                  pl.BlockSpec((nt, k), lambda i, j: (0, 0))],
        out_specs=[pl.BlockSpec((tm, tn), lambda i, j: (i, j)),
                   pl.BlockSpec((tm, nt), lambda i, j: (i, 0))],
        out_shape=[jax.ShapeDtypeStruct((m, n), BF16), jax.ShapeDtypeStruct((m, nt), F32)],
        compiler_params=_params(("parallel", "arbitrary"), vmem=60 * 1024 * 1024),
        name="in_proj",
    )(a, w_t, tail_t)


def _mm2_kernel(a1_ref, a2_ref, w_ref, o_ref):
    k1 = a1_ref.shape[1]
    acc = jnp.dot(a1_ref[...], w_ref[:k1, :], preferred_element_type=F32)
    acc = acc + jnp.dot(a2_ref[...], w_ref[k1:, :], preferred_element_type=F32)
    o_ref[...] = acc.astype(o_ref.dtype)


def _mm2(a1, a2, w, *, tm, tn):
    m, k1 = a1.shape
    k2 = a2.shape[1]
    n = w.shape[1]
    tm = min(tm, m)
    return pl.pallas_call(
        _mm2_kernel,
        grid=(m // tm, n // tn),
        in_specs=[pl.BlockSpec((tm, k1), lambda i, j: (i, 0)),
                  pl.BlockSpec((tm, k2), lambda i, j: (i, 0)),
                  pl.BlockSpec((k1 + k2, tn), lambda i, j: (0, j))],
        out_specs=pl.BlockSpec((tm, tn), lambda i, j: (i, j)),
        out_shape=jax.ShapeDtypeStruct((m, n), BF16),
        compiler_params=_params(("parallel", "arbitrary")),
        name="out_proj",
    )(a1, a2, w)


def _softcap(z):
    return GATE_SOFTCAP * jnp.tanh(z * (1.0 / GATE_SOFTCAP))


def _log_sigmoid(z):
    return jnp.minimum(z, 0.0) - jnp.log1p(jnp.exp(-jnp.abs(z)))


def _mlstm_kernel(qk_ref, prev_ref, v_ref, o_ref, gcol_ref, grow_ref, cw_ref, cb_ref, bcol_ref, brow_ref,
                  gout_ref, out_ref, c_scr, n_scr, m_scr, u_scr):
    c = pl.program_id(0)
    L = qk_ref.shape[0]
    halo = prev_ref.shape[0]
    dk, dv = M_QK_DIM, M_V_DIM

    @pl.when(c == 0)
    def _():
        c_scr[...] = jnp.zeros_like(c_scr)
        n_scr[...] = jnp.zeros_like(n_scr)
        m_scr[...] = jnp.zeros_like(m_scr)

    prev = prev_ref[...].astype(F32)
    u_scr[0:halo, :] = jnp.where(c == 0, jnp.zeros_like(prev), prev)
    u_scr[halo:halo + L, :] = qk_ref[...].astype(F32)

    def conv_silu(col, width):
        acc = cb_ref[:, col:col + width]
        for j in range(CONV_WIDTH):
            r0 = halo - (CONV_WIDTH - 1) + j
            acc = acc + u_scr[r0:r0 + L, col:col + width] * cw_ref[j:j + 1, col:col + width]
        return acc * jax.nn.sigmoid(acc)

    pre_c = _softcap(gcol_ref[...] + bcol_ref[...])
    lf_c = _log_sigmoid(pre_c)
    pre_r = _softcap(grow_ref[0] + brow_ref[...])
    lf_r = _log_sigmoid(pre_r)
    row = lax.broadcasted_iota(I32, (L, L), 0)
    col = lax.broadcasted_iota(I32, (L, L), 1)
    causal = col <= row
    tril = causal.astype(F32)
    triu = (row <= col).astype(F32)
    bcum_c = jnp.dot(tril, lf_c, preferred_element_type=F32, precision=lax.Precision.HIGHEST)
    bcum_r = jnp.dot(lf_r, triu, preferred_element_type=F32, precision=lax.Precision.HIGHEST)

    for h in range(M_HEADS):
        li_lane = TAIL_GATE_LANE + h
        lf_lane = TAIL_GATE_LANE + M_HEADS + h
        q = (conv_silu(h * dk, dk) * (dk ** -0.5)).astype(BF16)
        kf = conv_silu(M_HEADS * dk + h * dk, dk)
        kb = kf.astype(BF16)
        v = v_ref[:, h * dv:(h + 1) * dv]
        b_c = bcum_c[:, lf_lane:lf_lane + 1]
        li_c = pre_c[:, li_lane:li_lane + 1]
        b_r = bcum_r[M_HEADS + h:M_HEADS + h + 1, :]
        li_r = pre_r[h:h + 1, :]
        m_prev = m_scr[h:h + 1, 0:1]

        dm = jnp.where(causal, b_c - b_r + li_r, NEG_BIG)
        inter = b_c + m_prev
        m_t = jnp.maximum(jnp.max(dm, axis=1, keepdims=True), inter)
        decay = jnp.exp(inter - m_t)
        s = lax.dot_general(q, kb, (((1,), (1,)), ((), ())), preferred_element_type=F32) * jnp.exp(dm - m_t)
        c_state = c_scr[h]
        n_state = n_scr[h:h + 1, :]
        num = jnp.dot(s.astype(BF16), v, preferred_element_type=F32)
        num = num + decay * jnp.dot(q, c_state.astype(BF16), preferred_element_type=F32)
        den = jnp.sum(s, axis=1, keepdims=True) + decay * jnp.sum(q.astype(F32) * n_state, axis=1, keepdims=True)
        hh = num / jnp.maximum(jnp.abs(den), jnp.exp(-m_t))

        b_last = b_c[L - 1:L, :]
        m_new = jnp.maximum(b_last + m_prev, jnp.max(b_last - b_r + li_r, axis=1, keepdims=True))
        carry = jnp.exp(b_last + m_prev - m_new)
        w_c = jnp.exp(b_last - b_c + li_c - m_new)
        kw = kf * w_c
        c_scr[h] = carry * c_state + lax.dot_general(kw.astype(BF16), v, (((0,), (0,)), ((), ())),
                                                     preferred_element_type=F32)
        n_scr[h:h + 1, :] = carry * n_state + jnp.sum(kw, axis=0, keepdims=True)
        m_scr[h:h + 1, :] = jnp.broadcast_to(m_new, (1, LANES))

        hn = hh * lax.rsqrt(jnp.mean(hh * hh, axis=1, keepdims=True) + NORM_EPS) * gout_ref[:, h * dv:(h + 1) * dv]
        gate = jax.nn.sigmoid(o_ref[:, h * dv:(h + 1) * dv].astype(F32))
        out_ref[:, h * dv:(h + 1) * dv] = (hn * gate).astype(out_ref.dtype)


def _mlstm(proj, tail, conv_w, conv_b, b_igate, b_fgate, g_out, *, chunk):
    s_len = proj.shape[0]
    L = chunk
    n_c = s_len // L
    halo = 16
    gates = tail[:, TAIL_GATE_LANE:TAIL_GATE_LANE + 2 * M_HEADS]
    g_row = gates.reshape(n_c, L, 2 * M_HEADS).transpose(0, 2, 1)
    bias = jnp.concatenate([b_igate, b_fgate]).astype(F32)
    b_col = jnp.zeros((1, LANES), F32).at[0, TAIL_GATE_LANE:TAIL_GATE_LANE + 2 * M_HEADS].set(bias)
    b_row = bias.reshape(2 * M_HEADS, 1)
    w = M_QK_WIDTH
    return pl.pallas_call(
        _mlstm_kernel,
        grid=(n_c,),
        in_specs=[pl.BlockSpec((L, w), lambda c: (c, COL_QK // w)),
                  pl.BlockSpec((halo, w), lambda c: (jnp.maximum(c * (L // halo) - 1, 0), COL_QK // w)),
                  pl.BlockSpec((L, M_WIDTH), lambda c: (c, COL_V // M_WIDTH)),
                  pl.BlockSpec((L, M_WIDTH), lambda c: (c, COL_O // M_WIDTH)),
                  pl.BlockSpec((L, LANES), lambda c: (c, 0)),
                  pl.BlockSpec((1, 2 * M_HEADS, L), lambda c: (c, 0, 0)),
                  pl.BlockSpec((CONV_WIDTH, w), lambda c: (0, 0)),
                  pl.BlockSpec((1, w), lambda c: (0, 0)),
                  pl.BlockSpec((1, LANES), lambda c: (0, 0)),
                  pl.BlockSpec((2 * M_HEADS, 1), lambda c: (0, 0)),
                  pl.BlockSpec((1, M_WIDTH), lambda c: (0, 0))],
        out_specs=pl.BlockSpec((L, M_WIDTH), lambda c: (c, 0)),
        out_shape=jax.ShapeDtypeStruct((s_len, M_WIDTH), BF16),
        scratch_shapes=[pltpu.VMEM((M_HEADS, M_QK_DIM, M_V_DIM), F32),
                        pltpu.VMEM((8, M_QK_DIM), F32),
                        pltpu.VMEM((8, LANES), F32),
                        pltpu.VMEM((halo + L, w), F32)],
        compiler_params=_params(("arbitrary",)),
        name="mlstm",
    )(proj, proj, proj, proj, tail, g_row, conv_w, conv_b.reshape(1, w), b_col, b_row, g_out.reshape(1, M_WIDTH))


def _rope_kernel(q_ref, kn_ref, v_ref, tail_ref, cos_ref, sin_ref, qo_ref, ko_ref, vo_ref):
    cos = cos_ref[...]
    sin = sin_ref[...]
    nope_w = A_HEADS * A_NOPE_DIM
    kpe = (tail_ref[:, LANES:2 * LANES] * cos + tail_ref[:, 0:LANES] * sin).astype(BF16)
    lane = lax.broadcasted_iota(I32, (q_ref.shape[0], LANES), 1)
    ones_col = jnp.where(lane == 0, 1.0, 0.0).astype(BF16)
    for h in range(A_HEADS):
        vo_ref[:, h * A_HEAD_PAD:h * A_HEAD_PAD + LANES] = v_ref[:, h * LANES:(h + 1) * LANES]
        vo_ref[:, h * A_HEAD_PAD + LANES:(h + 1) * A_HEAD_PAD] = ones_col
        lo = h * A_HEAD_PAD
        qo_ref[:, lo:lo + LANES] = q_ref[:, h * LANES:(h + 1) * LANES]
        qr = q_ref[:, nope_w + h * LANES:nope_w + (h + 1) * LANES].astype(F32)
        qs = q_ref[:, 2 * nope_w + h * LANES:2 * nope_w + (h + 1) * LANES].astype(F32)
        qo_ref[:, lo + LANES:lo + 2 * LANES] = (qr * cos + qs * sin).astype(BF16)
        ko_ref[:, lo:lo + LANES] = kn_ref[:, h * LANES:(h + 1) * LANES]
        ko_ref[:, lo + LANES:lo + 2 * LANES] = kpe


def _rope_assemble(q_raw, kv_raw, tail, cos_t, sin_t, *, tm):
    s_len = q_raw.shape[0]
    tm = min(tm, s_len)
    nope_w = A_HEADS * A_NOPE_DIM
    wide = A_HEADS * A_HEAD_PAD
    return pl.pallas_call(
        _rope_kernel,
        grid=(s_len // tm,),
        in_specs=[pl.BlockSpec((tm, 3 * nope_w), lambda i: (i, 0)),
                  pl.BlockSpec((tm, nope_w), lambda i: (i, 0)),
                  pl.BlockSpec((tm, nope_w), lambda i: (i, 1)),
                  pl.BlockSpec((tm, 2 * LANES), lambda i: (i, 0)),
                  pl.BlockSpec((tm, LANES), lambda i: (i, 0)),
                  pl.BlockSpec((tm, LANES), lambda i: (i, 0))],
        out_specs=[pl.BlockSpec((tm, wide), lambda i: (i, 0))] * 3,
        out_shape=[jax.ShapeDtypeStruct((s_len, wide), BF16)] * 3,
        compiler_params=_params(("parallel",)),
        name="rope_assemble",
    )(q_raw, kv_raw, kv_raw, tail, cos_t, sin_t)


def _attn_kernel(q_ref, k_ref, v_ref, o_ref, m_scr, acc_scr, s_scr, *, chunk):
    qi = pl.program_id(1)
    tq = q_ref.shape[0]
    m_scr[...] = jnp.full_like(m_scr, NEG_BIG)
    acc_scr[...] = jnp.zeros_like(acc_scr)

    def scores(slot, blk):
        start = pl.multiple_of(blk * tq, tq)
        s_scr[slot] = lax.dot_general(q_ref[...], k_ref[pl.ds(start, tq), :], (((1,), (1,)), ((), ())),
                                      preferred_element_type=F32)

    def consume(slot, blk, masked):
        start = pl.multiple_of(blk * tq, tq)
        s = s_scr[slot]
        if masked:
            rq = lax.broadcasted_iota(I32, (tq, tq), 0) // chunk
            ck = lax.broadcasted_iota(I32, (tq, tq), 1) // chunk
            s = jnp.where(ck <= rq, s, NEG_BIG)
        m_prev = m_scr[...]
        m_new = jnp.maximum(m_prev, jnp.max(s, axis=1, keepdims=True))
        pr = jnp.exp(s - m_new).astype(BF16)
        acc_scr[...] = (jnp.exp(m_prev - m_new) * acc_scr[...]
                        + jnp.dot(pr, v_ref[pl.ds(start, tq), :], preferred_element_type=F32))
        m_scr[...] = m_new

    scores(0, 0)

    def pair(t, carry):
        scores(1, 2 * t + 1)
        consume(0, 2 * t, False)
        scores(0, 2 * t + 2)
        consume(1, 2 * t + 1, False)
        return carry
    lax.fori_loop(0, qi // 2, pair, 0)

    @pl.when(qi % 2 == 1)
    def _():
        scores(1, qi)
        consume(0, qi - 1, False)
        consume(1, qi, True)

    @pl.when(qi % 2 == 0)
    def _():
        consume(0, qi, True)

    acc = acc_scr[...]
    o_ref[...] = (acc[:, :A_V_DIM] / acc[:, A_V_DIM:A_V_DIM + 1]).astype(o_ref.dtype)


def _attention(q_cat, k_cat, v_cat, *, tq, chunk):
    s_len = q_cat.shape[0]
    tq = min(tq, s_len)
    return pl.pallas_call(
        functools.partial(_attn_kernel, chunk=chunk),
        grid=(A_HEADS, s_len // tq),
        in_specs=[pl.BlockSpec((tq, A_HEAD_PAD), lambda h, i: (i, h)),
                  pl.BlockSpec((s_len, A_HEAD_PAD), lambda h, i: (0, h)),
                  pl.BlockSpec((s_len, A_HEAD_PAD), lambda h, i: (0, h))],
        out_specs=pl.BlockSpec((tq, A_V_DIM), lambda h, i: (i, h)),
        out_shape=jax.ShapeDtypeStruct((s_len, A_HEADS * A_V_DIM), BF16),
        scratch_shapes=[pltpu.VMEM((tq, 1), F32), pltpu.VMEM((tq, A_HEAD_PAD), F32),
                        pltpu.VMEM((2, tq, tq), F32)],
        compiler_params=_params(("parallel", "arbitrary")),
        name="mla_attention",
    )(q_cat, k_cat, v_cat)


def _route_t(logits, bias_col):
    n = logits.shape[1]
    scores = jax.nn.sigmoid(logits)
    biased = scores + bias_col
    sub = lax.broadcasted_iota(I32, (GROUP_SIZE, n), 0)
    rows = []
    for g in range(N_GROUPS):
        x = biased[g * GROUP_SIZE:(g + 1) * GROUP_SIZE, :]
        m1 = jnp.max(x, axis=0, keepdims=True)
        i1 = jnp.min(jnp.where(x == m1, sub, GROUP_SIZE), axis=0, keepdims=True)
        m2 = jnp.max(jnp.where(sub == i1, -jnp.inf, x), axis=0, keepdims=True)
        rows.append(m1 + m2)
    gscore = jnp.concatenate(rows, axis=0)
    gio = lax.broadcasted_iota(I32, (N_GROUPS, n), 0)
    grank = jnp.zeros((N_GROUPS, n), I32)
    for g in range(N_GROUPS):
        r = gscore[g:g + 1, :]
        grank = grank + jnp.where(gio > g, jnp.where(r >= gscore, 1, 0), jnp.where(r > gscore, 1, 0))
    gsel = grank < TOPK_GROUPS
    masked = jnp.concatenate(
        [jnp.where(gsel[g:g + 1, :], biased[g * GROUP_SIZE:(g + 1) * GROUP_SIZE, :], -jnp.inf)
         for g in range(N_GROUPS)], axis=0)
    eio = lax.broadcasted_iota(I32, (N_EXPERTS, n), 0)
    rank = jnp.zeros((N_EXPERTS, n), I32)
    for e in range(N_EXPERTS):
        r = masked[e:e + 1, :]
        rank = rank + jnp.where(eio > e, jnp.where(r >= masked, 1, 0), jnp.where(r > masked, 1, 0))
    sel = rank < TOP_K
    denom = jnp.sum(jnp.where(sel, scores, 0.0), axis=0, keepdims=True)
    wnorm = scores / denom * ROUTED_SCALE
    eio_f = eio.astype(F32)
    ids, wts = [], []
    for k in range(TOP_K):
        hit = rank == k
        ids.append(jnp.sum(jnp.where(hit, eio_f, 0.0), axis=0, keepdims=True))
        wts.append(jnp.sum(jnp.where(hit, wnorm, 0.0), axis=0, keepdims=True))
    return jnp.concatenate(ids, axis=0).astype(I32), jnp.concatenate(wts, axis=0)


def _mid_kernel(y_ref, x_ref, gt_ref, gpost_ref, gs_ref, sh_ref, wr_ref, br_ref,
                x1_ref, hp_ref, idx_ref, wts_ref, h_scr, *, rc):
    tm, d = x_ref.shape
    half = d // 2

    def body(r):
        y = y_ref[pl.ds(r, rc), :].astype(F32)
        yn = y * lax.rsqrt(jnp.mean(y * y, axis=-1, keepdims=True) + NORM_EPS) * gpost_ref[...]
        x1 = x_ref[pl.ds(r, rc), :] + gt_ref[...] * yn
        x1_ref[pl.ds(r, rc), :] = x1
        hn = x1 * lax.rsqrt(jnp.mean(x1 * x1, axis=-1, keepdims=True) + NORM_EPS)
        h = hn * gs_ref[...] + sh_ref[...]
        h_scr[pl.ds(r, rc), :] = h
        hp_ref[pl.ds(r, rc), :] = _pack_pair(h[:, :half], h[:, half:])
    _row_loop(tm, rc, body)

    logits = lax.dot_general(wr_ref[...], h_scr[...], (((1,), (1,)), ((), ())),
                             preferred_element_type=F32, precision=lax.Precision.HIGHEST)
    ids, wts = _route_t(logits, br_ref[...])
    idx_ref[...] = ids
    wts_ref[...] = wts


def _mid(y, x, gt1, g_post, gs2, sh2, w_router, b_router, *, tm):
    s_len, d = x.shape
    tm = min(tm, s_len)
    vec = lambda a: a.reshape(1, d).astype(F32)
    return pl.pallas_call(
        functools.partial(_mid_kernel, rc=16),
        grid=(s_len // tm,),
        in_specs=[pl.BlockSpec((tm, d), lambda i: (i, 0)),
                  pl.BlockSpec((tm, d), lambda i: (i, 0)),
                  pl.BlockSpec((1, d), lambda i: (0, 0)),
                  pl.BlockSpec((1, d), lambda i: (0, 0)),
                  pl.BlockSpec((1, d), lambda i: (0, 0)),
                  pl.BlockSpec((1, d), lambda i: (0, 0)),
                  pl.BlockSpec((N_EXPERTS, d), lambda i: (0, 0)),
                  pl.BlockSpec((N_EXPERTS, 1), lambda i: (0, 0))],
        out_specs=[pl.BlockSpec((tm, d), lambda i: (i, 0)),
                   pl.BlockSpec((tm, d // 2), lambda i: (i, 0)),
                   pl.BlockSpec((TOP_K, tm), lambda i: (0, i)),
                   pl.BlockSpec((TOP_K, tm), lambda i: (0, i))],
        out_shape=[jax.ShapeDtypeStruct((s_len, d), F32),
                   jax.ShapeDtypeStruct((s_len, d // 2), U32),
                   jax.ShapeDtypeStruct((TOP_K, s_len), I32),
                   jax.ShapeDtypeStruct((TOP_K, s_len), F32)],
        scratch_shapes=[pltpu.VMEM((tm, d), F32)],
        compiler_params=_params(("parallel",)),
        name="mid_norm_route",
    )(y, x, vec(gt1), vec(g_post), vec(gs2), vec(sh2), w_router.T.astype(F32), b_router.reshape(N_EXPERTS, 1))


def _row_copy(src_hbm, dst_vmem, sem, src_row, dst_row):
    return pltpu.make_async_copy(src_hbm.at[pl.ds(src_row, 1), :], dst_vmem.at[pl.ds(dst_row, 1), :], sem)


ROW_GROUP = 8
ROW_DMA_PRIORITY = 0
WEIGHT_DMA_PRIORITY = 1
GATHER_RING = 3


def _expert_kernel(be_ref, nb_ref, nxt_ref, tok_ref, h_hbm, wg_hbm, wu_hbm, wd_hbm, sg_hbm, su_hbm, sd_hbm,
                   o_ref, stage_g, stage_u, stage_d, wb_g, wb_u, wb_d, xbuf, sems, xsems, *, rc):
    b = pl.program_id(0)
    e = be_ref[b]
    tmb = xbuf.shape[1]
    ring = xbuf.shape[0]
    cur = lax.rem(b, ring)
    ahead = lax.rem(b + ring - 1, ring)
    last = nb_ref[0] - 1
    stages = (stage_g, stage_u, stage_d)
    routed = (wg_hbm, wu_hbm, wd_hbm)
    shared = (sg_hbm, su_hbm, sd_hbm)

    def gather_loop(blk, slot):
        base = blk * tmb

        def step(gi, carry):
            r0 = gi * ROW_GROUP
            for u in range(ROW_GROUP):
                _row_copy(h_hbm, xbuf.at[slot], xsems.at[slot], tok_ref[base + r0 + u], r0 + u).start(
                    priority=ROW_DMA_PRIORITY)
            return carry
        lax.fori_loop(0, tmb // ROW_GROUP, step, 0)

    def gather_unrolled(blk, slot, part, n_parts):
        base = blk * tmb
        for r in range(part * tmb // n_parts, (part + 1) * tmb // n_parts):
            _row_copy(h_hbm, xbuf.at[slot], xsems.at[slot], tok_ref[base + r], r).start(priority=ROW_DMA_PRIORITY)

    def gather_wait(slot):
        pltpu.make_async_copy(h_hbm.at[pl.ds(0, tmb), :], xbuf.at[slot], xsems.at[slot]).wait()

    def fetch(ex):
        @pl.when(ex < N_EXPERTS)
        def _():
            for i in range(3):
                pltpu.make_async_copy(routed[i].at[ex], stages[i], sems.at[i]).start(priority=WEIGHT_DMA_PRIORITY)

        @pl.when(ex == N_EXPERTS)
        def _():
            for i in range(3):
                pltpu.make_async_copy(shared[i], stages[i], sems.at[i]).start(priority=WEIGHT_DMA_PRIORITY)

    @pl.when(b < nb_ref[0])
    def _():
        @pl.when(b == 0)
        def _():
            fetch(e)
            for s in range(ring - 1):
                gather_loop(jnp.minimum(s, last), s)

        is_first = jnp.logical_or(b == 0, be_ref[jnp.maximum(b - 1, 0)] != e)

        @pl.when(is_first)
        def _():
            for i in range(3):
                pltpu.make_async_copy(shared[i], stages[i], sems.at[i]).wait()
            for src, dst in ((stage_g, wb_g), (stage_u, wb_u), (stage_d, wb_d)):
                def cast(r, src=src, dst=dst):
                    dst[pl.ds(r, rc), :] = src[pl.ds(r, rc), :].astype(BF16)
                _row_loop(src.shape[0], rc, cast)
            nxt = nxt_ref[e]

            @pl.when(nxt >= 0)
            def _():
                fetch(nxt)

        gather_wait(cur)
        nxt_blk = jnp.minimum(b + ring - 1, last)
        lo, hi = _unpack_pair(xbuf[cur])
        x = jnp.concatenate([lo, hi], axis=1).astype(BF16)
        gather_unrolled(nxt_blk, ahead, 0, 3)
        g = jnp.dot(x, wb_g[...], preferred_element_type=F32)
        gather_unrolled(nxt_blk, ahead, 1, 3)
        u = jnp.dot(x, wb_u[...], preferred_element_type=F32)
        a = (g * jax.nn.sigmoid(g) * u).astype(BF16)
        gather_unrolled(nxt_blk, ahead, 2, 3)
        y = jnp.dot(a, wb_d[...], preferred_element_type=F32)
        half = y.shape[1] // 2
        o_ref[...] = _pack_pair(y[:, :half], y[:, half:])

        @pl.when(b == last)
        def _():
            for s in range(1, ring):
                gather_wait(lax.rem(b + s, ring))

    @pl.when(b >= nb_ref[0])
    def _():
        o_ref[...] = jnp.zeros_like(o_ref)


def _experts(block_e, n_used, next_e, slot_tok, hp, wg, wu, wd, sg, su, sd, *, tmb):
    n_slots = slot_tok.shape[0]
    wp = hp.shape[1]
    d, ff = wg.shape[1], wg.shape[2]
    nb = n_slots // tmb
    grid_spec = pltpu.PrefetchScalarGridSpec(
        num_scalar_prefetch=4,
        grid=(nb,),
        in_specs=[pl.BlockSpec(memory_space=pl.ANY)] * 7,
        out_specs=pl.BlockSpec((tmb, wp), lambda b, *_: (b, 0)),
        scratch_shapes=[pltpu.VMEM((d, ff), F32), pltpu.VMEM((d, ff), F32), pltpu.VMEM((ff, d), F32),
                        pltpu.VMEM((d, ff), BF16), pltpu.VMEM((d, ff), BF16), pltpu.VMEM((ff, d), BF16),
                        pltpu.VMEM((GATHER_RING, tmb, wp), U32),
                        pltpu.SemaphoreType.DMA((3,)), pltpu.SemaphoreType.DMA((GATHER_RING,))],
    )
    return pl.pallas_call(
        functools.partial(_expert_kernel, rc=128),
        grid_spec=grid_spec,
        out_shape=jax.ShapeDtypeStruct((n_slots, wp), U32),
        compiler_params=_params(("arbitrary",), vmem=60 * 1024 * 1024),
        name="moe_experts",
    )(block_e, n_used, next_e, slot_tok, hp, wg, wu, wd, sg, su, sd)


def _combine_kernel(slot_ref, w_ref, x1_ref, gt_ref, g_ref, ys_hbm, o_ref, buf, sems, *, rc, n_k):
    tc, d = x1_ref.shape
    half = d // 2
    i = pl.program_id(0)
    cur = lax.rem(i, 2)
    last = pl.num_programs(0) - 1

    def issue_plane(tile, slot, t0, k):
        base = tile * (tc * n_k)
        for t in range(rc):
            _row_copy(ys_hbm, buf.at[slot, k], sems.at[slot], slot_ref[base + (t0 + t) * n_k + k],
                      t0 + t).start(priority=k % 2)

    def issue_rows(tile, slot, t0):
        for k in range(n_k):
            issue_plane(tile, slot, t0, k)

    def wait_planes(slot):
        for k in range(n_k):
            pltpu.make_async_copy(ys_hbm.at[pl.ds(0, tc), :], buf.at[slot, k], sems.at[slot]).wait()

    @pl.when(i == 0)
    def _():
        _row_loop(tc, rc, lambda r: issue_rows(0, 0, r))

    wait_planes(cur)
    nxt = jnp.minimum(i + 1, last)

    def body(r):
        w = w_ref[pl.ds(r, rc), :]
        lo = jnp.zeros((rc, half), F32)
        hi = jnp.zeros((rc, half), F32)
        for k in range(n_k):
            issue_plane(nxt, 1 - cur, r, k)
            a, b = _unpack_pair(buf[cur, k, pl.ds(r, rc), :])
            lo = lo + w[:, k:k + 1] * a
            hi = hi + w[:, k:k + 1] * b
        ms = (jnp.sum(lo * lo, axis=-1, keepdims=True) + jnp.sum(hi * hi, axis=-1, keepdims=True)) * (1.0 / d)
        rs = lax.rsqrt(ms + NORM_EPS)
        o_ref[pl.ds(r, rc), 0:half] = (x1_ref[pl.ds(r, rc), 0:half]
                                       + gt_ref[:, 0:half] * (lo * rs * g_ref[:, 0:half]))
        o_ref[pl.ds(r, rc), half:d] = (x1_ref[pl.ds(r, rc), half:d]
                                       + gt_ref[:, half:d] * (hi * rs * g_ref[:, half:d]))
    for r0 in range(0, tc, rc):
        body(r0)

    @pl.when(i == last)
    def _():
        wait_planes(1 - cur)


def _combine(slots, w_tok, x1, gt2, g_post, ys, *, tc):
    s_len, d = x1.shape
    n_k = slots.shape[0] // s_len
    tc = min(tc, s_len)
    grid_spec = pltpu.PrefetchScalarGridSpec(
        num_scalar_prefetch=1,
        grid=(s_len // tc,),
        in_specs=[pl.BlockSpec((tc, w_tok.shape[1]), lambda i, sl: (i, 0)),
                  pl.BlockSpec((tc, d), lambda i, sl: (i, 0)),
                  pl.BlockSpec((1, d), lambda i, sl: (0, 0)),
                  pl.BlockSpec((1, d), lambda i, sl: (0, 0)),
                  pl.BlockSpec(memory_space=pl.ANY)],
        out_specs=pl.BlockSpec((tc, d), lambda i, sl: (i, 0)),
        scratch_shapes=[pltpu.VMEM((2, n_k, tc, d // 2), U32), pltpu.SemaphoreType.DMA((2,))],
    )
    return pl.pallas_call(
        functools.partial(_combine_kernel, rc=8, n_k=n_k),
        grid_spec=grid_spec,
        out_shape=jax.ShapeDtypeStruct((s_len, d), F32),
        compiler_params=_params(("arbitrary",)),
        name="moe_combine",
    )(slots, w_tok, x1, gt2.reshape(1, d), g_post.reshape(1, d), ys)


def _invert_kernel(slots_ref, lo_ref, hi_ref, o_ref, *, n_k, n_tok, stride):
    n_slots = o_ref.shape[0]
    n_e = lo_ref.shape[0]
    mask = (1 << (n_tok.bit_length() - 1)) - 1

    def fill(lo, hi):
        def body(j, carry):
            o_ref[j] = (j * stride) & mask
            return carry
        lax.fori_loop(lo, hi, body, 0)

    def per_expert(e, carry):
        fill(lo_ref[e], hi_ref[e])
        return carry
    lax.fori_loop(0, n_e, per_expert, 0)
    fill(hi_ref[n_e - 1], n_slots)

    def put(i, carry):
        for u in range(ROW_GROUP):
            t = i * ROW_GROUP + u
            for k in range(n_k):
                o_ref[slots_ref[t * n_k + k]] = t
        return carry
    lax.fori_loop(0, n_tok // ROW_GROUP, put, 0)


def _invert_slots(slots, pad_lo, pad_hi, *, n_slots, n_k, n_tok, stride):
    smem = pl.BlockSpec(memory_space=pltpu.SMEM)
    return pl.pallas_call(
        functools.partial(_invert_kernel, n_k=n_k, n_tok=n_tok, stride=stride),
        in_specs=[smem, smem, smem],
        out_specs=smem,
        out_shape=jax.ShapeDtypeStruct((n_slots,), I32),
        name="moe_invert_slots",
    )(slots, pad_lo, pad_hi)


def _spread_stride(n):
    m = int(n * 0.6180339887) | 1
    while math.gcd(m, n) != 1:
        m += 2
    return m


def _moe_plan(idx_t, wts_t, *, tmb):
    n_k, n_tok = idx_t.shape
    n_e = N_EXPERTS + 1
    eid = jnp.concatenate([idx_t, jnp.full((1, n_tok), N_EXPERTS, I32)], axis=0)
    wts = jnp.concatenate([wts_t, jnp.ones((1, n_tok), F32)], axis=0)
    picks = eid[:, None, :] == jnp.arange(n_e, dtype=I32)[None, :, None]
    sel = picks.any(axis=0).astype(I32)
    stride = _spread_stride(n_tok)
    visit = lax.rem(jnp.arange(n_tok, dtype=I32) * stride, n_tok)
    where = lax.rem(jnp.arange(n_tok, dtype=I32) * pow(stride, -1, n_tok), n_tok)
    csum = jnp.cumsum(sel[:, visit], axis=1)
    counts = csum[:, -1]
    padded = (counts + tmb - 1) // tmb * tmb
    ends = jnp.cumsum(padded)
    starts = ends - padded
    slot_dense = (starts[:, None] + csum - 1)[:, where]
    slot = jnp.sum(jnp.where(picks, slot_dense[None], 0), axis=1)
    n_slots = -(-(n_tok * (n_k + 1) + n_e * (tmb - 1)) // tmb) * tmb
    slots_tok_major = slot.T.reshape(-1)
    slot_tok = _invert_slots(slots_tok_major, (starts + counts).astype(I32), ends.astype(I32),
                             n_slots=n_slots, n_k=n_k + 1, n_tok=n_tok, stride=stride)
    n_used = (ends[-1] // tmb).astype(I32).reshape(1)
    block_start = jnp.arange(n_slots // tmb, dtype=I32) * tmb
    block_e = jnp.minimum(jnp.sum(ends[None, :] <= block_start[:, None], axis=1), n_e - 1).astype(I32)
    w_tok = jnp.zeros((n_tok, 16), F32).at[:, :n_k + 1].set(wts.T)
    owner = jnp.where(padded > 0, jnp.arange(n_e, dtype=I32), n_e)
    later = jnp.concatenate([lax.cummin(owner[::-1])[::-1][1:], jnp.full((1,), n_e, I32)])
    next_e = jnp.where(later >= n_e, -1, later).astype(I32)
    return slot_tok, block_e, n_used, next_e, slots_tok_major, w_tok


def _in_proj_weights(w_in):
    w_t = w_in.T
    d = w_t.shape[1]
    kr = w_t[COL_KR:COL_KR + A_ROPE_DIM]
    gates = w_t[COL_IG:COL_IG + 2 * M_HEADS]
    half = A_ROPE_DIM // 2
    z = lambda n: jnp.zeros((n, d), w_t.dtype)
    tail_t = jnp.concatenate([-kr[half:], kr[:half], gates, z(LANES - A_ROPE_DIM - 2 * M_HEADS),
                              kr, z(LANES - A_ROPE_DIM)], axis=0)
    return w_t.astype(BF16), tail_t.astype(BF16)


def _q_up_weight(w_uq):
    r = w_uq.shape[0]
    w = w_uq.reshape(r, A_HEADS, A_NOPE_DIM + A_ROPE_DIM)
    nope = w[:, :, :A_NOPE_DIM]
    rope = w[:, :, A_NOPE_DIM:]
    half = A_ROPE_DIM // 2
    swap = jnp.concatenate([-rope[:, :, half:], rope[:, :, :half]], axis=2)
    pad = jnp.zeros((r, A_HEADS, LANES - A_ROPE_DIM), w_uq.dtype)
    rope_p = jnp.concatenate([rope, pad], axis=2)
    swap_p = jnp.concatenate([swap, pad], axis=2)
    flat = lambda a: a.reshape(r, -1)
    return jnp.concatenate([flat(nope), flat(rope_p), flat(swap_p)], axis=1).astype(BF16)


def _kv_up_weight(w_ukv):
    r = w_ukv.shape[0]
    w = w_ukv.reshape(r, A_HEADS, A_NOPE_DIM + A_V_DIM)
    return jnp.concatenate([w[:, :, :A_NOPE_DIM].reshape(r, -1), w[:, :, A_NOPE_DIM:].reshape(r, -1)],
                           axis=1).astype(BF16)


def _rope_tables(s_len):
    pos = jnp.arange(s_len, dtype=F32)
    inv_freq = 1.0 / (ROPE_THETA ** (jnp.arange(0, A_ROPE_DIM, 2, dtype=F32) / A_ROPE_DIM))
    ang = pos[:, None] * inv_freq[None, :]
    pad = jnp.zeros((s_len, LANES - A_ROPE_DIM), F32)
    cos_t = jnp.concatenate([jnp.cos(ang), jnp.cos(ang), pad], axis=1)
    sin_t = jnp.concatenate([jnp.sin(ang), jnp.sin(ang), pad], axis=1)
    return cos_t, sin_t


def _block(x, c, w_ada, b_ada, g_pre_mix, g_post_mix, w_in, conv_w, conv_b, b_igate, b_fgate, g_mlstm_out,
           g_q_norm, w_uq, g_kv_norm, w_ukv, w_out, g_pre_ffn, g_post_ffn, w_router, b_router,
           w_gate, w_up, w_down, w_shared_gate, w_shared_up, w_shared_down):
    s_len, d = x.shape
    mod = _adaln(c, w_ada, b_ada)[0]
    sh1, sc1, gt1, sh2, sc2, gt2 = [mod[i * d:(i + 1) * d] for i in range(6)]

    w_main, w_tail = _in_proj_weights(w_in)
    h1 = _prenorm(x, g_pre_mix * (1.0 + sc1), sh1, tm=256)
    proj, tail = _mm_tail(h1, w_main, w_tail, tm=1024, tn=IN_TILE)
    h_m = _mlstm(proj, tail, conv_w, conv_b, b_igate, b_fgate, g_mlstm_out, chunk=min(M_CHUNK, s_len))
    scale = (A_NOPE_DIM + A_ROPE_DIM) ** -0.5
    q_raw = _norm_mm(proj[:, COL_CQ:COL_CQ + A_Q_RANK], g_q_norm * scale, _q_up_weight(w_uq), tm=1024, tn=1536)
    kv_raw = _norm_mm(proj[:, COL_CKV:COL_CKV + A_KV_RANK], g_kv_norm, _kv_up_weight(w_ukv), tm=1024, tn=1024)
    cos_t, sin_t = _rope_tables(s_len)
    q_cat, k_cat, v_cat = _rope_assemble(q_raw, kv_raw, tail, cos_t, sin_t, tm=256)
    h_a = _attention(q_cat, k_cat, v_cat, tq=1024, chunk=CHUNK)
    y = _mm2(h_m, h_a, w_out.astype(BF16), tm=1024, tn=1024)

    x1, hp, idx_t, wts_t = _mid(y, x, gt1, g_post_mix, g_pre_ffn * (1.0 + sc2), sh2, w_router, b_router, tm=256)
    tmb = 256
    slot_tok, block_e, n_used, next_e, slots, w_tok = _moe_plan(idx_t, wts_t, tmb=tmb)
    ys = _experts(block_e, n_used, next_e, slot_tok, hp, w_gate, w_up, w_down,
                  w_shared_gate, w_shared_up, w_shared_down, tmb=tmb)
    return _combine(slots, w_tok, x1, gt2, g_post_ffn, ys, tc=128)


def kernel(x, c, w_ada, b_ada, g_pre_mix, g_post_mix, w_in, conv_w, conv_b, b_igate, b_fgate, g_mlstm_out,
           g_q_norm, w_uq, g_kv_norm, w_ukv, w_out, g_pre_ffn, g_post_ffn, w_router, b_router,
           w_gate, w_up, w_down, w_shared_gate, w_shared_up, w_shared_down):
    assert x.shape[0] == 1 and w_ada.shape[0] == 1, "single sequence, single layer"
    layer = (w_ada, b_ada, g_pre_mix, g_post_mix, w_in, conv_w, conv_b, b_igate, b_fgate, g_mlstm_out,
             g_q_norm, w_uq, g_kv_norm, w_ukv, w_out, g_pre_ffn, g_post_ffn, w_router, b_router,
             w_gate, w_up, w_down, w_shared_gate, w_shared_up, w_shared_down)
    out = _block(x[0], c[0], *[p[0] for p in layer])
    return out[None]
```

```python
import functools
import math

import jax
import jax.numpy as jnp
from jax import lax
from jax.experimental import pallas as pl
from jax.experimental.pallas import tpu as pltpu

F32 = jnp.float32
BF16 = jnp.bfloat16
I32 = jnp.int32
U32 = jnp.uint32

NORM_EPS = 1e-6
CHUNK = 64
M_CHUNK = 128

M_HEADS = 4
M_QK_DIM = 256
M_V_DIM = 512
M_WIDTH = M_HEADS * M_V_DIM
M_QK_WIDTH = 2 * M_HEADS * M_QK_DIM
CONV_WIDTH = 4
GATE_SOFTCAP = 15.0

A_HEADS = 16
A_NOPE_DIM = 128
A_ROPE_DIM = 64
A_V_DIM = 128
A_Q_RANK = 768
A_KV_RANK = 512
A_HEAD_PAD = 256
ROPE_THETA = 10000.0

N_EXPERTS = 64
TOP_K = 8
N_GROUPS = 8
GROUP_SIZE = N_EXPERTS // N_GROUPS
TOPK_GROUPS = 4
ROUTED_SCALE = 2.5

LANES = 128
VMEM_LIMIT = 56 * 1024 * 1024
NEG_BIG = -1e30

COL_QK = 0
COL_V = COL_QK + M_QK_WIDTH
COL_O = COL_V + M_WIDTH
COL_IG = COL_O + M_WIDTH
COL_FG = COL_IG + M_HEADS
COL_CQ = COL_FG + M_HEADS
COL_CKV = COL_CQ + A_Q_RANK
COL_KR = COL_CKV + A_KV_RANK
N_IN = COL_KR + A_ROPE_DIM
IN_TILE = 1536
TAIL_W = 2 * LANES
TAIL_GATE_LANE = A_ROPE_DIM


def _params(sem, vmem=VMEM_LIMIT):
    return pltpu.CompilerParams(dimension_semantics=sem, vmem_limit_bytes=vmem)


def _row_loop(n_rows, rc, body):
    def step(i, carry):
        body(pl.multiple_of(i * rc, rc))
        return carry
    lax.fori_loop(0, n_rows // rc, step, 0)


def _pack_pair(a, b):
    lo = lax.bitcast_convert_type(a.astype(BF16).astype(F32), U32) >> 16
    hi = lax.bitcast_convert_type(b.astype(BF16).astype(F32), U32) & jnp.uint32(0xFFFF0000)
    return lo | hi


def _unpack_pair(u):
    lo = lax.bitcast_convert_type(u << 16, F32)
    hi = lax.bitcast_convert_type(u & jnp.uint32(0xFFFF0000), F32)
    return lo, hi


def _adaln_kernel(c_ref, w_ref, b_ref, o_ref, *, rc):
    d, tn = w_ref.shape
    nl = tn // LANES

    def step(i, accs):
        r = pl.multiple_of(i * rc, rc)
        c = c_ref[pl.ds(r, rc), :]
        ca = c * jax.nn.sigmoid(c)
        out = []
        for j in range(nl):
            prod = w_ref[pl.ds(r, rc), j * LANES:(j + 1) * LANES] * ca
            out.append(accs[j] + jnp.sum(prod.reshape(rc // 8, 8, LANES), axis=0))
        return tuple(out)

    accs = lax.fori_loop(0, d // rc, step, tuple(jnp.zeros((8, LANES), F32) for _ in range(nl)))
    for j in range(nl):
        o_ref[:, j * LANES:(j + 1) * LANES] = (
            jnp.sum(accs[j], axis=0, keepdims=True) + b_ref[:, j * LANES:(j + 1) * LANES])


def _adaln(c, w_ada, b_ada):
    d, n = w_ada.shape
    tn = 512
    c_b = jnp.broadcast_to(c.reshape(d, 1), (d, LANES))
    return pl.pallas_call(
        functools.partial(_adaln_kernel, rc=64),
        grid=(n // tn,),
        in_specs=[pl.BlockSpec((d, LANES), lambda j: (0, 0)),
                  pl.BlockSpec((d, tn), lambda j: (0, j)),
                  pl.BlockSpec((1, tn), lambda j: (0, j))],
        out_specs=pl.BlockSpec((1, tn), lambda j: (0, j)),
        out_shape=jax.ShapeDtypeStruct((1, n), F32),
        compiler_params=_params(("arbitrary",)),
        name="adaln",
    )(c_b, w_ada, b_ada.reshape(1, n))


def _prenorm_kernel(x_ref, gs_ref, sh_ref, o_ref, *, rc):
    def body(r):
        x = x_ref[pl.ds(r, rc), :]
        y = x * lax.rsqrt(jnp.mean(x * x, axis=-1, keepdims=True) + NORM_EPS)
        o_ref[pl.ds(r, rc), :] = (y * gs_ref[...] + sh_ref[...]).astype(o_ref.dtype)
    _row_loop(x_ref.shape[0], rc, body)


def _prenorm(x, gs, sh, *, tm):
    m, k = x.shape
    tm = min(tm, m)
    return pl.pallas_call(
        functools.partial(_prenorm_kernel, rc=32),
        grid=(m // tm,),
        in_specs=[pl.BlockSpec((tm, k), lambda i: (i, 0)),
                  pl.BlockSpec((1, k), lambda i: (0, 0)),
                  pl.BlockSpec((1, k), lambda i: (0, 0))],
        out_specs=pl.BlockSpec((tm, k), lambda i: (i, 0)),
        out_shape=jax.ShapeDtypeStruct((m, k), BF16),
        compiler_params=_params(("parallel",)),
        name="prenorm",
    )(x, gs.reshape(1, k), sh.reshape(1, k))


def _mm_tail_kernel(a_ref, wt_ref, tailt_ref, o_ref, t_ref):
    nt_dims = (((1,), (1,)), ((), ()))
    o_ref[...] = lax.dot_general(a_ref[...], wt_ref[...], nt_dims, preferred_element_type=F32).astype(o_ref.dtype)

    @pl.when(pl.program_id(1) == pl.num_programs(1) - 1)
    def _():
        t_ref[...] = lax.dot_general(a_ref[...], tailt_ref[...], nt_dims, preferred_element_type=F32)


def _mm_tail(a, w_t, tail_t, *, tm, tn):
    m, k = a.shape
    n, nt = w_t.shape[0], tail_t.shape[0]
    tm = min(tm, m)
    return pl.pallas_call(
        _mm_tail_kernel,
        grid=(m // tm, pl.cdiv(n, tn)),
        in_specs=[pl.BlockSpec((tm, k), lambda i, j: (i, 0)),
                  pl.BlockSpec((tn, k), lambda i, j: (j, 0)),
                  pl.BlockSpec((nt, k), lambda i, j: (0, 0))],
        out_specs=[pl.BlockSpec((tm, tn), lambda i, j: (i, j)),
                   pl.BlockSpec((tm, nt), lambda i, j: (i, 0))],
        out_shape=[jax.ShapeDtypeStruct((m, n), BF16), jax.ShapeDtypeStruct((m, nt), F32)],
        compiler_params=_params(("parallel", "arbitrary"), vmem=60 * 1024 * 1024),
        name="in_proj",
    )(a, w_t, tail_t)


def _mm2_kernel(a1_ref, a2_ref, w_ref, o_ref):
    k1 = a1_ref.shape[1]
    acc = jnp.dot(a1_ref[...], w_ref[:k1, :], preferred_element_type=F32)
    acc = acc + jnp.dot(a2_ref[...], w_ref[k1:, :], preferred_element_type=F32)
    o_ref[...] = acc.astype(o_ref.dtype)


def _mm2(a1, a2, w, *, tm, tn):
    m, k1 = a1.shape
    k2 = a2.shape[1]
    n = w.shape[1]
    tm = min(tm, m)
    return pl.pallas_call(
        _mm2_kernel,
        grid=(m // tm, n // tn),
        in_specs=[pl.BlockSpec((tm, k1), lambda i, j: (i, 0)),
                  pl.BlockSpec((tm, k2), lambda i, j: (i, 0)),
                  pl.BlockSpec((k1 + k2, tn), lambda i, j: (0, j))],
        out_specs=pl.BlockSpec((tm, tn), lambda i, j: (i, j)),
        out_shape=jax.ShapeDtypeStruct((m, n), BF16),
        compiler_params=_params(("parallel", "arbitrary")),
        name="out_proj",
    )(a1, a2, w)


def _softcap(z):
    return GATE_SOFTCAP * jnp.tanh(z * (1.0 / GATE_SOFTCAP))


def _log_sigmoid(z):
    return jnp.minimum(z, 0.0) - jnp.log1p(jnp.exp(-jnp.abs(z)))


def _mlstm_kernel(qk_ref, prev_ref, v_ref, o_ref, gcol_ref, grow_ref, cw_ref, cb_ref, bcol_ref, brow_ref,
                  gout_ref, out_ref, c_scr, n_scr, m_scr, u_scr):
    c = pl.program_id(0)
    L = qk_ref.shape[0]
    halo = prev_ref.shape[0]
    dk, dv = M_QK_DIM, M_V_DIM

    @pl.when(c == 0)
    def _():
        c_scr[...] = jnp.zeros_like(c_scr)
        n_scr[...] = jnp.zeros_like(n_scr)
        m_scr[...] = jnp.zeros_like(m_scr)

    prev = prev_ref[...].astype(F32)
    u_scr[0:halo, :] = jnp.where(c == 0, jnp.zeros_like(prev), prev)
    u_scr[halo:halo + L, :] = qk_ref[...].astype(F32)

    def conv_silu(col, width):
        acc = cb_ref[:, col:col + width]
        for j in range(CONV_WIDTH):
            r0 = halo - (CONV_WIDTH - 1) + j
            acc = acc + u_scr[r0:r0 + L, col:col + width] * cw_ref[j:j + 1, col:col + width]
        return acc * jax.nn.sigmoid(acc)

    pre_c = _softcap(gcol_ref[...] + bcol_ref[...])
    lf_c = _log_sigmoid(pre_c)
    pre_r = _softcap(grow_ref[0] + brow_ref[...])
    lf_r = _log_sigmoid(pre_r)
    row = lax.broadcasted_iota(I32, (L, L), 0)
    col = lax.broadcasted_iota(I32, (L, L), 1)
    causal = col <= row
    tril = causal.astype(F32)
    triu = (row <= col).astype(F32)
    bcum_c = jnp.dot(tril, lf_c, preferred_element_type=F32, precision=lax.Precision.HIGHEST)
    bcum_r = jnp.dot(lf_r, triu, preferred_element_type=F32, precision=lax.Precision.HIGHEST)

    for h in range(M_HEADS):
        li_lane = TAIL_GATE_LANE + h
        lf_lane = TAIL_GATE_LANE + M_HEADS + h
        q = (conv_silu(h * dk, dk) * (dk ** -0.5)).astype(BF16)
        kf = conv_silu(M_HEADS * dk + h * dk, dk)
        kb = kf.astype(BF16)
        v = v_ref[:, h * dv:(h + 1) * dv]
        b_c = bcum_c[:, lf_lane:lf_lane + 1]
        li_c = pre_c[:, li_lane:li_lane + 1]
        b_r = bcum_r[M_HEADS + h:M_HEADS + h + 1, :]
        li_r = pre_r[h:h + 1, :]
        m_prev = m_scr[h:h + 1, 0:1]

        dm = jnp.where(causal, b_c - b_r + li_r, NEG_BIG)
        inter = b_c + m_prev
        m_t = jnp.maximum(jnp.max(dm, axis=1, keepdims=True), inter)
        decay = jnp.exp(inter - m_t)
        s = lax.dot_general(q, kb, (((1,), (1,)), ((), ())), preferred_element_type=F32) * jnp.exp(dm - m_t)
        c_state = c_scr[h]
        n_state = n_scr[h:h + 1, :]
        num = jnp.dot(s.astype(BF16), v, preferred_element_type=F32)
        num = num + decay * jnp.dot(q, c_state.astype(BF16), preferred_element_type=F32)
        den = jnp.sum(s, axis=1, keepdims=True) + decay * jnp.sum(q.astype(F32) * n_state, axis=1, keepdims=True)
        hh = num / jnp.maximum(jnp.abs(den), jnp.exp(-m_t))

        b_last = b_c[L - 1:L, :]
        m_new = jnp.maximum(b_last + m_prev, jnp.max(b_last - b_r + li_r, axis=1, keepdims=True))
        carry = jnp.exp(b_last + m_prev - m_new)
        w_c = jnp.exp(b_last - b_c + li_c - m_new)
        kw = kf * w_c
        c_scr[h] = carry * c_state + lax.dot_general(kw.astype(BF16), v, (((0,), (0,)), ((), ())),
                                                     preferred_element_type=F32)
        n_scr[h:h + 1, :] = carry * n_state + jnp.sum(kw, axis=0, keepdims=True)
        m_scr[h:h + 1, :] = jnp.broadcast_to(m_new, (1, LANES))

        hn = hh * lax.rsqrt(jnp.mean(hh * hh, axis=1, keepdims=True) + NORM_EPS) * gout_ref[:, h * dv:(h + 1) * dv]
        gate = jax.nn.sigmoid(o_ref[:, h * dv:(h + 1) * dv].astype(F32))
        out_ref[:, h * dv:(h + 1) * dv] = (hn * gate).astype(out_ref.dtype)


def _mlstm(proj, tail, conv_w, conv_b, b_igate, b_fgate, g_out, *, chunk):
    s_len = proj.shape[0]
    L = chunk
    n_c = s_len // L
    halo = 16
    gates = tail[:, TAIL_GATE_LANE:TAIL_GATE_LANE + 2 * M_HEADS]
    g_row = gates.reshape(n_c, L, 2 * M_HEADS).transpose(0, 2, 1)
    bias = jnp.concatenate([b_igate, b_fgate]).astype(F32)
    b_col = jnp.zeros((1, LANES), F32).at[0, TAIL_GATE_LANE:TAIL_GATE_LANE + 2 * M_HEADS].set(bias)
    b_row = bias.reshape(2 * M_HEADS, 1)
    w = M_QK_WIDTH
    return pl.pallas_call(
        _mlstm_kernel,
        grid=(n_c,),
        in_specs=[pl.BlockSpec((L, w), lambda c: (c, COL_QK // w)),
                  pl.BlockSpec((halo, w), lambda c: (jnp.maximum(c * (L // halo) - 1, 0), COL_QK // w)),
                  pl.BlockSpec((L, M_WIDTH), lambda c: (c, COL_V // M_WIDTH)),
                  pl.BlockSpec((L, M_WIDTH), lambda c: (c, COL_O // M_WIDTH)),
                  pl.BlockSpec((L, LANES), lambda c: (c, 0)),
                  pl.BlockSpec((1, 2 * M_HEADS, L), lambda c: (c, 0, 0)),
                  pl.BlockSpec((CONV_WIDTH, w), lambda c: (0, 0)),
                  pl.BlockSpec((1, w), lambda c: (0, 0)),
                  pl.BlockSpec((1, LANES), lambda c: (0, 0)),
                  pl.BlockSpec((2 * M_HEADS, 1), lambda c: (0, 0)),
                  pl.BlockSpec((1, M_WIDTH), lambda c: (0, 0))],
        out_specs=pl.BlockSpec((L, M_WIDTH), lambda c: (c, 0)),
        out_shape=jax.ShapeDtypeStruct((s_len, M_WIDTH), BF16),
        scratch_shapes=[pltpu.VMEM((M_HEADS, M_QK_DIM, M_V_DIM), F32),
                        pltpu.VMEM((8, M_QK_DIM), F32),
                        pltpu.VMEM((8, LANES), F32),
                        pltpu.VMEM((halo + L, w), F32)],
        compiler_params=_params(("arbitrary",)),
        name="mlstm",
    )(proj, proj, proj, proj, tail, g_row, conv_w, conv_b.reshape(1, w), b_col, b_row, g_out.reshape(1, M_WIDTH))


Q_HEAD_COLS = 3 * LANES
Q_HEADS_PER_TILE = 4
KV_HEADS_PER_TILE = 8


def _norm_rows(x_ref, g_ref, h_scr, rc):
    @pl.when(pl.program_id(1) == 0)
    def _():
        def body(r):
            x = x_ref[pl.ds(r, rc), :].astype(F32)
            y = x * lax.rsqrt(jnp.mean(x * x, axis=-1, keepdims=True) + NORM_EPS)
            h_scr[pl.ds(r, rc), :] = (y * g_ref[...]).astype(BF16)
        _row_loop(x_ref.shape[0], rc, body)


def _q_proj_kernel(x_ref, g_ref, w_ref, cos_ref, sin_ref, o_ref, h_scr, *, rc):
    _norm_rows(x_ref, g_ref, h_scr, rc)
    res = jnp.dot(h_scr[...], w_ref[...], preferred_element_type=F32)
    cos = cos_ref[...]
    sin = sin_ref[...]
    for hh in range(Q_HEADS_PER_TILE):
        src = hh * Q_HEAD_COLS
        dst = hh * A_HEAD_PAD
        o_ref[:, dst:dst + LANES] = res[:, src:src + LANES].astype(BF16)
        o_ref[:, dst + LANES:dst + 2 * LANES] = (res[:, src + LANES:src + 2 * LANES] * cos
                                                 + res[:, src + 2 * LANES:src + 3 * LANES] * sin).astype(BF16)


def _q_proj(c_q, g, w, cos_t, sin_t, *, tm):
    m, k = c_q.shape
    tm = min(tm, m)
    tn = Q_HEADS_PER_TILE * Q_HEAD_COLS
    return pl.pallas_call(
        functools.partial(_q_proj_kernel, rc=32),
        grid=(m // tm, A_HEADS // Q_HEADS_PER_TILE),
        in_specs=[pl.BlockSpec((tm, k), lambda i, j: (i, 0)),
                  pl.BlockSpec((1, k), lambda i, j: (0, 0)),
                  pl.BlockSpec((k, tn), lambda i, j: (0, j)),
                  pl.BlockSpec((tm, LANES), lambda i, j: (i, 0)),
                  pl.BlockSpec((tm, LANES), lambda i, j: (i, 0))],
        out_specs=pl.BlockSpec((tm, Q_HEADS_PER_TILE * A_HEAD_PAD), lambda i, j: (i, j)),
        out_shape=jax.ShapeDtypeStruct((m, A_HEADS * A_HEAD_PAD), BF16),
        scratch_shapes=[pltpu.VMEM((tm, k), BF16)],
        compiler_params=_params(("parallel", "arbitrary")),
        name="q_proj_rope",
    )(c_q, g.reshape(1, k), w, cos_t, sin_t)


def _kv_proj_kernel(x_ref, g_ref, w_ref, tail_ref, cos_ref, sin_ref, k_ref, v_ref, h_scr, *, rc):
    j = pl.program_id(1)
    half_steps = pl.num_programs(1) // 2
    _norm_rows(x_ref, g_ref, h_scr, rc)
    res = jnp.dot(h_scr[...], w_ref[...], preferred_element_type=F32).astype(BF16)

    @pl.when(j < half_steps)
    def _():
        kpe = (tail_ref[:, LANES:2 * LANES] * cos_ref[...] + tail_ref[:, 0:LANES] * sin_ref[...]).astype(BF16)
        for hh in range(KV_HEADS_PER_TILE):
            dst = hh * A_HEAD_PAD
            k_ref[:, dst:dst + LANES] = res[:, hh * LANES:(hh + 1) * LANES]
            k_ref[:, dst + LANES:dst + 2 * LANES] = kpe

    @pl.when(j >= half_steps)
    def _():
        lane = lax.broadcasted_iota(I32, (x_ref.shape[0], LANES), 1)
        ones_col = jnp.where(lane == 0, 1.0, 0.0).astype(BF16)
        for hh in range(KV_HEADS_PER_TILE):
            dst = hh * A_HEAD_PAD
            v_ref[:, dst:dst + LANES] = res[:, hh * LANES:(hh + 1) * LANES]
            v_ref[:, dst + LANES:dst + 2 * LANES] = ones_col


def _kv_proj(c_kv, g, w, tail, cos_t, sin_t, *, tm):
    m, k = c_kv.shape
    tm = min(tm, m)
    tn = KV_HEADS_PER_TILE * LANES
    n_steps = w.shape[1] // tn
    half_steps = n_steps // 2
    wide = KV_HEADS_PER_TILE * A_HEAD_PAD
    return pl.pallas_call(
        functools.partial(_kv_proj_kernel, rc=32),
        grid=(m // tm, n_steps),
        in_specs=[pl.BlockSpec((tm, k), lambda i, j: (i, 0)),
                  pl.BlockSpec((1, k), lambda i, j: (0, 0)),
                  pl.BlockSpec((k, tn), lambda i, j: (0, j)),
                  pl.BlockSpec((tm, 2 * LANES), lambda i, j: (i, 0)),
                  pl.BlockSpec((tm, LANES), lambda i, j: (i, 0)),
                  pl.BlockSpec((tm, LANES), lambda i, j: (i, 0))],
        out_specs=[pl.BlockSpec((tm, wide), lambda i, j: (i, jnp.minimum(j, half_steps - 1))),
                   pl.BlockSpec((tm, wide), lambda i, j: (i, jnp.maximum(j - half_steps, 0)))],
        out_shape=[jax.ShapeDtypeStruct((m, A_HEADS * A_HEAD_PAD), BF16)] * 2,
        scratch_shapes=[pltpu.VMEM((tm, k), BF16)],
        compiler_params=_params(("parallel", "arbitrary")),
        name="kv_proj_rope",
    )(c_kv, g.reshape(1, k), w, tail, cos_t, sin_t)


def _attn_kernel(q_ref, k_ref, v_ref, o_ref, m_scr, acc_scr, s_scr, *, chunk):
    qi = pl.program_id(1)
    tq = q_ref.shape[0]
    m_scr[...] = jnp.full_like(m_scr, NEG_BIG)
    acc_scr[...] = jnp.zeros_like(acc_scr)

    def scores(slot, blk):
        start = pl.multiple_of(blk * tq, tq)
        s_scr[slot] = lax.dot_general(q_ref[...], k_ref[pl.ds(start, tq), :], (((1,), (1,)), ((), ())),
                                      preferred_element_type=F32)

    def consume(slot, blk, masked):
        start = pl.multiple_of(blk * tq, tq)
        s = s_scr[slot]
        if masked:
            rq = lax.broadcasted_iota(I32, (tq, tq), 0) // chunk
            ck = lax.broadcasted_iota(I32, (tq, tq), 1) // chunk
            s = jnp.where(ck <= rq, s, NEG_BIG)
        m_prev = m_scr[...]
        m_new = jnp.maximum(m_prev, jnp.max(s, axis=1, keepdims=True))
        pr = jnp.exp(s - m_new).astype(BF16)
        acc_scr[...] = (jnp.exp(m_prev - m_new) * acc_scr[...]
                        + jnp.dot(pr, v_ref[pl.ds(start, tq), :], preferred_element_type=F32))
        m_scr[...] = m_new

    scores(0, 0)

    def pair(t, carry):
        scores(1, 2 * t + 1)
        consume(0, 2 * t, False)
        scores(0, 2 * t + 2)
        consume(1, 2 * t + 1, False)
        return carry
    lax.fori_loop(0, qi // 2, pair, 0)

    @pl.when(qi % 2 == 1)
    def _():
        scores(1, qi)
        consume(0, qi - 1, False)
        consume(1, qi, True)

    @pl.when(qi % 2 == 0)
    def _():
        consume(0, qi, True)

    acc = acc_scr[...]
    o_ref[...] = (acc[:, :A_V_DIM] / acc[:, A_V_DIM:A_V_DIM + 1]).astype(o_ref.dtype)


def _attention(q_cat, k_cat, v_cat, *, tq, chunk):
    s_len = q_cat.shape[0]
    tq = min(tq, s_len)
    return pl.pallas_call(
        functools.partial(_attn_kernel, chunk=chunk),
        grid=(A_HEADS, s_len // tq),
        in_specs=[pl.BlockSpec((tq, A_HEAD_PAD), lambda h, i: (i, h)),
                  pl.BlockSpec((s_len, A_HEAD_PAD), lambda h, i: (0, h)),
                  pl.BlockSpec((s_len, A_HEAD_PAD), lambda h, i: (0, h))],
        out_specs=pl.BlockSpec((tq, A_V_DIM), lambda h, i: (i, h)),
        out_shape=jax.ShapeDtypeStruct((s_len, A_HEADS * A_V_DIM), BF16),
        scratch_shapes=[pltpu.VMEM((tq, 1), F32), pltpu.VMEM((tq, A_HEAD_PAD), F32),
                        pltpu.VMEM((2, tq, tq), F32)],
        compiler_params=_params(("parallel", "arbitrary")),
        name="mla_attention",
    )(q_cat, k_cat, v_cat)


def _route_t(logits, bias_col):
    n = logits.shape[1]
    scores = jax.nn.sigmoid(logits)
    biased = scores + bias_col
    sub = lax.broadcasted_iota(I32, (GROUP_SIZE, n), 0)
    rows = []
    for g in range(N_GROUPS):
        x = biased[g * GROUP_SIZE:(g + 1) * GROUP_SIZE, :]
        m1 = jnp.max(x, axis=0, keepdims=True)
        i1 = jnp.min(jnp.where(x == m1, sub, GROUP_SIZE), axis=0, keepdims=True)
        m2 = jnp.max(jnp.where(sub == i1, -jnp.inf, x), axis=0, keepdims=True)
        rows.append(m1 + m2)
    gscore = jnp.concatenate(rows, axis=0)
    gio = lax.broadcasted_iota(I32, (N_GROUPS, n), 0)
    grank = jnp.zeros((N_GROUPS, n), I32)
    for g in range(N_GROUPS):
        r = gscore[g:g + 1, :]
        grank = grank + jnp.where(gio > g, jnp.where(r >= gscore, 1, 0), jnp.where(r > gscore, 1, 0))
    gsel = grank < TOPK_GROUPS
    masked = jnp.concatenate(
        [jnp.where(gsel[g:g + 1, :], biased[g * GROUP_SIZE:(g + 1) * GROUP_SIZE, :], -jnp.inf)
         for g in range(N_GROUPS)], axis=0)
    eio = lax.broadcasted_iota(I32, (N_EXPERTS, n), 0)
    rank = jnp.zeros((N_EXPERTS, n), I32)
    for e in range(N_EXPERTS):
        r = masked[e:e + 1, :]
        rank = rank + jnp.where(eio > e, jnp.where(r >= masked, 1, 0), jnp.where(r > masked, 1, 0))
    sel = rank < TOP_K
    denom = jnp.sum(jnp.where(sel, scores, 0.0), axis=0, keepdims=True)
    wnorm = scores / denom * ROUTED_SCALE
    eio_f = eio.astype(F32)
    ids, wts = [], []
    for k in range(TOP_K):
        hit = rank == k
        ids.append(jnp.sum(jnp.where(hit, eio_f, 0.0), axis=0, keepdims=True))
        wts.append(jnp.sum(jnp.where(hit, wnorm, 0.0), axis=0, keepdims=True))
    return jnp.concatenate(ids, axis=0).astype(I32), jnp.concatenate(wts, axis=0)


def _mid_kernel(y_ref, x_ref, gt_ref, gpost_ref, gs_ref, sh_ref, wr_ref, br_ref,
                x1_ref, hp_ref, idx_ref, wts_ref, h_scr, *, rc):
    tm, d = x_ref.shape
    half = d // 2

    def body(r):
        y = y_ref[pl.ds(r, rc), :].astype(F32)
        yn = y * lax.rsqrt(jnp.mean(y * y, axis=-1, keepdims=True) + NORM_EPS) * gpost_ref[...]
        x1 = x_ref[pl.ds(r, rc), :] + gt_ref[...] * yn
        x1_ref[pl.ds(r, rc), :] = x1
        hn = x1 * lax.rsqrt(jnp.mean(x1 * x1, axis=-1, keepdims=True) + NORM_EPS)
        h = hn * gs_ref[...] + sh_ref[...]
        h_scr[pl.ds(r, rc), :] = h
        hp_ref[pl.ds(r, rc), :] = _pack_pair(h[:, :half], h[:, half:])
    _row_loop(tm, rc, body)

    logits = lax.dot_general(wr_ref[...], h_scr[...], (((1,), (1,)), ((), ())),
                             preferred_element_type=F32, precision=lax.Precision.HIGHEST)
    ids, wts = _route_t(logits, br_ref[...])
    idx_ref[...] = ids
    wts_ref[...] = wts


def _mid(y, x, gt1, g_post, gs2, sh2, w_router, b_router, *, tm):
    s_len, d = x.shape
    tm = min(tm, s_len)
    vec = lambda a: a.reshape(1, d).astype(F32)
    return pl.pallas_call(
        functools.partial(_mid_kernel, rc=16),
        grid=(s_len // tm,),
        in_specs=[pl.BlockSpec((tm, d), lambda i: (i, 0)),
                  pl.BlockSpec((tm, d), lambda i: (i, 0)),
                  pl.BlockSpec((1, d), lambda i: (0, 0)),
                  pl.BlockSpec((1, d), lambda i: (0, 0)),
                  pl.BlockSpec((1, d), lambda i: (0, 0)),
                  pl.BlockSpec((1, d), lambda i: (0, 0)),
                  pl.BlockSpec((N_EXPERTS, d), lambda i: (0, 0)),
                  pl.BlockSpec((N_EXPERTS, 1), lambda i: (0, 0))],
        out_specs=[pl.BlockSpec((tm, d), lambda i: (i, 0)),
                   pl.BlockSpec((tm, d // 2), lambda i: (i, 0)),
                   pl.BlockSpec((TOP_K, tm), lambda i: (0, i)),
                   pl.BlockSpec((TOP_K, tm), lambda i: (0, i))],
        out_shape=[jax.ShapeDtypeStruct((s_len, d), F32),
                   jax.ShapeDtypeStruct((s_len, d // 2), U32),
                   jax.ShapeDtypeStruct((TOP_K, s_len), I32),
                   jax.ShapeDtypeStruct((TOP_K, s_len), F32)],
        scratch_shapes=[pltpu.VMEM((tm, d), F32)],
        compiler_params=_params(("parallel",)),
        name="mid_norm_route",
    )(y, x, vec(gt1), vec(g_post), vec(gs2), vec(sh2), w_router.T.astype(F32), b_router.reshape(N_EXPERTS, 1))


def _row_copy(src_hbm, dst_vmem, sem, src_row, dst_row):
    return pltpu.make_async_copy(src_hbm.at[pl.ds(src_row, 1), :], dst_vmem.at[pl.ds(dst_row, 1), :], sem)


ROW_GROUP = 8
ROW_DMA_PRIORITY = 0
WEIGHT_DMA_PRIORITY = 1
GATHER_RING = 3


def _expert_kernel(be_ref, nb_ref, nxt_ref, tok_ref, h_hbm, wg_hbm, wu_hbm, wd_hbm, sg_hbm, su_hbm, sd_hbm,
                   o_ref, stage_g, stage_u, stage_d, wb_g, wb_u, wb_d, xbuf, sems, xsems, *, rc):
    b = pl.program_id(0)
    e = be_ref[b]
    tmb = xbuf.shape[1]
    ring = xbuf.shape[0]
    cur = lax.rem(b, ring)
    ahead = lax.rem(b + ring - 1, ring)
    last = nb_ref[0] - 1
    stages = (stage_g, stage_u, stage_d)
    routed = (wg_hbm, wu_hbm, wd_hbm)
    shared = (sg_hbm, su_hbm, sd_hbm)

    def gather_loop(blk, slot):
        base = blk * tmb

        def step(gi, carry):
            r0 = gi * ROW_GROUP
            for u in range(ROW_GROUP):
                _row_copy(h_hbm, xbuf.at[slot], xsems.at[slot], tok_ref[base + r0 + u], r0 + u).start(
                    priority=ROW_DMA_PRIORITY)
            return carry
        lax.fori_loop(0, tmb // ROW_GROUP, step, 0)

    def gather_unrolled(blk, slot, part, n_parts):
        base = blk * tmb
        for r in range(part * tmb // n_parts, (part + 1) * tmb // n_parts):
            _row_copy(h_hbm, xbuf.at[slot], xsems.at[slot], tok_ref[base + r], r).start(priority=ROW_DMA_PRIORITY)

    def gather_wait(slot):
        pltpu.make_async_copy(h_hbm.at[pl.ds(0, tmb), :], xbuf.at[slot], xsems.at[slot]).wait()

    def fetch(ex):
        @pl.when(ex < N_EXPERTS)
        def _():
            for i in range(3):
                pltpu.make_async_copy(routed[i].at[ex], stages[i], sems.at[i]).start(priority=WEIGHT_DMA_PRIORITY)

        @pl.when(ex == N_EXPERTS)
        def _():
            for i in range(3):
                pltpu.make_async_copy(shared[i], stages[i], sems.at[i]).start(priority=WEIGHT_DMA_PRIORITY)

    @pl.when(b < nb_ref[0])
    def _():
        @pl.when(b == 0)
        def _():
            fetch(e)
            for s in range(ring - 1):
                gather_loop(jnp.minimum(s, last), s)

        is_first = jnp.logical_or(b == 0, be_ref[jnp.maximum(b - 1, 0)] != e)

        @pl.when(is_first)
        def _():
            for i in range(3):
                pltpu.make_async_copy(shared[i], stages[i], sems.at[i]).wait()
            for src, dst in ((stage_g, wb_g), (stage_u, wb_u), (stage_d, wb_d)):
                def cast(r, src=src, dst=dst):
                    dst[pl.ds(r, rc), :] = src[pl.ds(r, rc), :].astype(BF16)
                _row_loop(src.shape[0], rc, cast)
            nxt = nxt_ref[e]

            @pl.when(nxt >= 0)
            def _():
                fetch(nxt)

        gather_wait(cur)
        nxt_blk = jnp.minimum(b + ring - 1, last)
        lo, hi = _unpack_pair(xbuf[cur])
        x = jnp.concatenate([lo, hi], axis=1).astype(BF16)
        gather_unrolled(nxt_blk, ahead, 0, 3)
        g = jnp.dot(x, wb_g[...], preferred_element_type=F32)
        gather_unrolled(nxt_blk, ahead, 1, 3)
        u = jnp.dot(x, wb_u[...], preferred_element_type=F32)
        a = (g * jax.nn.sigmoid(g) * u).astype(BF16)
        gather_unrolled(nxt_blk, ahead, 2, 3)
        y = jnp.dot(a, wb_d[...], preferred_element_type=F32)
        half = y.shape[1] // 2
        o_ref[...] = _pack_pair(y[:, :half], y[:, half:])

        @pl.when(b == last)
        def _():
            for s in range(1, ring):
                gather_wait(lax.rem(b + s, ring))

    @pl.when(b >= nb_ref[0])
    def _():
        o_ref[...] = jnp.zeros_like(o_ref)


def _experts(block_e, n_used, next_e, slot_tok, hp, wg, wu, wd, sg, su, sd, *, tmb):
    n_slots = slot_tok.shape[0]
    wp = hp.shape[1]
    d, ff = wg.shape[1], wg.shape[2]
    nb = n_slots // tmb
    grid_spec = pltpu.PrefetchScalarGridSpec(
        num_scalar_prefetch=4,
        grid=(nb,),
        in_specs=[pl.BlockSpec(memory_space=pl.ANY)] * 7,
        out_specs=pl.BlockSpec((tmb, wp), lambda b, *_: (b, 0)),
        scratch_shapes=[pltpu.VMEM((d, ff), F32), pltpu.VMEM((d, ff), F32), pltpu.VMEM((ff, d), F32),
                        pltpu.VMEM((d, ff), BF16), pltpu.VMEM((d, ff), BF16), pltpu.VMEM((ff, d), BF16),
                        pltpu.VMEM((GATHER_RING, tmb, wp), U32),
                        pltpu.SemaphoreType.DMA((3,)), pltpu.SemaphoreType.DMA((GATHER_RING,))],
    )
    return pl.pallas_call(
        functools.partial(_expert_kernel, rc=128),
        grid_spec=grid_spec,
        out_shape=jax.ShapeDtypeStruct((n_slots, wp), U32),
        compiler_params=_params(("arbitrary",), vmem=60 * 1024 * 1024),
        name="moe_experts",
    )(block_e, n_used, next_e, slot_tok, hp, wg, wu, wd, sg, su, sd)


def _combine_kernel(slot_ref, w_ref, x1_ref, gt_ref, g_ref, ys_hbm, o_ref, buf, sems, *, rc, n_k):
    tc, d = x1_ref.shape
    half = d // 2
    i = pl.program_id(0)
    cur = lax.rem(i, 2)
    last = pl.num_programs(0) - 1

    def issue_plane(tile, slot, t0, k):
        base = tile * (tc * n_k)
        for t in range(rc):
            _row_copy(ys_hbm, buf.at[slot, k], sems.at[slot], slot_ref[base + (t0 + t) * n_k + k],
                      t0 + t).start(priority=k % 2)

    def issue_rows(tile, slot, t0):
        for k in range(n_k):
            issue_plane(tile, slot, t0, k)

    def wait_planes(slot):
        for k in range(n_k):
            pltpu.make_async_copy(ys_hbm.at[pl.ds(0, tc), :], buf.at[slot, k], sems.at[slot]).wait()

    @pl.when(i == 0)
    def _():
        _row_loop(tc, rc, lambda r: issue_rows(0, 0, r))

    wait_planes(cur)
    nxt = jnp.minimum(i + 1, last)

    def body(r):
        w = w_ref[pl.ds(r, rc), :]
        lo = jnp.zeros((rc, half), F32)
        hi = jnp.zeros((rc, half), F32)
        for k in range(n_k):
            issue_plane(nxt, 1 - cur, r, k)
            a, b = _unpack_pair(buf[cur, k, pl.ds(r, rc), :])
            lo = lo + w[:, k:k + 1] * a
            hi = hi + w[:, k:k + 1] * b
        ms = (jnp.sum(lo * lo, axis=-1, keepdims=True) + jnp.sum(hi * hi, axis=-1, keepdims=True)) * (1.0 / d)
        rs = lax.rsqrt(ms + NORM_EPS)
        o_ref[pl.ds(r, rc), 0:half] = (x1_ref[pl.ds(r, rc), 0:half]
                                       + gt_ref[:, 0:half] * (lo * rs * g_ref[:, 0:half]))
        o_ref[pl.ds(r, rc), half:d] = (x1_ref[pl.ds(r, rc), half:d]
                                       + gt_ref[:, half:d] * (hi * rs * g_ref[:, half:d]))
    for r0 in range(0, tc, rc):
        body(r0)

    @pl.when(i == last)
    def _():
        wait_planes(1 - cur)


def _combine(slots, w_tok, x1, gt2, g_post, ys, *, tc):
    s_len, d = x1.shape
    n_k = slots.shape[0] // s_len
    tc = min(tc, s_len)
    grid_spec = pltpu.PrefetchScalarGridSpec(
        num_scalar_prefetch=1,
        grid=(s_len // tc,),
        in_specs=[pl.BlockSpec((tc, w_tok.shape[1]), lambda i, sl: (i, 0)),
                  pl.BlockSpec((tc, d), lambda i, sl: (i, 0)),
                  pl.BlockSpec((1, d), lambda i, sl: (0, 0)),
                  pl.BlockSpec((1, d), lambda i, sl: (0, 0)),
                  pl.BlockSpec(memory_space=pl.ANY)],
        out_specs=pl.BlockSpec((tc, d), lambda i, sl: (i, 0)),
        scratch_shapes=[pltpu.VMEM((2, n_k, tc, d // 2), U32), pltpu.SemaphoreType.DMA((2,))],
    )
    return pl.pallas_call(
        functools.partial(_combine_kernel, rc=8, n_k=n_k),
        grid_spec=grid_spec,
        out_shape=jax.ShapeDtypeStruct((s_len, d), F32),
        compiler_params=_params(("arbitrary",)),
        name="moe_combine",
    )(slots, w_tok, x1, gt2.reshape(1, d), g_post.reshape(1, d), ys)


def _invert_kernel(slots_ref, lo_ref, hi_ref, o_ref, *, n_k, n_tok, stride):
    n_slots = o_ref.shape[0]
    n_e = lo_ref.shape[0]
    mask = (1 << (n_tok.bit_length() - 1)) - 1

    def fill(lo, hi):
        def body(j, carry):
            o_ref[j] = (j * stride) & mask
            return carry
        lax.fori_loop(lo, hi, body, 0)

    def per_expert(e, carry):
        fill(lo_ref[e], hi_ref[e])
        return carry
    lax.fori_loop(0, n_e, per_expert, 0)
    fill(hi_ref[n_e - 1], n_slots)

    def put(i, carry):
        for u in range(ROW_GROUP):
            t = i * ROW_GROUP + u
            for k in range(n_k):
                o_ref[slots_ref[t * n_k + k]] = t
        return carry
    lax.fori_loop(0, n_tok // ROW_GROUP, put, 0)


def _invert_slots(slots, pad_lo, pad_hi, *, n_slots, n_k, n_tok, stride):
    smem = pl.BlockSpec(memory_space=pltpu.SMEM)
    return pl.pallas_call(
        functools.partial(_invert_kernel, n_k=n_k, n_tok=n_tok, stride=stride),
        in_specs=[smem, smem, smem],
        out_specs=smem,
        out_shape=jax.ShapeDtypeStruct((n_slots,), I32),
        name="moe_invert_slots",
    )(slots, pad_lo, pad_hi)


def _spread_stride(n):
    m = int(n * 0.6180339887) | 1
    while math.gcd(m, n) != 1:
        m += 2
    return m


def _moe_plan(idx_t, wts_t, *, tmb):
    n_k, n_tok = idx_t.shape
    n_e = N_EXPERTS + 1
    eid = jnp.concatenate([idx_t, jnp.full((1, n_tok), N_EXPERTS, I32)], axis=0)
    wts = jnp.concatenate([wts_t, jnp.ones((1, n_tok), F32)], axis=0)
    picks = eid[:, None, :] == jnp.arange(n_e, dtype=I32)[None, :, None]
    sel = picks.any(axis=0).astype(I32)
    stride = _spread_stride(n_tok)
    visit = lax.rem(jnp.arange(n_tok, dtype=I32) * stride, n_tok)
    where = lax.rem(jnp.arange(n_tok, dtype=I32) * pow(stride, -1, n_tok), n_tok)
    csum = jnp.cumsum(sel[:, visit], axis=1)
    counts = csum[:, -1]
    padded = (counts + tmb - 1) // tmb * tmb
    ends = jnp.cumsum(padded)
    starts = ends - padded
    slot_dense = (starts[:, None] + csum - 1)[:, where]
    slot = jnp.sum(jnp.where(picks, slot_dense[None], 0), axis=1)
    n_slots = -(-(n_tok * (n_k + 1) + n_e * (tmb - 1)) // tmb) * tmb
    slots_tok_major = slot.T.reshape(-1)
    slot_tok = _invert_slots(slots_tok_major, (starts + counts).astype(I32), ends.astype(I32),
                             n_slots=n_slots, n_k=n_k + 1, n_tok=n_tok, stride=stride)
    n_used = (ends[-1] // tmb).astype(I32).reshape(1)
    block_start = jnp.arange(n_slots // tmb, dtype=I32) * tmb
    block_e = jnp.minimum(jnp.sum(ends[None, :] <= block_start[:, None], axis=1), n_e - 1).astype(I32)
    w_tok = jnp.zeros((n_tok, 16), F32).at[:, :n_k + 1].set(wts.T)
    owner = jnp.where(padded > 0, jnp.arange(n_e, dtype=I32), n_e)
    later = jnp.concatenate([lax.cummin(owner[::-1])[::-1][1:], jnp.full((1,), n_e, I32)])
    next_e = jnp.where(later >= n_e, -1, later).astype(I32)
    return slot_tok, block_e, n_used, next_e, slots_tok_major, w_tok


def _in_proj_weights(w_in):
    w_t = w_in.T
    d = w_t.shape[1]
    kr = w_t[COL_KR:COL_KR + A_ROPE_DIM]
    gates = w_t[COL_IG:COL_IG + 2 * M_HEADS]
    half = A_ROPE_DIM // 2
    z = lambda n: jnp.zeros((n, d), w_t.dtype)
    tail_t = jnp.concatenate([-kr[half:], kr[:half], gates, z(LANES - A_ROPE_DIM - 2 * M_HEADS),
                              kr, z(LANES - A_ROPE_DIM)], axis=0)
    return w_t.astype(BF16), tail_t.astype(BF16)


def _q_up_weight(w_uq):
    r = w_uq.shape[0]
    w = w_uq.reshape(r, A_HEADS, A_NOPE_DIM + A_ROPE_DIM)
    nope = w[:, :, :A_NOPE_DIM]
    rope = w[:, :, A_NOPE_DIM:]
    half = A_ROPE_DIM // 2
    swap = jnp.concatenate([-rope[:, :, half:], rope[:, :, :half]], axis=2)
    pad = jnp.zeros((r, A_HEADS, LANES - A_ROPE_DIM), w_uq.dtype)
    rope_p = jnp.concatenate([rope, pad], axis=2)
    swap_p = jnp.concatenate([swap, pad], axis=2)
    return jnp.concatenate([nope, rope_p, swap_p], axis=2).reshape(r, A_HEADS * Q_HEAD_COLS).astype(BF16)


def _kv_up_weight(w_ukv):
    r = w_ukv.shape[0]
    w = w_ukv.reshape(r, A_HEADS, A_NOPE_DIM + A_V_DIM)
    return jnp.concatenate([w[:, :, :A_NOPE_DIM].reshape(r, -1), w[:, :, A_NOPE_DIM:].reshape(r, -1)],
                           axis=1).astype(BF16)


def _rope_tables(s_len):
    pos = jnp.arange(s_len, dtype=F32)
    inv_freq = 1.0 / (ROPE_THETA ** (jnp.arange(0, A_ROPE_DIM, 2, dtype=F32) / A_ROPE_DIM))
    ang = pos[:, None] * inv_freq[None, :]
    pad = jnp.zeros((s_len, LANES - A_ROPE_DIM), F32)
    cos_t = jnp.concatenate([jnp.cos(ang), jnp.cos(ang), pad], axis=1)
    sin_t = jnp.concatenate([jnp.sin(ang), jnp.sin(ang), pad], axis=1)
    return cos_t, sin_t


def _block(x, c, w_ada, b_ada, g_pre_mix, g_post_mix, w_in, conv_w, conv_b, b_igate, b_fgate, g_mlstm_out,
           g_q_norm, w_uq, g_kv_norm, w_ukv, w_out, g_pre_ffn, g_post_ffn, w_router, b_router,
           w_gate, w_up, w_down, w_shared_gate, w_shared_up, w_shared_down):
    s_len, d = x.shape
    mod = _adaln(c, w_ada, b_ada)[0]
    sh1, sc1, gt1, sh2, sc2, gt2 = [mod[i * d:(i + 1) * d] for i in range(6)]

    w_main, w_tail = _in_proj_weights(w_in)
    h1 = _prenorm(x, g_pre_mix * (1.0 + sc1), sh1, tm=256)
    proj, tail = _mm_tail(h1, w_main, w_tail, tm=1024, tn=IN_TILE)
    h_m = _mlstm(proj, tail, conv_w, conv_b, b_igate, b_fgate, g_mlstm_out, chunk=min(M_CHUNK, s_len))
    scale = (A_NOPE_DIM + A_ROPE_DIM) ** -0.5
    cos_t, sin_t = _rope_tables(s_len)
    q_cat = _q_proj(proj[:, COL_CQ:COL_CQ + A_Q_RANK], g_q_norm * scale, _q_up_weight(w_uq), cos_t, sin_t, tm=1024)
    k_cat, v_cat = _kv_proj(proj[:, COL_CKV:COL_CKV + A_KV_RANK], g_kv_norm, _kv_up_weight(w_ukv), tail,
                            cos_t, sin_t, tm=1024)
    h_a = _attention(q_cat, k_cat, v_cat, tq=1024, chunk=CHUNK)
    y = _mm2(h_m, h_a, w_out.astype(BF16), tm=1024, tn=1024)

    x1, hp, idx_t, wts_t = _mid(y, x, gt1, g_post_mix, g_pre_ffn * (1.0 + sc2), sh2, w_router, b_router, tm=256)
    tmb = 256
    slot_tok, block_e, n_used, next_e, slots, w_tok = _moe_plan(idx_t, wts_t, tmb=tmb)
    ys = _experts(block_e, n_used, next_e, slot_tok, hp, w_gate, w_up, w_down,
                  w_shared_gate, w_shared_up, w_shared_down, tmb=tmb)
    return _combine(slots, w_tok, x1, gt2, g_post_ffn, ys, tc=128)


def kernel(x, c, w_ada, b_ada, g_pre_mix, g_post_mix, w_in, conv_w, conv_b, b_igate, b_fgate, g_mlstm_out,
           g_q_norm, w_uq, g_kv_norm, w_ukv, w_out, g_pre_ffn, g_post_ffn, w_router, b_router,
           w_gate, w_up, w_down, w_shared_gate, w_shared_up, w_shared_down):
    assert x.shape[0] == 1 and w_ada.shape[0] == 1, "single sequence, single layer"
    layer = (w_ada, b_ada, g_pre_mix, g_post_mix, w_in, conv_w, conv_b, b_igate, b_fgate, g_mlstm_out,
             g_q_norm, w_uq, g_kv_norm, w_ukv, w_out, g_pre_ffn, g_post_ffn, w_router, b_router,
             w_gate, w_up, w_down, w_shared_gate, w_shared_up, w_shared_down)
    out = _block(x[0], c[0], *[p[0] for p in layer])
    return out[None]
```
